```python
import math
import jax, jax.numpy as jnp
from jax import lax
import numpy as np

D_MODEL = 2048
BATCH = 8
SEQ = 4096
DEPTH = 1

MIX_WIDTH = D_MODEL
C_CONV = MIX_WIDTH // 2
FOX_HEAD_DIM = 64
FOX_HEADS = (MIX_WIDTH - C_CONV) // FOX_HEAD_DIM
FOX_WIDTH = FOX_HEADS * FOX_HEAD_DIM
CONV_WIDTH = 31
Q_BLOCK = 128
D_FF = 5632
IN_COLS = 2 * C_CONV + 3 * FOX_WIDTH + FOX_HEADS
NORM_EPS = 1e-6
LN_EPS = 1e-5
NEG_INF = -1e30

kernel_name = "hymba_conformer_fox_macaron"


def rms_norm(x, g):
    xf = x.astype(jnp.float32)
    y = xf * lax.rsqrt(jnp.mean(xf * xf, axis=-1, keepdims=True) + NORM_EPS)
    return (y * g.astype(jnp.float32)).astype(x.dtype)


def layer_norm(x, g, b):
    xf = x.astype(jnp.float32)
    mu = jnp.mean(xf, axis=-1, keepdims=True)
    var = jnp.mean(jnp.square(xf - mu), axis=-1, keepdims=True)
    y = (xf - mu) * lax.rsqrt(var + LN_EPS)
    return (y * g.astype(jnp.float32) + b.astype(jnp.float32)).astype(x.dtype)


def swiglu(h, w_gate, w_up, w_down):
    return (jax.nn.silu(h @ w_gate) * (h @ w_up)) @ w_down


def conformer_conv_group(u_in, conv_w, conv_b, ln_g, ln_b):
    a, g = jnp.split(u_in, 2, axis=-1)
    u = a * jax.nn.sigmoid(g)
    y = lax.conv_general_dilated(
        u, conv_w[:, None, :].astype(u.dtype),
        window_strides=(1,),
        padding=[(CONV_WIDTH - 1, 0)],
        dimension_numbers=("NWC", "WIO", "NWC"),
        feature_group_count=C_CONV,
    ) + conv_b
    y = layer_norm(y, ln_g, ln_b)
    return jax.nn.silu(y)


def forgetting_attention(q, k, v, f_logit):
    B, S, _ = q.shape
    H, dh = FOX_HEADS, FOX_HEAD_DIM
    q = q.reshape(B, S, H, dh).transpose(0, 2, 1, 3)
    k = k.reshape(B, S, H, dh).transpose(0, 2, 1, 3)
    v = v.reshape(B, S, H, dh).transpose(0, 2, 1, 3)
    log_f = jax.nn.log_sigmoid(f_logit.astype(jnp.float32)).transpose(0, 2, 1)
    c = jnp.cumsum(log_f, axis=-1)
    scale = 1.0 / math.sqrt(dh)
    nb = S // Q_BLOCK
    q_blocks = q.reshape(B, H, nb, Q_BLOCK, dh).transpose(2, 0, 1, 3, 4)
    c_blocks = c.reshape(B, H, nb, Q_BLOCK).transpose(2, 0, 1, 3)
    pos_blocks = jnp.arange(S, dtype=jnp.int32).reshape(nb, Q_BLOCK)
    k_pos = jnp.arange(S, dtype=jnp.int32)

    def one_block(args):
        qb, cqb, pos_b = args
        s = jnp.einsum("bhqd,bhkd->bhqk", qb, k).astype(jnp.float32) * scale
        s = s + cqb[..., :, None] - c[:, :, None, :]
        causal = pos_b[:, None] >= k_pos[None, :]
        s = jnp.where(causal[None, None], s, NEG_INF)
        p = jax.nn.softmax(s, axis=-1)
        return jnp.einsum("bhqk,bhkd->bhqd", p.astype(v.dtype), v)

    out = lax.map(one_block, (q_blocks, c_blocks, pos_blocks))
    return out.transpose(1, 0, 3, 2, 4).reshape(B, S, H * dh)


def _fwd_setup_inputs(seed: int = 0) -> dict:
    key = jax.random.key(seed)
    ks = jax.random.split(key, 20)
    f32 = jnp.float32
    D, F, C = D_MODEL, D_FF, C_CONV

    def nrm(k, shape, fan_in):
        return jax.random.normal(k, shape, f32) * (fan_in ** -0.5)

    def gain(k, n):
        return 1.0 + 0.01 * jax.random.normal(k, (n,), f32)

    return {
        "x": jax.random.normal(ks[0], (BATCH, SEQ, D), f32),
        "ffn1_norm": gain(ks[1], D),
        "ffn1_w_gate": nrm(ks[2], (D, F), D),
        "ffn1_w_up": nrm(ks[3], (D, F), D),
        "ffn1_w_down": nrm(ks[4], (F, D), F),
        "mix_norm": gain(ks[5], D),
        "w_in": nrm(ks[6], (D, IN_COLS), D),
        "fgate_bias": jax.random.uniform(ks[7], (FOX_HEADS,), f32, 1.0, 3.0),
        "conv_w": nrm(ks[8], (CONV_WIDTH, C), CONV_WIDTH),
        "conv_b": 0.01 * jax.random.normal(ks[9], (C,), f32),
        "conv_ln_g": gain(ks[10], C),
        "conv_ln_b": 0.01 * jax.random.normal(ks[11], (C,), f32),
        "w_out": nrm(ks[12], (MIX_WIDTH, D), MIX_WIDTH),
        "ffn2_norm": gain(ks[13], D),
        "ffn2_w_gate": nrm(ks[14], (D, F), D),
        "ffn2_w_up": nrm(ks[15], (D, F), D),
        "ffn2_w_down": nrm(ks[16], (F, D), F),
        "final_norm": gain(ks[17], D),
    }


def _fwd_reference(x, ffn1_norm, ffn1_w_gate, ffn1_w_up, ffn1_w_down, mix_norm, w_in,
              fgate_bias, conv_w, conv_b, conv_ln_g, conv_ln_b, w_out,
              ffn2_norm, ffn2_w_gate, ffn2_w_up, ffn2_w_down, final_norm):
    for _ in range(DEPTH):
        x = x + 0.5 * swiglu(rms_norm(x, ffn1_norm), ffn1_w_gate, ffn1_w_up, ffn1_w_down)
        h = rms_norm(x, mix_norm)
        proj = h @ w_in
        o1 = 2 * C_CONV
        o2 = o1 + FOX_WIDTH
        o3 = o2 + FOX_WIDTH
        o4 = o3 + FOX_WIDTH
        u_conv = proj[..., :o1]
        q = proj[..., o1:o2]
        k = proj[..., o2:o3]
        v = proj[..., o3:o4]
        f_logit = proj[..., o4:] + fgate_bias
        y_conv = conformer_conv_group(u_conv, conv_w, conv_b, conv_ln_g, conv_ln_b)
        y_fox = forgetting_attention(q, k, v, f_logit)
        x = x + jnp.concatenate([y_conv, y_fox], axis=-1) @ w_out
        x = x + 0.5 * swiglu(rms_norm(x, ffn2_norm), ffn2_w_gate, ffn2_w_up, ffn2_w_down)
    return rms_norm(x, final_norm)


import jax as _jax
import jax.numpy as _jnp

TWIN_FORMAT = 'train_step'
FWD_PARAMS = ['x', 'ffn1_norm', 'ffn1_w_gate', 'ffn1_w_up', 'ffn1_w_down', 'mix_norm', 'w_in', 'fgate_bias', 'conv_w', 'conv_b', 'conv_ln_g', 'conv_ln_b', 'w_out', 'ffn2_norm', 'ffn2_w_gate', 'ffn2_w_up', 'ffn2_w_down', 'final_norm']
TWIN_WEIGHTS = ['ffn1_norm', 'ffn1_w_gate', 'ffn1_w_up', 'ffn1_w_down', 'mix_norm', 'w_in', 'fgate_bias', 'conv_w', 'conv_b', 'conv_ln_g', 'conv_ln_b', 'w_out', 'ffn2_norm', 'ffn2_w_gate', 'ffn2_w_up', 'ffn2_w_down', 'final_norm']
TWIN_DIFF_INPUT = 'x'
TWIN_INPUTS = ['x', 'ffn1_norm', 'ffn1_w_gate', 'ffn1_w_up', 'ffn1_w_down', 'mix_norm', 'w_in', 'fgate_bias', 'conv_w', 'conv_b', 'conv_ln_g', 'conv_ln_b', 'w_out', 'ffn2_norm', 'ffn2_w_gate', 'ffn2_w_up', 'ffn2_w_down', 'final_norm', 'loss_target', 'm_ffn1_norm', 'm_ffn1_w_gate', 'm_ffn1_w_up', 'm_ffn1_w_down', 'm_mix_norm', 'm_w_in', 'm_fgate_bias', 'm_conv_w', 'm_conv_b', 'm_conv_ln_g', 'm_conv_ln_b', 'm_w_out', 'm_ffn2_norm', 'm_ffn2_w_gate', 'm_ffn2_w_up', 'm_ffn2_w_down', 'm_final_norm', 'v_ffn1_norm', 'v_ffn1_w_gate', 'v_ffn1_w_up', 'v_ffn1_w_down', 'v_mix_norm', 'v_w_in', 'v_fgate_bias', 'v_conv_w', 'v_conv_b', 'v_conv_ln_g', 'v_conv_ln_b', 'v_w_out', 'v_ffn2_norm', 'v_ffn2_w_gate', 'v_ffn2_w_up', 'v_ffn2_w_down', 'v_final_norm']
TWIN_OUTPUTS = ['loss', 'grad_x', 'grad_ffn1_norm', 'grad_ffn1_w_gate', 'grad_ffn1_w_up', 'grad_ffn1_w_down', 'grad_mix_norm', 'grad_w_in', 'grad_fgate_bias', 'grad_conv_w', 'grad_conv_b', 'grad_conv_ln_g', 'grad_conv_ln_b', 'grad_w_out', 'grad_ffn2_norm', 'grad_ffn2_w_gate', 'grad_ffn2_w_up', 'grad_ffn2_w_down', 'grad_final_norm', 'delta_ffn1_norm', 'delta_ffn1_w_gate', 'delta_ffn1_w_up', 'delta_ffn1_w_down', 'delta_mix_norm', 'delta_w_in', 'delta_fgate_bias', 'delta_conv_w', 'delta_conv_b', 'delta_conv_ln_g', 'delta_conv_ln_b', 'delta_w_out', 'delta_ffn2_norm', 'delta_ffn2_w_gate', 'delta_ffn2_w_up', 'delta_ffn2_w_down', 'delta_final_norm', 'new_m_ffn1_norm', 'new_m_ffn1_w_gate', 'new_m_ffn1_w_up', 'new_m_ffn1_w_down', 'new_m_mix_norm', 'new_m_w_in', 'new_m_fgate_bias', 'new_m_conv_w', 'new_m_conv_b', 'new_m_conv_ln_g', 'new_m_conv_ln_b', 'new_m_w_out', 'new_m_ffn2_norm', 'new_m_ffn2_w_gate', 'new_m_ffn2_w_up', 'new_m_ffn2_w_down', 'new_m_final_norm', 'new_v_ffn1_norm', 'new_v_ffn1_w_gate', 'new_v_ffn1_w_up', 'new_v_ffn1_w_down', 'new_v_mix_norm', 'new_v_w_in', 'new_v_fgate_bias', 'new_v_conv_w', 'new_v_conv_b', 'new_v_conv_ln_g', 'new_v_conv_ln_b', 'new_v_w_out', 'new_v_ffn2_norm', 'new_v_ffn2_w_gate', 'new_v_ffn2_w_up', 'new_v_ffn2_w_down', 'new_v_final_norm']
TWIN_LEAF_KINDS = {'loss': 'loss', 'grad_x': 'grad_x', 'grad_ffn1_norm': 'grad_w', 'grad_ffn1_w_gate': 'grad_w', 'grad_ffn1_w_up': 'grad_w', 'grad_ffn1_w_down': 'grad_w', 'grad_mix_norm': 'grad_w', 'grad_w_in': 'grad_w', 'grad_fgate_bias': 'grad_w', 'grad_conv_w': 'grad_w', 'grad_conv_b': 'grad_w', 'grad_conv_ln_g': 'grad_w', 'grad_conv_ln_b': 'grad_w', 'grad_w_out': 'grad_w', 'grad_ffn2_norm': 'grad_w', 'grad_ffn2_w_gate': 'grad_w', 'grad_ffn2_w_up': 'grad_w', 'grad_ffn2_w_down': 'grad_w', 'grad_final_norm': 'grad_w', 'delta_ffn1_norm': 'delta_w', 'delta_ffn1_w_gate': 'delta_w', 'delta_ffn1_w_up': 'delta_w', 'delta_ffn1_w_down': 'delta_w', 'delta_mix_norm': 'delta_w', 'delta_w_in': 'delta_w', 'delta_fgate_bias': 'delta_w', 'delta_conv_w': 'delta_w', 'delta_conv_b': 'delta_w', 'delta_conv_ln_g': 'delta_w', 'delta_conv_ln_b': 'delta_w', 'delta_w_out': 'delta_w', 'delta_ffn2_norm': 'delta_w', 'delta_ffn2_w_gate': 'delta_w', 'delta_ffn2_w_up': 'delta_w', 'delta_ffn2_w_down': 'delta_w', 'delta_final_norm': 'delta_w', 'new_m_ffn1_norm': 'new_m', 'new_m_ffn1_w_gate': 'new_m', 'new_m_ffn1_w_up': 'new_m', 'new_m_ffn1_w_down': 'new_m', 'new_m_mix_norm': 'new_m', 'new_m_w_in': 'new_m', 'new_m_fgate_bias': 'new_m', 'new_m_conv_w': 'new_m', 'new_m_conv_b': 'new_m', 'new_m_conv_ln_g': 'new_m', 'new_m_conv_ln_b': 'new_m', 'new_m_w_out': 'new_m', 'new_m_ffn2_norm': 'new_m', 'new_m_ffn2_w_gate': 'new_m', 'new_m_ffn2_w_up': 'new_m', 'new_m_ffn2_w_down': 'new_m', 'new_m_final_norm': 'new_m', 'new_v_ffn1_norm': 'new_v', 'new_v_ffn1_w_gate': 'new_v', 'new_v_ffn1_w_up': 'new_v', 'new_v_ffn1_w_down': 'new_v', 'new_v_mix_norm': 'new_v', 'new_v_w_in': 'new_v', 'new_v_fgate_bias': 'new_v', 'new_v_conv_w': 'new_v', 'new_v_conv_b': 'new_v', 'new_v_conv_ln_g': 'new_v', 'new_v_conv_ln_b': 'new_v', 'new_v_w_out': 'new_v', 'new_v_ffn2_norm': 'new_v', 'new_v_ffn2_w_gate': 'new_v', 'new_v_ffn2_w_up': 'new_v', 'new_v_ffn2_w_down': 'new_v', 'new_v_final_norm': 'new_v'}


def _forward(args):
    return _fwd_reference(*[args[k] for k in FWD_PARAMS])


def _output_shape():
    def fwd():
        inp = _fwd_setup_inputs(0)
        return _fwd_reference(*[inp[k] for k in FWD_PARAMS])
    out = _jax.eval_shape(fwd)
    return out.shape, out.dtype

N_MICROBATCH = 1
ADAM_LR = 0.001
ADAM_B1 = 0.9
ADAM_B2 = 0.999
ADAM_EPS = 1e-08
ADAM_WD = 0.01
ADAM_STEP = 10
PER_EXAMPLE_BATCH_AXIS = {'x': 0, 'loss_target': 0}
SHARED_INPUTS = []
_WEIGHT_DTYPES = {'ffn1_norm': _jnp.float32, 'ffn1_w_gate': _jnp.float32, 'ffn1_w_up': _jnp.float32, 'ffn1_w_down': _jnp.float32, 'mix_norm': _jnp.float32, 'w_in': _jnp.float32, 'fgate_bias': _jnp.float32, 'conv_w': _jnp.float32, 'conv_b': _jnp.float32, 'conv_ln_g': _jnp.float32, 'conv_ln_b': _jnp.float32, 'w_out': _jnp.float32, 'ffn2_norm': _jnp.float32, 'ffn2_w_gate': _jnp.float32, 'ffn2_w_up': _jnp.float32, 'ffn2_w_down': _jnp.float32, 'final_norm': _jnp.float32}
MOMENT_SCALE = {'ffn1_norm': 3.997685e-02, 'ffn1_w_gate': 1.712808e-02, 'ffn1_w_up': 1.657728e-02, 'ffn1_w_down': 2.748755e-02, 'mix_norm': 5.053801e-02, 'w_in': 3.223334e-02, 'fgate_bias': 1.172900e-01, 'conv_w': 4.913564e-02, 'conv_b': 1.005786e-01, 'conv_ln_g': 5.775063e-02, 'conv_ln_b': 4.994277e-02, 'w_out': 4.097848e-02, 'ffn2_norm': 3.207200e-02, 'ffn2_w_gate': 1.367270e-02, 'ffn2_w_up': 1.324481e-02, 'ffn2_w_down': 2.195710e-02, 'final_norm': 1.598679e+01}


def _to_microbatches(a, axis):
    t = _jnp.moveaxis(a, axis, 0)
    t = t.reshape((N_MICROBATCH, t.shape[0] // N_MICROBATCH) + t.shape[1:])
    return _jnp.moveaxis(t, 1, axis + 1)


def setup_inputs(seed: int = 0) -> dict:
    inp = _fwd_setup_inputs(seed)
    key = _jax.random.fold_in(_jax.random.key(seed), 7919)
    shape, _ = _output_shape()
    out = dict(inp)
    out["loss_target"] = _jax.random.normal(_jax.random.fold_in(key, 0), shape, _jnp.float32)
    for i, name in enumerate(TWIN_WEIGHTS):
        w = inp[name].astype(_jnp.float32)
        if MOMENT_SCALE is None:
            s = _jnp.sqrt(_jnp.mean(_jnp.square(w)) + 1e-30)
        else:
            s = MOMENT_SCALE[name]
        km, kv = _jax.random.split(_jax.random.fold_in(key, i + 1))
        out[name] = w
        out["m_" + name] = s * _jax.random.normal(km, w.shape, _jnp.float32)
        out["v_" + name] = (s * s) * _jax.random.uniform(kv, w.shape, _jnp.float32, 0.5, 1.5)
    if N_MICROBATCH > 1:
        for name, axis in PER_EXAMPLE_BATCH_AXIS.items():
            out[name] = _to_microbatches(out[name], axis)
    return {'x': out['x'], 'ffn1_norm': out['ffn1_norm'], 'ffn1_w_gate': out['ffn1_w_gate'], 'ffn1_w_up': out['ffn1_w_up'], 'ffn1_w_down': out['ffn1_w_down'], 'mix_norm': out['mix_norm'], 'w_in': out['w_in'], 'fgate_bias': out['fgate_bias'], 'conv_w': out['conv_w'], 'conv_b': out['conv_b'], 'conv_ln_g': out['conv_ln_g'], 'conv_ln_b': out['conv_ln_b'], 'w_out': out['w_out'], 'ffn2_norm': out['ffn2_norm'], 'ffn2_w_gate': out['ffn2_w_gate'], 'ffn2_w_up': out['ffn2_w_up'], 'ffn2_w_down': out['ffn2_w_down'], 'final_norm': out['final_norm'], 'loss_target': out['loss_target'], 'm_ffn1_norm': out['m_ffn1_norm'], 'm_ffn1_w_gate': out['m_ffn1_w_gate'], 'm_ffn1_w_up': out['m_ffn1_w_up'], 'm_ffn1_w_down': out['m_ffn1_w_down'], 'm_mix_norm': out['m_mix_norm'], 'm_w_in': out['m_w_in'], 'm_fgate_bias': out['m_fgate_bias'], 'm_conv_w': out['m_conv_w'], 'm_conv_b': out['m_conv_b'], 'm_conv_ln_g': out['m_conv_ln_g'], 'm_conv_ln_b': out['m_conv_ln_b'], 'm_w_out': out['m_w_out'], 'm_ffn2_norm': out['m_ffn2_norm'], 'm_ffn2_w_gate': out['m_ffn2_w_gate'], 'm_ffn2_w_up': out['m_ffn2_w_up'], 'm_ffn2_w_down': out['m_ffn2_w_down'], 'm_final_norm': out['m_final_norm'], 'v_ffn1_norm': out['v_ffn1_norm'], 'v_ffn1_w_gate': out['v_ffn1_w_gate'], 'v_ffn1_w_up': out['v_ffn1_w_up'], 'v_ffn1_w_down': out['v_ffn1_w_down'], 'v_mix_norm': out['v_mix_norm'], 'v_w_in': out['v_w_in'], 'v_fgate_bias': out['v_fgate_bias'], 'v_conv_w': out['v_conv_w'], 'v_conv_b': out['v_conv_b'], 'v_conv_ln_g': out['v_conv_ln_g'], 'v_conv_ln_b': out['v_conv_ln_b'], 'v_w_out': out['v_w_out'], 'v_ffn2_norm': out['v_ffn2_norm'], 'v_ffn2_w_gate': out['v_ffn2_w_gate'], 'v_ffn2_w_up': out['v_ffn2_w_up'], 'v_ffn2_w_down': out['v_ffn2_w_down'], 'v_final_norm': out['v_final_norm']}


def _loss(weights, diff, rest, loss_target):
    with _jax.named_scope("forward"):
        args = {**rest, TWIN_DIFF_INPUT: diff, **{k: w.astype(_WEIGHT_DTYPES[k]) for k, w in weights.items()}}
        y = _forward(args)
    with _jax.named_scope("loss_head"):
        err = _jnp.square(y.astype(_jnp.float32) - loss_target)
        return 0.5 * _jnp.sum(_jnp.mean(err, axis=-1)) if err.ndim else 0.5 * err


def _adamw(w, g, m, v):
    m = ADAM_B1 * m + (1.0 - ADAM_B1) * g
    v = ADAM_B2 * v + (1.0 - ADAM_B2) * _jnp.square(g)
    m_hat = m / (1.0 - ADAM_B1 ** ADAM_STEP)
    v_hat = v / (1.0 - ADAM_B2 ** ADAM_STEP)
    delta = -ADAM_LR * (m_hat / (_jnp.sqrt(v_hat) + ADAM_EPS) + ADAM_WD * w)
    return delta, m, v


def reference(x, ffn1_norm, ffn1_w_gate, ffn1_w_up, ffn1_w_down, mix_norm, w_in, fgate_bias, conv_w, conv_b, conv_ln_g, conv_ln_b, w_out, ffn2_norm, ffn2_w_gate, ffn2_w_up, ffn2_w_down, final_norm, loss_target, m_ffn1_norm, m_ffn1_w_gate, m_ffn1_w_up, m_ffn1_w_down, m_mix_norm, m_w_in, m_fgate_bias, m_conv_w, m_conv_b, m_conv_ln_g, m_conv_ln_b, m_w_out, m_ffn2_norm, m_ffn2_w_gate, m_ffn2_w_up, m_ffn2_w_down, m_final_norm, v_ffn1_norm, v_ffn1_w_gate, v_ffn1_w_up, v_ffn1_w_down, v_mix_norm, v_w_in, v_fgate_bias, v_conv_w, v_conv_b, v_conv_ln_g, v_conv_ln_b, v_w_out, v_ffn2_norm, v_ffn2_w_gate, v_ffn2_w_up, v_ffn2_w_down, v_final_norm):
    given = dict(x=x, ffn1_norm=ffn1_norm, ffn1_w_gate=ffn1_w_gate, ffn1_w_up=ffn1_w_up, ffn1_w_down=ffn1_w_down, mix_norm=mix_norm, w_in=w_in, fgate_bias=fgate_bias, conv_w=conv_w, conv_b=conv_b, conv_ln_g=conv_ln_g, conv_ln_b=conv_ln_b, w_out=w_out, ffn2_norm=ffn2_norm, ffn2_w_gate=ffn2_w_gate, ffn2_w_up=ffn2_w_up, ffn2_w_down=ffn2_w_down, final_norm=final_norm, loss_target=loss_target, m_ffn1_norm=m_ffn1_norm, m_ffn1_w_gate=m_ffn1_w_gate, m_ffn1_w_up=m_ffn1_w_up, m_ffn1_w_down=m_ffn1_w_down, m_mix_norm=m_mix_norm, m_w_in=m_w_in, m_fgate_bias=m_fgate_bias, m_conv_w=m_conv_w, m_conv_b=m_conv_b, m_conv_ln_g=m_conv_ln_g, m_conv_ln_b=m_conv_ln_b, m_w_out=m_w_out, m_ffn2_norm=m_ffn2_norm, m_ffn2_w_gate=m_ffn2_w_gate, m_ffn2_w_up=m_ffn2_w_up, m_ffn2_w_down=m_ffn2_w_down, m_final_norm=m_final_norm, v_ffn1_norm=v_ffn1_norm, v_ffn1_w_gate=v_ffn1_w_gate, v_ffn1_w_up=v_ffn1_w_up, v_ffn1_w_down=v_ffn1_w_down, v_mix_norm=v_mix_norm, v_w_in=v_w_in, v_fgate_bias=v_fgate_bias, v_conv_w=v_conv_w, v_conv_b=v_conv_b, v_conv_ln_g=v_conv_ln_g, v_conv_ln_b=v_conv_ln_b, v_w_out=v_w_out, v_ffn2_norm=v_ffn2_norm, v_ffn2_w_gate=v_ffn2_w_gate, v_ffn2_w_up=v_ffn2_w_up, v_ffn2_w_down=v_ffn2_w_down, v_final_norm=v_final_norm)
    weights = {n: given[n] for n in TWIN_WEIGHTS}
    shared = {n: given[n] for n in SHARED_INPUTS}
    per_example = {n: given[n] for n in ['x']}
    grad_fn = _jax.value_and_grad(_loss, argnums=(0, 1))

    def one_microbatch(ex, loss_target):
        ex = dict(ex)
        diff = ex.pop(TWIN_DIFF_INPUT)
        return grad_fn(weights, diff, {**shared, **ex}, loss_target)

    if N_MICROBATCH == 1:
        loss, (grad_w, grad_x) = one_microbatch(per_example, given["loss_target"])
    else:
        def body(carry, xs):
            loss_sum, grad_sum = carry
            l_k, (gw_k, gx_k) = one_microbatch(xs[0], xs[1])
            with _jax.named_scope("update"):
                return (loss_sum + l_k, _jax.tree.map(_jnp.add, grad_sum, gw_k)), gx_k

        init = (_jnp.zeros((), _jnp.float32), _jax.tree.map(_jnp.zeros_like, weights))
        (loss, grad_w), grad_x = _jax.lax.scan(body, init, (per_example, given["loss_target"]))
    with _jax.named_scope("update"):
        delta_w, new_m, new_v = {}, {}, {}
        for n in TWIN_WEIGHTS:
            delta_w[n], new_m[n], new_v[n] = _adamw(weights[n], grad_w[n], given["m_" + n], given["v_" + n])
    return (loss, grad_x, *[grad_w[n] for n in TWIN_WEIGHTS], *[delta_w[n] for n in TWIN_WEIGHTS],
            *[new_m[n] for n in TWIN_WEIGHTS], *[new_v[n] for n in TWIN_WEIGHTS])
```

```python
import math

import jax
import jax.numpy as jnp
from jax import lax
from jax.experimental import pallas as pl
from jax.experimental.pallas import tpu as pltpu

F32 = jnp.float32
BF16 = jnp.bfloat16

N_DEV = 8
MESH_ID = pl.DeviceIdType.MESH
HEAD_DIM = 64
CONV_WIDTH = 31
CONV_HALO = 32
NORM_EPS = 1e-6
LN_EPS = 1e-5
NEG_INF = -1e30
LANES = 128
V7X_VMEM_LIMIT = 52 * 1024 * 1024

ADAM_LR = 0.001
ADAM_B1 = 0.9
ADAM_B2 = 0.999
ADAM_EPS = 1e-08
ADAM_WD = 0.01
ADAM_STEP = 10


def _params(n_grid_axes):
    return pltpu.CompilerParams(dimension_semantics=("arbitrary",) * n_grid_axes,
                                vmem_limit_bytes=V7X_VMEM_LIMIT)


def _tile(n, pref, mult=8):
    t = min(pref, n)
    while t >= mult:
        if n % t == 0 and t % mult == 0:
            return t
        t -= mult
    return n


def _dot(a, b):
    return jnp.dot(a, b, preferred_element_type=F32)


def _dot_nt(a, b):
    return lax.dot_general(a, b, (((1,), (1,)), ((), ())), preferred_element_type=F32)


def _dot_tn(a, b):
    return lax.dot_general(a, b, (((0,), (0,)), ((), ())), preferred_element_type=F32)


def _sigmoid(x):
    return 1.0 / (1.0 + jnp.exp(-x))


def _rms_fwd(x, g):
    r = lax.rsqrt(jnp.mean(x * x, axis=-1, keepdims=True) + NORM_EPS)
    return x * r * g


def _rms_bwd(dh, x, g):
    r = lax.rsqrt(jnp.mean(x * x, axis=-1, keepdims=True) + NORM_EPS)
    xh = x * r
    dxh = dh * g
    dx = r * (dxh - xh * jnp.mean(dxh * xh, axis=-1, keepdims=True))
    return dx, jnp.sum(dh * xh, axis=0, keepdims=True)


def _split3(x):
    hi = x.astype(BF16)
    r = x - hi.astype(F32)
    mid = r.astype(BF16)
    lo = (r - mid.astype(F32)).astype(BF16)
    return hi, mid, lo


def _blk(ref):
    return ref[0] if len(ref.shape) == 3 else ref[...]


def _mesh_pos():
    return lax.axis_index("x"), lax.axis_index("y"), lax.axis_index("c")


def _peer(pos, k):
    x, y, c = pos
    return (1 - x if k & 4 else x, 1 - y if k & 2 else y, 1 - c if k & 1 else c)


def _lin(pos):
    x, y, c = pos
    return 4 * x + 2 * y + c


def _exchange(name, arrays, all_to_all):
    n = len(arrays)
    out_shapes = [jax.ShapeDtypeStruct(a.shape if all_to_all else (N_DEV,) + a.shape, a.dtype)
                  for a in arrays]

    def body(*refs):
        ins, outs = refs[:n], refs[n:2 * n]
        send_sems, recv_sems, local_sems = refs[2 * n:]
        pos = _mesh_pos()
        me = _lin(pos)
        local = []
        for a in range(n):
            src = ins[a].at[me] if all_to_all else ins[a]
            cp = pltpu.make_async_copy(src, outs[a].at[me], local_sems.at[a])
            cp.start()
            local.append(cp)
        remote = []
        for a in range(n):
            for k in range(1, N_DEV):
                peer = _peer(pos, k)
                src = ins[a].at[_lin(peer)] if all_to_all else ins[a]
                cp = pltpu.make_async_remote_copy(
                    src_ref=src, dst_ref=outs[a].at[me],
                    send_sem=send_sems.at[a * (N_DEV - 1) + k - 1],
                    recv_sem=recv_sems.at[a * (N_DEV - 1) + k - 1],
                    device_id=peer, device_id_type=MESH_ID)
                cp.start()
                remote.append(cp)
        for cp in remote:
            cp.wait()
        for cp in local:
            cp.wait()

    any_spec = pl.BlockSpec(memory_space=pl.ANY)
    return pl.pallas_call(
        body, out_shape=out_shapes, in_specs=[any_spec] * n, out_specs=[any_spec] * n,
        scratch_shapes=[pltpu.SemaphoreType.DMA((n * (N_DEV - 1),)),
                        pltpu.SemaphoreType.DMA((n * (N_DEV - 1),)),
                        pltpu.SemaphoreType.DMA((n,))],
        name=name)(*arrays)


def _reduce_adam(name, parts, w=None, m=None, v=None):
    n, R, C = parts.shape
    tr = _tile(R, 256)
    do_adam = w is not None
    bc1 = 1.0 - ADAM_B1 ** ADAM_STEP
    bc2 = 1.0 - ADAM_B2 ** ADAM_STEP

    def body(*refs):
        p_ref = refs[0]
        g = p_ref[0].astype(F32)
        for d in range(1, n):
            g = g + p_ref[d].astype(F32)
        if not do_adam:
            refs[1][...] = g
            return
        w_ref, m_ref, v_ref, g_ref, d_ref, nm_ref, nv_ref = refs[1:]
        g_ref[...] = g
        nm = ADAM_B1 * m_ref[...] + (1.0 - ADAM_B1) * g
        nv = ADAM_B2 * v_ref[...] + (1.0 - ADAM_B2) * (g * g)
        m_hat = nm / bc1
        v_hat = nv / bc2
        d_ref[...] = -ADAM_LR * (m_hat / (jnp.sqrt(v_hat) + ADAM_EPS) + ADAM_WD * w_ref[...])
        nm_ref[...] = nm
        nv_ref[...] = nv

    row = pl.BlockSpec((tr, C), lambda i: (i, 0))
    part = pl.BlockSpec((n, tr, C), lambda i: (0, i, 0))
    shard = jax.ShapeDtypeStruct((R, C), F32)
    if do_adam:
        return pl.pallas_call(body, grid=(R // tr,), in_specs=[part, row, row, row],
                              out_specs=[row] * 4, out_shape=[shard] * 4,
                              compiler_params=_params(1), name=name)(parts, w, m, v)
    return pl.pallas_call(body, grid=(R // tr,), in_specs=[part], out_specs=row, out_shape=shard,
                          compiler_params=_params(1), name=name)(parts)


def _ffn_fwd(name, x, gain, wg, wu, wd):
    S, D = x.shape
    nb, _, Fs = wg.shape
    tm = _tile(S, 512)

    def body(x_ref, g_ref, wg_ref, wu_ref, wd_ref, xo_ref, h_ref, G_ref, U_ref, acc_ref):
        j = pl.program_id(1)

        @pl.when(j == 0)
        def _():
            h_ref[...] = _rms_fwd(x_ref[...], g_ref[...]).astype(BF16)
            acc_ref[...] = jnp.zeros_like(acc_ref)

        h = h_ref[...]
        G = _dot(h, wg_ref[0])
        U = _dot(h, wu_ref[0])
        G_ref[0] = G
        U_ref[0] = U
        a = G * _sigmoid(G) * U
        acc_ref[...] += _dot(a.astype(BF16), wd_ref[0])

        @pl.when(j == nb - 1)
        def _():
            xo_ref[...] = x_ref[...] + 0.5 * acc_ref[...]

    row = pl.BlockSpec((tm, D), lambda i, j: (i, 0))
    act = pl.BlockSpec((1, tm, Fs), lambda i, j: (j, i, 0))
    return pl.pallas_call(
        body, grid=(S // tm, nb),
        in_specs=[row, pl.BlockSpec((1, D), lambda i, j: (0, 0)),
                  pl.BlockSpec((1, D, Fs), lambda i, j: (j, 0, 0)),
                  pl.BlockSpec((1, D, Fs), lambda i, j: (j, 0, 0)),
                  pl.BlockSpec((1, Fs, D), lambda i, j: (j, 0, 0))],
        out_specs=[row, row, act, act],
        out_shape=[jax.ShapeDtypeStruct((S, D), F32), jax.ShapeDtypeStruct((S, D), BF16),
                   jax.ShapeDtypeStruct((nb, S, Fs), F32), jax.ShapeDtypeStruct((nb, S, Fs), F32)],
        scratch_shapes=[pltpu.VMEM((tm, D), F32)],
        compiler_params=_params(2), name=name)(x, gain, wg, wu, wd)


def _ffn_bwd_act(name, dxo, x_in, gain, G, U, wg, wu, wd):
    S, D = x_in.shape
    nb, _, Fs = wg.shape
    tm = _tile(S, 256)

    def body(dxo_ref, x_ref, g_ref, G_ref, U_ref, wg_ref, wu_ref, wd_ref,
             dG_ref, dU_ref, A_ref, dx_ref, dgain_ref, dxb_ref, acc_ref):
        i, j = pl.program_id(0), pl.program_id(1)

        @pl.when(j == 0)
        def _():
            dxb_ref[...] = dxo_ref[...].astype(BF16)
            acc_ref[...] = jnp.zeros_like(acc_ref)

        @pl.when((i == 0) & (j == 0))
        def _():
            dgain_ref[...] = jnp.zeros_like(dgain_ref)

        dA = 0.5 * _dot_nt(dxb_ref[...], wd_ref[0])
        Gv = G_ref[0]
        Uv = U_ref[0]
        sg = _sigmoid(Gv)
        sl = Gv * sg
        dG = (dA * Uv * (sg * (1.0 + Gv * (1.0 - sg)))).astype(BF16)
        dU = (dA * sl).astype(BF16)
        dG_ref[0] = dG
        dU_ref[0] = dU
        A_ref[0] = (sl * Uv).astype(BF16)
        acc_ref[...] += _dot_nt(dG, wg_ref[0]) + _dot_nt(dU, wu_ref[0])

        @pl.when(j == nb - 1)
        def _():
            dx, dg = _rms_bwd(acc_ref[...], x_ref[...], g_ref[...])
            dx_ref[...] = dxo_ref[...] + dx
            dgain_ref[...] += dg

    row = pl.BlockSpec((tm, D), lambda i, j: (i, 0))
    vec = pl.BlockSpec((1, D), lambda i, j: (0, 0))
    act = pl.BlockSpec((1, tm, Fs), lambda i, j: (j, i, 0))
    wcol = pl.BlockSpec((1, D, Fs), lambda i, j: (j, 0, 0))
    wrow = pl.BlockSpec((1, Fs, D), lambda i, j: (j, 0, 0))
    act_shape = jax.ShapeDtypeStruct((nb, S, Fs), BF16)
    return pl.pallas_call(
        body, grid=(S // tm, nb),
        in_specs=[row, row, vec, act, act, wcol, wcol, wrow],
        out_specs=[act, act, act, row, vec],
        out_shape=[act_shape, act_shape, act_shape, jax.ShapeDtypeStruct((S, D), F32),
                   jax.ShapeDtypeStruct((1, D), F32)],
        scratch_shapes=[pltpu.VMEM((tm, D), BF16), pltpu.VMEM((tm, D), F32)],
        compiler_params=_params(2), name=name)(dxo, x_in, gain, G, U, wg, wu, wd)


def _mm_tn(name, lhs, rhs, out_shape, lhs_spec, rhs_spec, out_spec, grid, scale, out_dtype):
    acc_shape = tuple(out_spec.block_shape[-2:])
    n_red = grid[-1]

    def body(l_ref, r_ref, o_ref, acc_ref):
        i = pl.program_id(len(grid) - 1)

        @pl.when(i == 0)
        def _():
            acc_ref[...] = jnp.zeros_like(acc_ref)

        acc_ref[...] += _dot_tn(_blk(l_ref).astype(BF16), _blk(r_ref).astype(BF16))

        @pl.when(i == n_red - 1)
        def _():
            res = (scale * acc_ref[...]).astype(out_dtype)
            if len(o_ref.shape) == 3:
                o_ref[0] = res
            else:
                o_ref[...] = res

    return pl.pallas_call(
        body, grid=grid, in_specs=[lhs_spec, rhs_spec], out_specs=out_spec,
        out_shape=jax.ShapeDtypeStruct(out_shape, out_dtype),
        scratch_shapes=[pltpu.VMEM(acc_shape, F32)],
        compiler_params=_params(len(grid)), name=name)(lhs, rhs)


def _norm_mm(name, x, gain, w, tn):
    S, D = x.shape
    N = w.shape[1]
    tm = _tile(S, 512)

    def body(x_ref, g_ref, w_ref, o_ref, h_ref):
        @pl.when(pl.program_id(1) == 0)
        def _():
            h_ref[...] = _rms_fwd(x_ref[...], g_ref[...]).astype(BF16)

        o_ref[...] = _dot(h_ref[...], w_ref[...])

    row = pl.BlockSpec((tm, D), lambda i, j: (i, 0))
    return pl.pallas_call(
        body, grid=(S // tm, N // tn),
        in_specs=[row, pl.BlockSpec((1, D), lambda i, j: (0, 0)),
                  pl.BlockSpec((D, tn), lambda i, j: (0, j))],
        out_specs=[pl.BlockSpec((tm, tn), lambda i, j: (i, j)), row],
        out_shape=[jax.ShapeDtypeStruct((S, N), F32), jax.ShapeDtypeStruct((S, D), BF16)],
        compiler_params=_params(2), name=name)(x, gain, w)


def _mm_res(name, a, w, res, tn):
    S, K = a.shape
    N = w.shape[1]
    tm = _tile(S, 512)

    def body(a_ref, w_ref, r_ref, o_ref):
        o_ref[...] = r_ref[...] + _dot(a_ref[...], w_ref[...])

    tile = pl.BlockSpec((tm, tn), lambda i, j: (i, j))
    return pl.pallas_call(
        body, grid=(S // tm, N // tn),
        in_specs=[pl.BlockSpec((tm, K), lambda i, j: (i, 0)),
                  pl.BlockSpec((K, tn), lambda i, j: (0, j)), tile],
        out_specs=tile, out_shape=jax.ShapeDtypeStruct((S, N), F32),
        compiler_params=_params(2), name=name)(a, w, res)


def _mm_nt(name, a, b, tk, norm_bwd=None):
    S, K = a.shape
    N = b.shape[0]
    tm = _tile(S, 512)
    nk = K // tk

    def body(*refs):
        if norm_bwd is None:
            a_ref, b_ref, o_ref, acc_ref = refs
        else:
            a_ref, b_ref, x_ref, g_ref, dres_ref, o_ref, dgain_ref, acc_ref = refs
        i, k = pl.program_id(0), pl.program_id(1)

        @pl.when(k == 0)
        def _():
            acc_ref[...] = jnp.zeros_like(acc_ref)

        acc_ref[...] += _dot_nt(a_ref[...].astype(BF16), b_ref[...])

        if norm_bwd is None:
            @pl.when(k == nk - 1)
            def _():
                o_ref[...] = acc_ref[...]
        else:
            @pl.when((i == 0) & (k == 0))
            def _():
                dgain_ref[...] = jnp.zeros_like(dgain_ref)

            @pl.when(k == nk - 1)
            def _():
                dx, dg = _rms_bwd(acc_ref[...], x_ref[...], g_ref[...])
                o_ref[...] = dres_ref[...] + dx
                dgain_ref[...] += dg

    a_spec = pl.BlockSpec((tm, tk), lambda i, k: (i, k))
    b_spec = pl.BlockSpec((N, tk), lambda i, k: (0, k))
    row = pl.BlockSpec((tm, N), lambda i, k: (i, 0))
    vec = pl.BlockSpec((1, N), lambda i, k: (0, 0))
    out = jax.ShapeDtypeStruct((S, N), F32)
    scratch = [pltpu.VMEM((tm, N), F32)]
    if norm_bwd is None:
        return pl.pallas_call(body, grid=(S // tm, nk), in_specs=[a_spec, b_spec], out_specs=row,
                              out_shape=out, scratch_shapes=scratch,
                              compiler_params=_params(2), name=name)(a, b)
    x_in, gain, dres = norm_bwd
    return pl.pallas_call(body, grid=(S // tm, nk), in_specs=[a_spec, b_spec, row, vec, row],
                          out_specs=[row, vec],
                          out_shape=[out, jax.ShapeDtypeStruct((1, N), F32)],
                          scratch_shapes=scratch,
                          compiler_params=_params(2), name=name)(a, b, x_in, gain, dres)


def _final(name, x, gain, target):
    S, D = x.shape
    tm = _tile(S, 512)

    def body(x_ref, g_ref, t_ref, dx_ref, dgain_ref, loss_ref):
        @pl.when(pl.program_id(0) == 0)
        def _():
            dgain_ref[...] = jnp.zeros_like(dgain_ref)
            loss_ref[...] = jnp.zeros_like(loss_ref)

        xv = x_ref[...]
        err = _rms_fwd(xv, g_ref[...]) - t_ref[...]
        per_tok = jnp.mean(err * err, axis=-1, keepdims=True)
        loss_ref[...] += 0.5 * jnp.sum(per_tok, axis=0, keepdims=True)
        dx, dg = _rms_bwd(err * (1.0 / D), xv, g_ref[...])
        dx_ref[...] = dx
        dgain_ref[...] += dg

    row = pl.BlockSpec((tm, D), lambda i: (i, 0))
    vec = pl.BlockSpec((1, D), lambda i: (0, 0))
    return pl.pallas_call(
        body, grid=(S // tm,), in_specs=[row, vec, row],
        out_specs=[row, vec, pl.BlockSpec((1, LANES), lambda i: (0, 0))],
        out_shape=[jax.ShapeDtypeStruct((S, D), F32), jax.ShapeDtypeStruct((1, D), F32),
                   jax.ShapeDtypeStruct((1, LANES), F32)],
        compiler_params=_params(1), name=name)(x, gain, target)


def _conv_tiles(S):
    ts = _tile(S, 256, CONV_HALO)
    return ts, ts // CONV_HALO


def _ln_stats(yc):
    mu = jnp.mean(yc, axis=-1, keepdims=True)
    d = yc - mu
    rs = lax.rsqrt(jnp.mean(d * d, axis=-1, keepdims=True) + LN_EPS)
    return d * rs, rs


def _conv_fwd(name, proj, C, cw, cb, lg, lb):
    S = proj.shape[0]
    ts, hb = _conv_tiles(S)

    def body(a_ref, g_ref, ah_ref, gh_ref, cw_ref, cb_ref, lg_ref, lb_ref, yc_ref, y_ref, ubuf):
        i = pl.program_id(0)
        uh = ah_ref[...] * _sigmoid(gh_ref[...])
        ubuf[pl.ds(0, CONV_HALO), :] = jnp.where(i > 0, uh, 0.0)
        ubuf[pl.ds(CONV_HALO, ts), :] = a_ref[...] * _sigmoid(g_ref[...])
        acc = jnp.zeros((ts, C), F32)
        for k in range(CONV_WIDTH):
            acc = acc + cw_ref[pl.ds(k, 1), :] * ubuf[pl.ds(k + CONV_HALO - CONV_WIDTH + 1, ts), :]
        yc = acc + cb_ref[...]
        yc_ref[...] = yc
        yn, _ = _ln_stats(yc)
        z = yn * lg_ref[...] + lb_ref[...]
        y_ref[...] = (z * _sigmoid(z)).astype(BF16)

    main = lambda col: pl.BlockSpec((ts, C), lambda i: (i, col))
    halo = lambda col: pl.BlockSpec((CONV_HALO, C), lambda i: (jnp.maximum(i * hb - 1, 0), col))
    vec = pl.BlockSpec((1, C), lambda i: (0, 0))
    return pl.pallas_call(
        body, grid=(S // ts,),
        in_specs=[main(0), main(1), halo(0), halo(1),
                  pl.BlockSpec((CONV_HALO, C), lambda i: (0, 0)), vec, vec, vec],
        out_specs=[pl.BlockSpec((ts, C), lambda i: (i, 0))] * 2,
        out_shape=[jax.ShapeDtypeStruct((S, C), F32), jax.ShapeDtypeStruct((S, C), BF16)],
        scratch_shapes=[pltpu.VMEM((ts + CONV_HALO, C), F32)],
        compiler_params=_params(1), name=name)(proj, proj, proj, proj, cw, cb, lg, lb)


def _conv_bwd(name, dmix, yc, proj, C, cw, lg, lb):
    S = proj.shape[0]
    ts, hb = _conv_tiles(S)
    n_t = S // ts

    def body(dy_ref, yc_ref, dyh_ref, ych_ref, a_ref, g_ref, ah_ref, gh_ref, cw_ref, lg_ref, lb_ref,
             dag_ref, dcw_ref, dcb_ref, dlg_ref, dlb_ref, ubuf, dbuf):
        i = pl.program_id(0)

        @pl.when(i == 0)
        def _():
            dcw_ref[...] = jnp.zeros_like(dcw_ref)
            dcb_ref[...] = jnp.zeros_like(dcb_ref)
            dlg_ref[...] = jnp.zeros_like(dlg_ref)
            dlb_ref[...] = jnp.zeros_like(dlb_ref)

        def ln_bwd(dy, ycv):
            yn, rs = _ln_stats(ycv)
            z = yn * lg_ref[...] + lb_ref[...]
            sg = _sigmoid(z)
            dz = dy * (sg * (1.0 + z * (1.0 - sg)))
            dyn = dz * lg_ref[...]
            dyc = rs * (dyn - jnp.mean(dyn, axis=-1, keepdims=True)
                        - yn * jnp.mean(dyn * yn, axis=-1, keepdims=True))
            return dyc, dz, yn

        dyc, dz, yn = ln_bwd(dy_ref[...], yc_ref[...])
        dlg_ref[...] += jnp.sum(dz * yn, axis=0, keepdims=True)
        dlb_ref[...] += jnp.sum(dz, axis=0, keepdims=True)
        dcb_ref[...] += jnp.sum(dyc, axis=0, keepdims=True)
        dych, _, _ = ln_bwd(dyh_ref[...], ych_ref[...])
        dbuf[pl.ds(0, ts), :] = dyc
        dbuf[pl.ds(ts, CONV_HALO), :] = jnp.where(i < n_t - 1, dych, 0.0)

        av = a_ref[...]
        sgm = _sigmoid(g_ref[...])
        uh = ah_ref[...] * _sigmoid(gh_ref[...])
        ubuf[pl.ds(0, CONV_HALO), :] = jnp.where(i > 0, uh, 0.0)
        ubuf[pl.ds(CONV_HALO, ts), :] = av * sgm

        du = jnp.zeros((ts, C), F32)
        for k in range(CONV_WIDTH):
            du = du + cw_ref[pl.ds(k, 1), :] * dbuf[pl.ds(CONV_WIDTH - 1 - k, ts), :]
            tap = ubuf[pl.ds(k + CONV_HALO - CONV_WIDTH + 1, ts), :]
            dcw_ref[pl.ds(k, 1), :] += jnp.sum(dyc * tap, axis=0, keepdims=True)
        dag_ref[:, pl.ds(0, C)] = (du * sgm).astype(BF16)
        dag_ref[:, pl.ds(C, C)] = (du * av * sgm * (1.0 - sgm)).astype(BF16)

    main = lambda col: pl.BlockSpec((ts, C), lambda i: (i, col))
    past = lambda col: pl.BlockSpec((CONV_HALO, C), lambda i: (jnp.maximum(i * hb - 1, 0), col))
    nxt = pl.BlockSpec((CONV_HALO, C), lambda i: (jnp.minimum((i + 1) * hb, n_t * hb - 1), 0))
    vec = pl.BlockSpec((1, C), lambda i: (0, 0))
    full = pl.BlockSpec((CONV_HALO, C), lambda i: (0, 0))
    vshape = jax.ShapeDtypeStruct((1, C), F32)
    return pl.pallas_call(
        body, grid=(n_t,),
        in_specs=[main(0), main(0), nxt, nxt, main(0), main(1), past(0), past(1), full, vec, vec],
        out_specs=[pl.BlockSpec((ts, 2 * C), lambda i: (i, 0)), full, vec, vec, vec],
        out_shape=[jax.ShapeDtypeStruct((S, 2 * C), BF16),
                   jax.ShapeDtypeStruct((CONV_HALO, C), F32), vshape, vshape, vshape],
        scratch_shapes=[pltpu.VMEM((ts + CONV_HALO, C), F32), pltpu.VMEM((ts + CONV_HALO, C), F32)],
        compiler_params=_params(1),
        name=name)(dmix, yc, dmix, yc, proj, proj, proj, proj, cw, lg, lb)


AUG = 3


def _gate_prep(name, proj, f_blk, fbias, n_pair):
    S = proj.shape[0]
    ts = _tile(S, 512)
    W = LANES * n_pair

    def body(pf_ref, fb_ref, ka_ref, carry):
        @pl.when(pl.program_id(0) == 0)
        def _():
            carry[...] = jnp.zeros_like(carry)

        f = pf_ref[...] + fb_ref[...]
        logf = jnp.minimum(f, 0.0) - jnp.log(1.0 + jnp.exp(-jnp.abs(f)))
        r = lax.broadcasted_iota(jnp.int32, (ts, ts), 0)
        c = lax.broadcasted_iota(jnp.int32, (ts, ts), 1)
        ltri = (c <= r).astype(BF16)
        hi, mid, lo = _split3(logf)
        cs = _dot(ltri, hi) + _dot(ltri, mid) + _dot(ltri, lo) + carry[...]
        carry[...] = cs[ts - 1:ts, :]
        hh = lax.broadcasted_iota(jnp.int32, (LANES, W), 0)
        ll = lax.broadcasted_iota(jnp.int32, (LANES, W), 1)
        pair, w = ll >> 7, ll & (LANES - 1)
        ka = jnp.zeros((ts, W), F32)
        for p, piece in enumerate(_split3(-cs)):
            e = (((w == HEAD_DIM + p) & (hh == 2 * pair)) | ((w == p) & (hh == 2 * pair + 1)))
            ka = ka + _dot(piece, e.astype(BF16))
        lw = lax.broadcasted_iota(jnp.int32, (1, W), 1) & (LANES - 1)
        ka = ka + ((lw == HEAD_DIM + AUG) | (lw == AUG)).astype(F32)
        ka_ref[...] = ka.astype(BF16)

    return pl.pallas_call(
        body, grid=(S // ts,),
        in_specs=[pl.BlockSpec((ts, LANES), lambda i: (i, f_blk)),
                  pl.BlockSpec((1, LANES), lambda i: (0, 0))],
        out_specs=pl.BlockSpec((ts, W), lambda i: (i, 0)),
        out_shape=jax.ShapeDtypeStruct((S, W), BF16),
        scratch_shapes=[pltpu.VMEM((1, LANES), F32)],
        compiler_params=_params(1), name=name)(proj, fbias)


def _gate_bwd(name, sp, rs, proj, f_blk, fbias, n_pair):
    S = proj.shape[0]
    ts = _tile(S, 512)
    n_t = S // ts
    W = LANES * n_pair

    def body(sp_ref, rs_ref, pf_ref, fb_ref, df_ref, dfb_ref, carry):
        @pl.when(pl.program_id(0) == 0)
        def _():
            carry[...] = jnp.zeros_like(carry)
            dfb_ref[...] = jnp.zeros_like(dfb_ref)

        ll = lax.broadcasted_iota(jnp.int32, (W, LANES), 0)
        hh = lax.broadcasted_iota(jnp.int32, (W, LANES), 1)
        pair, w = ll >> 7, ll & (LANES - 1)
        first, second = hh == 2 * pair, hh == 2 * pair + 1

        def pick(ref, lane_first, lane_second):
            sel = (((w == lane_first) & first) | ((w == lane_second) & second)).astype(BF16)
            hi, mid, lo = _split3(ref[...])
            return _dot(hi, sel) + _dot(mid, sel) + _dot(lo, sel)

        dc = pick(rs_ref, HEAD_DIM + AUG, AUG) - pick(sp_ref, HEAD_DIM, 0)
        r = lax.broadcasted_iota(jnp.int32, (ts, ts), 0)
        c = lax.broadcasted_iota(jnp.int32, (ts, ts), 1)
        utri = (c >= r).astype(BF16)
        hi, mid, lo = _split3(dc)
        dlogf = _dot(utri, hi) + _dot(utri, mid) + _dot(utri, lo) + carry[...]
        carry[...] = dlogf[0:1, :]
        f = pf_ref[...] + fb_ref[...]
        lane = lax.broadcasted_iota(jnp.int32, (ts, LANES), 1)
        df = jnp.where(lane < 2 * n_pair, dlogf * _sigmoid(-f), 0.0)
        df_ref[...] = df.astype(BF16)
        dfb_ref[...] += jnp.sum(df, axis=0, keepdims=True)

    rev = lambda blk: (lambda i: (n_t - 1 - i, blk))
    return pl.pallas_call(
        body, grid=(n_t,),
        in_specs=[pl.BlockSpec((ts, W), rev(0)), pl.BlockSpec((ts, W), rev(0)),
                  pl.BlockSpec((ts, LANES), rev(f_blk)), pl.BlockSpec((1, LANES), lambda i: (0, 0))],
        out_specs=[pl.BlockSpec((ts, LANES), rev(0)), pl.BlockSpec((1, LANES), lambda i: (0, 0))],
        out_shape=[jax.ShapeDtypeStruct((S, LANES), BF16), jax.ShapeDtypeStruct((1, LANES), F32)],
        scratch_shapes=[pltpu.VMEM((1, LANES), F32)],
        compiler_params=_params(1), name=name)(sp, rs, proj, fbias)


def _head_operands(h, lane, q2, k2, ka2):
    act = (lane < HEAD_DIM) if h == 0 else (lane >= HEAD_DIM)
    base = HEAD_DIM if h == 0 else 0
    ones = ((lane >= base) & (lane < base + AUG)).astype(F32)
    qa = jnp.where(act, q2 * (1.0 / math.sqrt(HEAD_DIM)), ones).astype(BF16)
    ka = jnp.where(act, k2.astype(BF16), ka2)
    return act, qa, ka


def _attn_fwd(name, proj, ka, q_blk, k_blk, v_blk, n_pair):
    S = proj.shape[0]
    tq = _tile(S, 512)
    n_t = S // tq
    W = LANES * n_pair

    def body(q_ref, k_ref, v_ref, ka_ref, o_ref, o32_ref, lse_ref, m_ref, l_ref, acc_ref):
        i, j = pl.program_id(1), pl.program_id(2)

        @pl.when(j == 0)
        def _():
            m_ref[...] = jnp.full_like(m_ref, NEG_INF)
            l_ref[...] = jnp.zeros_like(l_ref)
            acc_ref[...] = jnp.zeros_like(acc_ref)

        @pl.when(j <= i)
        def _():
            lane = lax.broadcasted_iota(jnp.int32, (tq, LANES), 1)
            row = lax.broadcasted_iota(jnp.int32, (tq, tq), 0)
            col = lax.broadcasted_iota(jnp.int32, (tq, tq), 1)
            visible = (j < i) | (row >= col)
            q2, k2, v2, ka2 = q_ref[...], k_ref[...], v_ref[...], ka_ref[...]
            for h in range(2):
                act, qa, kaug = _head_operands(h, lane, q2, k2, ka2)
                s = jnp.where(visible, _dot_nt(qa, kaug), NEG_INF)
                m_prev = m_ref[h]
                m_new = jnp.maximum(m_prev, jnp.max(s, axis=-1, keepdims=True))
                alpha = jnp.exp(m_prev - m_new)
                p = jnp.exp(s - m_new)
                l_ref[h] = alpha * l_ref[h] + jnp.sum(p, axis=-1, keepdims=True)
                vm = jnp.where(act, v2, 0.0).astype(BF16)
                acc_ref[h] = alpha * acc_ref[h] + _dot(p.astype(BF16), vm)
                m_ref[h] = m_new

        @pl.when(j == i)
        def _():
            lane = lax.broadcasted_iota(jnp.int32, (tq, LANES), 1)
            first = lane < HEAD_DIM
            out = jnp.where(first, acc_ref[0] / l_ref[0], acc_ref[1] / l_ref[1])
            o_ref[...] = out.astype(BF16)
            o32_ref[...] = out
            lse_ref[...] = jnp.where(first, m_ref[0] + jnp.log(l_ref[0]), m_ref[1] + jnp.log(l_ref[1]))

    qspec = lambda blk: pl.BlockSpec((tq, LANES), lambda p, i, j: (i, blk + p))
    kspec = lambda blk: pl.BlockSpec((tq, LANES), lambda p, i, j: (jnp.minimum(j, i), blk + p))
    out = pl.BlockSpec((tq, LANES), lambda p, i, j: (i, p))
    return pl.pallas_call(
        body, grid=(n_pair, n_t, n_t),
        in_specs=[qspec(q_blk), kspec(k_blk), kspec(v_blk), kspec(0)],
        out_specs=[out, out, out],
        out_shape=[jax.ShapeDtypeStruct((S, W), BF16), jax.ShapeDtypeStruct((S, W), F32),
                   jax.ShapeDtypeStruct((S, W), F32)],
        scratch_shapes=[pltpu.VMEM((2, tq, 1), F32), pltpu.VMEM((2, tq, 1), F32),
                        pltpu.VMEM((2, tq, LANES), F32)],
        compiler_params=_params(3), name=name)(proj, proj, proj, ka)


def _attn_bwd(name, proj, ka, o, lse, dmix, q_blk, k_blk, v_blk, do_blk, n_pair):
    S = proj.shape[0]
    tq = _tile(S, 512)
    n_t = S // tq
    W = LANES * n_pair
    scale = 1.0 / math.sqrt(HEAD_DIM)

    def body(q_ref, k_ref, v_ref, ka_ref, o_ref, lse_ref, do_ref,
             dq_ref, dk_ref, dv_ref, sp_ref, rs_ref, dk_acc, dv_acc):
        j, i = pl.program_id(1), pl.program_id(2)

        @pl.when((j == 0) & (i == 0))
        def _():
            dq_ref[...] = jnp.zeros_like(dq_ref)
            rs_ref[...] = jnp.zeros_like(rs_ref)

        @pl.when(i == 0)
        def _():
            dk_acc[...] = jnp.zeros_like(dk_acc)
            dv_acc[...] = jnp.zeros_like(dv_acc)

        @pl.when(i >= j)
        def _():
            lane = lax.broadcasted_iota(jnp.int32, (tq, LANES), 1)
            row = lax.broadcasted_iota(jnp.int32, (tq, tq), 0)
            col = lax.broadcasted_iota(jnp.int32, (tq, tq), 1)
            visible = (j < i) | (row >= col)
            q2, k2, v2, ka2 = q_ref[...], k_ref[...], v_ref[...], ka_ref[...]
            o2, do2, lse2 = o_ref[...], do_ref[...], lse_ref[...]
            dq = []
            for h in range(2):
                act, qa, kaug = _head_operands(h, lane, q2, k2, ka2)
                s = jnp.where(visible, _dot_nt(qa, kaug), NEG_INF)
                p = jnp.exp(s - lse2[:, h * HEAD_DIM:h * HEAD_DIM + 1])
                dom = jnp.where(act, do2, 0.0)
                delta = jnp.sum(dom * o2, axis=-1, keepdims=True)
                dob = dom.astype(BF16)
                dp = _dot_nt(dob, jnp.where(act, v2, 0.0).astype(BF16))
                ds = (p * (dp - delta)).astype(BF16)
                dv_acc[...] += _dot_tn(p.astype(BF16), dob)
                dk_acc[h] += _dot_tn(ds, qa)
                dq.append(_dot(ds, kaug))
            rows = pl.ds(pl.multiple_of(i * tq, tq), tq)
            first = lane < HEAD_DIM
            dq_ref[rows, :] += jnp.where(first, dq[0], dq[1])
            rs_ref[rows, :] += jnp.where(first, dq[1], dq[0])

        @pl.when(i == n_t - 1)
        def _():
            lane = lax.broadcasted_iota(jnp.int32, (tq, LANES), 1)
            first = lane < HEAD_DIM
            dk_ref[...] = jnp.where(first, dk_acc[0], dk_acc[1]).astype(BF16)
            sp_ref[...] = jnp.where(first, dk_acc[1], dk_acc[0])
            dv_ref[...] = dv_acc[...].astype(BF16)

        @pl.when((j == n_t - 1) & (i == n_t - 1))
        def _():
            dq_ref[...] = dq_ref[...] * scale

    qspec = lambda blk: pl.BlockSpec((tq, LANES), lambda p, j, i: (jnp.maximum(i, j), blk + p))
    kspec = lambda blk: pl.BlockSpec((tq, LANES), lambda p, j, i: (j, blk + p))
    kout = pl.BlockSpec((tq, LANES), lambda p, j, i: (j, p))
    qres = pl.BlockSpec((S, LANES), lambda p, j, i: (0, p))
    return pl.pallas_call(
        body, grid=(n_pair, n_t, n_t),
        in_specs=[qspec(q_blk), kspec(k_blk), kspec(v_blk), kspec(0),
                  qspec(0), qspec(0), qspec(do_blk)],
        out_specs=[qres, kout, kout, kout, qres],
        out_shape=[jax.ShapeDtypeStruct((S, W), F32), jax.ShapeDtypeStruct((S, W), BF16),
                   jax.ShapeDtypeStruct((S, W), BF16), jax.ShapeDtypeStruct((S, W), F32),
                   jax.ShapeDtypeStruct((S, W), F32)],
        scratch_shapes=[pltpu.VMEM((2, tq, LANES), F32), pltpu.VMEM((tq, LANES), F32)],
        compiler_params=_params(3), name=name)(proj, proj, proj, ka, o, lse, dmix)


def _device_major(w_cols):
    R, N = w_cols.shape
    return w_cols.reshape(R, N_DEV, N // N_DEV).transpose(1, 0, 2)


def _from_device_major(blocks):
    n, R, c = blocks.shape
    return blocks.transpose(1, 0, 2).reshape(R, n * c)


def _ffn_weight_grads(tag, h, dxo, dG, dU, A):
    nb, S, Fs = dG.shape
    D = h.shape[1]
    tm = _tile(S, 512)
    grid = (nb, S // tm)
    row = pl.BlockSpec((tm, D), lambda j, i: (i, 0))
    act = pl.BlockSpec((1, tm, Fs), lambda j, i: (j, i, 0))
    wcol = pl.BlockSpec((1, D, Fs), lambda j, i: (j, 0, 0))
    wrow = pl.BlockSpec((1, Fs, D), lambda j, i: (j, 0, 0))
    dwg = _mm_tn(tag + "_dwg", h, dG, (nb, D, Fs), row, act, wcol, grid, 1.0, BF16)
    dwu = _mm_tn(tag + "_dwu", h, dU, (nb, D, Fs), row, act, wcol, grid, 1.0, BF16)
    dwd = _mm_tn(tag + "_dwd", A, dxo, (nb, Fs, D), act, row, wrow, grid, 0.5, BF16)
    return dwg, dwu, dwd


def kernel(x, ffn1_norm, ffn1_w_gate, ffn1_w_up, ffn1_w_down, mix_norm, w_in, fgate_bias, conv_w, conv_b, conv_ln_g, conv_ln_b, w_out, ffn2_norm, ffn2_w_gate, ffn2_w_up, ffn2_w_down, final_norm, loss_target, m_ffn1_norm, m_ffn1_w_gate, m_ffn1_w_up, m_ffn1_w_down, m_mix_norm, m_w_in, m_fgate_bias, m_conv_w, m_conv_b, m_conv_ln_g, m_conv_ln_b, m_w_out, m_ffn2_norm, m_ffn2_w_gate, m_ffn2_w_up, m_ffn2_w_down, m_final_norm, v_ffn1_norm, v_ffn1_w_gate, v_ffn1_w_up, v_ffn1_w_down, v_mix_norm, v_w_in, v_fgate_bias, v_conv_w, v_conv_b, v_conv_ln_g, v_conv_ln_b, v_w_out, v_ffn2_norm, v_ffn2_w_gate, v_ffn2_w_up, v_ffn2_w_down, v_final_norm):
    xs = x[0]
    S, D = xs.shape
    C = conv_b.shape[0]
    n_heads = fgate_bias.shape[0]
    FW = n_heads * HEAD_DIM
    n_pair = n_heads // 2
    MIX = C + FW
    in_cols = w_in.shape[1] * N_DEV
    NP = -(-in_cols // 512) * 512
    q_blk, k_blk, v_blk = 2 * C // LANES, (2 * C + FW) // LANES, (2 * C + 2 * FW) // LANES
    f_blk = (2 * C + 3 * FW) // LANES
    assert C % LANES == 0 and FW % LANES == 0 and n_heads % 2 == 0 and n_heads <= LANES
    assert in_cols == 2 * C + 3 * FW + n_heads and MIX == w_out.shape[0] * N_DEV

    vec = lambda a: a.reshape(1, -1)
    bf = lambda a: a.astype(BF16)

    wg1, wu1, wd1 = _exchange("ag_ffn1", [bf(ffn1_w_gate), bf(ffn1_w_up), bf(ffn1_w_down)], False)
    win_g, wout_g, cw_g = _exchange("ag_mix", [bf(w_in), bf(w_out), conv_w], False)
    wg2, wu2, wd2 = _exchange("ag_ffn2", [bf(ffn2_w_gate), bf(ffn2_w_up), bf(ffn2_w_down)], False)
    win = jnp.pad(_from_device_major(win_g), ((0, 0), (0, NP - in_cols)))
    wout = wout_g.reshape(MIX, D)
    cw = jnp.pad(_from_device_major(cw_g), ((0, CONV_HALO - CONV_WIDTH), (0, 0)))
    fbias = jnp.pad(vec(fgate_bias), ((0, 0), (0, LANES - n_heads)))

    x1, h1, G1, U1 = _ffn_fwd("ffn1_fwd", xs, vec(ffn1_norm), wg1, wu1, wd1)
    proj, h2 = _norm_mm("proj_in", x1, vec(mix_norm), win, 512)
    yc, y_conv = _conv_fwd("conv_fwd", proj, C, cw, vec(conv_b), vec(conv_ln_g), vec(conv_ln_b))
    ka = _gate_prep("gate_prep", proj, f_blk, fbias, n_pair)
    o, o32, lse = _attn_fwd("attn_fwd", proj, ka, q_blk, k_blk, v_blk, n_pair)
    mix = jnp.concatenate([y_conv, o], axis=1)
    x2 = _mm_res("proj_out", mix, wout, x1, 512)
    x3, h3, G2, U2 = _ffn_fwd("ffn2_fwd", x2, vec(ffn2_norm), wg2, wu2, wd2)

    dx3, d_final_norm, loss_part = _final("final", x3, vec(final_norm), loss_target[0])
    dG2, dU2, A2, dx2, d_ffn2_norm = _ffn_bwd_act("ffn2_bwd", dx3, x2, vec(ffn2_norm), G2, U2, wg2, wu2, wd2)
    dwg2, dwu2, dwd2 = _ffn_weight_grads("ffn2", h3, dx3, dG2, dU2, A2)

    tm = _tile(S, 512)
    dmix = _mm_nt("dmix", dx2, wout, 512)
    d_wout = _mm_tn("dwout", mix, dx2, (MIX, D),
                    pl.BlockSpec((tm, MIX), lambda j, i: (i, 0)), pl.BlockSpec((tm, 512), lambda j, i: (i, j)),
                    pl.BlockSpec((MIX, 512), lambda j, i: (0, j)), (D // 512, S // tm), 1.0, BF16)
    dag, d_cw, d_cb, d_lg, d_lb = _conv_bwd("conv_bwd", dmix, yc, proj, C, cw, vec(conv_ln_g), vec(conv_ln_b))
    dq, dk, dv, sp, rs = _attn_bwd("attn_bwd", proj, ka, o32, lse, dmix, q_blk, k_blk, v_blk, C // LANES, n_pair)
    df, d_fb = _gate_bwd("gate_bwd", sp, rs, proj, f_blk, fbias, n_pair)
    dproj = jnp.concatenate([dag, bf(dq), dk, dv, df, jnp.zeros((S, NP - f_blk * LANES - LANES), BF16)], axis=1)
    d_win = _mm_tn("dwin", h2, dproj, (D, NP),
                   pl.BlockSpec((tm, D), lambda j, i: (i, 0)), pl.BlockSpec((tm, 512), lambda j, i: (i, j)),
                   pl.BlockSpec((D, 512), lambda j, i: (0, j)), (NP // 512, S // tm), 1.0, BF16)
    dx1, d_mix_norm = _mm_nt("dh2", dproj, win, 512, norm_bwd=(x1, vec(mix_norm), dx2))
    dG1, dU1, A1, dx0, d_ffn1_norm = _ffn_bwd_act("ffn1_bwd", dx1, xs, vec(ffn1_norm), G1, U1, wg1, wu1, wd1)
    dwg1, dwu1, dwd1 = _ffn_weight_grads("ffn1", h1, dx1, dG1, dU1, A1)

    d_win_blocks = _device_major(d_win[:, :in_cols])
    d_wout_blocks = d_wout.reshape(N_DEV, MIX // N_DEV, D)
    r_g1, r_u1, r_d1 = _exchange("a2a_ffn1", [dwg1, dwu1, dwd1], True)
    r_in, r_out = _exchange("a2a_mix", [d_win_blocks, d_wout_blocks], True)
    r_g2, r_u2, r_d2 = _exchange("a2a_ffn2", [dwg2, dwu2, dwd2], True)

    rows = lambda a: a.reshape(-1, C)
    pad_row = lambda a: jnp.pad(a.reshape(1, -1), ((0, 0), (0, C - a.size)))
    pieces = [rows(d_ffn1_norm), rows(d_mix_norm), rows(d_ffn2_norm), rows(d_final_norm),
              d_cw[:CONV_WIDTH], d_cb, d_lg, d_lb, pad_row(d_fb[0, :n_heads]), pad_row(loss_part[0, :1])]
    pack = jnp.concatenate(pieces, axis=0)
    n_rows = pack.shape[0]
    pack = jnp.pad(pack, ((0, -n_rows % 8), (0, 0)))
    (pack_g,) = _exchange("ag_small", [pack], False)
    tot = _reduce_adam("sum_small", pack_g)
    nd = D // C
    g_ffn1_norm, g_mix_norm, g_ffn2_norm, g_final_norm = (tot[k * nd:(k + 1) * nd].reshape(D) for k in range(4))
    r0 = 4 * nd
    me = _lin(_mesh_pos())
    cs = C // N_DEV
    g_conv_w = lax.dynamic_slice(tot[r0:r0 + CONV_WIDTH], (0, me * cs), (CONV_WIDTH, cs))
    g_conv_b, g_ln_g, g_ln_b = tot[r0 + CONV_WIDTH], tot[r0 + CONV_WIDTH + 1], tot[r0 + CONV_WIDTH + 2]
    g_fb = tot[r0 + CONV_WIDTH + 3, :n_heads]
    loss = tot[r0 + CONV_WIDTH + 4, 0]

    small = [(g_ffn1_norm, ffn1_norm, m_ffn1_norm, v_ffn1_norm), (g_mix_norm, mix_norm, m_mix_norm, v_mix_norm),
             (g_fb, fgate_bias, m_fgate_bias, v_fgate_bias), (g_conv_w, conv_w, m_conv_w, v_conv_w),
             (g_conv_b, conv_b, m_conv_b, v_conv_b), (g_ln_g, conv_ln_g, m_conv_ln_g, v_conv_ln_g),
             (g_ln_b, conv_ln_b, m_conv_ln_b, v_conv_ln_b), (g_ffn2_norm, ffn2_norm, m_ffn2_norm, v_ffn2_norm),
             (g_final_norm, final_norm, m_final_norm, v_final_norm)]
    sizes = [g.size for g, _, _, _ in small]
    total = sum(sizes)
    padded = -(-total // (8 * LANES)) * (8 * LANES)

    def flat_pack(k):
        flat = jnp.concatenate([t[k].reshape(-1) for t in small])
        return jnp.pad(flat, (0, padded - total)).reshape(1, padded // LANES, LANES)

    v_flat = jnp.concatenate([t[3].reshape(-1) for t in small])
    v_pack = jnp.pad(v_flat, (0, padded - total), constant_values=1.0).reshape(padded // LANES, LANES)
    sg, sd, sm, sv = _reduce_adam("adam_small", flat_pack(0), flat_pack(1)[0], flat_pack(2)[0], v_pack)

    def unpack(packed):
        flat = packed.reshape(-1)
        out, off = [], 0
        for (g, _, _, _), n in zip(small, sizes):
            out.append(flat[off:off + n].reshape(g.shape))
            off += n
        return out

    s_g, s_d, s_m, s_v = unpack(sg), unpack(sd), unpack(sm), unpack(sv)

    big = {
        "ffn1_w_gate": _reduce_adam("adam_ffn1_wg", r_g1, ffn1_w_gate, m_ffn1_w_gate, v_ffn1_w_gate),
        "ffn1_w_up": _reduce_adam("adam_ffn1_wu", r_u1, ffn1_w_up, m_ffn1_w_up, v_ffn1_w_up),
        "ffn1_w_down": _reduce_adam("adam_ffn1_wd", r_d1, ffn1_w_down, m_ffn1_w_down, v_ffn1_w_down),
        "w_in": _reduce_adam("adam_w_in", r_in, w_in, m_w_in, v_w_in),
        "w_out": _reduce_adam("adam_w_out", r_out, w_out, m_w_out, v_w_out),
        "ffn2_w_gate": _reduce_adam("adam_ffn2_wg", r_g2, ffn2_w_gate, m_ffn2_w_gate, v_ffn2_w_gate),
        "ffn2_w_up": _reduce_adam("adam_ffn2_wu", r_u2, ffn2_w_up, m_ffn2_w_up, v_ffn2_w_up),
        "ffn2_w_down": _reduce_adam("adam_ffn2_wd", r_d2, ffn2_w_down, m_ffn2_w_down, v_ffn2_w_down),
    }
    small_names = ["ffn1_norm", "mix_norm", "fgate_bias", "conv_w", "conv_b", "conv_ln_g", "conv_ln_b",
                   "ffn2_norm", "final_norm"]
    res = dict(big)
    for k, n in enumerate(small_names):
        res[n] = (s_g[k], s_d[k], s_m[k], s_v[k])
    order = ["ffn1_norm", "ffn1_w_gate", "ffn1_w_up", "ffn1_w_down", "mix_norm", "w_in", "fgate_bias",
             "conv_w", "conv_b", "conv_ln_g", "conv_ln_b", "w_out", "ffn2_norm", "ffn2_w_gate", "ffn2_w_up",
             "ffn2_w_down", "final_norm"]
    outs = [loss, dx0[None]]
    for k in range(4):
        outs += [res[n][k] for n in order]
    return tuple(outs)
```

```python
import math

import jax
import jax.numpy as jnp
from jax import lax
from jax.experimental import pallas as pl
from jax.experimental.pallas import tpu as pltpu

F32 = jnp.float32
BF16 = jnp.bfloat16

N_DEV = 8
MESH_ID = pl.DeviceIdType.MESH
HEAD_DIM = 64
CONV_WIDTH = 31
CONV_HALO = 32
NORM_EPS = 1e-6
LN_EPS = 1e-5
NEG_INF = -1e30
LANES = 128
SUBLANES = 8
V7X_VMEM_LIMIT = 52 * 1024 * 1024

ADAM_LR = 0.001
ADAM_B1 = 0.9
ADAM_B2 = 0.999
ADAM_EPS = 1e-08
ADAM_WD = 0.01
ADAM_STEP = 10


def _params(n_grid_axes):
    return pltpu.CompilerParams(dimension_semantics=("arbitrary",) * n_grid_axes,
                                vmem_limit_bytes=V7X_VMEM_LIMIT)


def _tile(n, pref, mult=8):
    t = min(pref, n)
    while t >= mult:
        if n % t == 0 and t % mult == 0:
            return t
        t -= mult
    return n


def _dot(a, b):
    return jnp.dot(a, b, preferred_element_type=F32)


def _dot_nt(a, b):
    return lax.dot_general(a, b, (((1,), (1,)), ((), ())), preferred_element_type=F32)


def _dot_tn(a, b):
    return lax.dot_general(a, b, (((0,), (0,)), ((), ())), preferred_element_type=F32)


def _sigmoid(x):
    return 1.0 / (1.0 + jnp.exp(-x))


def _rms_fwd(x, g):
    r = lax.rsqrt(jnp.mean(x * x, axis=-1, keepdims=True) + NORM_EPS)
    return x * r * g


def _rms_bwd(dh, x, g):
    r = lax.rsqrt(jnp.mean(x * x, axis=-1, keepdims=True) + NORM_EPS)
    xh = x * r
    dxh = dh * g
    dx = r * (dxh - xh * jnp.mean(dxh * xh, axis=-1, keepdims=True))
    return dx, jnp.sum(dh * xh, axis=0, keepdims=True)


def _split3(x):
    hi = x.astype(BF16)
    r = x - hi.astype(F32)
    mid = r.astype(BF16)
    lo = (r - mid.astype(F32)).astype(BF16)
    return hi, mid, lo


def _blk(ref):
    return ref[0] if len(ref.shape) == 3 else ref[...]


_DEP_SPEC = pl.BlockSpec(memory_space=pl.ANY)


def _mesh_pos():
    return lax.axis_index("x"), lax.axis_index("y"), lax.axis_index("c")


def _peer(pos, k):
    x, y, c = pos
    return (1 - x if k & 4 else x, 1 - y if k & 2 else y, 1 - c if k & 1 else c)


def _lin(pos):
    x, y, c = pos
    return 4 * x + 2 * y + c


def _remote_copy(src, land, send_sems, recv_sems, pos, k, all_to_all):
    peer = _peer(pos, k)
    return pltpu.make_async_remote_copy(
        src_ref=src.at[_lin(peer)] if all_to_all else src, dst_ref=land.at[_lin(pos)],
        send_sem=send_sems.at[k - 1], recv_sem=recv_sems.at[k - 1],
        device_id=peer, device_id_type=MESH_ID)


def _landing_shape(a, all_to_all):
    return a.shape if all_to_all else (N_DEV,) + a.shape


def _exchange(name, arrays, all_to_all):
    n = len(arrays)
    out_shapes = [jax.ShapeDtypeStruct(_landing_shape(a, all_to_all), a.dtype) for a in arrays]

    def body(*refs):
        ins, outs = refs[:n], refs[n:2 * n]
        send_sems, recv_sems, local_sems = refs[2 * n:]
        pos = _mesh_pos()
        me = _lin(pos)
        local = []
        for a in range(n):
            src = ins[a].at[me] if all_to_all else ins[a]
            cp = pltpu.make_async_copy(src, outs[a].at[me], local_sems.at[a])
            cp.start()
            local.append(cp)
        remote = [_remote_copy(ins[a], outs[a], send_sems.at[a], recv_sems.at[a], pos, k, all_to_all)
                  for a in range(n) for k in range(1, N_DEV)]
        for cp in remote:
            cp.start()
        for cp in remote:
            cp.wait()
        for cp in local:
            cp.wait()

    any_spec = pl.BlockSpec(memory_space=pl.ANY)
    return pl.pallas_call(
        body, out_shape=out_shapes, in_specs=[any_spec] * n, out_specs=[any_spec] * n,
        scratch_shapes=[pltpu.SemaphoreType.DMA((n, N_DEV - 1)),
                        pltpu.SemaphoreType.DMA((n, N_DEV - 1)),
                        pltpu.SemaphoreType.DMA((n,))],
        name=name)(*arrays)


_HBM_SPEC = pl.BlockSpec(memory_space=pltpu.HBM)
_SEM_SPEC = pl.BlockSpec(memory_space=pltpu.SEMAPHORE)
_SIDE_EFFECT = pltpu.SideEffectType.DATAFLOW_SIDE_EFFECTING


def _xchg_start(name, arrays, all_to_all):
    n = len(arrays)
    lands = [lax.empty(_landing_shape(a, all_to_all), a.dtype) for a in arrays]

    def body(*refs):
        srcs, lnds = refs[:n], refs[n:2 * n]
        outs = refs[2 * n:]
        send, recv = outs[:n], outs[n:2 * n]
        token, local_sems = outs[4 * n], outs[4 * n + 1]
        pos = _mesh_pos()
        me = _lin(pos)
        local = []
        for a in range(n):
            src = srcs[a].at[me] if all_to_all else srcs[a]
            cp = pltpu.make_async_copy(src, lnds[a].at[me], local_sems.at[a])
            cp.start()
            local.append(cp)
        for a in range(n):
            for k in range(1, N_DEV):
                _remote_copy(srcs[a], lnds[a], send[a], recv[a], pos, k, all_to_all).start()
        for cp in local:
            cp.wait()
        token[...] = jnp.zeros_like(token)

    hbm = lambda a: pltpu.HBM(a.shape, a.dtype)
    sems = [pltpu.SemaphoreType.DMA((N_DEV - 1,))] * (2 * n)
    res = pl.pallas_call(
        body, name=name,
        out_shape=sems + [hbm(a) for a in arrays] + [hbm(l) for l in lands]
        + [jax.ShapeDtypeStruct((SUBLANES, LANES), F32)],
        in_specs=[_HBM_SPEC] * (2 * n),
        out_specs=[_SEM_SPEC] * (2 * n) + [_HBM_SPEC] * (2 * n) + [pl.BlockSpec(memory_space=pltpu.VMEM)],
        input_output_aliases={a: 2 * n + a for a in range(2 * n)},
        scratch_shapes=[pltpu.SemaphoreType.DMA((n,))],
        compiler_params=pltpu.CompilerParams(has_side_effects=_SIDE_EFFECT),
    )(*[pltpu.with_memory_space_constraint(a, pltpu.HBM) for a in list(arrays) + lands])
    per_array = [(res[a], res[n + a], res[2 * n + a], res[3 * n + a]) for a in range(n)]
    return per_array, res[4 * n]


def _xchg_wait(name, started, all_to_all, after):
    n = len(started)

    def body(*refs):
        srcs, lnds, send, recv = refs[:n], refs[n:2 * n], refs[2 * n:3 * n], refs[3 * n:4 * n]
        pos = _mesh_pos()
        for a in range(n):
            for k in range(1, N_DEV):
                cp = _remote_copy(srcs[a], lnds[a], send[a], recv[a], pos, k, all_to_all)
                cp.wait_send()
                cp.wait_recv()

    hbm = lambda a: pltpu.HBM(a.shape, a.dtype)
    srcs = [s[2] for s in started]
    lands = [s[3] for s in started]
    res = pl.pallas_call(
        body, name=name,
        out_shape=[hbm(a) for a in srcs + lands],
        in_specs=[_HBM_SPEC] * (2 * n) + [_SEM_SPEC] * (2 * n) + [_DEP_SPEC],
        out_specs=[_HBM_SPEC] * (2 * n),
        input_output_aliases={a: a for a in range(2 * n)},
        compiler_params=pltpu.CompilerParams(has_side_effects=_SIDE_EFFECT),
    )(*srcs, *lands, *[s[0] for s in started], *[s[1] for s in started], after)
    return list(res[n:])


def _reduce_adam(name, parts, w=None, m=None, v=None):
    n, R, C = parts.shape
    tr = _tile(R, 256)
    do_adam = w is not None
    bc1 = 1.0 - ADAM_B1 ** ADAM_STEP
    bc2 = 1.0 - ADAM_B2 ** ADAM_STEP

    def body(*refs):
        p_ref = refs[0]
        g = p_ref[0].astype(F32)
        for d in range(1, n):
            g = g + p_ref[d].astype(F32)
        if not do_adam:
            refs[1][...] = g
            return
        w_ref, m_ref, v_ref, g_ref, d_ref, nm_ref, nv_ref = refs[1:]
        g_ref[...] = g
        nm = ADAM_B1 * m_ref[...] + (1.0 - ADAM_B1) * g
        nv = ADAM_B2 * v_ref[...] + (1.0 - ADAM_B2) * (g * g)
        m_hat = nm / bc1
        v_hat = nv / bc2
        d_ref[...] = -ADAM_LR * (m_hat / (jnp.sqrt(v_hat) + ADAM_EPS) + ADAM_WD * w_ref[...])
        nm_ref[...] = nm
        nv_ref[...] = nv

    tc = _tile(C, 512, LANES) if tr == R and R > 256 else C
    row = pl.BlockSpec((tr, tc), lambda i, j: (i, j))
    part = pl.BlockSpec((n, tr, tc), lambda i, j: (0, i, j))
    shard = jax.ShapeDtypeStruct((R, C), F32)
    grid = (R // tr, C // tc)
    if do_adam:
        return pl.pallas_call(body, grid=grid, in_specs=[part, row, row, row],
                              out_specs=[row] * 4, out_shape=[shard] * 4,
                              compiler_params=_params(2), name=name)(parts, w, m, v)
    return pl.pallas_call(body, grid=grid, in_specs=[part], out_specs=row, out_shape=shard,
                          compiler_params=_params(2), name=name)(parts)


def _ffn_fwd(name, x, gain, wgt, wut, wd):
    S, D = x.shape
    nb, Fs, _ = wgt.shape
    tm = _tile(S, 512)

    def body(x_ref, g_ref, wg_ref, wu_ref, wd_ref, xo_ref, h_ref, G_ref, U_ref, acc_ref):
        j = pl.program_id(1)

        @pl.when(j == 0)
        def _():
            h_ref[...] = _rms_fwd(x_ref[...], g_ref[...]).astype(BF16)
            acc_ref[...] = jnp.zeros_like(acc_ref)

        h = h_ref[...]
        G = _dot_nt(h, wg_ref[0])
        U = _dot_nt(h, wu_ref[0])
        G_ref[0] = G
        U_ref[0] = U
        a = G * _sigmoid(G) * U
        acc_ref[...] += _dot(a.astype(BF16), wd_ref[0])

        @pl.when(j == nb - 1)
        def _():
            xo_ref[...] = x_ref[...] + 0.5 * acc_ref[...]

    row = pl.BlockSpec((tm, D), lambda i, j: (i, 0))
    act = pl.BlockSpec((1, tm, Fs), lambda i, j: (j, i, 0))
    wblk = pl.BlockSpec((1, Fs, D), lambda i, j: (j, 0, 0))
    return pl.pallas_call(
        body, grid=(S // tm, nb),
        in_specs=[row, pl.BlockSpec((1, D), lambda i, j: (0, 0)), wblk, wblk, wblk],
        out_specs=[row, row, act, act],
        out_shape=[jax.ShapeDtypeStruct((S, D), F32), jax.ShapeDtypeStruct((S, D), BF16),
                   jax.ShapeDtypeStruct((nb, S, Fs), F32), jax.ShapeDtypeStruct((nb, S, Fs), F32)],
        scratch_shapes=[pltpu.VMEM((tm, D), F32)],
        compiler_params=_params(2), name=name)(x, gain, wgt, wut, wd)


def _ffn_bwd_act(name, dxo, x_in, gain, G, U, wgt, wut, wd):
    S, D = x_in.shape
    nb, Fs, _ = wgt.shape
    tm = _tile(S, 256)

    def body(dxo_ref, x_ref, g_ref, G_ref, U_ref, wg_ref, wu_ref, wd_ref,
             dG_ref, dU_ref, A_ref, dx_ref, dgain_ref, dxb_ref, acc_ref):
        i, j = pl.program_id(0), pl.program_id(1)

        @pl.when(j == 0)
        def _():
            dxb_ref[...] = dxo_ref[...].astype(BF16)
            acc_ref[...] = jnp.zeros_like(acc_ref)

        @pl.when((i == 0) & (j == 0))
        def _():
            dgain_ref[...] = jnp.zeros_like(dgain_ref)

        dA = 0.5 * _dot_nt(dxb_ref[...], wd_ref[0])
        Gv = G_ref[0]
        Uv = U_ref[0]
        sg = _sigmoid(Gv)
        sl = Gv * sg
        dG = (dA * Uv * (sg * (1.0 + Gv * (1.0 - sg)))).astype(BF16)
        dU = (dA * sl).astype(BF16)
        dG_ref[0] = dG
        dU_ref[0] = dU
        A_ref[0] = (sl * Uv).astype(BF16)
        acc_ref[...] += _dot(dG, wg_ref[0]) + _dot(dU, wu_ref[0])

        @pl.when(j == nb - 1)
        def _():
            dx, dg = _rms_bwd(acc_ref[...], x_ref[...], g_ref[...])
            dx_ref[...] = dxo_ref[...] + dx
            dgain_ref[...] += dg

    row = pl.BlockSpec((tm, D), lambda i, j: (i, 0))
    vec = pl.BlockSpec((1, D), lambda i, j: (0, 0))
    act = pl.BlockSpec((1, tm, Fs), lambda i, j: (j, i, 0))
    wblk = pl.BlockSpec((1, Fs, D), lambda i, j: (j, 0, 0))
    act_shape = jax.ShapeDtypeStruct((nb, S, Fs), BF16)
    return pl.pallas_call(
        body, grid=(S // tm, nb),
        in_specs=[row, row, vec, act, act, wblk, wblk, wblk],
        out_specs=[act, act, act, row, vec],
        out_shape=[act_shape, act_shape, act_shape, jax.ShapeDtypeStruct((S, D), F32),
                   jax.ShapeDtypeStruct((1, D), F32)],
        scratch_shapes=[pltpu.VMEM((tm, D), BF16), pltpu.VMEM((tm, D), F32)],
        compiler_params=_params(2), name=name)(dxo, x_in, gain, G, U, wgt, wut, wd)


def _mm_tn(name, lhs, rhs, out_shape, lhs_spec, rhs_spec, out_spec, grid, scale, out_dtype, dep=None):
    acc_shape = tuple(out_spec.block_shape[-2:])
    n_red = grid[-1]

    def body(l_ref, r_ref, *rest):
        o_ref, acc_ref = rest[-2:]
        i = pl.program_id(len(grid) - 1)

        @pl.when(i == 0)
        def _():
            acc_ref[...] = jnp.zeros_like(acc_ref)

        acc_ref[...] += _dot_tn(_blk(l_ref).astype(BF16), _blk(r_ref).astype(BF16))

        @pl.when(i == n_red - 1)
        def _():
            res = (scale * acc_ref[...]).astype(out_dtype)
            if len(o_ref.shape) == 3:
                o_ref[0] = res
            else:
                o_ref[...] = res

    deps = [] if dep is None else [dep]
    return pl.pallas_call(
        body, grid=grid, in_specs=[lhs_spec, rhs_spec] + [_DEP_SPEC] * len(deps), out_specs=out_spec,
        out_shape=jax.ShapeDtypeStruct(out_shape, out_dtype),
        scratch_shapes=[pltpu.VMEM(acc_shape, F32)],
        compiler_params=_params(len(grid)), name=name)(lhs, rhs, *deps)


def _norm_mm_nt(name, x, gain, wt, tn):
    S, D = x.shape
    N = wt.shape[0]
    tm = _tile(S, 512)

    def body(x_ref, g_ref, w_ref, o_ref, h_ref):
        @pl.when(pl.program_id(1) == 0)
        def _():
            h_ref[...] = _rms_fwd(x_ref[...], g_ref[...]).astype(BF16)

        o_ref[...] = _dot_nt(h_ref[...], w_ref[...])

    row = pl.BlockSpec((tm, D), lambda i, j: (i, 0))
    return pl.pallas_call(
        body, grid=(S // tm, N // tn),
        in_specs=[row, pl.BlockSpec((1, D), lambda i, j: (0, 0)),
                  pl.BlockSpec((tn, D), lambda i, j: (j, 0))],
        out_specs=[pl.BlockSpec((tm, tn), lambda i, j: (i, j)), row],
        out_shape=[jax.ShapeDtypeStruct((S, N), F32), jax.ShapeDtypeStruct((S, D), BF16)],
        compiler_params=_params(2), name=name)(x, gain, wt)


def _mm_res(name, a, w, res, tn):
    S, K = a.shape
    N = w.shape[1]
    tm = _tile(S, 512)

    def body(a_ref, w_ref, r_ref, o_ref):
        o_ref[...] = r_ref[...] + _dot(a_ref[...], w_ref[...])

    tile = pl.BlockSpec((tm, tn), lambda i, j: (i, j))
    return pl.pallas_call(
        body, grid=(S // tm, N // tn),
        in_specs=[pl.BlockSpec((tm, K), lambda i, j: (i, 0)),
                  pl.BlockSpec((K, tn), lambda i, j: (0, j)), tile],
        out_specs=tile, out_shape=jax.ShapeDtypeStruct((S, N), F32),
        compiler_params=_params(2), name=name)(a, w, res)


def _mm_k(name, a, b, tk, transpose_b, norm_bwd=None, dep=None):
    S, K = a.shape
    N = b.shape[0] if transpose_b else b.shape[1]
    tm = _tile(S, 512)
    nk = K // tk
    n_extra = 0 if norm_bwd is None else 3
    deps = [] if dep is None else [dep]

    def body(*refs):
        a_ref, b_ref = refs[:2]
        outs = refs[2 + n_extra + len(deps):]
        acc_ref = outs[-1]
        i, k = pl.program_id(0), pl.program_id(1)

        @pl.when(k == 0)
        def _():
            acc_ref[...] = jnp.zeros_like(acc_ref)

        av = a_ref[...].astype(BF16)
        acc_ref[...] += _dot_nt(av, b_ref[...]) if transpose_b else _dot(av, b_ref[...])

        if norm_bwd is None:
            @pl.when(k == nk - 1)
            def _():
                outs[0][...] = acc_ref[...]
        else:
            x_ref, g_ref, dres_ref = refs[2:5]
            o_ref, dgain_ref = outs[:2]

            @pl.when((i == 0) & (k == 0))
            def _():
                dgain_ref[...] = jnp.zeros_like(dgain_ref)

            @pl.when(k == nk - 1)
            def _():
                dx, dg = _rms_bwd(acc_ref[...], x_ref[...], g_ref[...])
                o_ref[...] = dres_ref[...] + dx
                dgain_ref[...] += dg

    a_spec = pl.BlockSpec((tm, tk), lambda i, k: (i, k))
    b_spec = (pl.BlockSpec((N, tk), lambda i, k: (0, k)) if transpose_b
              else pl.BlockSpec((tk, N), lambda i, k: (k, 0)))
    row = pl.BlockSpec((tm, N), lambda i, k: (i, 0))
    vec = pl.BlockSpec((1, N), lambda i, k: (0, 0))
    out = jax.ShapeDtypeStruct((S, N), F32)
    scratch = [pltpu.VMEM((tm, N), F32)]
    dep_specs = [_DEP_SPEC] * len(deps)
    if norm_bwd is None:
        return pl.pallas_call(body, grid=(S // tm, nk), in_specs=[a_spec, b_spec] + dep_specs,
                              out_specs=row, out_shape=out, scratch_shapes=scratch,
                              compiler_params=_params(2), name=name)(a, b, *deps)
    x_in, gain, dres = norm_bwd
    return pl.pallas_call(body, grid=(S // tm, nk), in_specs=[a_spec, b_spec, row, vec, row] + dep_specs,
                          out_specs=[row, vec],
                          out_shape=[out, jax.ShapeDtypeStruct((1, N), F32)],
                          scratch_shapes=scratch,
                          compiler_params=_params(2), name=name)(a, b, x_in, gain, dres, *deps)


def _final(name, x, gain, target):
    S, D = x.shape
    tm = _tile(S, 512)

    def body(x_ref, g_ref, t_ref, dx_ref, dgain_ref, loss_ref):
        @pl.when(pl.program_id(0) == 0)
        def _():
            dgain_ref[...] = jnp.zeros_like(dgain_ref)
            loss_ref[...] = jnp.zeros_like(loss_ref)

        xv = x_ref[...]
        err = _rms_fwd(xv, g_ref[...]) - t_ref[...]
        per_tok = jnp.mean(err * err, axis=-1, keepdims=True)
        loss_ref[...] += 0.5 * jnp.sum(per_tok, axis=0, keepdims=True)
        dx, dg = _rms_bwd(err * (1.0 / D), xv, g_ref[...])
        dx_ref[...] = dx
        dgain_ref[...] += dg

    row = pl.BlockSpec((tm, D), lambda i: (i, 0))
    vec = pl.BlockSpec((1, D), lambda i: (0, 0))
    return pl.pallas_call(
        body, grid=(S // tm,), in_specs=[row, vec, row],
        out_specs=[row, vec, pl.BlockSpec((1, LANES), lambda i: (0, 0))],
        out_shape=[jax.ShapeDtypeStruct((S, D), F32), jax.ShapeDtypeStruct((1, D), F32),
                   jax.ShapeDtypeStruct((1, LANES), F32)],
        compiler_params=_params(1), name=name)(x, gain, target)


def _conv_tiles(S):
    ts = _tile(S, 256, CONV_HALO)
    return ts, ts // CONV_HALO


def _ln_stats(yc):
    mu = jnp.mean(yc, axis=-1, keepdims=True)
    d = yc - mu
    rs = lax.rsqrt(jnp.mean(d * d, axis=-1, keepdims=True) + LN_EPS)
    return d * rs, rs


def _conv_fwd(name, proj, C, cw, cb, lg, lb):
    S = proj.shape[0]
    ts, hb = _conv_tiles(S)

    def body(a_ref, g_ref, ah_ref, gh_ref, cw_ref, cb_ref, lg_ref, lb_ref, yc_ref, y_ref, ubuf):
        i = pl.program_id(0)
        uh = ah_ref[...] * _sigmoid(gh_ref[...])
        ubuf[pl.ds(0, CONV_HALO), :] = jnp.where(i > 0, uh, 0.0)
        ubuf[pl.ds(CONV_HALO, ts), :] = a_ref[...] * _sigmoid(g_ref[...])
        acc = jnp.zeros((ts, C), F32)
        for k in range(CONV_WIDTH):
            acc = acc + cw_ref[pl.ds(k, 1), :] * ubuf[pl.ds(k + CONV_HALO - CONV_WIDTH + 1, ts), :]
        yc = acc + cb_ref[...]
        yc_ref[...] = yc
        yn, _ = _ln_stats(yc)
        z = yn * lg_ref[...] + lb_ref[...]
        y_ref[...] = (z * _sigmoid(z)).astype(BF16)

    main = lambda col: pl.BlockSpec((ts, C), lambda i: (i, col))
    halo = lambda col: pl.BlockSpec((CONV_HALO, C), lambda i: (jnp.maximum(i * hb - 1, 0), col))
    vec = pl.BlockSpec((1, C), lambda i: (0, 0))
    return pl.pallas_call(
        body, grid=(S // ts,),
        in_specs=[main(0), main(1), halo(0), halo(1),
                  pl.BlockSpec((CONV_HALO, C), lambda i: (0, 0)), vec, vec, vec],
        out_specs=[pl.BlockSpec((ts, C), lambda i: (i, 0))] * 2,
        out_shape=[jax.ShapeDtypeStruct((S, C), F32), jax.ShapeDtypeStruct((S, C), BF16)],
        scratch_shapes=[pltpu.VMEM((ts + CONV_HALO, C), F32)],
        compiler_params=_params(1), name=name)(proj, proj, proj, proj, cw, cb, lg, lb)


def _conv_bwd(name, dmix, yc, proj, C, cw, lg, lb):
    S = proj.shape[0]
    ts, hb = _conv_tiles(S)
    n_t = S // ts

    def body(dy_ref, yc_ref, dyh_ref, ych_ref, a_ref, g_ref, ah_ref, gh_ref, cw_ref, lg_ref, lb_ref,
             dag_ref, dcw_ref, dcb_ref, dlg_ref, dlb_ref, ubuf, dbuf):
        i = pl.program_id(0)

        @pl.when(i == 0)
        def _():
            dcw_ref[...] = jnp.zeros_like(dcw_ref)
            dcb_ref[...] = jnp.zeros_like(dcb_ref)
            dlg_ref[...] = jnp.zeros_like(dlg_ref)
            dlb_ref[...] = jnp.zeros_like(dlb_ref)

        def ln_bwd(dy, ycv):
            yn, rs = _ln_stats(ycv)
            z = yn * lg_ref[...] + lb_ref[...]
            sg = _sigmoid(z)
            dz = dy * (sg * (1.0 + z * (1.0 - sg)))
            dyn = dz * lg_ref[...]
            dyc = rs * (dyn - jnp.mean(dyn, axis=-1, keepdims=True)
                        - yn * jnp.mean(dyn * yn, axis=-1, keepdims=True))
            return dyc, dz, yn

        dyc, dz, yn = ln_bwd(dy_ref[...], yc_ref[...])
        dlg_ref[...] += jnp.sum(dz * yn, axis=0, keepdims=True)
        dlb_ref[...] += jnp.sum(dz, axis=0, keepdims=True)
        dcb_ref[...] += jnp.sum(dyc, axis=0, keepdims=True)
        dych, _, _ = ln_bwd(dyh_ref[...], ych_ref[...])
        dbuf[pl.ds(0, ts), :] = dyc
        dbuf[pl.ds(ts, CONV_HALO), :] = jnp.where(i < n_t - 1, dych, 0.0)

        av = a_ref[...]
        sgm = _sigmoid(g_ref[...])
        uh = ah_ref[...] * _sigmoid(gh_ref[...])
        ubuf[pl.ds(0, CONV_HALO), :] = jnp.where(i > 0, uh, 0.0)
        ubuf[pl.ds(CONV_HALO, ts), :] = av * sgm

        du = jnp.zeros((ts, C), F32)
        for k in range(CONV_WIDTH):
            du = du + cw_ref[pl.ds(k, 1), :] * dbuf[pl.ds(CONV_WIDTH - 1 - k, ts), :]
            tap = ubuf[pl.ds(k + CONV_HALO - CONV_WIDTH + 1, ts), :]
            dcw_ref[pl.ds(k, 1), :] += jnp.sum(dyc * tap, axis=0, keepdims=True)
        dag_ref[:, pl.ds(0, C)] = (du * sgm).astype(BF16)
        dag_ref[:, pl.ds(C, C)] = (du * av * sgm * (1.0 - sgm)).astype(BF16)

    main = lambda col: pl.BlockSpec((ts, C), lambda i: (i, col))
    past = lambda col: pl.BlockSpec((CONV_HALO, C), lambda i: (jnp.maximum(i * hb - 1, 0), col))
    nxt = pl.BlockSpec((CONV_HALO, C), lambda i: (jnp.minimum((i + 1) * hb, n_t * hb - 1), 0))
    vec = pl.BlockSpec((1, C), lambda i: (0, 0))
    full = pl.BlockSpec((CONV_HALO, C), lambda i: (0, 0))
    vshape = jax.ShapeDtypeStruct((1, C), F32)
    return pl.pallas_call(
        body, grid=(n_t,),
        in_specs=[main(0), main(0), nxt, nxt, main(0), main(1), past(0), past(1), full, vec, vec],
        out_specs=[pl.BlockSpec((ts, 2 * C), lambda i: (i, 0)), full, vec, vec, vec],
        out_shape=[jax.ShapeDtypeStruct((S, 2 * C), BF16),
                   jax.ShapeDtypeStruct((CONV_HALO, C), F32), vshape, vshape, vshape],
        scratch_shapes=[pltpu.VMEM((ts + CONV_HALO, C), F32), pltpu.VMEM((ts + CONV_HALO, C), F32)],
        compiler_params=_params(1),
        name=name)(dmix, yc, dmix, yc, proj, proj, proj, proj, cw, lg, lb)


AUG = 3


def _gate_prep(name, proj, f_blk, fbias, n_pair):
    S = proj.shape[0]
    ts = _tile(S, 512)
    W = LANES * n_pair

    def body(pf_ref, fb_ref, ka_ref, carry):
        @pl.when(pl.program_id(0) == 0)
        def _():
            carry[...] = jnp.zeros_like(carry)

        f = pf_ref[...] + fb_ref[...]
        logf = jnp.minimum(f, 0.0) - jnp.log(1.0 + jnp.exp(-jnp.abs(f)))
        r = lax.broadcasted_iota(jnp.int32, (ts, ts), 0)
        c = lax.broadcasted_iota(jnp.int32, (ts, ts), 1)
        ltri = (c <= r).astype(BF16)
        hi, mid, lo = _split3(logf)
        cs = _dot(ltri, hi) + _dot(ltri, mid) + _dot(ltri, lo) + carry[...]
        carry[...] = cs[ts - 1:ts, :]
        hh = lax.broadcasted_iota(jnp.int32, (LANES, W), 0)
        ll = lax.broadcasted_iota(jnp.int32, (LANES, W), 1)
        pair, w = ll >> 7, ll & (LANES - 1)
        ka = jnp.zeros((ts, W), F32)
        for p, piece in enumerate(_split3(-cs)):
            e = (((w == HEAD_DIM + p) & (hh == 2 * pair)) | ((w == p) & (hh == 2 * pair + 1)))
            ka = ka + _dot(piece, e.astype(BF16))
        lw = lax.broadcasted_iota(jnp.int32, (1, W), 1) & (LANES - 1)
        ka = ka + ((lw == HEAD_DIM + AUG) | (lw == AUG)).astype(F32)
        ka_ref[...] = ka.astype(BF16)

    return pl.pallas_call(
        body, grid=(S // ts,),
        in_specs=[pl.BlockSpec((ts, LANES), lambda i: (i, f_blk)),
                  pl.BlockSpec((1, LANES), lambda i: (0, 0))],
        out_specs=pl.BlockSpec((ts, W), lambda i: (i, 0)),
        out_shape=jax.ShapeDtypeStruct((S, W), BF16),
        scratch_shapes=[pltpu.VMEM((1, LANES), F32)],
        compiler_params=_params(1), name=name)(proj, fbias)


def _gate_bwd(name, sp, rs, proj, f_blk, fbias, n_pair):
    S = proj.shape[0]
    ts = _tile(S, 512)
    n_t = S // ts
    W = LANES * n_pair

    def body(sp_ref, rs_ref, pf_ref, fb_ref, df_ref, dfb_ref, carry):
        @pl.when(pl.program_id(0) == 0)
        def _():
            carry[...] = jnp.zeros_like(carry)
            dfb_ref[...] = jnp.zeros_like(dfb_ref)

        ll = lax.broadcasted_iota(jnp.int32, (W, LANES), 0)
        hh = lax.broadcasted_iota(jnp.int32, (W, LANES), 1)
        pair, w = ll >> 7, ll & (LANES - 1)
        first, second = hh == 2 * pair, hh == 2 * pair + 1

        def pick(ref, lane_first, lane_second):
            sel = (((w == lane_first) & first) | ((w == lane_second) & second)).astype(BF16)
            hi, mid, lo = _split3(ref[...])
            return _dot(hi, sel) + _dot(mid, sel) + _dot(lo, sel)

        dc = pick(rs_ref, HEAD_DIM + AUG, AUG) - pick(sp_ref, HEAD_DIM, 0)
        r = lax.broadcasted_iota(jnp.int32, (ts, ts), 0)
        c = lax.broadcasted_iota(jnp.int32, (ts, ts), 1)
        utri = (c >= r).astype(BF16)
        hi, mid, lo = _split3(dc)
        dlogf = _dot(utri, hi) + _dot(utri, mid) + _dot(utri, lo) + carry[...]
        carry[...] = dlogf[0:1, :]
        f = pf_ref[...] + fb_ref[...]
        lane = lax.broadcasted_iota(jnp.int32, (ts, LANES), 1)
        df = jnp.where(lane < 2 * n_pair, dlogf * _sigmoid(-f), 0.0)
        df_ref[...] = df.astype(BF16)
        dfb_ref[...] += jnp.sum(df, axis=0, keepdims=True)

    rev = lambda blk: (lambda i: (n_t - 1 - i, blk))
    return pl.pallas_call(
        body, grid=(n_t,),
        in_specs=[pl.BlockSpec((ts, W), rev(0)), pl.BlockSpec((ts, W), rev(0)),
                  pl.BlockSpec((ts, LANES), rev(f_blk)), pl.BlockSpec((1, LANES), lambda i: (0, 0))],
        out_specs=[pl.BlockSpec((ts, LANES), rev(0)), pl.BlockSpec((1, LANES), lambda i: (0, 0))],
        out_shape=[jax.ShapeDtypeStruct((S, LANES), BF16), jax.ShapeDtypeStruct((1, LANES), F32)],
        scratch_shapes=[pltpu.VMEM((1, LANES), F32)],
        compiler_params=_params(1), name=name)(sp, rs, proj, fbias)


def _head_operands(h, lane, q2, k2, ka2):
    act = (lane < HEAD_DIM) if h == 0 else (lane >= HEAD_DIM)
    base = HEAD_DIM if h == 0 else 0
    ones = ((lane >= base) & (lane < base + AUG)).astype(F32)
    qa = jnp.where(act, q2 * (1.0 / math.sqrt(HEAD_DIM)), ones).astype(BF16)
    ka = jnp.where(act, k2.astype(BF16), ka2)
    return act, qa, ka


def _attn_fwd(name, proj, ka, q_blk, k_blk, v_blk, n_pair):
    S = proj.shape[0]
    tq = _tile(S, 512)
    n_t = S // tq
    W = LANES * n_pair

    def body(q_ref, k_ref, v_ref, ka_ref, o_ref, o32_ref, lse_ref, m_ref, l_ref, acc_ref):
        i, j = pl.program_id(1), pl.program_id(2)

        @pl.when(j == 0)
        def _():
            m_ref[...] = jnp.full_like(m_ref, NEG_INF)
            l_ref[...] = jnp.zeros_like(l_ref)
            acc_ref[...] = jnp.zeros_like(acc_ref)

        @pl.when(j <= i)
        def _():
            lane = lax.broadcasted_iota(jnp.int32, (tq, LANES), 1)
            row = lax.broadcasted_iota(jnp.int32, (tq, tq), 0)
            col = lax.broadcasted_iota(jnp.int32, (tq, tq), 1)
            visible = (j < i) | (row >= col)
            q2, k2, v2, ka2 = q_ref[...], k_ref[...], v_ref[...], ka_ref[...]
            for h in range(2):
                act, qa, kaug = _head_operands(h, lane, q2, k2, ka2)
                s = jnp.where(visible, _dot_nt(qa, kaug), NEG_INF)
                m_prev = m_ref[h]
                m_new = jnp.maximum(m_prev, jnp.max(s, axis=-1, keepdims=True))
                alpha = jnp.exp(m_prev - m_new)
                p = jnp.exp(s - m_new)
                l_ref[h] = alpha * l_ref[h] + jnp.sum(p, axis=-1, keepdims=True)
                vm = jnp.where(act, v2, 0.0).astype(BF16)
                acc_ref[h] = alpha * acc_ref[h] + _dot(p.astype(BF16), vm)
                m_ref[h] = m_new

        @pl.when(j == i)
        def _():
            lane = lax.broadcasted_iota(jnp.int32, (tq, LANES), 1)
            first = lane < HEAD_DIM
            out = jnp.where(first, acc_ref[0] / l_ref[0], acc_ref[1] / l_ref[1])
            o_ref[...] = out.astype(BF16)
            o32_ref[...] = out
            lse_ref[...] = jnp.where(first, m_ref[0] + jnp.log(l_ref[0]), m_ref[1] + jnp.log(l_ref[1]))

    qspec = lambda blk: pl.BlockSpec((tq, LANES), lambda p, i, j: (i, blk + p))
    kspec = lambda blk: pl.BlockSpec((tq, LANES), lambda p, i, j: (jnp.minimum(j, i), blk + p))
    out = pl.BlockSpec((tq, LANES), lambda p, i, j: (i, p))
    return pl.pallas_call(
        body, grid=(n_pair, n_t, n_t),
        in_specs=[qspec(q_blk), kspec(k_blk), kspec(v_blk), kspec(0)],
        out_specs=[out, out, out],
        out_shape=[jax.ShapeDtypeStruct((S, W), BF16), jax.ShapeDtypeStruct((S, W), F32),
                   jax.ShapeDtypeStruct((S, W), F32)],
        scratch_shapes=[pltpu.VMEM((2, tq, 1), F32), pltpu.VMEM((2, tq, 1), F32),
                        pltpu.VMEM((2, tq, LANES), F32)],
        compiler_params=_params(3), name=name)(proj, proj, proj, ka)


def _attn_bwd(name, proj, ka, o, lse, dmix, q_blk, k_blk, v_blk, do_blk, n_pair):
    S = proj.shape[0]
    tq = _tile(S, 512)
    n_t = S // tq
    W = LANES * n_pair
    scale = 1.0 / math.sqrt(HEAD_DIM)

    def body(q_ref, k_ref, v_ref, ka_ref, o_ref, lse_ref, do_ref,
             dq_ref, dk_ref, dv_ref, sp_ref, rs_ref, dk_acc, dv_acc):
        j, i = pl.program_id(1), pl.program_id(2)

        @pl.when((j == 0) & (i == 0))
        def _():
            dq_ref[...] = jnp.zeros_like(dq_ref)
            rs_ref[...] = jnp.zeros_like(rs_ref)

        @pl.when(i == 0)
        def _():
            dk_acc[...] = jnp.zeros_like(dk_acc)
            dv_acc[...] = jnp.zeros_like(dv_acc)

        @pl.when(i >= j)
        def _():
            lane = lax.broadcasted_iota(jnp.int32, (tq, LANES), 1)
            row = lax.broadcasted_iota(jnp.int32, (tq, tq), 0)
            col = lax.broadcasted_iota(jnp.int32, (tq, tq), 1)
            visible = (j < i) | (row >= col)
            q2, k2, v2, ka2 = q_ref[...], k_ref[...], v_ref[...], ka_ref[...]
            o2, do2, lse2 = o_ref[...], do_ref[...], lse_ref[...]
            dq = []
            for h in range(2):
                act, qa, kaug = _head_operands(h, lane, q2, k2, ka2)
                s = jnp.where(visible, _dot_nt(qa, kaug), NEG_INF)
                p = jnp.exp(s - lse2[:, h * HEAD_DIM:h * HEAD_DIM + 1])
                dom = jnp.where(act, do2, 0.0)
                delta = jnp.sum(dom * o2, axis=-1, keepdims=True)
                dob = dom.astype(BF16)
                dp = _dot_nt(dob, jnp.where(act, v2, 0.0).astype(BF16))
                ds = (p * (dp - delta)).astype(BF16)
                dv_acc[...] += _dot_tn(p.astype(BF16), dob)
                dk_acc[h] += _dot_tn(ds, qa)
                dq.append(_dot(ds, kaug))
            rows = pl.ds(pl.multiple_of(i * tq, tq), tq)
            first = lane < HEAD_DIM
            dq_ref[rows, :] += jnp.where(first, dq[0], dq[1])
            rs_ref[rows, :] += jnp.where(first, dq[1], dq[0])

        @pl.when(i == n_t - 1)
        def _():
            lane = lax.broadcasted_iota(jnp.int32, (tq, LANES), 1)
            first = lane < HEAD_DIM
            dk_ref[...] = jnp.where(first, dk_acc[0], dk_acc[1]).astype(BF16)
            sp_ref[...] = jnp.where(first, dk_acc[1], dk_acc[0])
            dv_ref[...] = dv_acc[...].astype(BF16)

        @pl.when((j == n_t - 1) & (i == n_t - 1))
        def _():
            dq_ref[...] = dq_ref[...] * scale

    qspec = lambda blk: pl.BlockSpec((tq, LANES), lambda p, j, i: (jnp.maximum(i, j), blk + p))
    kspec = lambda blk: pl.BlockSpec((tq, LANES), lambda p, j, i: (j, blk + p))
    kout = pl.BlockSpec((tq, LANES), lambda p, j, i: (j, p))
    qres = pl.BlockSpec((S, LANES), lambda p, j, i: (0, p))
    return pl.pallas_call(
        body, grid=(n_pair, n_t, n_t),
        in_specs=[qspec(q_blk), kspec(k_blk), kspec(v_blk), kspec(0),
                  qspec(0), qspec(0), qspec(do_blk)],
        out_specs=[qres, kout, kout, kout, qres],
        out_shape=[jax.ShapeDtypeStruct((S, W), F32), jax.ShapeDtypeStruct((S, W), BF16),
                   jax.ShapeDtypeStruct((S, W), BF16), jax.ShapeDtypeStruct((S, W), F32),
                   jax.ShapeDtypeStruct((S, W), F32)],
        scratch_shapes=[pltpu.VMEM((2, tq, LANES), F32), pltpu.VMEM((tq, LANES), F32)],
        compiler_params=_params(3), name=name)(proj, proj, proj, ka, o, lse, dmix)


def _ffn_block_grad(name, act, rows, scale, dep=None):
    nb, S, Fs = act.shape
    D = rows.shape[1]
    tm = _tile(S, 512)
    return _mm_tn(name, act, rows, (nb, Fs, D),
                  pl.BlockSpec((1, tm, Fs), lambda j, i: (j, i, 0)),
                  pl.BlockSpec((tm, D), lambda j, i: (i, 0)),
                  pl.BlockSpec((1, Fs, D), lambda j, i: (j, 0, 0)), (nb, S // tm), scale, BF16, dep=dep)


def kernel(x, ffn1_norm, ffn1_w_gate, ffn1_w_up, ffn1_w_down, mix_norm, w_in, fgate_bias, conv_w, conv_b, conv_ln_g, conv_ln_b, w_out, ffn2_norm, ffn2_w_gate, ffn2_w_up, ffn2_w_down, final_norm, loss_target, m_ffn1_norm, m_ffn1_w_gate, m_ffn1_w_up, m_ffn1_w_down, m_mix_norm, m_w_in, m_fgate_bias, m_conv_w, m_conv_b, m_conv_ln_g, m_conv_ln_b, m_w_out, m_ffn2_norm, m_ffn2_w_gate, m_ffn2_w_up, m_ffn2_w_down, m_final_norm, v_ffn1_norm, v_ffn1_w_gate, v_ffn1_w_up, v_ffn1_w_down, v_mix_norm, v_w_in, v_fgate_bias, v_conv_w, v_conv_b, v_conv_ln_g, v_conv_ln_b, v_w_out, v_ffn2_norm, v_ffn2_w_gate, v_ffn2_w_up, v_ffn2_w_down, v_final_norm):
    xs = x[0]
    S, D = xs.shape
    C = conv_b.shape[0]
    n_heads = fgate_bias.shape[0]
    FW = n_heads * HEAD_DIM
    n_pair = n_heads // 2
    MIX = C + FW
    in_shard = w_in.shape[1]
    in_cols = in_shard * N_DEV
    NP = -(-in_cols // 512) * 512
    q_blk, k_blk, v_blk = 2 * C // LANES, (2 * C + FW) // LANES, (2 * C + 2 * FW) // LANES
    f_blk = (2 * C + 3 * FW) // LANES
    assert C % LANES == 0 and FW % LANES == 0 and n_heads % 2 == 0 and n_heads <= LANES
    assert in_cols == 2 * C + 3 * FW + n_heads and MIX == w_out.shape[0] * N_DEV

    vec = lambda a: a.reshape(1, -1)
    bf = lambda a: a.astype(BF16)
    tm = _tile(S, 512)

    ag, ag_token = _xchg_start("ag_start", [
        bf(ffn1_w_gate).T, bf(ffn1_w_up).T, bf(ffn1_w_down), bf(w_in).T, conv_w, bf(w_out),
        bf(ffn2_w_gate).T, bf(ffn2_w_up).T, bf(ffn2_w_down)], False)
    fbias = jnp.pad(vec(fgate_bias), ((0, 0), (0, LANES - n_heads)))

    wgt1, wut1, wd1 = _xchg_wait("ag_wait_ffn1", ag[0:3], False, ag_token)
    x1, h1, G1, U1 = _ffn_fwd("ffn1_fwd", xs, vec(ffn1_norm), wgt1, wut1, wd1)
    win_g, cw_g = _xchg_wait("ag_wait_in", ag[3:5], False, x1)
    wint = jnp.pad(win_g.reshape(in_cols, D), ((0, NP - in_cols), (0, 0)))
    cw = jnp.pad(cw_g.transpose(1, 0, 2).reshape(CONV_WIDTH, C), ((0, CONV_HALO - CONV_WIDTH), (0, 0)))
    proj, h2 = _norm_mm_nt("proj_in", x1, vec(mix_norm), wint, 512)
    yc, y_conv = _conv_fwd("conv_fwd", proj, C, cw, vec(conv_b), vec(conv_ln_g), vec(conv_ln_b))
    ka = _gate_prep("gate_prep", proj, f_blk, fbias, n_pair)
    o, o32, lse = _attn_fwd("attn_fwd", proj, ka, q_blk, k_blk, v_blk, n_pair)
    (wout_g,) = _xchg_wait("ag_wait_out", ag[5:6], False, lse)
    wout = wout_g.reshape(MIX, D)
    mix = jnp.concatenate([y_conv, o], axis=1)
    x2 = _mm_res("proj_out", mix, wout, x1, 512)
    wgt2, wut2, wd2 = _xchg_wait("ag_wait_ffn2", ag[6:9], False, x2)
    x3, h3, G2, U2 = _ffn_fwd("ffn2_fwd", x2, vec(ffn2_norm), wgt2, wut2, wd2)

    dx3, d_final_norm, loss_part = _final("final", x3, vec(final_norm), loss_target[0])
    dG2, dU2, A2, dx2, d_ffn2_norm = _ffn_bwd_act("ffn2_bwd", dx3, x2, vec(ffn2_norm), G2, U2, wgt2, wut2, wd2)
    dwd2 = _ffn_block_grad("ffn2_dwd", A2, dx3, 0.5)
    s_d2, tok = _xchg_start("a2a_start_ffn2_wd", [dwd2], True)
    dwg2 = _ffn_block_grad("ffn2_dwg", dG2, h3, 1.0, dep=tok)
    s_g2, tok = _xchg_start("a2a_start_ffn2_wg", [dwg2], True)
    dwu2 = _ffn_block_grad("ffn2_dwu", dU2, h3, 1.0, dep=tok)
    s_u2, tok = _xchg_start("a2a_start_ffn2_wu", [dwu2], True)

    dmix = _mm_k("dmix", dx2, wout, 512, True, dep=tok)
    d_wout = _mm_tn("dwout", mix, dx2, (MIX, D),
                    pl.BlockSpec((tm, MIX), lambda j, i: (i, 0)), pl.BlockSpec((tm, 512), lambda j, i: (i, j)),
                    pl.BlockSpec((MIX, 512), lambda j, i: (0, j)), (D // 512, S // tm), 1.0, BF16)
    s_out, tok = _xchg_start("a2a_start_w_out", [d_wout.reshape(N_DEV, MIX // N_DEV, D)], True)
    dag, d_cw, d_cb, d_lg, d_lb = _conv_bwd("conv_bwd", dmix, yc, proj, C, cw, vec(conv_ln_g), vec(conv_ln_b))
    dq, dk, dv, sp, rs = _attn_bwd("attn_bwd", proj, ka, o32, lse, dmix, q_blk, k_blk, v_blk, C // LANES, n_pair)
    df, d_fb = _gate_bwd("gate_bwd", sp, rs, proj, f_blk, fbias, n_pair)
    dproj = jnp.concatenate([dag, bf(dq), dk, dv, df, jnp.zeros((S, NP - f_blk * LANES - LANES), BF16)], axis=1)
    d_wint = _mm_tn("dwin", dproj, h2, (NP, D),
                    pl.BlockSpec((tm, 512), lambda j, i: (i, j)), pl.BlockSpec((tm, D), lambda j, i: (i, 0)),
                    pl.BlockSpec((512, D), lambda j, i: (j, 0)), (NP // 512, S // tm), 1.0, BF16, dep=tok)
    s_in, tok = _xchg_start("a2a_start_w_in", [d_wint[:in_cols].reshape(N_DEV, in_shard, D)], True)
    dx1, d_mix_norm = _mm_k("dh2", dproj, wint, 512, False, norm_bwd=(x1, vec(mix_norm), dx2), dep=tok)
    dG1, dU1, A1, dx0, d_ffn1_norm = _ffn_bwd_act("ffn1_bwd", dx1, xs, vec(ffn1_norm), G1, U1, wgt1, wut1, wd1)
    dwd1 = _ffn_block_grad("ffn1_dwd", A1, dx1, 0.5)
    s_d1, tok = _xchg_start("a2a_start_ffn1_wd", [dwd1], True)
    dwg1 = _ffn_block_grad("ffn1_dwg", dG1, h1, 1.0, dep=tok)
    s_g1, tok = _xchg_start("a2a_start_ffn1_wg", [dwg1], True)
    dwu1 = _ffn_block_grad("ffn1_dwu", dU1, h1, 1.0, dep=tok)
    s_u1, tok = _xchg_start("a2a_start_ffn1_wu", [dwu1], True)

    rows = lambda a: a.reshape(-1, C)
    pad_row = lambda a: jnp.pad(a.reshape(1, -1), ((0, 0), (0, C - a.size)))
    pieces = [rows(d_ffn1_norm), rows(d_mix_norm), rows(d_ffn2_norm), rows(d_final_norm),
              d_cw[:CONV_WIDTH], d_cb, d_lg, d_lb, pad_row(d_fb[0, :n_heads]), pad_row(loss_part[0, :1])]
    pack = jnp.concatenate(pieces, axis=0)
    n_rows = pack.shape[0]
    pack = jnp.pad(pack, ((0, -n_rows % SUBLANES), (0, 0))) + tok[0:1, 0:1]
    (pack_g,) = _exchange("ag_small", [pack], False)
    tot = _reduce_adam("sum_small", pack_g)
    nd = D // C
    g_ffn1_norm, g_mix_norm, g_ffn2_norm, g_final_norm = (tot[k * nd:(k + 1) * nd].reshape(D) for k in range(4))
    r0 = 4 * nd
    me = _lin(_mesh_pos())
    cs = C // N_DEV
    g_conv_w = lax.dynamic_slice(tot[r0:r0 + CONV_WIDTH], (0, me * cs), (CONV_WIDTH, cs))
    g_conv_b, g_ln_g, g_ln_b = tot[r0 + CONV_WIDTH], tot[r0 + CONV_WIDTH + 1], tot[r0 + CONV_WIDTH + 2]
    g_fb = tot[r0 + CONV_WIDTH + 3, :n_heads]
    loss = tot[r0 + CONV_WIDTH + 4, 0]

    small = [(g_ffn1_norm, ffn1_norm, m_ffn1_norm, v_ffn1_norm), (g_mix_norm, mix_norm, m_mix_norm, v_mix_norm),
             (g_fb, fgate_bias, m_fgate_bias, v_fgate_bias), (g_conv_w, conv_w, m_conv_w, v_conv_w),
             (g_conv_b, conv_b, m_conv_b, v_conv_b), (g_ln_g, conv_ln_g, m_conv_ln_g, v_conv_ln_g),
             (g_ln_b, conv_ln_b, m_conv_ln_b, v_conv_ln_b), (g_ffn2_norm, ffn2_norm, m_ffn2_norm, v_ffn2_norm),
             (g_final_norm, final_norm, m_final_norm, v_final_norm)]
    sizes = [g.size for g, _, _, _ in small]
    total = sum(sizes)
    padded = -(-total // (SUBLANES * LANES)) * (SUBLANES * LANES)

    def flat_pack(k, fill):
        flat = jnp.concatenate([t[k].reshape(-1) for t in small])
        return jnp.pad(flat, (0, padded - total), constant_values=fill).reshape(padded // LANES, LANES)

    sg, sd, sm, sv = _reduce_adam("adam_small", flat_pack(0, 0.0)[None], flat_pack(1, 0.0), flat_pack(2, 0.0),
                                  flat_pack(3, 1.0))

    def unpack(packed):
        flat = packed.reshape(-1)
        out, off = [], 0
        for (g, _, _, _), n in zip(small, sizes):
            out.append(flat[off:off + n].reshape(g.shape))
            off += n
        return out

    s_g, s_d, s_m, s_v = unpack(sg), unpack(sd), unpack(sm), unpack(sv)

    r_d2, r_g2, r_u2, r_out, r_in = _xchg_wait("a2a_wait_a", s_d2 + s_g2 + s_u2 + s_out + s_in, True, sd)
    tr = lambda a: a.T

    def adam_t(name, recv, w, m, v):
        return tuple(tr(r) for r in _reduce_adam(name, recv, tr(w), tr(m), tr(v)))

    res = {
        "ffn2_w_down": _reduce_adam("adam_ffn2_wd", r_d2, ffn2_w_down, m_ffn2_w_down, v_ffn2_w_down),
        "ffn2_w_gate": adam_t("adam_ffn2_wg", r_g2, ffn2_w_gate, m_ffn2_w_gate, v_ffn2_w_gate),
        "ffn2_w_up": adam_t("adam_ffn2_wu", r_u2, ffn2_w_up, m_ffn2_w_up, v_ffn2_w_up),
        "w_out": _reduce_adam("adam_w_out", r_out, w_out, m_w_out, v_w_out),
        "w_in": adam_t("adam_w_in", r_in, w_in, m_w_in, v_w_in),
    }
    r_d1, r_g1, r_u1 = _xchg_wait("a2a_wait_b", s_d1 + s_g1 + s_u1, True, res["w_in"][0])
    res["ffn1_w_down"] = _reduce_adam("adam_ffn1_wd", r_d1, ffn1_w_down, m_ffn1_w_down, v_ffn1_w_down)
    res["ffn1_w_gate"] = adam_t("adam_ffn1_wg", r_g1, ffn1_w_gate, m_ffn1_w_gate, v_ffn1_w_gate)
    res["ffn1_w_up"] = adam_t("adam_ffn1_wu", r_u1, ffn1_w_up, m_ffn1_w_up, v_ffn1_w_up)

    small_names = ["ffn1_norm", "mix_norm", "fgate_bias", "conv_w", "conv_b", "conv_ln_g", "conv_ln_b",
                   "ffn2_norm", "final_norm"]
    for k, n in enumerate(small_names):
        res[n] = (s_g[k], s_d[k], s_m[k], s_v[k])
    order = ["ffn1_norm", "ffn1_w_gate", "ffn1_w_up", "ffn1_w_down", "mix_norm", "w_in", "fgate_bias",
             "conv_w", "conv_b", "conv_ln_g", "conv_ln_b", "w_out", "ffn2_norm", "ffn2_w_gate", "ffn2_w_up",
             "ffn2_w_down", "final_norm"]
    outs = [loss, dx0[None]]
    for k in range(4):
        outs += [res[n][k] for n in order]
    return tuple(outs)
```

```python
import math

import jax
import jax.numpy as jnp
from jax import lax
from jax.experimental import pallas as pl
from jax.experimental.pallas import tpu as pltpu

F32 = jnp.float32
BF16 = jnp.bfloat16

N_DEV = 8
MESH_ID = pl.DeviceIdType.MESH
HEAD_DIM = 64
CONV_WIDTH = 31
CONV_HALO = 32
NORM_EPS = 1e-6
LN_EPS = 1e-5
NEG_INF = -1e30
LANES = 128
SUBLANES = 8
V7X_VMEM_LIMIT = 52 * 1024 * 1024

ADAM_LR = 0.001
ADAM_B1 = 0.9
ADAM_B2 = 0.999
ADAM_EPS = 1e-08
ADAM_WD = 0.01
ADAM_STEP = 10


def _params(n_grid_axes):
    return pltpu.CompilerParams(dimension_semantics=("arbitrary",) * n_grid_axes,
                                vmem_limit_bytes=V7X_VMEM_LIMIT)


def _tile(n, pref, mult=8):
    t = min(pref, n)
    while t >= mult:
        if n % t == 0 and t % mult == 0:
            return t
        t -= mult
    return n


def _dot(a, b):
    return jnp.dot(a, b, preferred_element_type=F32)


def _dot_nt(a, b):
    return lax.dot_general(a, b, (((1,), (1,)), ((), ())), preferred_element_type=F32)


def _dot_tn(a, b):
    return lax.dot_general(a, b, (((0,), (0,)), ((), ())), preferred_element_type=F32)


def _sigmoid(x):
    return 1.0 / (1.0 + jnp.exp(-x))


def _rms_fwd(x, g):
    r = lax.rsqrt(jnp.mean(x * x, axis=-1, keepdims=True) + NORM_EPS)
    return x * r * g


def _rms_bwd(dh, x, g):
    r = lax.rsqrt(jnp.mean(x * x, axis=-1, keepdims=True) + NORM_EPS)
    xh = x * r
    dxh = dh * g
    dx = r * (dxh - xh * jnp.mean(dxh * xh, axis=-1, keepdims=True))
    return dx, jnp.sum(dh * xh, axis=0, keepdims=True)


def _split3(x):
    hi = x.astype(BF16)
    r = x - hi.astype(F32)
    mid = r.astype(BF16)
    lo = (r - mid.astype(F32)).astype(BF16)
    return hi, mid, lo


def _blk(ref):
    return ref[0] if len(ref.shape) == 3 else ref[...]


_DEP_SPEC = pl.BlockSpec(memory_space=pl.ANY)


def _mesh_pos():
    return lax.axis_index("x"), lax.axis_index("y"), lax.axis_index("c")


def _peer(pos, k):
    x, y, c = pos
    return (1 - x if k & 4 else x, 1 - y if k & 2 else y, 1 - c if k & 1 else c)


def _lin(pos):
    x, y, c = pos
    return 4 * x + 2 * y + c


def _remote_copy(src, land, send_sems, recv_sems, pos, k, all_to_all):
    peer = _peer(pos, k)
    return pltpu.make_async_remote_copy(
        src_ref=src.at[_lin(peer)] if all_to_all else src, dst_ref=land.at[_lin(pos)],
        send_sem=send_sems.at[k - 1], recv_sem=recv_sems.at[k - 1],
        device_id=peer, device_id_type=MESH_ID)


def _landing_shape(a, all_to_all):
    return a.shape if all_to_all else (N_DEV,) + a.shape


def _exchange(name, arrays, all_to_all):
    n = len(arrays)
    out_shapes = [jax.ShapeDtypeStruct(_landing_shape(a, all_to_all), a.dtype) for a in arrays]

    def body(*refs):
        ins, outs = refs[:n], refs[n:2 * n]
        send_sems, recv_sems, local_sems = refs[2 * n:]
        pos = _mesh_pos()
        me = _lin(pos)
        local = []
        for a in range(n):
            src = ins[a].at[me] if all_to_all else ins[a]
            cp = pltpu.make_async_copy(src, outs[a].at[me], local_sems.at[a])
            cp.start()
            local.append(cp)
        remote = [_remote_copy(ins[a], outs[a], send_sems.at[a], recv_sems.at[a], pos, k, all_to_all)
                  for a in range(n) for k in range(1, N_DEV)]
        for cp in remote:
            cp.start()
        for cp in remote:
            cp.wait()
        for cp in local:
            cp.wait()

    any_spec = pl.BlockSpec(memory_space=pl.ANY)
    return pl.pallas_call(
        body, out_shape=out_shapes, in_specs=[any_spec] * n, out_specs=[any_spec] * n,
        scratch_shapes=[pltpu.SemaphoreType.DMA((n, N_DEV - 1)),
                        pltpu.SemaphoreType.DMA((n, N_DEV - 1)),
                        pltpu.SemaphoreType.DMA((n,))],
        name=name)(*arrays)


def _all_gather_two_level(name, shards):
    n = len(shards)
    out_shapes = [jax.ShapeDtypeStruct((N_DEV,) + a.shape, a.dtype) for a in shards]

    def body(*refs):
        ins, outs = refs[:n], refs[n:2 * n]
        send_sems, recv_sems, local_sems = refs[2 * n:]
        x, y, c = pos = _mesh_pos()
        sibling = (x, y, 1 - c)
        chips = [(1 - x, y), (x, 1 - y), (1 - x, 1 - y)]

        def copy(a, k, block, to, src=None):
            slot = outs[a].at[_lin(block)]
            return pltpu.make_async_remote_copy(
                src_ref=slot if src is None else src, dst_ref=slot,
                send_sem=send_sems.at[a, k], recv_sem=recv_sems.at[a, k],
                device_id=to, device_id_type=MESH_ID)

        local = [pltpu.make_async_copy(ins[a], outs[a].at[_lin(pos)], local_sems.at[a]) for a in range(n)]
        first = [copy(a, 1 + j, pos, (*chip, c), src=ins[a]) for j, chip in enumerate(chips) for a in range(n)]
        first += [copy(a, 0, pos, sibling, src=ins[a]) for a in range(n)]
        for cp in first + local:
            cp.start()
        passed = []
        for j, chip in enumerate(chips):
            for a in range(n):
                copy(a, 1 + j, (*chip, c), pos).wait_recv()
                cp = copy(a, 4 + j, (*chip, c), sibling)
                cp.start()
                passed.append(cp)
        for a in range(n):
            copy(a, 0, sibling, pos).wait_recv()
        for j, chip in enumerate(chips):
            for a in range(n):
                copy(a, 4 + j, (*chip, 1 - c), pos).wait_recv()
        for cp in first + passed:
            cp.wait_send()
        for cp in local:
            cp.wait()

    any_spec = pl.BlockSpec(memory_space=pl.ANY)
    return pl.pallas_call(
        body, out_shape=out_shapes, in_specs=[any_spec] * n, out_specs=[any_spec] * n,
        scratch_shapes=[pltpu.SemaphoreType.DMA((n, N_DEV - 1)),
                        pltpu.SemaphoreType.DMA((n, N_DEV - 1)),
                        pltpu.SemaphoreType.DMA((n,))],
        name=name)(*shards)


_HBM_SPEC = pl.BlockSpec(memory_space=pltpu.HBM)
_SEM_SPEC = pl.BlockSpec(memory_space=pltpu.SEMAPHORE)
_SIDE_EFFECT = pltpu.SideEffectType.DATAFLOW_SIDE_EFFECTING


def _xchg_start(name, arrays, all_to_all, dep=None):
    n = len(arrays)
    me = _lin(_mesh_pos())
    lands = [lax.dynamic_update_index_in_dim(
        lax.empty(_landing_shape(a, all_to_all), a.dtype),
        lax.dynamic_index_in_dim(a, me, 0, keepdims=False) if all_to_all else a, me, 0) for a in arrays]

    deps = [] if dep is None else [dep]

    def body(*refs):
        srcs, lnds = refs[:n], refs[n:2 * n]
        outs = refs[2 * n + len(deps):]
        send, recv = outs[:n], outs[n:2 * n]
        token = outs[4 * n]
        pos = _mesh_pos()
        for a in range(n):
            for k in range(1, N_DEV):
                _remote_copy(srcs[a], lnds[a], send[a], recv[a], pos, k, all_to_all).start()
        token[...] = jnp.zeros_like(token)

    hbm = lambda a: pltpu.HBM(a.shape, a.dtype)
    sems = [pltpu.SemaphoreType.DMA((N_DEV - 1,))] * (2 * n)
    res = pl.pallas_call(
        body, name=name,
        out_shape=sems + [hbm(a) for a in arrays] + [hbm(l) for l in lands]
        + [jax.ShapeDtypeStruct((SUBLANES, LANES), F32)],
        in_specs=[_HBM_SPEC] * (2 * n) + [_DEP_SPEC] * len(deps),
        out_specs=[_SEM_SPEC] * (2 * n) + [_HBM_SPEC] * (2 * n) + [pl.BlockSpec(memory_space=pltpu.VMEM)],
        input_output_aliases={a: 2 * n + a for a in range(2 * n)},
        compiler_params=pltpu.CompilerParams(has_side_effects=_SIDE_EFFECT),
    )(*[pltpu.with_memory_space_constraint(a, pltpu.HBM) for a in list(arrays) + lands], *deps)
    per_array = [(res[a], res[n + a], res[2 * n + a], res[3 * n + a]) for a in range(n)]
    return per_array, res[4 * n]


def _xchg_wait(name, started, all_to_all, after):
    n = len(started)

    def body(*refs):
        srcs, lnds, send, recv = refs[:n], refs[n:2 * n], refs[2 * n:3 * n], refs[3 * n:4 * n]
        pos = _mesh_pos()
        for a in range(n):
            for k in range(1, N_DEV):
                cp = _remote_copy(srcs[a], lnds[a], send[a], recv[a], pos, k, all_to_all)
                cp.wait_send()
                cp.wait_recv()

    hbm = lambda a: pltpu.HBM(a.shape, a.dtype)
    srcs = [s[2] for s in started]
    lands = [s[3] for s in started]
    res = pl.pallas_call(
        body, name=name,
        out_shape=[hbm(a) for a in srcs + lands],
        in_specs=[_HBM_SPEC] * (2 * n) + [_SEM_SPEC] * (2 * n) + [_DEP_SPEC],
        out_specs=[_HBM_SPEC] * (2 * n),
        input_output_aliases={a: a for a in range(2 * n)},
        compiler_params=pltpu.CompilerParams(has_side_effects=_SIDE_EFFECT),
    )(*srcs, *lands, *[s[0] for s in started], *[s[1] for s in started], after)
    return list(res[n:])


def _reduce_adam(name, parts, w=None, m=None, v=None):
    n, R, C = parts.shape
    tr = _tile(R, 256)
    do_adam = w is not None
    bc1 = 1.0 - ADAM_B1 ** ADAM_STEP
    bc2 = 1.0 - ADAM_B2 ** ADAM_STEP

    def body(*refs):
        p_ref = refs[0]
        g = p_ref[0].astype(F32)
        for d in range(1, n):
            g = g + p_ref[d].astype(F32)
        if not do_adam:
            refs[1][...] = g
            return
        w_ref, m_ref, v_ref, g_ref, d_ref, nm_ref, nv_ref = refs[1:]
        g_ref[...] = g
        nm = ADAM_B1 * m_ref[...] + (1.0 - ADAM_B1) * g
        nv = ADAM_B2 * v_ref[...] + (1.0 - ADAM_B2) * (g * g)
        m_hat = nm / bc1
        v_hat = nv / bc2
        d_ref[...] = -ADAM_LR * (m_hat / (jnp.sqrt(v_hat) + ADAM_EPS) + ADAM_WD * w_ref[...])
        nm_ref[...] = nm
        nv_ref[...] = nv

    tc = _tile(C, 512, LANES) if tr == R and R > 256 else C
    row = pl.BlockSpec((tr, tc), lambda i, j: (i, j))
    part = pl.BlockSpec((n, tr, tc), lambda i, j: (0, i, j))
    shard = jax.ShapeDtypeStruct((R, C), F32)
    grid = (R // tr, C // tc)
    if do_adam:
        return pl.pallas_call(body, grid=grid, in_specs=[part, row, row, row],
                              out_specs=[row] * 4, out_shape=[shard] * 4,
                              compiler_params=_params(2), name=name)(parts, w, m, v)
    return pl.pallas_call(body, grid=grid, in_specs=[part], out_specs=row, out_shape=shard,
                          compiler_params=_params(2), name=name)(parts)


def _ffn_fwd(name, x, gain, wgt, wut, wd):
    S, D = x.shape
    nb, Fs, _ = wgt.shape
    tm = _tile(S, 512)

    def body(x_ref, g_ref, wg_ref, wu_ref, wd_ref, xo_ref, h_ref, G_ref, U_ref, acc_ref):
        j = pl.program_id(1)

        @pl.when(j == 0)
        def _():
            h_ref[...] = _rms_fwd(x_ref[...], g_ref[...]).astype(BF16)
            acc_ref[...] = jnp.zeros_like(acc_ref)

        h = h_ref[...]
        G = _dot_nt(h, wg_ref[0])
        U = _dot_nt(h, wu_ref[0])
        G_ref[0] = G
        U_ref[0] = U
        a = G * _sigmoid(G) * U
        acc_ref[...] += _dot(a.astype(BF16), wd_ref[0])

        @pl.when(j == nb - 1)
        def _():
            xo_ref[...] = x_ref[...] + 0.5 * acc_ref[...]

    row = pl.BlockSpec((tm, D), lambda i, j: (i, 0))
    act = pl.BlockSpec((1, tm, Fs), lambda i, j: (j, i, 0))
    wblk = pl.BlockSpec((1, Fs, D), lambda i, j: (j, 0, 0))
    return pl.pallas_call(
        body, grid=(S // tm, nb),
        in_specs=[row, pl.BlockSpec((1, D), lambda i, j: (0, 0)), wblk, wblk, wblk],
        out_specs=[row, row, act, act],
        out_shape=[jax.ShapeDtypeStruct((S, D), F32), jax.ShapeDtypeStruct((S, D), BF16),
                   jax.ShapeDtypeStruct((nb, S, Fs), F32), jax.ShapeDtypeStruct((nb, S, Fs), F32)],
        scratch_shapes=[pltpu.VMEM((tm, D), F32)],
        compiler_params=_params(2), name=name)(x, gain, wgt, wut, wd)


def _ffn_bwd_act(name, dxo, x_in, gain, G, U, wgt, wut, wd):
    S, D = x_in.shape
    nb, Fs, _ = wgt.shape
    tm = _tile(S, 256)

    def body(dxo_ref, x_ref, g_ref, G_ref, U_ref, wg_ref, wu_ref, wd_ref,
             dG_ref, dU_ref, A_ref, dx_ref, dgain_ref, dxb_ref, acc_ref):
        i, j = pl.program_id(0), pl.program_id(1)

        @pl.when(j == 0)
        def _():
            dxb_ref[...] = dxo_ref[...].astype(BF16)
            acc_ref[...] = jnp.zeros_like(acc_ref)

        @pl.when((i == 0) & (j == 0))
        def _():
            dgain_ref[...] = jnp.zeros_like(dgain_ref)

        dA = 0.5 * _dot_nt(dxb_ref[...], wd_ref[0])
        Gv = G_ref[0]
        Uv = U_ref[0]
        sg = _sigmoid(Gv)
        sl = Gv * sg
        dG = (dA * Uv * (sg * (1.0 + Gv * (1.0 - sg)))).astype(BF16)
        dU = (dA * sl).astype(BF16)
        dG_ref[0] = dG
        dU_ref[0] = dU
        A_ref[0] = (sl * Uv).astype(BF16)
        acc_ref[...] += _dot(dG, wg_ref[0]) + _dot(dU, wu_ref[0])

        @pl.when(j == nb - 1)
        def _():
            dx, dg = _rms_bwd(acc_ref[...], x_ref[...], g_ref[...])
            dx_ref[...] = dxo_ref[...] + dx
            dgain_ref[...] += dg

    row = pl.BlockSpec((tm, D), lambda i, j: (i, 0))
    vec = pl.BlockSpec((1, D), lambda i, j: (0, 0))
    act = pl.BlockSpec((1, tm, Fs), lambda i, j: (j, i, 0))
    wblk = pl.BlockSpec((1, Fs, D), lambda i, j: (j, 0, 0))
    act_shape = jax.ShapeDtypeStruct((nb, S, Fs), BF16)
    return pl.pallas_call(
        body, grid=(S // tm, nb),
        in_specs=[row, row, vec, act, act, wblk, wblk, wblk],
        out_specs=[act, act, act, row, vec],
        out_shape=[act_shape, act_shape, act_shape, jax.ShapeDtypeStruct((S, D), F32),
                   jax.ShapeDtypeStruct((1, D), F32)],
        scratch_shapes=[pltpu.VMEM((tm, D), BF16), pltpu.VMEM((tm, D), F32)],
        compiler_params=_params(2), name=name)(dxo, x_in, gain, G, U, wgt, wut, wd)


def _mm_tn(name, lhs, rhs, out_shape, lhs_spec, rhs_spec, out_spec, grid, scale, out_dtype, dep=None):
    acc_shape = tuple(out_spec.block_shape[-2:])
    n_red = grid[-1]

    def body(l_ref, r_ref, *rest):
        o_ref, acc_ref = rest[-2:]
        i = pl.program_id(len(grid) - 1)

        @pl.when(i == 0)
        def _():
            acc_ref[...] = jnp.zeros_like(acc_ref)

        acc_ref[...] += _dot_tn(_blk(l_ref).astype(BF16), _blk(r_ref).astype(BF16))

        @pl.when(i == n_red - 1)
        def _():
            res = (scale * acc_ref[...]).astype(out_dtype)
            if len(o_ref.shape) == 3:
                o_ref[0] = res
            else:
                o_ref[...] = res

    deps = [] if dep is None else [dep]
    return pl.pallas_call(
        body, grid=grid, in_specs=[lhs_spec, rhs_spec] + [_DEP_SPEC] * len(deps), out_specs=out_spec,
        out_shape=jax.ShapeDtypeStruct(out_shape, out_dtype),
        scratch_shapes=[pltpu.VMEM(acc_shape, F32)],
        compiler_params=_params(len(grid)), name=name)(lhs, rhs, *deps)


def _norm_mm_nt(name, x, gain, wt, tn):
    S, D = x.shape
    N = wt.shape[0]
    tm = _tile(S, 512)

    def body(x_ref, g_ref, w_ref, o_ref, h_ref):
        @pl.when(pl.program_id(1) == 0)
        def _():
            h_ref[...] = _rms_fwd(x_ref[...], g_ref[...]).astype(BF16)

        o_ref[...] = _dot_nt(h_ref[...], w_ref[...])

    row = pl.BlockSpec((tm, D), lambda i, j: (i, 0))
    return pl.pallas_call(
        body, grid=(S // tm, N // tn),
        in_specs=[row, pl.BlockSpec((1, D), lambda i, j: (0, 0)),
                  pl.BlockSpec((tn, D), lambda i, j: (j, 0))],
        out_specs=[pl.BlockSpec((tm, tn), lambda i, j: (i, j)), row],
        out_shape=[jax.ShapeDtypeStruct((S, N), F32), jax.ShapeDtypeStruct((S, D), BF16)],
        compiler_params=_params(2), name=name)(x, gain, wt)


def _mm_res(name, a, w, res, tn):
    S, K = a.shape
    N = w.shape[1]
    tm = _tile(S, 512)

    def body(a_ref, w_ref, r_ref, o_ref):
        o_ref[...] = r_ref[...] + _dot(a_ref[...], w_ref[...])

    tile = pl.BlockSpec((tm, tn), lambda i, j: (i, j))
    return pl.pallas_call(
        body, grid=(S // tm, N // tn),
        in_specs=[pl.BlockSpec((tm, K), lambda i, j: (i, 0)),
                  pl.BlockSpec((K, tn), lambda i, j: (0, j)), tile],
        out_specs=tile, out_shape=jax.ShapeDtypeStruct((S, N), F32),
        compiler_params=_params(2), name=name)(a, w, res)


def _mm_k(name, a, b, tk, transpose_b, norm_bwd=None, dep=None):
    S, K = a.shape
    N = b.shape[0] if transpose_b else b.shape[1]
    tm = _tile(S, 512)
    nk = K // tk
    n_extra = 0 if norm_bwd is None else 3
    deps = [] if dep is None else [dep]

    def body(*refs):
        a_ref, b_ref = refs[:2]
        outs = refs[2 + n_extra + len(deps):]
        acc_ref = outs[-1]
        i, k = pl.program_id(0), pl.program_id(1)

        @pl.when(k == 0)
        def _():
            acc_ref[...] = jnp.zeros_like(acc_ref)

        av = a_ref[...].astype(BF16)
        acc_ref[...] += _dot_nt(av, b_ref[...]) if transpose_b else _dot(av, b_ref[...])

        if norm_bwd is None:
            @pl.when(k == nk - 1)
            def _():
                outs[0][...] = acc_ref[...]
        else:
            x_ref, g_ref, dres_ref = refs[2:5]
            o_ref, dgain_ref = outs[:2]

            @pl.when((i == 0) & (k == 0))
            def _():
                dgain_ref[...] = jnp.zeros_like(dgain_ref)

            @pl.when(k == nk - 1)
            def _():
                dx, dg = _rms_bwd(acc_ref[...], x_ref[...], g_ref[...])
                o_ref[...] = dres_ref[...] + dx
                dgain_ref[...] += dg

    a_spec = pl.BlockSpec((tm, tk), lambda i, k: (i, k))
    b_spec = (pl.BlockSpec((N, tk), lambda i, k: (0, k)) if transpose_b
              else pl.BlockSpec((tk, N), lambda i, k: (k, 0)))
    row = pl.BlockSpec((tm, N), lambda i, k: (i, 0))
    vec = pl.BlockSpec((1, N), lambda i, k: (0, 0))
    out = jax.ShapeDtypeStruct((S, N), F32)
    scratch = [pltpu.VMEM((tm, N), F32)]
    dep_specs = [_DEP_SPEC] * len(deps)
    if norm_bwd is None:
        return pl.pallas_call(body, grid=(S // tm, nk), in_specs=[a_spec, b_spec] + dep_specs,
                              out_specs=row, out_shape=out, scratch_shapes=scratch,
                              compiler_params=_params(2), name=name)(a, b, *deps)
    x_in, gain, dres = norm_bwd
    return pl.pallas_call(body, grid=(S // tm, nk), in_specs=[a_spec, b_spec, row, vec, row] + dep_specs,
                          out_specs=[row, vec],
                          out_shape=[out, jax.ShapeDtypeStruct((1, N), F32)],
                          scratch_shapes=scratch,
                          compiler_params=_params(2), name=name)(a, b, x_in, gain, dres, *deps)


def _final(name, x, gain, target):
    S, D = x.shape
    tm = _tile(S, 512)

    def body(x_ref, g_ref, t_ref, dx_ref, dgain_ref, loss_ref):
        @pl.when(pl.program_id(0) == 0)
        def _():
            dgain_ref[...] = jnp.zeros_like(dgain_ref)
            loss_ref[...] = jnp.zeros_like(loss_ref)

        xv = x_ref[...]
        err = _rms_fwd(xv, g_ref[...]) - t_ref[...]
        per_tok = jnp.mean(err * err, axis=-1, keepdims=True)
        loss_ref[...] += 0.5 * jnp.sum(per_tok, axis=0, keepdims=True)
        dx, dg = _rms_bwd(err * (1.0 / D), xv, g_ref[...])
        dx_ref[...] = dx
        dgain_ref[...] += dg

    row = pl.BlockSpec((tm, D), lambda i: (i, 0))
    vec = pl.BlockSpec((1, D), lambda i: (0, 0))
    return pl.pallas_call(
        body, grid=(S // tm,), in_specs=[row, vec, row],
        out_specs=[row, vec, pl.BlockSpec((1, LANES), lambda i: (0, 0))],
        out_shape=[jax.ShapeDtypeStruct((S, D), F32), jax.ShapeDtypeStruct((1, D), F32),
                   jax.ShapeDtypeStruct((1, LANES), F32)],
        compiler_params=_params(1), name=name)(x, gain, target)


def _conv_tiles(S):
    ts = _tile(S, 256, CONV_HALO)
    return ts, ts // CONV_HALO


def _ln_stats(yc):
    mu = jnp.mean(yc, axis=-1, keepdims=True)
    d = yc - mu
    rs = lax.rsqrt(jnp.mean(d * d, axis=-1, keepdims=True) + LN_EPS)
    return d * rs, rs


def _conv_fwd(name, proj, C, cw, cb, lg, lb):
    S = proj.shape[0]
    ts, hb = _conv_tiles(S)

    def body(a_ref, g_ref, ah_ref, gh_ref, cw_ref, cb_ref, lg_ref, lb_ref, yc_ref, y_ref, ubuf):
        i = pl.program_id(0)
        uh = ah_ref[...] * _sigmoid(gh_ref[...])
        ubuf[pl.ds(0, CONV_HALO), :] = jnp.where(i > 0, uh, 0.0)
        ubuf[pl.ds(CONV_HALO, ts), :] = a_ref[...] * _sigmoid(g_ref[...])
        acc = jnp.zeros((ts, C), F32)
        for k in range(CONV_WIDTH):
            acc = acc + cw_ref[pl.ds(k, 1), :] * ubuf[pl.ds(k + CONV_HALO - CONV_WIDTH + 1, ts), :]
        yc = acc + cb_ref[...]
        yc_ref[...] = yc
        yn, _ = _ln_stats(yc)
        z = yn * lg_ref[...] + lb_ref[...]
        y_ref[...] = (z * _sigmoid(z)).astype(BF16)

    main = lambda col: pl.BlockSpec((ts, C), lambda i: (i, col))
    halo = lambda col: pl.BlockSpec((CONV_HALO, C), lambda i: (jnp.maximum(i * hb - 1, 0), col))
    vec = pl.BlockSpec((1, C), lambda i: (0, 0))
    return pl.pallas_call(
        body, grid=(S // ts,),
        in_specs=[main(0), main(1), halo(0), halo(1),
                  pl.BlockSpec((CONV_HALO, C), lambda i: (0, 0)), vec, vec, vec],
        out_specs=[pl.BlockSpec((ts, C), lambda i: (i, 0))] * 2,
        out_shape=[jax.ShapeDtypeStruct((S, C), F32), jax.ShapeDtypeStruct((S, C), BF16)],
        scratch_shapes=[pltpu.VMEM((ts + CONV_HALO, C), F32)],
        compiler_params=_params(1), name=name)(proj, proj, proj, proj, cw, cb, lg, lb)


def _conv_bwd(name, dmix, yc, proj, C, cw, lg, lb):
    S = proj.shape[0]
    ts, hb = _conv_tiles(S)
    n_t = S // ts

    def body(dy_ref, yc_ref, dyh_ref, ych_ref, a_ref, g_ref, ah_ref, gh_ref, cw_ref, lg_ref, lb_ref,
             dag_ref, dcw_ref, dcb_ref, dlg_ref, dlb_ref, ubuf, dbuf):
        i = pl.program_id(0)

        @pl.when(i == 0)
        def _():
            dcw_ref[...] = jnp.zeros_like(dcw_ref)
            dcb_ref[...] = jnp.zeros_like(dcb_ref)
            dlg_ref[...] = jnp.zeros_like(dlg_ref)
            dlb_ref[...] = jnp.zeros_like(dlb_ref)

        def ln_bwd(dy, ycv):
            yn, rs = _ln_stats(ycv)
            z = yn * lg_ref[...] + lb_ref[...]
            sg = _sigmoid(z)
            dz = dy * (sg * (1.0 + z * (1.0 - sg)))
            dyn = dz * lg_ref[...]
            dyc = rs * (dyn - jnp.mean(dyn, axis=-1, keepdims=True)
                        - yn * jnp.mean(dyn * yn, axis=-1, keepdims=True))
            return dyc, dz, yn

        dyc, dz, yn = ln_bwd(dy_ref[...], yc_ref[...])
        dlg_ref[...] += jnp.sum(dz * yn, axis=0, keepdims=True)
        dlb_ref[...] += jnp.sum(dz, axis=0, keepdims=True)
        dcb_ref[...] += jnp.sum(dyc, axis=0, keepdims=True)
        dych, _, _ = ln_bwd(dyh_ref[...], ych_ref[...])
        dbuf[pl.ds(0, ts), :] = dyc
        dbuf[pl.ds(ts, CONV_HALO), :] = jnp.where(i < n_t - 1, dych, 0.0)

        av = a_ref[...]
        sgm = _sigmoid(g_ref[...])
        uh = ah_ref[...] * _sigmoid(gh_ref[...])
        ubuf[pl.ds(0, CONV_HALO), :] = jnp.where(i > 0, uh, 0.0)
        ubuf[pl.ds(CONV_HALO, ts), :] = av * sgm

        du = jnp.zeros((ts, C), F32)
        for k in range(CONV_WIDTH):
            du = du + cw_ref[pl.ds(k, 1), :] * dbuf[pl.ds(CONV_WIDTH - 1 - k, ts), :]
            tap = ubuf[pl.ds(k + CONV_HALO - CONV_WIDTH + 1, ts), :]
            dcw_ref[pl.ds(k, 1), :] += jnp.sum(dyc * tap, axis=0, keepdims=True)
        dag_ref[:, pl.ds(0, C)] = (du * sgm).astype(BF16)
        dag_ref[:, pl.ds(C, C)] = (du * av * sgm * (1.0 - sgm)).astype(BF16)

    main = lambda col: pl.BlockSpec((ts, C), lambda i: (i, col))
    past = lambda col: pl.BlockSpec((CONV_HALO, C), lambda i: (jnp.maximum(i * hb - 1, 0), col))
    nxt = pl.BlockSpec((CONV_HALO, C), lambda i: (jnp.minimum((i + 1) * hb, n_t * hb - 1), 0))
    vec = pl.BlockSpec((1, C), lambda i: (0, 0))
    full = pl.BlockSpec((CONV_HALO, C), lambda i: (0, 0))
    vshape = jax.ShapeDtypeStruct((1, C), F32)
    return pl.pallas_call(
        body, grid=(n_t,),
        in_specs=[main(0), main(0), nxt, nxt, main(0), main(1), past(0), past(1), full, vec, vec],
        out_specs=[pl.BlockSpec((ts, 2 * C), lambda i: (i, 0)), full, vec, vec, vec],
        out_shape=[jax.ShapeDtypeStruct((S, 2 * C), BF16),
                   jax.ShapeDtypeStruct((CONV_HALO, C), F32), vshape, vshape, vshape],
        scratch_shapes=[pltpu.VMEM((ts + CONV_HALO, C), F32), pltpu.VMEM((ts + CONV_HALO, C), F32)],
        compiler_params=_params(1),
        name=name)(dmix, yc, dmix, yc, proj, proj, proj, proj, cw, lg, lb)


AUG = 3


def _gate_prep(name, proj, f_blk, fbias, n_pair):
    S = proj.shape[0]
    ts = _tile(S, 512)
    W = LANES * n_pair

    def body(pf_ref, fb_ref, ka_ref, carry):
        @pl.when(pl.program_id(0) == 0)
        def _():
            carry[...] = jnp.zeros_like(carry)

        f = pf_ref[...] + fb_ref[...]
        logf = jnp.minimum(f, 0.0) - jnp.log(1.0 + jnp.exp(-jnp.abs(f)))
        r = lax.broadcasted_iota(jnp.int32, (ts, ts), 0)
        c = lax.broadcasted_iota(jnp.int32, (ts, ts), 1)
        ltri = (c <= r).astype(BF16)
        hi, mid, lo = _split3(logf)
        cs = _dot(ltri, hi) + _dot(ltri, mid) + _dot(ltri, lo) + carry[...]
        carry[...] = cs[ts - 1:ts, :]
        hh = lax.broadcasted_iota(jnp.int32, (LANES, W), 0)
        ll = lax.broadcasted_iota(jnp.int32, (LANES, W), 1)
        pair, w = ll >> 7, ll & (LANES - 1)
        ka = jnp.zeros((ts, W), F32)
        for p, piece in enumerate(_split3(-cs)):
            e = (((w == HEAD_DIM + p) & (hh == 2 * pair)) | ((w == p) & (hh == 2 * pair + 1)))
            ka = ka + _dot(piece, e.astype(BF16))
        lw = lax.broadcasted_iota(jnp.int32, (1, W), 1) & (LANES - 1)
        ka = ka + ((lw == HEAD_DIM + AUG) | (lw == AUG)).astype(F32)
        ka_ref[...] = ka.astype(BF16)

    return pl.pallas_call(
        body, grid=(S // ts,),
        in_specs=[pl.BlockSpec((ts, LANES), lambda i: (i, f_blk)),
                  pl.BlockSpec((1, LANES), lambda i: (0, 0))],
        out_specs=pl.BlockSpec((ts, W), lambda i: (i, 0)),
        out_shape=jax.ShapeDtypeStruct((S, W), BF16),
        scratch_shapes=[pltpu.VMEM((1, LANES), F32)],
        compiler_params=_params(1), name=name)(proj, fbias)


def _gate_bwd(name, sp, rs, proj, f_blk, fbias, n_pair):
    S = proj.shape[0]
    ts = _tile(S, 512)
    n_t = S // ts
    W = LANES * n_pair

    def body(sp_ref, rs_ref, pf_ref, fb_ref, df_ref, dfb_ref, carry):
        @pl.when(pl.program_id(0) == 0)
        def _():
            carry[...] = jnp.zeros_like(carry)
            dfb_ref[...] = jnp.zeros_like(dfb_ref)

        ll = lax.broadcasted_iota(jnp.int32, (W, LANES), 0)
        hh = lax.broadcasted_iota(jnp.int32, (W, LANES), 1)
        pair, w = ll >> 7, ll & (LANES - 1)
        first, second = hh == 2 * pair, hh == 2 * pair + 1

        def pick(ref, lane_first, lane_second):
            sel = (((w == lane_first) & first) | ((w == lane_second) & second)).astype(BF16)
            hi, mid, lo = _split3(ref[...])
            return _dot(hi, sel) + _dot(mid, sel) + _dot(lo, sel)

        dc = pick(rs_ref, HEAD_DIM + AUG, AUG) - pick(sp_ref, HEAD_DIM, 0)
        r = lax.broadcasted_iota(jnp.int32, (ts, ts), 0)
        c = lax.broadcasted_iota(jnp.int32, (ts, ts), 1)
        utri = (c >= r).astype(BF16)
        hi, mid, lo = _split3(dc)
        dlogf = _dot(utri, hi) + _dot(utri, mid) + _dot(utri, lo) + carry[...]
        carry[...] = dlogf[0:1, :]
        f = pf_ref[...] + fb_ref[...]
        lane = lax.broadcasted_iota(jnp.int32, (ts, LANES), 1)
        df = jnp.where(lane < 2 * n_pair, dlogf * _sigmoid(-f), 0.0)
        df_ref[...] = df.astype(BF16)
        dfb_ref[...] += jnp.sum(df, axis=0, keepdims=True)

    rev = lambda blk: (lambda i: (n_t - 1 - i, blk))
    return pl.pallas_call(
        body, grid=(n_t,),
        in_specs=[pl.BlockSpec((ts, W), rev(0)), pl.BlockSpec((ts, W), rev(0)),
                  pl.BlockSpec((ts, LANES), rev(f_blk)), pl.BlockSpec((1, LANES), lambda i: (0, 0))],
        out_specs=[pl.BlockSpec((ts, LANES), rev(0)), pl.BlockSpec((1, LANES), lambda i: (0, 0))],
        out_shape=[jax.ShapeDtypeStruct((S, LANES), BF16), jax.ShapeDtypeStruct((1, LANES), F32)],
        scratch_shapes=[pltpu.VMEM((1, LANES), F32)],
        compiler_params=_params(1), name=name)(sp, rs, proj, fbias)


def _head_operands(h, lane, q2, k2, ka2):
    act = (lane < HEAD_DIM) if h == 0 else (lane >= HEAD_DIM)
    base = HEAD_DIM if h == 0 else 0
    ones = ((lane >= base) & (lane < base + AUG)).astype(F32)
    qa = jnp.where(act, q2 * (1.0 / math.sqrt(HEAD_DIM)), ones).astype(BF16)
    ka = jnp.where(act, k2.astype(BF16), ka2)
    return act, qa, ka


def _attn_fwd(name, proj, ka, q_blk, k_blk, v_blk, n_pair):
    S = proj.shape[0]
    tq = _tile(S, 512)
    n_t = S // tq
    W = LANES * n_pair

    def body(q_ref, k_ref, v_ref, ka_ref, o_ref, o32_ref, lse_ref, m_ref, l_ref, acc_ref):
        i, j = pl.program_id(1), pl.program_id(2)

        @pl.when(j == 0)
        def _():
            m_ref[...] = jnp.full_like(m_ref, NEG_INF)
            l_ref[...] = jnp.zeros_like(l_ref)
            acc_ref[...] = jnp.zeros_like(acc_ref)

        @pl.when(j <= i)
        def _():
            lane = lax.broadcasted_iota(jnp.int32, (tq, LANES), 1)
            row = lax.broadcasted_iota(jnp.int32, (tq, tq), 0)
            col = lax.broadcasted_iota(jnp.int32, (tq, tq), 1)
            visible = (j < i) | (row >= col)
            q2, k2, v2, ka2 = q_ref[...], k_ref[...], v_ref[...], ka_ref[...]
            for h in range(2):
                act, qa, kaug = _head_operands(h, lane, q2, k2, ka2)
                s = jnp.where(visible, _dot_nt(qa, kaug), NEG_INF)
                m_prev = m_ref[h]
                m_new = jnp.maximum(m_prev, jnp.max(s, axis=-1, keepdims=True))
                alpha = jnp.exp(m_prev - m_new)
                p = jnp.exp(s - m_new)
                l_ref[h] = alpha * l_ref[h] + jnp.sum(p, axis=-1, keepdims=True)
                vm = jnp.where(act, v2, 0.0).astype(BF16)
                acc_ref[h] = alpha * acc_ref[h] + _dot(p.astype(BF16), vm)
                m_ref[h] = m_new

        @pl.when(j == i)
        def _():
            lane = lax.broadcasted_iota(jnp.int32, (tq, LANES), 1)
            first = lane < HEAD_DIM
            out = jnp.where(first, acc_ref[0] / l_ref[0], acc_ref[1] / l_ref[1])
            o_ref[...] = out.astype(BF16)
            o32_ref[...] = out
            lse_ref[...] = jnp.where(first, m_ref[0] + jnp.log(l_ref[0]), m_ref[1] + jnp.log(l_ref[1]))

    qspec = lambda blk: pl.BlockSpec((tq, LANES), lambda p, i, j: (i, blk + p))
    kspec = lambda blk: pl.BlockSpec((tq, LANES), lambda p, i, j: (jnp.minimum(j, i), blk + p))
    out = pl.BlockSpec((tq, LANES), lambda p, i, j: (i, p))
    return pl.pallas_call(
        body, grid=(n_pair, n_t, n_t),
        in_specs=[qspec(q_blk), kspec(k_blk), kspec(v_blk), kspec(0)],
        out_specs=[out, out, out],
        out_shape=[jax.ShapeDtypeStruct((S, W), BF16), jax.ShapeDtypeStruct((S, W), F32),
                   jax.ShapeDtypeStruct((S, W), F32)],
        scratch_shapes=[pltpu.VMEM((2, tq, 1), F32), pltpu.VMEM((2, tq, 1), F32),
                        pltpu.VMEM((2, tq, LANES), F32)],
        compiler_params=_params(3), name=name)(proj, proj, proj, ka)


def _attn_bwd(name, proj, ka, o, lse, dmix, q_blk, k_blk, v_blk, do_blk, n_pair):
    S = proj.shape[0]
    tq = _tile(S, 512)
    n_t = S // tq
    W = LANES * n_pair
    scale = 1.0 / math.sqrt(HEAD_DIM)

    def body(q_ref, k_ref, v_ref, ka_ref, o_ref, lse_ref, do_ref,
             dq_ref, dk_ref, dv_ref, sp_ref, rs_ref, dk_acc, dv_acc):
        j, i = pl.program_id(1), pl.program_id(2)

        @pl.when((j == 0) & (i == 0))
        def _():
            dq_ref[...] = jnp.zeros_like(dq_ref)
            rs_ref[...] = jnp.zeros_like(rs_ref)

        @pl.when(i == 0)
        def _():
            dk_acc[...] = jnp.zeros_like(dk_acc)
            dv_acc[...] = jnp.zeros_like(dv_acc)

        @pl.when(i >= j)
        def _():
            lane = lax.broadcasted_iota(jnp.int32, (tq, LANES), 1)
            row = lax.broadcasted_iota(jnp.int32, (tq, tq), 0)
            col = lax.broadcasted_iota(jnp.int32, (tq, tq), 1)
            visible = (j < i) | (row >= col)
            q2, k2, v2, ka2 = q_ref[...], k_ref[...], v_ref[...], ka_ref[...]
            o2, do2, lse2 = o_ref[...], do_ref[...], lse_ref[...]
            dq = []
            for h in range(2):
                act, qa, kaug = _head_operands(h, lane, q2, k2, ka2)
                s = jnp.where(visible, _dot_nt(qa, kaug), NEG_INF)
                p = jnp.exp(s - lse2[:, h * HEAD_DIM:h * HEAD_DIM + 1])
                dom = jnp.where(act, do2, 0.0)
                delta = jnp.sum(dom * o2, axis=-1, keepdims=True)
                dob = dom.astype(BF16)
                dp = _dot_nt(dob, jnp.where(act, v2, 0.0).astype(BF16))
                ds = (p * (dp - delta)).astype(BF16)
                dv_acc[...] += _dot_tn(p.astype(BF16), dob)
                dk_acc[h] += _dot_tn(ds, qa)
                dq.append(_dot(ds, kaug))
            rows = pl.ds(pl.multiple_of(i * tq, tq), tq)
            first = lane < HEAD_DIM
            dq_ref[rows, :] += jnp.where(first, dq[0], dq[1])
            rs_ref[rows, :] += jnp.where(first, dq[1], dq[0])

        @pl.when(i == n_t - 1)
        def _():
            lane = lax.broadcasted_iota(jnp.int32, (tq, LANES), 1)
            first = lane < HEAD_DIM
            dk_ref[...] = jnp.where(first, dk_acc[0], dk_acc[1]).astype(BF16)
            sp_ref[...] = jnp.where(first, dk_acc[1], dk_acc[0])
            dv_ref[...] = dv_acc[...].astype(BF16)

        @pl.when((j == n_t - 1) & (i == n_t - 1))
        def _():
            dq_ref[...] = dq_ref[...] * scale

    qspec = lambda blk: pl.BlockSpec((tq, LANES), lambda p, j, i: (jnp.maximum(i, j), blk + p))
    kspec = lambda blk: pl.BlockSpec((tq, LANES), lambda p, j, i: (j, blk + p))
    kout = pl.BlockSpec((tq, LANES), lambda p, j, i: (j, p))
    qres = pl.BlockSpec((S, LANES), lambda p, j, i: (0, p))
    return pl.pallas_call(
        body, grid=(n_pair, n_t, n_t),
        in_specs=[qspec(q_blk), kspec(k_blk), kspec(v_blk), kspec(0),
                  qspec(0), qspec(0), qspec(do_blk)],
        out_specs=[qres, kout, kout, kout, qres],
        out_shape=[jax.ShapeDtypeStruct((S, W), F32), jax.ShapeDtypeStruct((S, W), BF16),
                   jax.ShapeDtypeStruct((S, W), BF16), jax.ShapeDtypeStruct((S, W), F32),
                   jax.ShapeDtypeStruct((S, W), F32)],
        scratch_shapes=[pltpu.VMEM((2, tq, LANES), F32), pltpu.VMEM((tq, LANES), F32)],
        compiler_params=_params(3), name=name)(proj, proj, proj, ka, o, lse, dmix)


def _ffn_block_grad(name, act, rows, scale, dep=None):
    nb, S, Fs = act.shape
    D = rows.shape[1]
    tm = _tile(S, 512)
    return _mm_tn(name, act, rows, (nb, Fs, D),
                  pl.BlockSpec((1, tm, Fs), lambda j, i: (j, i, 0)),
                  pl.BlockSpec((tm, D), lambda j, i: (i, 0)),
                  pl.BlockSpec((1, Fs, D), lambda j, i: (j, 0, 0)), (nb, S // tm), scale, BF16, dep=dep)


def kernel(x, ffn1_norm, ffn1_w_gate, ffn1_w_up, ffn1_w_down, mix_norm, w_in, fgate_bias, conv_w, conv_b, conv_ln_g, conv_ln_b, w_out, ffn2_norm, ffn2_w_gate, ffn2_w_up, ffn2_w_down, final_norm, loss_target, m_ffn1_norm, m_ffn1_w_gate, m_ffn1_w_up, m_ffn1_w_down, m_mix_norm, m_w_in, m_fgate_bias, m_conv_w, m_conv_b, m_conv_ln_g, m_conv_ln_b, m_w_out, m_ffn2_norm, m_ffn2_w_gate, m_ffn2_w_up, m_ffn2_w_down, m_final_norm, v_ffn1_norm, v_ffn1_w_gate, v_ffn1_w_up, v_ffn1_w_down, v_mix_norm, v_w_in, v_fgate_bias, v_conv_w, v_conv_b, v_conv_ln_g, v_conv_ln_b, v_w_out, v_ffn2_norm, v_ffn2_w_gate, v_ffn2_w_up, v_ffn2_w_down, v_final_norm):
    xs = x[0]
    S, D = xs.shape
    C = conv_b.shape[0]
    n_heads = fgate_bias.shape[0]
    FW = n_heads * HEAD_DIM
    n_pair = n_heads // 2
    MIX = C + FW
    in_shard = w_in.shape[1]
    in_cols = in_shard * N_DEV
    NP = -(-in_cols // 512) * 512
    q_blk, k_blk, v_blk = 2 * C // LANES, (2 * C + FW) // LANES, (2 * C + 2 * FW) // LANES
    f_blk = (2 * C + 3 * FW) // LANES
    assert C % LANES == 0 and FW % LANES == 0 and n_heads % 2 == 0 and n_heads <= LANES
    assert in_cols == 2 * C + 3 * FW + n_heads and MIX == w_out.shape[0] * N_DEV

    vec = lambda a: a.reshape(1, -1)
    bf = lambda a: a.astype(BF16)
    tm = _tile(S, 512)

    wgt1, wut1, wd1 = _all_gather_two_level("ag_ffn1", [bf(ffn1_w_gate).T, bf(ffn1_w_up).T, bf(ffn1_w_down)])
    ag, _ = _xchg_start("ag_start", [bf(w_in).T, conv_w, bf(w_out), bf(ffn2_w_gate).T, bf(ffn2_w_up).T,
                                     bf(ffn2_w_down)], False, dep=wd1)
    fbias = jnp.pad(vec(fgate_bias), ((0, 0), (0, LANES - n_heads)))

    x1, h1, G1, U1 = _ffn_fwd("ffn1_fwd", xs, vec(ffn1_norm), wgt1, wut1, wd1)
    win_g, cw_g = _xchg_wait("ag_wait_in", ag[0:2], False, x1)
    wint = jnp.pad(win_g.reshape(in_cols, D), ((0, NP - in_cols), (0, 0)))
    cw = jnp.pad(cw_g.transpose(1, 0, 2).reshape(CONV_WIDTH, C), ((0, CONV_HALO - CONV_WIDTH), (0, 0)))
    proj, h2 = _norm_mm_nt("proj_in", x1, vec(mix_norm), wint, 512)
    yc, y_conv = _conv_fwd("conv_fwd", proj, C, cw, vec(conv_b), vec(conv_ln_g), vec(conv_ln_b))
    ka = _gate_prep("gate_prep", proj, f_blk, fbias, n_pair)
    o, o32, lse = _attn_fwd("attn_fwd", proj, ka, q_blk, k_blk, v_blk, n_pair)
    (wout_g,) = _xchg_wait("ag_wait_out", ag[2:3], False, lse)
    wout = wout_g.reshape(MIX, D)
    mix = jnp.concatenate([y_conv, o], axis=1)
    x2 = _mm_res("proj_out", mix, wout, x1, 512)
    wgt2, wut2, wd2 = _xchg_wait("ag_wait_ffn2", ag[3:6], False, x2)
    x3, h3, G2, U2 = _ffn_fwd("ffn2_fwd", x2, vec(ffn2_norm), wgt2, wut2, wd2)

    dx3, d_final_norm, loss_part = _final("final", x3, vec(final_norm), loss_target[0])
    dG2, dU2, A2, dx2, d_ffn2_norm = _ffn_bwd_act("ffn2_bwd", dx3, x2, vec(ffn2_norm), G2, U2, wgt2, wut2, wd2)
    dwd2 = _ffn_block_grad("ffn2_dwd", A2, dx3, 0.5)
    s_d2, tok = _xchg_start("a2a_start_ffn2_wd", [dwd2], True)
    dwg2 = _ffn_block_grad("ffn2_dwg", dG2, h3, 1.0, dep=tok)
    s_g2, tok = _xchg_start("a2a_start_ffn2_wg", [dwg2], True)
    dwu2 = _ffn_block_grad("ffn2_dwu", dU2, h3, 1.0, dep=tok)
    s_u2, tok = _xchg_start("a2a_start_ffn2_wu", [dwu2], True)

    dmix = _mm_k("dmix", dx2, wout, 512, True, dep=tok)
    d_wout = _mm_tn("dwout", mix, dx2, (MIX, D),
                    pl.BlockSpec((tm, MIX), lambda j, i: (i, 0)), pl.BlockSpec((tm, 512), lambda j, i: (i, j)),
                    pl.BlockSpec((MIX, 512), lambda j, i: (0, j)), (D // 512, S // tm), 1.0, BF16)
    s_out, tok = _xchg_start("a2a_start_w_out", [d_wout.reshape(N_DEV, MIX // N_DEV, D)], True)
    dag, d_cw, d_cb, d_lg, d_lb = _conv_bwd("conv_bwd", dmix, yc, proj, C, cw, vec(conv_ln_g), vec(conv_ln_b))
    dq, dk, dv, sp, rs = _attn_bwd("attn_bwd", proj, ka, o32, lse, dmix, q_blk, k_blk, v_blk, C // LANES, n_pair)
    df, d_fb = _gate_bwd("gate_bwd", sp, rs, proj, f_blk, fbias, n_pair)
    dproj = jnp.concatenate([dag, bf(dq), dk, dv, df, jnp.zeros((S, NP - f_blk * LANES - LANES), BF16)], axis=1)
    d_wint = _mm_tn("dwin", dproj, h2, (NP, D),
                    pl.BlockSpec((tm, 512), lambda j, i: (i, j)), pl.BlockSpec((tm, D), lambda j, i: (i, 0)),
                    pl.BlockSpec((512, D), lambda j, i: (j, 0)), (NP // 512, S // tm), 1.0, BF16, dep=tok)
    s_in, tok = _xchg_start("a2a_start_w_in", [d_wint[:in_cols].reshape(N_DEV, in_shard, D)], True)
    dx1, d_mix_norm = _mm_k("dh2", dproj, wint, 512, False, norm_bwd=(x1, vec(mix_norm), dx2), dep=tok)
    dG1, dU1, A1, dx0, d_ffn1_norm = _ffn_bwd_act("ffn1_bwd", dx1, xs, vec(ffn1_norm), G1, U1, wgt1, wut1, wd1)

    rows = lambda a: a.reshape(-1, C)
    pad_row = lambda a: jnp.pad(a.reshape(1, -1), ((0, 0), (0, C - a.size)))
    pieces = [rows(d_ffn1_norm), rows(d_mix_norm), rows(d_ffn2_norm), rows(d_final_norm),
              d_cw[:CONV_WIDTH], d_cb, d_lg, d_lb, pad_row(d_fb[0, :n_heads]), pad_row(loss_part[0, :1])]
    pack = jnp.concatenate(pieces, axis=0)
    n_rows = pack.shape[0]
    pack = jnp.pad(pack, ((0, -n_rows % SUBLANES), (0, 0)))
    (pack_g,) = _exchange("ag_small", [pack], False)
    tot = _reduce_adam("sum_small", pack_g)
    nd = D // C
    g_ffn1_norm, g_mix_norm, g_ffn2_norm, g_final_norm = (tot[k * nd:(k + 1) * nd].reshape(D) for k in range(4))
    r0 = 4 * nd
    me = _lin(_mesh_pos())
    cs = C // N_DEV
    g_conv_w = lax.dynamic_slice(tot[r0:r0 + CONV_WIDTH], (0, me * cs), (CONV_WIDTH, cs))
    g_conv_b, g_ln_g, g_ln_b = tot[r0 + CONV_WIDTH], tot[r0 + CONV_WIDTH + 1], tot[r0 + CONV_WIDTH + 2]
    g_fb = tot[r0 + CONV_WIDTH + 3, :n_heads]
    loss = tot[r0 + CONV_WIDTH + 4, 0]

    small = [(g_ffn1_norm, ffn1_norm, m_ffn1_norm, v_ffn1_norm), (g_mix_norm, mix_norm, m_mix_norm, v_mix_norm),
             (g_fb, fgate_bias, m_fgate_bias, v_fgate_bias), (g_conv_w, conv_w, m_conv_w, v_conv_w),
             (g_conv_b, conv_b, m_conv_b, v_conv_b), (g_ln_g, conv_ln_g, m_conv_ln_g, v_conv_ln_g),
             (g_ln_b, conv_ln_b, m_conv_ln_b, v_conv_ln_b), (g_ffn2_norm, ffn2_norm, m_ffn2_norm, v_ffn2_norm),
             (g_final_norm, final_norm, m_final_norm, v_final_norm)]
    sizes = [g.size for g, _, _, _ in small]
    total = sum(sizes)
    padded = -(-total // (SUBLANES * LANES)) * (SUBLANES * LANES)

    def flat_pack(k, fill):
        flat = jnp.concatenate([t[k].reshape(-1) for t in small])
        return jnp.pad(flat, (0, padded - total), constant_values=fill).reshape(padded // LANES, LANES)

    sg, sd, sm, sv = _reduce_adam("adam_small", flat_pack(0, 0.0)[None], flat_pack(1, 0.0), flat_pack(2, 0.0),
                                  flat_pack(3, 1.0))

    def unpack(packed):
        flat = packed.reshape(-1)
        out, off = [], 0
        for (g, _, _, _), n in zip(small, sizes):
            out.append(flat[off:off + n].reshape(g.shape))
            off += n
        return out

    s_g, s_d, s_m, s_v = unpack(sg), unpack(sd), unpack(sm), unpack(sv)

    dwd1 = _ffn_block_grad("ffn1_dwd", A1, dx1, 0.5, dep=sd)
    s_d1, tok = _xchg_start("a2a_start_ffn1_wd", [dwd1], True)
    dwg1 = _ffn_block_grad("ffn1_dwg", dG1, h1, 1.0, dep=tok)
    s_g1, tok = _xchg_start("a2a_start_ffn1_wg", [dwg1], True)
    dwu1 = _ffn_block_grad("ffn1_dwu", dU1, h1, 1.0, dep=tok)
    s_u1, tok = _xchg_start("a2a_start_ffn1_wu", [dwu1], True)

    r_d2, r_g2, r_u2, r_out, r_in = _xchg_wait("a2a_wait_a", s_d2 + s_g2 + s_u2 + s_out + s_in, True, tok)
    tr = lambda a: a.T

    def adam_t(name, recv, w, m, v):
        return tuple(tr(r) for r in _reduce_adam(name, recv, tr(w), tr(m), tr(v)))

    res = {
        "ffn2_w_down": _reduce_adam("adam_ffn2_wd", r_d2, ffn2_w_down, m_ffn2_w_down, v_ffn2_w_down),
        "ffn2_w_gate": adam_t("adam_ffn2_wg", r_g2, ffn2_w_gate, m_ffn2_w_gate, v_ffn2_w_gate),
        "ffn2_w_up": adam_t("adam_ffn2_wu", r_u2, ffn2_w_up, m_ffn2_w_up, v_ffn2_w_up),
        "w_out": _reduce_adam("adam_w_out", r_out, w_out, m_w_out, v_w_out),
        "w_in": adam_t("adam_w_in", r_in, w_in, m_w_in, v_w_in),
    }
    (r_d1,) = _xchg_wait("a2a_wait_d1", s_d1, True, res["w_in"][0])
    res["ffn1_w_down"] = _reduce_adam("adam_ffn1_wd", r_d1, ffn1_w_down, m_ffn1_w_down, v_ffn1_w_down)
    (r_g1,) = _xchg_wait("a2a_wait_g1", s_g1, True, res["ffn1_w_down"][0])
    res["ffn1_w_gate"] = adam_t("adam_ffn1_wg", r_g1, ffn1_w_gate, m_ffn1_w_gate, v_ffn1_w_gate)
    (r_u1,) = _xchg_wait("a2a_wait_u1", s_u1, True, res["ffn1_w_gate"][0])
    res["ffn1_w_up"] = adam_t("adam_ffn1_wu", r_u1, ffn1_w_up, m_ffn1_w_up, v_ffn1_w_up)

    small_names = ["ffn1_norm", "mix_norm", "fgate_bias", "conv_w", "conv_b", "conv_ln_g", "conv_ln_b",
                   "ffn2_norm", "final_norm"]
    for k, n in enumerate(small_names):
        res[n] = (s_g[k], s_d[k], s_m[k], s_v[k])
    order = ["ffn1_norm", "ffn1_w_gate", "ffn1_w_up", "ffn1_w_down", "mix_norm", "w_in", "fgate_bias",
             "conv_w", "conv_b", "conv_ln_g", "conv_ln_b", "w_out", "ffn2_norm", "ffn2_w_gate", "ffn2_w_up",
             "ffn2_w_down", "final_norm"]
    outs = [loss, dx0[None]]
    for k in range(4):
        outs += [res[n][k] for n in order]
    return tuple(outs)
```

```python
import math

import jax
import jax.numpy as jnp
from jax import lax
from jax.experimental import pallas as pl
from jax.experimental.pallas import tpu as pltpu

F32 = jnp.float32
BF16 = jnp.bfloat16

N_DEV = 8
MESH_ID = pl.DeviceIdType.MESH
HEAD_DIM = 64
CONV_WIDTH = 31
CONV_HALO = 32
NORM_EPS = 1e-6
LN_EPS = 1e-5
NEG_INF = -1e30
LANES = 128
SUBLANES = 8
V7X_VMEM_LIMIT = 52 * 1024 * 1024

ADAM_LR = 0.001
ADAM_B1 = 0.9
ADAM_B2 = 0.999
ADAM_EPS = 1e-08
ADAM_WD = 0.01
ADAM_STEP = 10


def _params(n_grid_axes):
    return pltpu.CompilerParams(dimension_semantics=("arbitrary",) * n_grid_axes,
                                vmem_limit_bytes=V7X_VMEM_LIMIT)


def _tile(n, pref, mult=8):
    t = min(pref, n)
    while t >= mult:
        if n % t == 0 and t % mult == 0:
            return t
        t -= mult
    return n


def _dot(a, b):
    return jnp.dot(a, b, preferred_element_type=F32)


def _dot_nt(a, b):
    return lax.dot_general(a, b, (((1,), (1,)), ((), ())), preferred_element_type=F32)


def _dot_tn(a, b):
    return lax.dot_general(a, b, (((0,), (0,)), ((), ())), preferred_element_type=F32)


def _sigmoid(x):
    return 1.0 / (1.0 + jnp.exp(-x))


def _rms_fwd(x, g):
    r = lax.rsqrt(jnp.mean(x * x, axis=-1, keepdims=True) + NORM_EPS)
    return x * r * g


def _rms_bwd(dh, x, g):
    r = lax.rsqrt(jnp.mean(x * x, axis=-1, keepdims=True) + NORM_EPS)
    xh = x * r
    dxh = dh * g
    dx = r * (dxh - xh * jnp.mean(dxh * xh, axis=-1, keepdims=True))
    return dx, jnp.sum(dh * xh, axis=0, keepdims=True)


def _split3(x):
    hi = x.astype(BF16)
    r = x - hi.astype(F32)
    mid = r.astype(BF16)
    lo = (r - mid.astype(F32)).astype(BF16)
    return hi, mid, lo


def _blk(ref):
    return ref[0] if len(ref.shape) == 3 else ref[...]


_DEP_SPEC = pl.BlockSpec(memory_space=pl.ANY)


def _mesh_pos():
    return lax.axis_index("x"), lax.axis_index("y"), lax.axis_index("c")


def _peer(pos, k):
    x, y, c = pos
    return (1 - x if k & 4 else x, 1 - y if k & 2 else y, 1 - c if k & 1 else c)


def _lin(pos):
    x, y, c = pos
    return 4 * x + 2 * y + c


def _remote_copy(src, land, send_sems, recv_sems, pos, k, all_to_all):
    peer = _peer(pos, k)
    return pltpu.make_async_remote_copy(
        src_ref=src.at[_lin(peer)] if all_to_all else src, dst_ref=land.at[_lin(pos)],
        send_sem=send_sems.at[k - 1], recv_sem=recv_sems.at[k - 1],
        device_id=peer, device_id_type=MESH_ID)


def _landing_shape(a, all_to_all):
    return a.shape if all_to_all else (N_DEV,) + a.shape


def _exchange(name, arrays, all_to_all):
    n = len(arrays)
    out_shapes = [jax.ShapeDtypeStruct(_landing_shape(a, all_to_all), a.dtype) for a in arrays]

    def body(*refs):
        ins, outs = refs[:n], refs[n:2 * n]
        send_sems, recv_sems, local_sems = refs[2 * n:]
        pos = _mesh_pos()
        me = _lin(pos)
        local = []
        for a in range(n):
            src = ins[a].at[me] if all_to_all else ins[a]
            cp = pltpu.make_async_copy(src, outs[a].at[me], local_sems.at[a])
            cp.start()
            local.append(cp)
        remote = [_remote_copy(ins[a], outs[a], send_sems.at[a], recv_sems.at[a], pos, k, all_to_all)
                  for a in range(n) for k in range(1, N_DEV)]
        for cp in remote:
            cp.start()
        for cp in remote:
            cp.wait()
        for cp in local:
            cp.wait()

    any_spec = pl.BlockSpec(memory_space=pl.ANY)
    return pl.pallas_call(
        body, out_shape=out_shapes, in_specs=[any_spec] * n, out_specs=[any_spec] * n,
        scratch_shapes=[pltpu.SemaphoreType.DMA((n, N_DEV - 1)),
                        pltpu.SemaphoreType.DMA((n, N_DEV - 1)),
                        pltpu.SemaphoreType.DMA((n,))],
        name=name)(*arrays)


def _all_gather_two_level(name, shards):
    n = len(shards)
    out_shapes = [jax.ShapeDtypeStruct((N_DEV,) + a.shape, a.dtype) for a in shards]

    def body(*refs):
        ins, outs = refs[:n], refs[n:2 * n]
        send_sems, recv_sems, local_sems = refs[2 * n:]
        x, y, c = pos = _mesh_pos()
        sibling = (x, y, 1 - c)
        chips = [(1 - x, y), (x, 1 - y), (1 - x, 1 - y)]

        def copy(a, k, block, to, src=None):
            slot = outs[a].at[_lin(block)]
            return pltpu.make_async_remote_copy(
                src_ref=slot if src is None else src, dst_ref=slot,
                send_sem=send_sems.at[a, k], recv_sem=recv_sems.at[a, k],
                device_id=to, device_id_type=MESH_ID)

        local = [pltpu.make_async_copy(ins[a], outs[a].at[_lin(pos)], local_sems.at[a]) for a in range(n)]
        first = [copy(a, 1 + j, pos, (*chip, c), src=ins[a]) for j, chip in enumerate(chips) for a in range(n)]
        first += [copy(a, 0, pos, sibling, src=ins[a]) for a in range(n)]
        for cp in first + local:
            cp.start()
        passed = []
        for j, chip in enumerate(chips):
            for a in range(n):
                copy(a, 1 + j, (*chip, c), pos).wait_recv()
                cp = copy(a, 4 + j, (*chip, c), sibling)
                cp.start()
                passed.append(cp)
        for a in range(n):
            copy(a, 0, sibling, pos).wait_recv()
        for j, chip in enumerate(chips):
            for a in range(n):
                copy(a, 4 + j, (*chip, 1 - c), pos).wait_recv()
        for cp in first + passed:
            cp.wait_send()
        for cp in local:
            cp.wait()

    any_spec = pl.BlockSpec(memory_space=pl.ANY)
    return pl.pallas_call(
        body, out_shape=out_shapes, in_specs=[any_spec] * n, out_specs=[any_spec] * n,
        scratch_shapes=[pltpu.SemaphoreType.DMA((n, N_DEV - 1)),
                        pltpu.SemaphoreType.DMA((n, N_DEV - 1)),
                        pltpu.SemaphoreType.DMA((n,))],
        name=name)(*shards)


_HBM_SPEC = pl.BlockSpec(memory_space=pltpu.HBM)
_SEM_SPEC = pl.BlockSpec(memory_space=pltpu.SEMAPHORE)
_SIDE_EFFECT = pltpu.SideEffectType.DATAFLOW_SIDE_EFFECTING


def _xchg_start(name, arrays, all_to_all, dep=None):
    n = len(arrays)
    me = _lin(_mesh_pos())
    lands = [lax.dynamic_update_index_in_dim(
        lax.empty(_landing_shape(a, all_to_all), a.dtype),
        lax.dynamic_index_in_dim(a, me, 0, keepdims=False) if all_to_all else a, me, 0) for a in arrays]

    deps = [] if dep is None else [dep]

    def body(*refs):
        srcs, lnds = refs[:n], refs[n:2 * n]
        outs = refs[2 * n + len(deps):]
        send, recv = outs[:n], outs[n:2 * n]
        token = outs[4 * n]
        pos = _mesh_pos()
        for a in range(n):
            for k in range(1, N_DEV):
                _remote_copy(srcs[a], lnds[a], send[a], recv[a], pos, k, all_to_all).start()
        token[...] = jnp.zeros_like(token)

    hbm = lambda a: pltpu.HBM(a.shape, a.dtype)
    sems = [pltpu.SemaphoreType.DMA((N_DEV - 1,))] * (2 * n)
    res = pl.pallas_call(
        body, name=name,
        out_shape=sems + [hbm(a) for a in arrays] + [hbm(l) for l in lands]
        + [jax.ShapeDtypeStruct((SUBLANES, LANES), F32)],
        in_specs=[_HBM_SPEC] * (2 * n) + [_DEP_SPEC] * len(deps),
        out_specs=[_SEM_SPEC] * (2 * n) + [_HBM_SPEC] * (2 * n) + [pl.BlockSpec(memory_space=pltpu.VMEM)],
        input_output_aliases={a: 2 * n + a for a in range(2 * n)},
        compiler_params=pltpu.CompilerParams(has_side_effects=_SIDE_EFFECT),
    )(*[pltpu.with_memory_space_constraint(a, pltpu.HBM) for a in list(arrays) + lands], *deps)
    per_array = [(res[a], res[n + a], res[2 * n + a], res[3 * n + a]) for a in range(n)]
    return per_array, res[4 * n]


def _xchg_wait(name, started, all_to_all, after):
    n = len(started)

    def body(*refs):
        srcs, lnds, send, recv = refs[:n], refs[n:2 * n], refs[2 * n:3 * n], refs[3 * n:4 * n]
        pos = _mesh_pos()
        for a in range(n):
            for k in range(1, N_DEV):
                cp = _remote_copy(srcs[a], lnds[a], send[a], recv[a], pos, k, all_to_all)
                cp.wait_send()
                cp.wait_recv()

    hbm = lambda a: pltpu.HBM(a.shape, a.dtype)
    srcs = [s[2] for s in started]
    lands = [s[3] for s in started]
    res = pl.pallas_call(
        body, name=name,
        out_shape=[hbm(a) for a in srcs + lands],
        in_specs=[_HBM_SPEC] * (2 * n) + [_SEM_SPEC] * (2 * n) + [_DEP_SPEC],
        out_specs=[_HBM_SPEC] * (2 * n),
        input_output_aliases={a: a for a in range(2 * n)},
        compiler_params=pltpu.CompilerParams(has_side_effects=_SIDE_EFFECT),
    )(*srcs, *lands, *[s[0] for s in started], *[s[1] for s in started], after)
    return list(res[n:])


def _reduce_adam(name, parts, w=None, m=None, v=None):
    n, R, C = parts.shape
    tr = _tile(R, 256)
    do_adam = w is not None
    bc1 = 1.0 - ADAM_B1 ** ADAM_STEP
    bc2 = 1.0 - ADAM_B2 ** ADAM_STEP

    def body(*refs):
        p_ref = refs[0]
        g = p_ref[0].astype(F32)
        for d in range(1, n):
            g = g + p_ref[d].astype(F32)
        if not do_adam:
            refs[1][...] = g
            return
        w_ref, m_ref, v_ref, g_ref, d_ref, nm_ref, nv_ref = refs[1:]
        g_ref[...] = g
        nm = ADAM_B1 * m_ref[...] + (1.0 - ADAM_B1) * g
        nv = ADAM_B2 * v_ref[...] + (1.0 - ADAM_B2) * (g * g)
        m_hat = nm / bc1
        v_hat = nv / bc2
        d_ref[...] = -ADAM_LR * (m_hat / (jnp.sqrt(v_hat) + ADAM_EPS) + ADAM_WD * w_ref[...])
        nm_ref[...] = nm
        nv_ref[...] = nv

    tc = _tile(C, 512, LANES) if tr == R and R > 256 else C
    row = pl.BlockSpec((tr, tc), lambda i, j: (i, j))
    part = pl.BlockSpec((n, tr, tc), lambda i, j: (0, i, j))
    shard = jax.ShapeDtypeStruct((R, C), F32)
    grid = (R // tr, C // tc)
    if do_adam:
        return pl.pallas_call(body, grid=grid, in_specs=[part, row, row, row],
                              out_specs=[row] * 4, out_shape=[shard] * 4,
                              compiler_params=_params(2), name=name)(parts, w, m, v)
    return pl.pallas_call(body, grid=grid, in_specs=[part], out_specs=row, out_shape=shard,
                          compiler_params=_params(2), name=name)(parts)


def _ffn_fwd(name, x, gain, wgt, wut, wd):
    S, D = x.shape
    nb, Fs, _ = wgt.shape
    tm = _tile(S, 512)

    def body(x_ref, g_ref, wg_ref, wu_ref, wd_ref, xo_ref, h_ref, G_ref, U_ref, acc_ref):
        j = pl.program_id(1)

        @pl.when(j == 0)
        def _():
            h_ref[...] = _rms_fwd(x_ref[...], g_ref[...]).astype(BF16)
            acc_ref[...] = jnp.zeros_like(acc_ref)

        h = h_ref[...]
        G = _dot_nt(h, wg_ref[0])
        U = _dot_nt(h, wu_ref[0])
        G_ref[0] = G
        U_ref[0] = U
        a = G * _sigmoid(G) * U
        acc_ref[...] += _dot(a.astype(BF16), wd_ref[0])

        @pl.when(j == nb - 1)
        def _():
            xo_ref[...] = x_ref[...] + 0.5 * acc_ref[...]

    row = pl.BlockSpec((tm, D), lambda i, j: (i, 0))
    act = pl.BlockSpec((1, tm, Fs), lambda i, j: (j, i, 0))
    wblk = pl.BlockSpec((1, Fs, D), lambda i, j: (j, 0, 0))
    return pl.pallas_call(
        body, grid=(S // tm, nb),
        in_specs=[row, pl.BlockSpec((1, D), lambda i, j: (0, 0)), wblk, wblk, wblk],
        out_specs=[row, row, act, act],
        out_shape=[jax.ShapeDtypeStruct((S, D), F32), jax.ShapeDtypeStruct((S, D), BF16),
                   jax.ShapeDtypeStruct((nb, S, Fs), F32), jax.ShapeDtypeStruct((nb, S, Fs), F32)],
        scratch_shapes=[pltpu.VMEM((tm, D), F32)],
        compiler_params=_params(2), name=name)(x, gain, wgt, wut, wd)


def _ffn_bwd_act(name, dxo, x_in, gain, G, U, wgt, wut, wd):
    S, D = x_in.shape
    nb, Fs, _ = wgt.shape
    tm = _tile(S, 512)
    rows_per_chunk = _tile(tm, 128)

    def gate_body(dxo_ref, G_ref, U_ref, wd_ref, dG_ref, dU_ref, A_ref, dxb_ref):
        @pl.when(pl.program_id(1) == 0)
        def _():
            dxb_ref[...] = dxo_ref[...].astype(BF16)

        dA = 0.5 * _dot_nt(dxb_ref[...], wd_ref[0])
        Gv = G_ref[0]
        Uv = U_ref[0]
        sg = _sigmoid(Gv)
        sl = Gv * sg
        dG_ref[0] = (dA * Uv * (sg * (1.0 + Gv * (1.0 - sg)))).astype(BF16)
        dU_ref[0] = (dA * sl).astype(BF16)
        A_ref[0] = (sl * Uv).astype(BF16)

    def in_body(dG_ref, dU_ref, wg_ref, wu_ref, dxo_ref, x_ref, g_ref, dx_ref, dgain_ref, acc_ref):
        i, j = pl.program_id(0), pl.program_id(1)

        @pl.when(j == 0)
        def _():
            acc_ref[...] = jnp.zeros_like(acc_ref)

        @pl.when((i == 0) & (j == 0))
        def _():
            dgain_ref[...] = jnp.zeros_like(dgain_ref)

        acc_ref[...] += _dot(dG_ref[0], wg_ref[0]) + _dot(dU_ref[0], wu_ref[0])

        @pl.when(j == nb - 1)
        def _():
            def chunk(r, dg_sum):
                rows = pl.ds(pl.multiple_of(r * rows_per_chunk, rows_per_chunk), rows_per_chunk)
                dx, dg = _rms_bwd(acc_ref[rows, :], x_ref[rows, :], g_ref[...])
                dx_ref[rows, :] = dxo_ref[rows, :] + dx
                return dg_sum + dg

            dgain_ref[...] += lax.fori_loop(0, tm // rows_per_chunk, chunk, jnp.zeros((1, D), F32))

    row = pl.BlockSpec((tm, D), lambda i, j: (i, 0))
    vec = pl.BlockSpec((1, D), lambda i, j: (0, 0))
    act = pl.BlockSpec((1, tm, Fs), lambda i, j: (j, i, 0))
    wblk = pl.BlockSpec((1, Fs, D), lambda i, j: (j, 0, 0))
    act_shape = jax.ShapeDtypeStruct((nb, S, Fs), BF16)
    dG, dU, A = pl.pallas_call(
        gate_body, grid=(S // tm, nb), in_specs=[row, act, act, wblk], out_specs=[act, act, act],
        out_shape=[act_shape, act_shape, act_shape],
        scratch_shapes=[pltpu.VMEM((tm, D), BF16)],
        compiler_params=_params(2), name=name + "_gate")(dxo, G, U, wd)
    dx, dgain = pl.pallas_call(
        in_body, grid=(S // tm, nb), in_specs=[act, act, wblk, wblk, row, row, vec],
        out_specs=[row, vec],
        out_shape=[jax.ShapeDtypeStruct((S, D), F32), jax.ShapeDtypeStruct((1, D), F32)],
        scratch_shapes=[pltpu.VMEM((tm, D), F32)],
        compiler_params=_params(2), name=name + "_in")(dG, dU, wgt, wut, dxo, x_in, gain)
    return dG, dU, A, dx, dgain


def _mm_tn(name, lhs, rhs, out_shape, lhs_spec, rhs_spec, out_spec, grid, scale, out_dtype, dep=None):
    acc_shape = tuple(out_spec.block_shape[-2:])
    n_red = grid[-1]

    def body(l_ref, r_ref, *rest):
        o_ref, acc_ref = rest[-2:]
        i = pl.program_id(len(grid) - 1)

        @pl.when(i == 0)
        def _():
            acc_ref[...] = jnp.zeros_like(acc_ref)

        acc_ref[...] += _dot_tn(_blk(l_ref).astype(BF16), _blk(r_ref).astype(BF16))

        @pl.when(i == n_red - 1)
        def _():
            res = (scale * acc_ref[...]).astype(out_dtype)
            if len(o_ref.shape) == 3:
                o_ref[0] = res
            else:
                o_ref[...] = res

    deps = [] if dep is None else [dep]
    return pl.pallas_call(
        body, grid=grid, in_specs=[lhs_spec, rhs_spec] + [_DEP_SPEC] * len(deps), out_specs=out_spec,
        out_shape=jax.ShapeDtypeStruct(out_shape, out_dtype),
        scratch_shapes=[pltpu.VMEM(acc_shape, F32)],
        compiler_params=_params(len(grid)), name=name)(lhs, rhs, *deps)


def _norm_mm_nt(name, x, gain, wt, tn):
    S, D = x.shape
    N = wt.shape[0]
    tm = _tile(S, 512)

    def body(x_ref, g_ref, w_ref, o_ref, h_ref):
        @pl.when(pl.program_id(1) == 0)
        def _():
            h_ref[...] = _rms_fwd(x_ref[...], g_ref[...]).astype(BF16)

        o_ref[...] = _dot_nt(h_ref[...], w_ref[...])

    row = pl.BlockSpec((tm, D), lambda i, j: (i, 0))
    return pl.pallas_call(
        body, grid=(S // tm, N // tn),
        in_specs=[row, pl.BlockSpec((1, D), lambda i, j: (0, 0)),
                  pl.BlockSpec((tn, D), lambda i, j: (j, 0))],
        out_specs=[pl.BlockSpec((tm, tn), lambda i, j: (i, j)), row],
        out_shape=[jax.ShapeDtypeStruct((S, N), F32), jax.ShapeDtypeStruct((S, D), BF16)],
        compiler_params=_params(2), name=name)(x, gain, wt)


def _mm_res(name, a, w, res, tn):
    S, K = a.shape
    N = w.shape[1]
    tm = _tile(S, 512)

    def body(a_ref, w_ref, r_ref, o_ref):
        o_ref[...] = r_ref[...] + _dot(a_ref[...], w_ref[...])

    tile = pl.BlockSpec((tm, tn), lambda i, j: (i, j))
    return pl.pallas_call(
        body, grid=(S // tm, N // tn),
        in_specs=[pl.BlockSpec((tm, K), lambda i, j: (i, 0)),
                  pl.BlockSpec((K, tn), lambda i, j: (0, j)), tile],
        out_specs=tile, out_shape=jax.ShapeDtypeStruct((S, N), F32),
        compiler_params=_params(2), name=name)(a, w, res)


def _mm_k(name, a, b, tk, transpose_b, norm_bwd=None, dep=None):
    S, K = a.shape
    N = b.shape[0] if transpose_b else b.shape[1]
    tm = _tile(S, 512)
    nk = K // tk
    n_extra = 0 if norm_bwd is None else 3
    deps = [] if dep is None else [dep]

    def body(*refs):
        a_ref, b_ref = refs[:2]
        outs = refs[2 + n_extra + len(deps):]
        acc_ref = outs[-1]
        i, k = pl.program_id(0), pl.program_id(1)

        @pl.when(k == 0)
        def _():
            acc_ref[...] = jnp.zeros_like(acc_ref)

        av = a_ref[...].astype(BF16)
        acc_ref[...] += _dot_nt(av, b_ref[...]) if transpose_b else _dot(av, b_ref[...])

        if norm_bwd is None:
            @pl.when(k == nk - 1)
            def _():
                outs[0][...] = acc_ref[...]
        else:
            x_ref, g_ref, dres_ref = refs[2:5]
            o_ref, dgain_ref = outs[:2]

            @pl.when((i == 0) & (k == 0))
            def _():
                dgain_ref[...] = jnp.zeros_like(dgain_ref)

            @pl.when(k == nk - 1)
            def _():
                dx, dg = _rms_bwd(acc_ref[...], x_ref[...], g_ref[...])
                o_ref[...] = dres_ref[...] + dx
                dgain_ref[...] += dg

    a_spec = pl.BlockSpec((tm, tk), lambda i, k: (i, k))
    b_spec = (pl.BlockSpec((N, tk), lambda i, k: (0, k)) if transpose_b
              else pl.BlockSpec((tk, N), lambda i, k: (k, 0)))
    row = pl.BlockSpec((tm, N), lambda i, k: (i, 0))
    vec = pl.BlockSpec((1, N), lambda i, k: (0, 0))
    out = jax.ShapeDtypeStruct((S, N), F32)
    scratch = [pltpu.VMEM((tm, N), F32)]
    dep_specs = [_DEP_SPEC] * len(deps)
    if norm_bwd is None:
        return pl.pallas_call(body, grid=(S // tm, nk), in_specs=[a_spec, b_spec] + dep_specs,
                              out_specs=row, out_shape=out, scratch_shapes=scratch,
                              compiler_params=_params(2), name=name)(a, b, *deps)
    x_in, gain, dres = norm_bwd
    return pl.pallas_call(body, grid=(S // tm, nk), in_specs=[a_spec, b_spec, row, vec, row] + dep_specs,
                          out_specs=[row, vec],
                          out_shape=[out, jax.ShapeDtypeStruct((1, N), F32)],
                          scratch_shapes=scratch,
                          compiler_params=_params(2), name=name)(a, b, x_in, gain, dres, *deps)


def _final(name, x, gain, target):
    S, D = x.shape
    tm = _tile(S, 512)

    def body(x_ref, g_ref, t_ref, dx_ref, dgain_ref, loss_ref):
        @pl.when(pl.program_id(0) == 0)
        def _():
            dgain_ref[...] = jnp.zeros_like(dgain_ref)
            loss_ref[...] = jnp.zeros_like(loss_ref)

        xv = x_ref[...]
        err = _rms_fwd(xv, g_ref[...]) - t_ref[...]
        per_tok = jnp.mean(err * err, axis=-1, keepdims=True)
        loss_ref[...] += 0.5 * jnp.sum(per_tok, axis=0, keepdims=True)
        dx, dg = _rms_bwd(err * (1.0 / D), xv, g_ref[...])
        dx_ref[...] = dx
        dgain_ref[...] += dg

    row = pl.BlockSpec((tm, D), lambda i: (i, 0))
    vec = pl.BlockSpec((1, D), lambda i: (0, 0))
    return pl.pallas_call(
        body, grid=(S // tm,), in_specs=[row, vec, row],
        out_specs=[row, vec, pl.BlockSpec((1, LANES), lambda i: (0, 0))],
        out_shape=[jax.ShapeDtypeStruct((S, D), F32), jax.ShapeDtypeStruct((1, D), F32),
                   jax.ShapeDtypeStruct((1, LANES), F32)],
        compiler_params=_params(1), name=name)(x, gain, target)


def _conv_tiles(S):
    ts = _tile(S, 256, CONV_HALO)
    return ts, ts // CONV_HALO


def _ln_stats(yc):
    mu = jnp.mean(yc, axis=-1, keepdims=True)
    d = yc - mu
    rs = lax.rsqrt(jnp.mean(d * d, axis=-1, keepdims=True) + LN_EPS)
    return d * rs, rs


def _conv_fwd(name, proj, C, cw, cb, lg, lb):
    S = proj.shape[0]
    ts, hb = _conv_tiles(S)

    def body(a_ref, g_ref, ah_ref, gh_ref, cw_ref, cb_ref, lg_ref, lb_ref, yc_ref, y_ref, ubuf):
        i = pl.program_id(0)
        uh = ah_ref[...] * _sigmoid(gh_ref[...])
        ubuf[pl.ds(0, CONV_HALO), :] = jnp.where(i > 0, uh, 0.0)
        ubuf[pl.ds(CONV_HALO, ts), :] = a_ref[...] * _sigmoid(g_ref[...])
        acc = jnp.zeros((ts, C), F32)
        for k in range(CONV_WIDTH):
            acc = acc + cw_ref[pl.ds(k, 1), :] * ubuf[pl.ds(k + CONV_HALO - CONV_WIDTH + 1, ts), :]
        yc = acc + cb_ref[...]
        yc_ref[...] = yc
        yn, _ = _ln_stats(yc)
        z = yn * lg_ref[...] + lb_ref[...]
        y_ref[...] = (z * _sigmoid(z)).astype(BF16)

    main = lambda col: pl.BlockSpec((ts, C), lambda i: (i, col))
    halo = lambda col: pl.BlockSpec((CONV_HALO, C), lambda i: (jnp.maximum(i * hb - 1, 0), col))
    vec = pl.BlockSpec((1, C), lambda i: (0, 0))
    return pl.pallas_call(
        body, grid=(S // ts,),
        in_specs=[main(0), main(1), halo(0), halo(1),
                  pl.BlockSpec((CONV_HALO, C), lambda i: (0, 0)), vec, vec, vec],
        out_specs=[pl.BlockSpec((ts, C), lambda i: (i, 0))] * 2,
        out_shape=[jax.ShapeDtypeStruct((S, C), F32), jax.ShapeDtypeStruct((S, C), BF16)],
        scratch_shapes=[pltpu.VMEM((ts + CONV_HALO, C), F32)],
        compiler_params=_params(1), name=name)(proj, proj, proj, proj, cw, cb, lg, lb)


def _conv_bwd(name, dmix, yc, proj, C, cw, lg, lb):
    S = proj.shape[0]
    ts, hb = _conv_tiles(S)
    n_t = S // ts

    def body(dy_ref, yc_ref, dyh_ref, ych_ref, a_ref, g_ref, ah_ref, gh_ref, cw_ref, lg_ref, lb_ref,
             dag_ref, dcw_ref, dcb_ref, dlg_ref, dlb_ref, ubuf, dbuf):
        i = pl.program_id(0)

        @pl.when(i == 0)
        def _():
            dcw_ref[...] = jnp.zeros_like(dcw_ref)
            dcb_ref[...] = jnp.zeros_like(dcb_ref)
            dlg_ref[...] = jnp.zeros_like(dlg_ref)
            dlb_ref[...] = jnp.zeros_like(dlb_ref)

        def ln_bwd(dy, ycv):
            yn, rs = _ln_stats(ycv)
            z = yn * lg_ref[...] + lb_ref[...]
            sg = _sigmoid(z)
            dz = dy * (sg * (1.0 + z * (1.0 - sg)))
            dyn = dz * lg_ref[...]
            dyc = rs * (dyn - jnp.mean(dyn, axis=-1, keepdims=True)
                        - yn * jnp.mean(dyn * yn, axis=-1, keepdims=True))
            return dyc, dz, yn

        dyc, dz, yn = ln_bwd(dy_ref[...], yc_ref[...])
        dlg_ref[...] += jnp.sum(dz * yn, axis=0, keepdims=True)
        dlb_ref[...] += jnp.sum(dz, axis=0, keepdims=True)
        dcb_ref[...] += jnp.sum(dyc, axis=0, keepdims=True)
        dych, _, _ = ln_bwd(dyh_ref[...], ych_ref[...])
        dbuf[pl.ds(0, ts), :] = dyc
        dbuf[pl.ds(ts, CONV_HALO), :] = jnp.where(i < n_t - 1, dych, 0.0)

        av = a_ref[...]
        sgm = _sigmoid(g_ref[...])
        uh = ah_ref[...] * _sigmoid(gh_ref[...])
        ubuf[pl.ds(0, CONV_HALO), :] = jnp.where(i > 0, uh, 0.0)
        ubuf[pl.ds(CONV_HALO, ts), :] = av * sgm

        du = jnp.zeros((ts, C), F32)
        for k in range(CONV_WIDTH):
            du = du + cw_ref[pl.ds(k, 1), :] * dbuf[pl.ds(CONV_WIDTH - 1 - k, ts), :]
            tap = ubuf[pl.ds(k + CONV_HALO - CONV_WIDTH + 1, ts), :]
            dcw_ref[pl.ds(k, 1), :] += jnp.sum(dyc * tap, axis=0, keepdims=True)
        dag_ref[:, pl.ds(0, C)] = (du * sgm).astype(BF16)
        dag_ref[:, pl.ds(C, C)] = (du * av * sgm * (1.0 - sgm)).astype(BF16)

    main = lambda col: pl.BlockSpec((ts, C), lambda i: (i, col))
    past = lambda col: pl.BlockSpec((CONV_HALO, C), lambda i: (jnp.maximum(i * hb - 1, 0), col))
    nxt = pl.BlockSpec((CONV_HALO, C), lambda i: (jnp.minimum((i + 1) * hb, n_t * hb - 1), 0))
    vec = pl.BlockSpec((1, C), lambda i: (0, 0))
    full = pl.BlockSpec((CONV_HALO, C), lambda i: (0, 0))
    vshape = jax.ShapeDtypeStruct((1, C), F32)
    return pl.pallas_call(
        body, grid=(n_t,),
        in_specs=[main(0), main(0), nxt, nxt, main(0), main(1), past(0), past(1), full, vec, vec],
        out_specs=[pl.BlockSpec((ts, 2 * C), lambda i: (i, 0)), full, vec, vec, vec],
        out_shape=[jax.ShapeDtypeStruct((S, 2 * C), BF16),
                   jax.ShapeDtypeStruct((CONV_HALO, C), F32), vshape, vshape, vshape],
        scratch_shapes=[pltpu.VMEM((ts + CONV_HALO, C), F32), pltpu.VMEM((ts + CONV_HALO, C), F32)],
        compiler_params=_params(1),
        name=name)(dmix, yc, dmix, yc, proj, proj, proj, proj, cw, lg, lb)


AUG = 3
ROW_CHUNK = 32


def _gate_prep(name, proj, f_blk, fbias, n_pair):
    S = proj.shape[0]
    ts = _tile(S, 512)
    W = LANES * n_pair

    def body(pf_ref, fb_ref, ka_ref, carry):
        @pl.when(pl.program_id(0) == 0)
        def _():
            carry[...] = jnp.zeros_like(carry)

        f = pf_ref[...] + fb_ref[...]
        logf = jnp.minimum(f, 0.0) - jnp.log(1.0 + jnp.exp(-jnp.abs(f)))
        r = lax.broadcasted_iota(jnp.int32, (ts, ts), 0)
        c = lax.broadcasted_iota(jnp.int32, (ts, ts), 1)
        ltri = (c <= r).astype(BF16)
        hi, mid, lo = _split3(logf)
        cs = _dot(ltri, hi) + _dot(ltri, mid) + _dot(ltri, lo) + carry[...]
        carry[...] = cs[ts - 1:ts, :]
        hh = lax.broadcasted_iota(jnp.int32, (LANES, W), 0)
        ll = lax.broadcasted_iota(jnp.int32, (LANES, W), 1)
        pair, w = ll >> 7, ll & (LANES - 1)
        ka = jnp.zeros((ts, W), F32)
        for p, piece in enumerate(_split3(-cs)):
            e = (((w == HEAD_DIM + p) & (hh == 2 * pair)) | ((w == p) & (hh == 2 * pair + 1)))
            ka = ka + _dot(piece, e.astype(BF16))
        lw = lax.broadcasted_iota(jnp.int32, (1, W), 1) & (LANES - 1)
        ka = ka + ((lw == HEAD_DIM + AUG) | (lw == AUG)).astype(F32)
        ka_ref[...] = ka.astype(BF16)

    return pl.pallas_call(
        body, grid=(S // ts,),
        in_specs=[pl.BlockSpec((ts, LANES), lambda i: (i, f_blk)),
                  pl.BlockSpec((1, LANES), lambda i: (0, 0))],
        out_specs=pl.BlockSpec((ts, W), lambda i: (i, 0)),
        out_shape=jax.ShapeDtypeStruct((S, W), BF16),
        scratch_shapes=[pltpu.VMEM((1, LANES), F32)],
        compiler_params=_params(1), name=name)(proj, fbias)


def _gate_bwd(name, sp, rs, proj, f_blk, fbias, n_pair):
    S = proj.shape[0]
    ts = _tile(S, 512)
    n_t = S // ts
    W = LANES * n_pair

    def body(sp_ref, rs_ref, pf_ref, fb_ref, df_ref, dfb_ref, carry):
        @pl.when(pl.program_id(0) == 0)
        def _():
            carry[...] = jnp.zeros_like(carry)
            dfb_ref[...] = jnp.zeros_like(dfb_ref)

        ll = lax.broadcasted_iota(jnp.int32, (W, LANES), 0)
        hh = lax.broadcasted_iota(jnp.int32, (W, LANES), 1)
        pair, w = ll >> 7, ll & (LANES - 1)
        first, second = hh == 2 * pair, hh == 2 * pair + 1

        def pick(ref, lane_first, lane_second):
            sel = (((w == lane_first) & first) | ((w == lane_second) & second)).astype(BF16)
            hi, mid, lo = _split3(ref[...])
            return _dot(hi, sel) + _dot(mid, sel) + _dot(lo, sel)

        dc = pick(rs_ref, HEAD_DIM + AUG, AUG) - pick(sp_ref, HEAD_DIM, 0)
        r = lax.broadcasted_iota(jnp.int32, (ts, ts), 0)
        c = lax.broadcasted_iota(jnp.int32, (ts, ts), 1)
        utri = (c >= r).astype(BF16)
        hi, mid, lo = _split3(dc)
        dlogf = _dot(utri, hi) + _dot(utri, mid) + _dot(utri, lo) + carry[...]
        carry[...] = dlogf[0:1, :]
        f = pf_ref[...] + fb_ref[...]
        lane = lax.broadcasted_iota(jnp.int32, (ts, LANES), 1)
        df = jnp.where(lane < 2 * n_pair, dlogf * _sigmoid(-f), 0.0)
        df_ref[...] = df.astype(BF16)
        dfb_ref[...] += jnp.sum(df, axis=0, keepdims=True)

    rev = lambda blk: (lambda i: (n_t - 1 - i, blk))
    return pl.pallas_call(
        body, grid=(n_t,),
        in_specs=[pl.BlockSpec((ts, W), rev(0)), pl.BlockSpec((ts, W), rev(0)),
                  pl.BlockSpec((ts, LANES), rev(f_blk)), pl.BlockSpec((1, LANES), lambda i: (0, 0))],
        out_specs=[pl.BlockSpec((ts, LANES), rev(0)), pl.BlockSpec((1, LANES), lambda i: (0, 0))],
        out_shape=[jax.ShapeDtypeStruct((S, LANES), BF16), jax.ShapeDtypeStruct((1, LANES), F32)],
        scratch_shapes=[pltpu.VMEM((1, LANES), F32)],
        compiler_params=_params(1), name=name)(sp, rs, proj, fbias)


def _spare_lane(h):
    return HEAD_DIM if h == 0 else 0


def _head_operands(h, lane, q2, k2, ka2):
    act = (lane < HEAD_DIM) if h == 0 else (lane >= HEAD_DIM)
    base = HEAD_DIM if h == 0 else 0
    ones = ((lane >= base) & (lane < base + AUG)).astype(F32)
    qa = jnp.where(act, q2 * (1.0 / math.sqrt(HEAD_DIM)), ones).astype(BF16)
    ka = jnp.where(act, k2.astype(BF16), ka2)
    return act, qa, ka


def _attn_fwd(name, proj, ka, q_blk, k_blk, v_blk, n_pair):
    S = proj.shape[0]
    tq = _tile(S, 512)
    n_t = S // tq
    W = LANES * n_pair

    rc = _tile(tq, ROW_CHUNK)

    def body(q_ref, k_ref, v_ref, ka_ref, o_ref, o32_ref, lse_ref, m_ref, acc_ref, s_ref, p_ref):
        i, j = pl.program_id(1), pl.program_id(2)

        @pl.when(j == 0)
        def _():
            m_ref[...] = jnp.full_like(m_ref, NEG_INF)
            acc_ref[...] = jnp.zeros_like(acc_ref)

        def step(diagonal):
            lane = lax.broadcasted_iota(jnp.int32, (tq, LANES), 1)
            q2, k2, v2, ka2 = q_ref[...], k_ref[...], v_ref[...], ka_ref[...]
            for h in range(2):
                act, qa, kaug = _head_operands(h, lane, q2, k2, ka2)
                s = _dot_nt(qa, kaug)
                if diagonal:
                    row = lax.broadcasted_iota(jnp.int32, (tq, tq), 0)
                    col = lax.broadcasted_iota(jnp.int32, (tq, tq), 1)
                    s = jnp.where(row >= col, s, NEG_INF)
                s_ref[...] = s
                m_prev = m_ref[h]
                m_new = jnp.maximum(m_prev, jnp.max(s, axis=-1, keepdims=True))
                m_ref[h] = m_new
                for r in range(tq // rc):
                    rows = pl.ds(r * rc, rc)
                    p_ref[rows, :] = jnp.exp(s_ref[rows, :] - m_ref[h, rows, :]).astype(BF16)
                vm = jnp.where(act, v2, (lane == _spare_lane(h)).astype(F32)).astype(BF16)
                acc_ref[h] = jnp.exp(m_prev - m_new) * acc_ref[h] + _dot(p_ref[...], vm)

        @pl.when(j < i)
        def _():
            step(False)

        @pl.when(j == i)
        def _():
            step(True)
            lane = lax.broadcasted_iota(jnp.int32, (tq, LANES), 1)
            first = lane < HEAD_DIM
            acc = [acc_ref[0], acc_ref[1]]
            den = [acc[h][:, _spare_lane(h):_spare_lane(h) + 1] for h in range(2)]
            out = jnp.where(first, acc[0] / den[0], acc[1] / den[1])
            o_ref[...] = out.astype(BF16)
            o32_ref[...] = out
            lse_ref[...] = jnp.where(first, m_ref[0] + jnp.log(den[0]), m_ref[1] + jnp.log(den[1]))

    qspec = lambda blk: pl.BlockSpec((tq, LANES), lambda p, i, j: (i, blk + p))
    kspec = lambda blk: pl.BlockSpec((tq, LANES), lambda p, i, j: (jnp.minimum(j, i), blk + p))
    out = pl.BlockSpec((tq, LANES), lambda p, i, j: (i, p))
    return pl.pallas_call(
        body, grid=(n_pair, n_t, n_t),
        in_specs=[qspec(q_blk), kspec(k_blk), kspec(v_blk), kspec(0)],
        out_specs=[out, out, out],
        out_shape=[jax.ShapeDtypeStruct((S, W), BF16), jax.ShapeDtypeStruct((S, W), F32),
                   jax.ShapeDtypeStruct((S, W), F32)],
        scratch_shapes=[pltpu.VMEM((2, tq, 1), F32), pltpu.VMEM((2, tq, LANES), F32),
                        pltpu.VMEM((tq, tq), F32), pltpu.VMEM((tq, tq), BF16)],
        compiler_params=_params(3), name=name)(proj, proj, proj, ka)


def _attn_bwd(name, proj, ka, o, lse, dmix, q_blk, k_blk, v_blk, do_blk, n_pair):
    S = proj.shape[0]
    tq = _tile(S, 512)
    n_t = S // tq
    W = LANES * n_pair
    scale = 1.0 / math.sqrt(HEAD_DIM)
    rc = _tile(tq, ROW_CHUNK)

    def body(q_ref, k_ref, v_ref, ka_ref, o_ref, lse_ref, do_ref,
             dq_ref, dk_ref, dv_ref, sp_ref, rs_ref, dk_acc, dv_acc, s_ref, dp_ref, p_ref, ds_ref, d_ref):
        j, i = pl.program_id(1), pl.program_id(2)

        @pl.when((j == 0) & (i == 0))
        def _():
            dq_ref[...] = jnp.zeros_like(dq_ref)
            rs_ref[...] = jnp.zeros_like(rs_ref)

        @pl.when(i == 0)
        def _():
            dk_acc[...] = jnp.zeros_like(dk_acc)
            dv_acc[...] = jnp.zeros_like(dv_acc)

        def step(diagonal):
            lane = lax.broadcasted_iota(jnp.int32, (tq, LANES), 1)
            q2, k2, v2, ka2 = q_ref[...], k_ref[...], v_ref[...], ka_ref[...]
            o2, do2 = o_ref[...], do_ref[...]
            dq = []
            for h in range(2):
                act, qa, kaug = _head_operands(h, lane, q2, k2, ka2)
                dom = jnp.where(act, do2, 0.0)
                d_ref[...] = jnp.sum(dom * o2, axis=-1, keepdims=True)
                dob = dom.astype(BF16)
                s_ref[...] = _dot_nt(qa, kaug)
                dp_ref[...] = _dot_nt(dob, jnp.where(act, v2, 0.0).astype(BF16))
                for r in range(tq // rc):
                    rows = pl.ds(r * rc, rc)
                    sc = s_ref[rows, :]
                    if diagonal:
                        row = lax.broadcasted_iota(jnp.int32, (rc, tq), 0) + r * rc
                        col = lax.broadcasted_iota(jnp.int32, (rc, tq), 1)
                        sc = jnp.where(row >= col, sc, NEG_INF)
                    p = jnp.exp(sc - lse_ref[rows, :][:, h * HEAD_DIM:h * HEAD_DIM + 1])
                    p_ref[rows, :] = p.astype(BF16)
                    ds_ref[rows, :] = (p * (dp_ref[rows, :] - d_ref[rows, :])).astype(BF16)
                dv_acc[...] += _dot_tn(p_ref[...], dob)
                dk_acc[h] += _dot_tn(ds_ref[...], qa)
                dq.append(_dot(ds_ref[...], kaug))
            rows = pl.ds(pl.multiple_of(i * tq, tq), tq)
            first = lane < HEAD_DIM
            dq_ref[rows, :] += jnp.where(first, dq[0], dq[1])
            rs_ref[rows, :] += jnp.where(first, dq[1], dq[0])

        @pl.when(i > j)
        def _():
            step(False)

        @pl.when(i == j)
        def _():
            step(True)

        @pl.when(i == n_t - 1)
        def _():
            lane = lax.broadcasted_iota(jnp.int32, (tq, LANES), 1)
            first = lane < HEAD_DIM
            dk_ref[...] = jnp.where(first, dk_acc[0], dk_acc[1]).astype(BF16)
            sp_ref[...] = jnp.where(first, dk_acc[1], dk_acc[0])
            dv_ref[...] = dv_acc[...].astype(BF16)

        @pl.when((j == n_t - 1) & (i == n_t - 1))
        def _():
            dq_ref[...] = dq_ref[...] * scale

    qspec = lambda blk: pl.BlockSpec((tq, LANES), lambda p, j, i: (jnp.maximum(i, j), blk + p))
    kspec = lambda blk: pl.BlockSpec((tq, LANES), lambda p, j, i: (j, blk + p))
    kout = pl.BlockSpec((tq, LANES), lambda p, j, i: (j, p))
    qres = pl.BlockSpec((S, LANES), lambda p, j, i: (0, p))
    return pl.pallas_call(
        body, grid=(n_pair, n_t, n_t),
        in_specs=[qspec(q_blk), kspec(k_blk), kspec(v_blk), kspec(0),
                  qspec(0), qspec(0), qspec(do_blk)],
        out_specs=[qres, kout, kout, kout, qres],
        out_shape=[jax.ShapeDtypeStruct((S, W), F32), jax.ShapeDtypeStruct((S, W), BF16),
                   jax.ShapeDtypeStruct((S, W), BF16), jax.ShapeDtypeStruct((S, W), F32),
                   jax.ShapeDtypeStruct((S, W), F32)],
        scratch_shapes=[pltpu.VMEM((2, tq, LANES), F32), pltpu.VMEM((tq, LANES), F32),
                        pltpu.VMEM((tq, tq), F32), pltpu.VMEM((tq, tq), F32),
                        pltpu.VMEM((tq, tq), BF16), pltpu.VMEM((tq, tq), BF16), pltpu.VMEM((tq, 1), F32)],
        compiler_params=_params(3), name=name)(proj, proj, proj, ka, o, lse, dmix)


def _ffn_block_grad(name, act, rows, scale, dep=None):
    nb, S, Fs = act.shape
    D = rows.shape[1]
    tm = _tile(S, 512)
    return _mm_tn(name, act, rows, (nb, Fs, D),
                  pl.BlockSpec((1, tm, Fs), lambda j, i: (j, i, 0)),
                  pl.BlockSpec((tm, D), lambda j, i: (i, 0)),
                  pl.BlockSpec((1, Fs, D), lambda j, i: (j, 0, 0)), (nb, S // tm), scale, BF16, dep=dep)


def kernel(x, ffn1_norm, ffn1_w_gate, ffn1_w_up, ffn1_w_down, mix_norm, w_in, fgate_bias, conv_w, conv_b, conv_ln_g, conv_ln_b, w_out, ffn2_norm, ffn2_w_gate, ffn2_w_up, ffn2_w_down, final_norm, loss_target, m_ffn1_norm, m_ffn1_w_gate, m_ffn1_w_up, m_ffn1_w_down, m_mix_norm, m_w_in, m_fgate_bias, m_conv_w, m_conv_b, m_conv_ln_g, m_conv_ln_b, m_w_out, m_ffn2_norm, m_ffn2_w_gate, m_ffn2_w_up, m_ffn2_w_down, m_final_norm, v_ffn1_norm, v_ffn1_w_gate, v_ffn1_w_up, v_ffn1_w_down, v_mix_norm, v_w_in, v_fgate_bias, v_conv_w, v_conv_b, v_conv_ln_g, v_conv_ln_b, v_w_out, v_ffn2_norm, v_ffn2_w_gate, v_ffn2_w_up, v_ffn2_w_down, v_final_norm):
    xs = x[0]
    S, D = xs.shape
    C = conv_b.shape[0]
    n_heads = fgate_bias.shape[0]
    FW = n_heads * HEAD_DIM
    n_pair = n_heads // 2
    MIX = C + FW
    in_shard = w_in.shape[1]
    in_cols = in_shard * N_DEV
    NP = -(-in_cols // 512) * 512
    q_blk, k_blk, v_blk = 2 * C // LANES, (2 * C + FW) // LANES, (2 * C + 2 * FW) // LANES
    f_blk = (2 * C + 3 * FW) // LANES
    assert C % LANES == 0 and FW % LANES == 0 and n_heads % 2 == 0 and n_heads <= LANES
    assert in_cols == 2 * C + 3 * FW + n_heads and MIX == w_out.shape[0] * N_DEV

    vec = lambda a: a.reshape(1, -1)
    bf = lambda a: a.astype(BF16)
    tm = _tile(S, 512)

    wgt1, wut1, wd1 = _all_gather_two_level("ag_ffn1", [bf(ffn1_w_gate).T, bf(ffn1_w_up).T, bf(ffn1_w_down)])
    ag, _ = _xchg_start("ag_start", [bf(w_in).T, conv_w, bf(w_out), bf(ffn2_w_gate).T, bf(ffn2_w_up).T,
                                     bf(ffn2_w_down)], False, dep=wd1)
    fbias = jnp.pad(vec(fgate_bias), ((0, 0), (0, LANES - n_heads)))

    x1, h1, G1, U1 = _ffn_fwd("ffn1_fwd", xs, vec(ffn1_norm), wgt1, wut1, wd1)
    win_g, cw_g = _xchg_wait("ag_wait_in", ag[0:2], False, x1)
    wint = jnp.pad(win_g.reshape(in_cols, D), ((0, NP - in_cols), (0, 0)))
    cw = jnp.pad(cw_g.transpose(1, 0, 2).reshape(CONV_WIDTH, C), ((0, CONV_HALO - CONV_WIDTH), (0, 0)))
    proj, h2 = _norm_mm_nt("proj_in", x1, vec(mix_norm), wint, 512)
    yc, y_conv = _conv_fwd("conv_fwd", proj, C, cw, vec(conv_b), vec(conv_ln_g), vec(conv_ln_b))
    ka = _gate_prep("gate_prep", proj, f_blk, fbias, n_pair)
    o, o32, lse = _attn_fwd("attn_fwd", proj, ka, q_blk, k_blk, v_blk, n_pair)
    (wout_g,) = _xchg_wait("ag_wait_out", ag[2:3], False, lse)
    wout = wout_g.reshape(MIX, D)
    mix = jnp.concatenate([y_conv, o], axis=1)
    x2 = _mm_res("proj_out", mix, wout, x1, 512)
    wgt2, wut2, wd2 = _xchg_wait("ag_wait_ffn2", ag[3:6], False, x2)
    x3, h3, G2, U2 = _ffn_fwd("ffn2_fwd", x2, vec(ffn2_norm), wgt2, wut2, wd2)

    dx3, d_final_norm, loss_part = _final("final", x3, vec(final_norm), loss_target[0])
    dG2, dU2, A2, dx2, d_ffn2_norm = _ffn_bwd_act("ffn2_bwd", dx3, x2, vec(ffn2_norm), G2, U2, wgt2, wut2, wd2)
    dwd2 = _ffn_block_grad("ffn2_dwd", A2, dx3, 0.5)
    s_d2, tok = _xchg_start("a2a_start_ffn2_wd", [dwd2], True)
    dwg2 = _ffn_block_grad("ffn2_dwg", dG2, h3, 1.0, dep=tok)
    s_g2, tok = _xchg_start("a2a_start_ffn2_wg", [dwg2], True)
    dwu2 = _ffn_block_grad("ffn2_dwu", dU2, h3, 1.0, dep=tok)
    s_u2, tok = _xchg_start("a2a_start_ffn2_wu", [dwu2], True)

    dmix = _mm_k("dmix", dx2, wout, 512, True, dep=tok)
    d_wout = _mm_tn("dwout", mix, dx2, (MIX, D),
                    pl.BlockSpec((tm, MIX), lambda j, i: (i, 0)), pl.BlockSpec((tm, 512), lambda j, i: (i, j)),
                    pl.BlockSpec((MIX, 512), lambda j, i: (0, j)), (D // 512, S // tm), 1.0, BF16)
    s_out, tok = _xchg_start("a2a_start_w_out", [d_wout.reshape(N_DEV, MIX // N_DEV, D)], True)
    dag, d_cw, d_cb, d_lg, d_lb = _conv_bwd("conv_bwd", dmix, yc, proj, C, cw, vec(conv_ln_g), vec(conv_ln_b))
    dq, dk, dv, sp, rs = _attn_bwd("attn_bwd", proj, ka, o32, lse, dmix, q_blk, k_blk, v_blk, C // LANES, n_pair)
    df, d_fb = _gate_bwd("gate_bwd", sp, rs, proj, f_blk, fbias, n_pair)
    dproj = jnp.concatenate([dag, bf(dq), dk, dv, df, jnp.zeros((S, NP - f_blk * LANES - LANES), BF16)], axis=1)
    d_wint = _mm_tn("dwin", dproj, h2, (NP, D),
                    pl.BlockSpec((tm, 512), lambda j, i: (i, j)), pl.BlockSpec((tm, D), lambda j, i: (i, 0)),
                    pl.BlockSpec((512, D), lambda j, i: (j, 0)), (NP // 512, S // tm), 1.0, BF16, dep=tok)
    s_in, tok = _xchg_start("a2a_start_w_in", [d_wint[:in_cols].reshape(N_DEV, in_shard, D)], True)
    dx1, d_mix_norm = _mm_k("dh2", dproj, wint, 512, False, norm_bwd=(x1, vec(mix_norm), dx2), dep=tok)
    dG1, dU1, A1, dx0, d_ffn1_norm = _ffn_bwd_act("ffn1_bwd", dx1, xs, vec(ffn1_norm), G1, U1, wgt1, wut1, wd1)

    rows = lambda a: a.reshape(-1, C)
    pad_row = lambda a: jnp.pad(a.reshape(1, -1), ((0, 0), (0, C - a.size)))
    pieces = [rows(d_ffn1_norm), rows(d_mix_norm), rows(d_ffn2_norm), rows(d_final_norm),
              d_cw[:CONV_WIDTH], d_cb, d_lg, d_lb, pad_row(d_fb[0, :n_heads]), pad_row(loss_part[0, :1])]
    pack = jnp.concatenate(pieces, axis=0)
    n_rows = pack.shape[0]
    pack = jnp.pad(pack, ((0, -n_rows % SUBLANES), (0, 0)))
    (pack_g,) = _exchange("ag_small", [pack], False)
    tot = _reduce_adam("sum_small", pack_g)
    nd = D // C
    g_ffn1_norm, g_mix_norm, g_ffn2_norm, g_final_norm = (tot[k * nd:(k + 1) * nd].reshape(D) for k in range(4))
    r0 = 4 * nd
    me = _lin(_mesh_pos())
    cs = C // N_DEV
    g_conv_w = lax.dynamic_slice(tot[r0:r0 + CONV_WIDTH], (0, me * cs), (CONV_WIDTH, cs))
    g_conv_b, g_ln_g, g_ln_b = tot[r0 + CONV_WIDTH], tot[r0 + CONV_WIDTH + 1], tot[r0 + CONV_WIDTH + 2]
    g_fb = tot[r0 + CONV_WIDTH + 3, :n_heads]
    loss = tot[r0 + CONV_WIDTH + 4, 0]

    small = [(g_ffn1_norm, ffn1_norm, m_ffn1_norm, v_ffn1_norm), (g_mix_norm, mix_norm, m_mix_norm, v_mix_norm),
             (g_fb, fgate_bias, m_fgate_bias, v_fgate_bias), (g_conv_w, conv_w, m_conv_w, v_conv_w),
             (g_conv_b, conv_b, m_conv_b, v_conv_b), (g_ln_g, conv_ln_g, m_conv_ln_g, v_conv_ln_g),
             (g_ln_b, conv_ln_b, m_conv_ln_b, v_conv_ln_b), (g_ffn2_norm, ffn2_norm, m_ffn2_norm, v_ffn2_norm),
             (g_final_norm, final_norm, m_final_norm, v_final_norm)]
    sizes = [g.size for g, _, _, _ in small]
    total = sum(sizes)
    padded = -(-total // (SUBLANES * LANES)) * (SUBLANES * LANES)

    def flat_pack(k, fill):
        flat = jnp.concatenate([t[k].reshape(-1) for t in small])
        return jnp.pad(flat, (0, padded - total), constant_values=fill).reshape(padded // LANES, LANES)

    sg, sd, sm, sv = _reduce_adam("adam_small", flat_pack(0, 0.0)[None], flat_pack(1, 0.0), flat_pack(2, 0.0),
                                  flat_pack(3, 1.0))

    def unpack(packed):
        flat = packed.reshape(-1)
        out, off = [], 0
        for (g, _, _, _), n in zip(small, sizes):
            out.append(flat[off:off + n].reshape(g.shape))
            off += n
        return out

    s_g, s_d, s_m, s_v = unpack(sg), unpack(sd), unpack(sm), unpack(sv)

    dwd1 = _ffn_block_grad("ffn1_dwd", A1, dx1, 0.5, dep=sd)
    s_d1, tok = _xchg_start("a2a_start_ffn1_wd", [dwd1], True)
    dwg1 = _ffn_block_grad("ffn1_dwg", dG1, h1, 1.0, dep=tok)
    s_g1, tok = _xchg_start("a2a_start_ffn1_wg", [dwg1], True)
    dwu1 = _ffn_block_grad("ffn1_dwu", dU1, h1, 1.0, dep=tok)
    s_u1, tok = _xchg_start("a2a_start_ffn1_wu", [dwu1], True)

    r_d2, r_g2, r_u2, r_out, r_in = _xchg_wait("a2a_wait_a", s_d2 + s_g2 + s_u2 + s_out + s_in, True, tok)
    tr = lambda a: a.T

    def adam_t(name, recv, w, m, v):
        return tuple(tr(r) for r in _reduce_adam(name, recv, tr(w), tr(m), tr(v)))

    res = {
        "ffn2_w_down": _reduce_adam("adam_ffn2_wd", r_d2, ffn2_w_down, m_ffn2_w_down, v_ffn2_w_down),
        "ffn2_w_gate": adam_t("adam_ffn2_wg", r_g2, ffn2_w_gate, m_ffn2_w_gate, v_ffn2_w_gate),
        "ffn2_w_up": adam_t("adam_ffn2_wu", r_u2, ffn2_w_up, m_ffn2_w_up, v_ffn2_w_up),
        "w_out": _reduce_adam("adam_w_out", r_out, w_out, m_w_out, v_w_out),
        "w_in": adam_t("adam_w_in", r_in, w_in, m_w_in, v_w_in),
    }
    (r_d1,) = _xchg_wait("a2a_wait_d1", s_d1, True, res["w_in"][0])
    res["ffn1_w_down"] = _reduce_adam("adam_ffn1_wd", r_d1, ffn1_w_down, m_ffn1_w_down, v_ffn1_w_down)
    (r_g1,) = _xchg_wait("a2a_wait_g1", s_g1, True, res["ffn1_w_down"][0])
    res["ffn1_w_gate"] = adam_t("adam_ffn1_wg", r_g1, ffn1_w_gate, m_ffn1_w_gate, v_ffn1_w_gate)
    (r_u1,) = _xchg_wait("a2a_wait_u1", s_u1, True, res["ffn1_w_gate"][0])
    res["ffn1_w_up"] = adam_t("adam_ffn1_wu", r_u1, ffn1_w_up, m_ffn1_w_up, v_ffn1_w_up)

    small_names = ["ffn1_norm", "mix_norm", "fgate_bias", "conv_w", "conv_b", "conv_ln_g", "conv_ln_b",
                   "ffn2_norm", "final_norm"]
    for k, n in enumerate(small_names):
        res[n] = (s_g[k], s_d[k], s_m[k], s_v[k])
    order = ["ffn1_norm", "ffn1_w_gate", "ffn1_w_up", "ffn1_w_down", "mix_norm", "w_in", "fgate_bias",
             "conv_w", "conv_b", "conv_ln_g", "conv_ln_b", "w_out", "ffn2_norm", "ffn2_w_gate", "ffn2_w_up",
             "ffn2_w_down", "final_norm"]
    outs = [loss, dx0[None]]
    for k in range(4):
        outs += [res[n][k] for n in order]
    return tuple(outs)
```

```python
import math

import jax
import jax.numpy as jnp
from jax import lax
from jax.experimental import pallas as pl
from jax.experimental.pallas import tpu as pltpu

F32 = jnp.float32
BF16 = jnp.bfloat16

N_DEV = 8
MESH_ID = pl.DeviceIdType.MESH
HEAD_DIM = 64
CONV_WIDTH = 31
CONV_HALO = 32
NORM_EPS = 1e-6
LN_EPS = 1e-5
NEG_INF = -1e30
LANES = 128
SUBLANES = 8
V7X_VMEM_LIMIT = 52 * 1024 * 1024

ADAM_LR = 0.001
ADAM_B1 = 0.9
ADAM_B2 = 0.999
ADAM_EPS = 1e-08
ADAM_WD = 0.01
ADAM_STEP = 10


def _params(n_grid_axes):
    return pltpu.CompilerParams(dimension_semantics=("arbitrary",) * n_grid_axes,
                                vmem_limit_bytes=V7X_VMEM_LIMIT)


def _tile(n, pref, mult=8):
    t = min(pref, n)
    while t >= mult:
        if n % t == 0 and t % mult == 0:
            return t
        t -= mult
    return n


def _dot(a, b):
    return jnp.dot(a, b, preferred_element_type=F32)


def _dot_nt(a, b):
    return lax.dot_general(a, b, (((1,), (1,)), ((), ())), preferred_element_type=F32)


def _dot_tn(a, b):
    return lax.dot_general(a, b, (((0,), (0,)), ((), ())), preferred_element_type=F32)


def _sigmoid(x):
    return 1.0 / (1.0 + jnp.exp(-x))


def _rms_fwd(x, g):
    r = lax.rsqrt(jnp.mean(x * x, axis=-1, keepdims=True) + NORM_EPS)
    return x * r * g


def _rms_bwd(dh, x, g):
    r = lax.rsqrt(jnp.mean(x * x, axis=-1, keepdims=True) + NORM_EPS)
    xh = x * r
    dxh = dh * g
    dx = r * (dxh - xh * jnp.mean(dxh * xh, axis=-1, keepdims=True))
    return dx, jnp.sum(dh * xh, axis=0, keepdims=True)


def _split3(x):
    hi = x.astype(BF16)
    r = x - hi.astype(F32)
    mid = r.astype(BF16)
    lo = (r - mid.astype(F32)).astype(BF16)
    return hi, mid, lo


def _blk(ref):
    return ref[0] if len(ref.shape) == 3 else ref[...]


_DEP_SPEC = pl.BlockSpec(memory_space=pl.ANY)


def _mesh_pos():
    return lax.axis_index("x"), lax.axis_index("y"), lax.axis_index("c")


def _peer(pos, k):
    x, y, c = pos
    return (1 - x if k & 4 else x, 1 - y if k & 2 else y, 1 - c if k & 1 else c)


def _lin(pos):
    x, y, c = pos
    return 4 * x + 2 * y + c


def _remote_copy(src, land, send_sems, recv_sems, pos, k, all_to_all):
    peer = _peer(pos, k)
    return pltpu.make_async_remote_copy(
        src_ref=src.at[_lin(peer)] if all_to_all else src, dst_ref=land.at[_lin(pos)],
        send_sem=send_sems.at[k - 1], recv_sem=recv_sems.at[k - 1],
        device_id=peer, device_id_type=MESH_ID)


def _landing_shape(a, all_to_all):
    return a.shape if all_to_all else (N_DEV,) + a.shape


def _exchange(name, arrays, all_to_all, dep=None):
    n = len(arrays)
    deps = [] if dep is None else [dep]
    out_shapes = [jax.ShapeDtypeStruct(_landing_shape(a, all_to_all), a.dtype) for a in arrays]

    def body(*refs):
        ins, outs = refs[:n], refs[n + len(deps):2 * n + len(deps)]
        send_sems, recv_sems, local_sems = refs[2 * n + len(deps):]
        pos = _mesh_pos()
        me = _lin(pos)
        local = []
        for a in range(n):
            src = ins[a].at[me] if all_to_all else ins[a]
            cp = pltpu.make_async_copy(src, outs[a].at[me], local_sems.at[a])
            cp.start()
            local.append(cp)
        remote = [_remote_copy(ins[a], outs[a], send_sems.at[a], recv_sems.at[a], pos, k, all_to_all)
                  for a in range(n) for k in range(1, N_DEV)]
        for cp in remote:
            cp.start()
        for cp in remote:
            cp.wait()
        for cp in local:
            cp.wait()

    any_spec = pl.BlockSpec(memory_space=pl.ANY)
    return pl.pallas_call(
        body, out_shape=out_shapes, in_specs=[any_spec] * (n + len(deps)), out_specs=[any_spec] * n,
        scratch_shapes=[pltpu.SemaphoreType.DMA((n, N_DEV - 1)),
                        pltpu.SemaphoreType.DMA((n, N_DEV - 1)),
                        pltpu.SemaphoreType.DMA((n,))],
        name=name)(*arrays, *deps)


def _all_gather_two_level(name, shards):
    n = len(shards)
    out_shapes = [jax.ShapeDtypeStruct((N_DEV,) + a.shape, a.dtype) for a in shards]

    def body(*refs):
        ins, outs = refs[:n], refs[n:2 * n]
        send_sems, recv_sems, local_sems = refs[2 * n:]
        x, y, c = pos = _mesh_pos()
        sibling = (x, y, 1 - c)
        chips = [(1 - x, y), (x, 1 - y), (1 - x, 1 - y)]

        def copy(a, k, block, to, src=None):
            slot = outs[a].at[_lin(block)]
            return pltpu.make_async_remote_copy(
                src_ref=slot if src is None else src, dst_ref=slot,
                send_sem=send_sems.at[a, k], recv_sem=recv_sems.at[a, k],
                device_id=to, device_id_type=MESH_ID)

        local = [pltpu.make_async_copy(ins[a], outs[a].at[_lin(pos)], local_sems.at[a]) for a in range(n)]
        first = [copy(a, 1 + j, pos, (*chip, c), src=ins[a]) for j, chip in enumerate(chips) for a in range(n)]
        first += [copy(a, 0, pos, sibling, src=ins[a]) for a in range(n)]
        for cp in first + local:
            cp.start()
        passed = []
        for j, chip in enumerate(chips):
            for a in range(n):
                copy(a, 1 + j, (*chip, c), pos).wait_recv()
                cp = copy(a, 4 + j, (*chip, c), sibling)
                cp.start()
                passed.append(cp)
        for a in range(n):
            copy(a, 0, sibling, pos).wait_recv()
        for j, chip in enumerate(chips):
            for a in range(n):
                copy(a, 4 + j, (*chip, 1 - c), pos).wait_recv()
        for cp in first + passed:
            cp.wait_send()
        for cp in local:
            cp.wait()

    any_spec = pl.BlockSpec(memory_space=pl.ANY)
    return pl.pallas_call(
        body, out_shape=out_shapes, in_specs=[any_spec] * n, out_specs=[any_spec] * n,
        scratch_shapes=[pltpu.SemaphoreType.DMA((n, N_DEV - 1)),
                        pltpu.SemaphoreType.DMA((n, N_DEV - 1)),
                        pltpu.SemaphoreType.DMA((n,))],
        name=name)(*shards)


_HBM_SPEC = pl.BlockSpec(memory_space=pltpu.HBM)
_SEM_SPEC = pl.BlockSpec(memory_space=pltpu.SEMAPHORE)
_SIDE_EFFECT = pltpu.SideEffectType.DATAFLOW_SIDE_EFFECTING


def _xchg_start(name, arrays, all_to_all, dep=None):
    n = len(arrays)
    me = _lin(_mesh_pos())
    lands = [lax.dynamic_update_index_in_dim(
        lax.empty(_landing_shape(a, all_to_all), a.dtype),
        lax.dynamic_index_in_dim(a, me, 0, keepdims=False) if all_to_all else a, me, 0) for a in arrays]

    deps = [] if dep is None else [dep]

    def body(*refs):
        srcs, lnds = refs[:n], refs[n:2 * n]
        outs = refs[2 * n + len(deps):]
        send, recv = outs[:n], outs[n:2 * n]
        token = outs[4 * n]
        pos = _mesh_pos()
        for a in range(n):
            for k in range(1, N_DEV):
                _remote_copy(srcs[a], lnds[a], send[a], recv[a], pos, k, all_to_all).start()
        token[...] = jnp.zeros_like(token)

    hbm = lambda a: pltpu.HBM(a.shape, a.dtype)
    sems = [pltpu.SemaphoreType.DMA((N_DEV - 1,))] * (2 * n)
    res = pl.pallas_call(
        body, name=name,
        out_shape=sems + [hbm(a) for a in arrays] + [hbm(l) for l in lands]
        + [jax.ShapeDtypeStruct((SUBLANES, LANES), F32)],
        in_specs=[_HBM_SPEC] * (2 * n) + [_DEP_SPEC] * len(deps),
        out_specs=[_SEM_SPEC] * (2 * n) + [_HBM_SPEC] * (2 * n) + [pl.BlockSpec(memory_space=pltpu.VMEM)],
        input_output_aliases={a: 2 * n + a for a in range(2 * n)},
        compiler_params=pltpu.CompilerParams(has_side_effects=_SIDE_EFFECT),
    )(*[pltpu.with_memory_space_constraint(a, pltpu.HBM) for a in list(arrays) + lands], *deps)
    per_array = [(res[a], res[n + a], res[2 * n + a], res[3 * n + a]) for a in range(n)]
    return per_array, res[4 * n]


def _xchg_wait(name, started, all_to_all, after):
    n = len(started)

    def body(*refs):
        srcs, lnds, send, recv = refs[:n], refs[n:2 * n], refs[2 * n:3 * n], refs[3 * n:4 * n]
        pos = _mesh_pos()
        for a in range(n):
            for k in range(1, N_DEV):
                cp = _remote_copy(srcs[a], lnds[a], send[a], recv[a], pos, k, all_to_all)
                cp.wait_send()
                cp.wait_recv()

    hbm = lambda a: pltpu.HBM(a.shape, a.dtype)
    srcs = [s[2] for s in started]
    lands = [s[3] for s in started]
    res = pl.pallas_call(
        body, name=name,
        out_shape=[hbm(a) for a in srcs + lands],
        in_specs=[_HBM_SPEC] * (2 * n) + [_SEM_SPEC] * (2 * n) + [_DEP_SPEC],
        out_specs=[_HBM_SPEC] * (2 * n),
        input_output_aliases={a: a for a in range(2 * n)},
        compiler_params=pltpu.CompilerParams(has_side_effects=_SIDE_EFFECT),
    )(*srcs, *lands, *[s[0] for s in started], *[s[1] for s in started], after)
    return list(res[n:])


def _reduce_adam(name, parts, w=None, m=None, v=None):
    n, R, C = parts.shape
    tr = _tile(R, 256)
    do_adam = w is not None
    bc1 = 1.0 - ADAM_B1 ** ADAM_STEP
    bc2 = 1.0 - ADAM_B2 ** ADAM_STEP

    def body(*refs):
        p_ref = refs[0]
        g = p_ref[0].astype(F32)
        for d in range(1, n):
            g = g + p_ref[d].astype(F32)
        if not do_adam:
            refs[1][...] = g
            return
        w_ref, m_ref, v_ref, g_ref, d_ref, nm_ref, nv_ref = refs[1:]
        g_ref[...] = g
        nm = ADAM_B1 * m_ref[...] + (1.0 - ADAM_B1) * g
        nv = ADAM_B2 * v_ref[...] + (1.0 - ADAM_B2) * (g * g)
        m_hat = nm / bc1
        v_hat = nv / bc2
        d_ref[...] = -ADAM_LR * (m_hat / (jnp.sqrt(v_hat) + ADAM_EPS) + ADAM_WD * w_ref[...])
        nm_ref[...] = nm
        nv_ref[...] = nv

    tc = _tile(C, 512, LANES) if tr == R and R > 256 else C
    row = pl.BlockSpec((tr, tc), lambda i, j: (i, j))
    part = pl.BlockSpec((n, tr, tc), lambda i, j: (0, i, j))
    shard = jax.ShapeDtypeStruct((R, C), F32)
    grid = (R // tr, C // tc)
    if do_adam:
        return pl.pallas_call(body, grid=grid, in_specs=[part, row, row, row],
                              out_specs=[row] * 4, out_shape=[shard] * 4,
                              compiler_params=_params(2), name=name)(parts, w, m, v)
    return pl.pallas_call(body, grid=grid, in_specs=[part], out_specs=row, out_shape=shard,
                          compiler_params=_params(2), name=name)(parts)


def _ffn_fwd(name, x, gain, wgt, wut, wd):
    S, D = x.shape
    nb, Fs, _ = wgt.shape
    tm = _tile(S, 512)

    def body(x_ref, g_ref, wg_ref, wu_ref, wd_ref, xo_ref, h_ref, G_ref, U_ref, acc_ref):
        j = pl.program_id(1)

        @pl.when(j == 0)
        def _():
            h_ref[...] = _rms_fwd(x_ref[...], g_ref[...]).astype(BF16)
            acc_ref[...] = jnp.zeros_like(acc_ref)

        h = h_ref[...]
        G = _dot_nt(h, wg_ref[0])
        U = _dot_nt(h, wu_ref[0])
        G_ref[0] = G
        U_ref[0] = U
        a = G * _sigmoid(G) * U
        acc_ref[...] += _dot(a.astype(BF16), wd_ref[0])

        @pl.when(j == nb - 1)
        def _():
            xo_ref[...] = x_ref[...] + 0.5 * acc_ref[...]

    row = pl.BlockSpec((tm, D), lambda i, j: (i, 0))
    act = pl.BlockSpec((1, tm, Fs), lambda i, j: (j, i, 0))
    wblk = pl.BlockSpec((1, Fs, D), lambda i, j: (j, 0, 0))
    return pl.pallas_call(
        body, grid=(S // tm, nb),
        in_specs=[row, pl.BlockSpec((1, D), lambda i, j: (0, 0)), wblk, wblk, wblk],
        out_specs=[row, row, act, act],
        out_shape=[jax.ShapeDtypeStruct((S, D), F32), jax.ShapeDtypeStruct((S, D), BF16),
                   jax.ShapeDtypeStruct((nb, S, Fs), F32), jax.ShapeDtypeStruct((nb, S, Fs), F32)],
        scratch_shapes=[pltpu.VMEM((tm, D), F32)],
        compiler_params=_params(2), name=name)(x, gain, wgt, wut, wd)


def _ffn_bwd_gate(name, dxo, G, U, wd):
    S, D = dxo.shape
    nb, Fs, _ = wd.shape
    tm = _tile(S, 512)

    def gate_body(dxo_ref, G_ref, U_ref, wd_ref, dG_ref, dU_ref, A_ref, dxb_ref):
        @pl.when(pl.program_id(1) == 0)
        def _():
            dxb_ref[...] = dxo_ref[...].astype(BF16)

        dA = 0.5 * _dot_nt(dxb_ref[...], wd_ref[0])
        Gv = G_ref[0]
        Uv = U_ref[0]
        sg = _sigmoid(Gv)
        sl = Gv * sg
        dG_ref[0] = (dA * Uv * (sg * (1.0 + Gv * (1.0 - sg)))).astype(BF16)
        dU_ref[0] = (dA * sl).astype(BF16)
        A_ref[0] = (sl * Uv).astype(BF16)

    row = pl.BlockSpec((tm, D), lambda i, j: (i, 0))
    act = pl.BlockSpec((1, tm, Fs), lambda i, j: (j, i, 0))
    wblk = pl.BlockSpec((1, Fs, D), lambda i, j: (j, 0, 0))
    act_shape = jax.ShapeDtypeStruct((nb, S, Fs), BF16)
    return pl.pallas_call(
        gate_body, grid=(S // tm, nb), in_specs=[row, act, act, wblk], out_specs=[act, act, act],
        out_shape=[act_shape, act_shape, act_shape],
        scratch_shapes=[pltpu.VMEM((tm, D), BF16)],
        compiler_params=_params(2), name=name)(dxo, G, U, wd)


def _ffn_bwd_in(name, dG, dU, wgt, wut, dxo, x_in, gain):
    S, D = x_in.shape
    nb, Fs, _ = wgt.shape
    tm = _tile(S, 512)
    rows_per_chunk = _tile(tm, 128)

    def in_body(dG_ref, dU_ref, wg_ref, wu_ref, dxo_ref, x_ref, g_ref, dx_ref, dgain_ref, acc_ref):
        i, j = pl.program_id(0), pl.program_id(1)

        @pl.when(j == 0)
        def _():
            acc_ref[...] = jnp.zeros_like(acc_ref)

        @pl.when((i == 0) & (j == 0))
        def _():
            dgain_ref[...] = jnp.zeros_like(dgain_ref)

        acc_ref[...] += _dot(dG_ref[0], wg_ref[0]) + _dot(dU_ref[0], wu_ref[0])

        @pl.when(j == nb - 1)
        def _():
            def chunk(r, dg_sum):
                rows = pl.ds(pl.multiple_of(r * rows_per_chunk, rows_per_chunk), rows_per_chunk)
                dx, dg = _rms_bwd(acc_ref[rows, :], x_ref[rows, :], g_ref[...])
                dx_ref[rows, :] = dxo_ref[rows, :] + dx
                return dg_sum + dg

            dgain_ref[...] += lax.fori_loop(0, tm // rows_per_chunk, chunk, jnp.zeros((1, D), F32))

    row = pl.BlockSpec((tm, D), lambda i, j: (i, 0))
    vec = pl.BlockSpec((1, D), lambda i, j: (0, 0))
    act = pl.BlockSpec((1, tm, Fs), lambda i, j: (j, i, 0))
    wblk = pl.BlockSpec((1, Fs, D), lambda i, j: (j, 0, 0))
    return pl.pallas_call(
        in_body, grid=(S // tm, nb), in_specs=[act, act, wblk, wblk, row, row, vec],
        out_specs=[row, vec],
        out_shape=[jax.ShapeDtypeStruct((S, D), F32), jax.ShapeDtypeStruct((1, D), F32)],
        scratch_shapes=[pltpu.VMEM((tm, D), F32)],
        compiler_params=_params(2), name=name)(dG, dU, wgt, wut, dxo, x_in, gain)


def _mm_tn(name, lhs, rhs, out_shape, lhs_spec, rhs_spec, out_spec, grid, scale, out_dtype, dep=None):
    acc_shape = tuple(out_spec.block_shape[-2:])
    n_red = grid[-1]

    def body(l_ref, r_ref, *rest):
        o_ref, acc_ref = rest[-2:]
        i = pl.program_id(len(grid) - 1)

        @pl.when(i == 0)
        def _():
            acc_ref[...] = jnp.zeros_like(acc_ref)

        acc_ref[...] += _dot_tn(_blk(l_ref).astype(BF16), _blk(r_ref).astype(BF16))

        @pl.when(i == n_red - 1)
        def _():
            res = (scale * acc_ref[...]).astype(out_dtype)
            if len(o_ref.shape) == 3:
                o_ref[0] = res
            else:
                o_ref[...] = res

    deps = [] if dep is None else [dep]
    return pl.pallas_call(
        body, grid=grid, in_specs=[lhs_spec, rhs_spec] + [_DEP_SPEC] * len(deps), out_specs=out_spec,
        out_shape=jax.ShapeDtypeStruct(out_shape, out_dtype),
        scratch_shapes=[pltpu.VMEM(acc_shape, F32)],
        compiler_params=_params(len(grid)), name=name)(lhs, rhs, *deps)


def _norm_mm_nt(name, x, gain, wt, tn):
    S, D = x.shape
    N = wt.shape[0]
    tm = _tile(S, 1024)

    def body(x_ref, g_ref, w_ref, o_ref, h_ref):
        @pl.when(pl.program_id(1) == 0)
        def _():
            h_ref[...] = _rms_fwd(x_ref[...], g_ref[...]).astype(BF16)

        o_ref[...] = _dot_nt(h_ref[...], w_ref[...])

    row = pl.BlockSpec((tm, D), lambda i, j: (i, 0))
    return pl.pallas_call(
        body, grid=(S // tm, N // tn),
        in_specs=[row, pl.BlockSpec((1, D), lambda i, j: (0, 0)),
                  pl.BlockSpec((tn, D), lambda i, j: (j, 0))],
        out_specs=[pl.BlockSpec((tm, tn), lambda i, j: (i, j)), row],
        out_shape=[jax.ShapeDtypeStruct((S, N), F32), jax.ShapeDtypeStruct((S, D), BF16)],
        compiler_params=_params(2), name=name)(x, gain, wt)


def _mm_res(name, a, w, res, tn):
    S, K = a.shape
    N = w.shape[1]
    tm = _tile(S, 512)

    def body(a_ref, w_ref, r_ref, o_ref):
        o_ref[...] = r_ref[...] + _dot(a_ref[...], w_ref[...])

    tile = pl.BlockSpec((tm, tn), lambda i, j: (i, j))
    return pl.pallas_call(
        body, grid=(S // tm, N // tn),
        in_specs=[pl.BlockSpec((tm, K), lambda i, j: (i, 0)),
                  pl.BlockSpec((K, tn), lambda i, j: (0, j)), tile],
        out_specs=tile, out_shape=jax.ShapeDtypeStruct((S, N), F32),
        compiler_params=_params(2), name=name)(a, w, res)


def _mm_k(name, a, b, tk, transpose_b, norm_bwd=None, dep=None):
    S, K = a.shape
    N = b.shape[0] if transpose_b else b.shape[1]
    tm = _tile(S, 512)
    nk = K // tk
    n_extra = 0 if norm_bwd is None else 3
    deps = [] if dep is None else [dep]

    def body(*refs):
        a_ref, b_ref = refs[:2]
        outs = refs[2 + n_extra + len(deps):]
        acc_ref = outs[-1]
        i, k = pl.program_id(0), pl.program_id(1)

        @pl.when(k == 0)
        def _():
            acc_ref[...] = jnp.zeros_like(acc_ref)

        av = a_ref[...].astype(BF16)
        acc_ref[...] += _dot_nt(av, b_ref[...]) if transpose_b else _dot(av, b_ref[...])

        if norm_bwd is None:
            @pl.when(k == nk - 1)
            def _():
                outs[0][...] = acc_ref[...]
        else:
            x_ref, g_ref, dres_ref = refs[2:5]
            o_ref, dgain_ref = outs[:2]

            @pl.when((i == 0) & (k == 0))
            def _():
                dgain_ref[...] = jnp.zeros_like(dgain_ref)

            @pl.when(k == nk - 1)
            def _():
                dx, dg = _rms_bwd(acc_ref[...], x_ref[...], g_ref[...])
                o_ref[...] = dres_ref[...] + dx
                dgain_ref[...] += dg

    a_spec = pl.BlockSpec((tm, tk), lambda i, k: (i, k))
    b_spec = (pl.BlockSpec((N, tk), lambda i, k: (0, k)) if transpose_b
              else pl.BlockSpec((tk, N), lambda i, k: (k, 0)))
    row = pl.BlockSpec((tm, N), lambda i, k: (i, 0))
    vec = pl.BlockSpec((1, N), lambda i, k: (0, 0))
    out = jax.ShapeDtypeStruct((S, N), F32)
    scratch = [pltpu.VMEM((tm, N), F32)]
    dep_specs = [_DEP_SPEC] * len(deps)
    if norm_bwd is None:
        return pl.pallas_call(body, grid=(S // tm, nk), in_specs=[a_spec, b_spec] + dep_specs,
                              out_specs=row, out_shape=out, scratch_shapes=scratch,
                              compiler_params=_params(2), name=name)(a, b, *deps)
    x_in, gain, dres = norm_bwd
    return pl.pallas_call(body, grid=(S // tm, nk), in_specs=[a_spec, b_spec, row, vec, row] + dep_specs,
                          out_specs=[row, vec],
                          out_shape=[out, jax.ShapeDtypeStruct((1, N), F32)],
                          scratch_shapes=scratch,
                          compiler_params=_params(2), name=name)(a, b, x_in, gain, dres, *deps)


def _final(name, x, gain, target):
    S, D = x.shape
    tm = _tile(S, 512)

    def body(x_ref, g_ref, t_ref, dx_ref, dgain_ref, loss_ref):
        @pl.when(pl.program_id(0) == 0)
        def _():
            dgain_ref[...] = jnp.zeros_like(dgain_ref)
            loss_ref[...] = jnp.zeros_like(loss_ref)

        xv = x_ref[...]
        err = _rms_fwd(xv, g_ref[...]) - t_ref[...]
        per_tok = jnp.mean(err * err, axis=-1, keepdims=True)
        loss_ref[...] += 0.5 * jnp.sum(per_tok, axis=0, keepdims=True)
        dx, dg = _rms_bwd(err * (1.0 / D), xv, g_ref[...])
        dx_ref[...] = dx
        dgain_ref[...] += dg

    row = pl.BlockSpec((tm, D), lambda i: (i, 0))
    vec = pl.BlockSpec((1, D), lambda i: (0, 0))
    return pl.pallas_call(
        body, grid=(S // tm,), in_specs=[row, vec, row],
        out_specs=[row, vec, pl.BlockSpec((1, LANES), lambda i: (0, 0))],
        out_shape=[jax.ShapeDtypeStruct((S, D), F32), jax.ShapeDtypeStruct((1, D), F32),
                   jax.ShapeDtypeStruct((1, LANES), F32)],
        compiler_params=_params(1), name=name)(x, gain, target)


def _conv_tiles(S):
    ts = _tile(S, 256, CONV_HALO)
    return ts, ts // CONV_HALO


def _ln_stats(yc):
    mu = jnp.mean(yc, axis=-1, keepdims=True)
    d = yc - mu
    rs = lax.rsqrt(jnp.mean(d * d, axis=-1, keepdims=True) + LN_EPS)
    return d * rs, rs


def _conv_fwd(name, proj, C, cw, cb, lg, lb):
    S = proj.shape[0]
    ts, hb = _conv_tiles(S)

    def body(a_ref, g_ref, ah_ref, gh_ref, cw_ref, cb_ref, lg_ref, lb_ref, yc_ref, y_ref, ubuf):
        i = pl.program_id(0)
        uh = ah_ref[...] * _sigmoid(gh_ref[...])
        ubuf[pl.ds(0, CONV_HALO), :] = jnp.where(i > 0, uh, 0.0)
        ubuf[pl.ds(CONV_HALO, ts), :] = a_ref[...] * _sigmoid(g_ref[...])
        acc = jnp.zeros((ts, C), F32)
        for k in range(CONV_WIDTH):
            acc = acc + cw_ref[pl.ds(k, 1), :] * ubuf[pl.ds(k + CONV_HALO - CONV_WIDTH + 1, ts), :]
        yc = acc + cb_ref[...]
        yc_ref[...] = yc
        yn, _ = _ln_stats(yc)
        z = yn * lg_ref[...] + lb_ref[...]
        y_ref[...] = (z * _sigmoid(z)).astype(BF16)

    main = lambda col: pl.BlockSpec((ts, C), lambda i: (i, col))
    halo = lambda col: pl.BlockSpec((CONV_HALO, C), lambda i: (jnp.maximum(i * hb - 1, 0), col))
    vec = pl.BlockSpec((1, C), lambda i: (0, 0))
    return pl.pallas_call(
        body, grid=(S // ts,),
        in_specs=[main(0), main(1), halo(0), halo(1),
                  pl.BlockSpec((CONV_HALO, C), lambda i: (0, 0)), vec, vec, vec],
        out_specs=[pl.BlockSpec((ts, C), lambda i: (i, 0))] * 2,
        out_shape=[jax.ShapeDtypeStruct((S, C), F32), jax.ShapeDtypeStruct((S, C), BF16)],
        scratch_shapes=[pltpu.VMEM((ts + CONV_HALO, C), F32)],
        compiler_params=_params(1), name=name)(proj, proj, proj, proj, cw, cb, lg, lb)


def _conv_bwd(name, dmix, yc, proj, C, cw, lg, lb):
    S = proj.shape[0]
    ts, hb = _conv_tiles(S)
    n_t = S // ts

    def body(dy_ref, yc_ref, dyh_ref, ych_ref, a_ref, g_ref, ah_ref, gh_ref, cw_ref, lg_ref, lb_ref,
             dag_ref, dcw_ref, dcb_ref, dlg_ref, dlb_ref, ubuf, dbuf):
        i = pl.program_id(0)

        @pl.when(i == 0)
        def _():
            dcw_ref[...] = jnp.zeros_like(dcw_ref)
            dcb_ref[...] = jnp.zeros_like(dcb_ref)
            dlg_ref[...] = jnp.zeros_like(dlg_ref)
            dlb_ref[...] = jnp.zeros_like(dlb_ref)

        def ln_bwd(dy, ycv):
            yn, rs = _ln_stats(ycv)
            z = yn * lg_ref[...] + lb_ref[...]
            sg = _sigmoid(z)
            dz = dy * (sg * (1.0 + z * (1.0 - sg)))
            dyn = dz * lg_ref[...]
            dyc = rs * (dyn - jnp.mean(dyn, axis=-1, keepdims=True)
                        - yn * jnp.mean(dyn * yn, axis=-1, keepdims=True))
            return dyc, dz, yn

        dyc, dz, yn = ln_bwd(dy_ref[...], yc_ref[...])
        dlg_ref[...] += jnp.sum(dz * yn, axis=0, keepdims=True)
        dlb_ref[...] += jnp.sum(dz, axis=0, keepdims=True)
        dcb_ref[...] += jnp.sum(dyc, axis=0, keepdims=True)
        dych, _, _ = ln_bwd(dyh_ref[...], ych_ref[...])
        dbuf[pl.ds(0, ts), :] = dyc
        dbuf[pl.ds(ts, CONV_HALO), :] = jnp.where(i < n_t - 1, dych, 0.0)

        av = a_ref[...]
        sgm = _sigmoid(g_ref[...])
        uh = ah_ref[...] * _sigmoid(gh_ref[...])
        ubuf[pl.ds(0, CONV_HALO), :] = jnp.where(i > 0, uh, 0.0)
        ubuf[pl.ds(CONV_HALO, ts), :] = av * sgm

        du = jnp.zeros((ts, C), F32)
        for k in range(CONV_WIDTH):
            du = du + cw_ref[pl.ds(k, 1), :] * dbuf[pl.ds(CONV_WIDTH - 1 - k, ts), :]
            tap = ubuf[pl.ds(k + CONV_HALO - CONV_WIDTH + 1, ts), :]
            dcw_ref[pl.ds(k, 1), :] += jnp.sum(dyc * tap, axis=0, keepdims=True)
        dag_ref[:, pl.ds(0, C)] = (du * sgm).astype(BF16)
        dag_ref[:, pl.ds(C, C)] = (du * av * sgm * (1.0 - sgm)).astype(BF16)

    main = lambda col: pl.BlockSpec((ts, C), lambda i: (i, col))
    past = lambda col: pl.BlockSpec((CONV_HALO, C), lambda i: (jnp.maximum(i * hb - 1, 0), col))
    nxt = pl.BlockSpec((CONV_HALO, C), lambda i: (jnp.minimum((i + 1) * hb, n_t * hb - 1), 0))
    vec = pl.BlockSpec((1, C), lambda i: (0, 0))
    full = pl.BlockSpec((CONV_HALO, C), lambda i: (0, 0))
    vshape = jax.ShapeDtypeStruct((1, C), F32)
    return pl.pallas_call(
        body, grid=(n_t,),
        in_specs=[main(0), main(0), nxt, nxt, main(0), main(1), past(0), past(1), full, vec, vec],
        out_specs=[pl.BlockSpec((ts, 2 * C), lambda i: (i, 0)), full, vec, vec, vec],
        out_shape=[jax.ShapeDtypeStruct((S, 2 * C), BF16),
                   jax.ShapeDtypeStruct((CONV_HALO, C), F32), vshape, vshape, vshape],
        scratch_shapes=[pltpu.VMEM((ts + CONV_HALO, C), F32), pltpu.VMEM((ts + CONV_HALO, C), F32)],
        compiler_params=_params(1),
        name=name)(dmix, yc, dmix, yc, proj, proj, proj, proj, cw, lg, lb)


AUG = 3
ROW_CHUNK = 32


def _gate_prep(name, proj, f_blk, fbias, n_pair):
    S = proj.shape[0]
    ts = _tile(S, 512)
    W = LANES * n_pair

    def body(pf_ref, fb_ref, ka_ref, carry):
        @pl.when(pl.program_id(0) == 0)
        def _():
            carry[...] = jnp.zeros_like(carry)

        f = pf_ref[...] + fb_ref[...]
        logf = jnp.minimum(f, 0.0) - jnp.log(1.0 + jnp.exp(-jnp.abs(f)))
        r = lax.broadcasted_iota(jnp.int32, (ts, ts), 0)
        c = lax.broadcasted_iota(jnp.int32, (ts, ts), 1)
        ltri = (c <= r).astype(BF16)
        hi, mid, lo = _split3(logf)
        cs = _dot(ltri, hi) + _dot(ltri, mid) + _dot(ltri, lo) + carry[...]
        carry[...] = cs[ts - 1:ts, :]
        hh = lax.broadcasted_iota(jnp.int32, (LANES, W), 0)
        ll = lax.broadcasted_iota(jnp.int32, (LANES, W), 1)
        pair, w = ll >> 7, ll & (LANES - 1)
        ka = jnp.zeros((ts, W), F32)
        for p, piece in enumerate(_split3(-cs)):
            e = (((w == HEAD_DIM + p) & (hh == 2 * pair)) | ((w == p) & (hh == 2 * pair + 1)))
            ka = ka + _dot(piece, e.astype(BF16))
        lw = lax.broadcasted_iota(jnp.int32, (1, W), 1) & (LANES - 1)
        ka = ka + ((lw == HEAD_DIM + AUG) | (lw == AUG)).astype(F32)
        ka_ref[...] = ka.astype(BF16)

    return pl.pallas_call(
        body, grid=(S // ts,),
        in_specs=[pl.BlockSpec((ts, LANES), lambda i: (i, f_blk)),
                  pl.BlockSpec((1, LANES), lambda i: (0, 0))],
        out_specs=pl.BlockSpec((ts, W), lambda i: (i, 0)),
        out_shape=jax.ShapeDtypeStruct((S, W), BF16),
        scratch_shapes=[pltpu.VMEM((1, LANES), F32)],
        compiler_params=_params(1), name=name)(proj, fbias)


def _gate_bwd(name, sp, rs, proj, f_blk, fbias, n_pair):
    S = proj.shape[0]
    ts = _tile(S, 512)
    n_t = S // ts
    W = LANES * n_pair

    def body(sp_ref, rs_ref, pf_ref, fb_ref, df_ref, dfb_ref, carry):
        @pl.when(pl.program_id(0) == 0)
        def _():
            carry[...] = jnp.zeros_like(carry)
            dfb_ref[...] = jnp.zeros_like(dfb_ref)

        ll = lax.broadcasted_iota(jnp.int32, (W, LANES), 0)
        hh = lax.broadcasted_iota(jnp.int32, (W, LANES), 1)
        pair, w = ll >> 7, ll & (LANES - 1)
        first, second = hh == 2 * pair, hh == 2 * pair + 1

        def pick(ref, lane_first, lane_second):
            sel = (((w == lane_first) & first) | ((w == lane_second) & second)).astype(BF16)
            hi, mid, lo = _split3(ref[...])
            return _dot(hi, sel) + _dot(mid, sel) + _dot(lo, sel)

        dc = pick(rs_ref, HEAD_DIM + AUG, AUG) - pick(sp_ref, HEAD_DIM, 0)
        r = lax.broadcasted_iota(jnp.int32, (ts, ts), 0)
        c = lax.broadcasted_iota(jnp.int32, (ts, ts), 1)
        utri = (c >= r).astype(BF16)
        hi, mid, lo = _split3(dc)
        dlogf = _dot(utri, hi) + _dot(utri, mid) + _dot(utri, lo) + carry[...]
        carry[...] = dlogf[0:1, :]
        f = pf_ref[...] + fb_ref[...]
        lane = lax.broadcasted_iota(jnp.int32, (ts, LANES), 1)
        df = jnp.where(lane < 2 * n_pair, dlogf * _sigmoid(-f), 0.0)
        df_ref[...] = df.astype(BF16)
        dfb_ref[...] += jnp.sum(df, axis=0, keepdims=True)

    rev = lambda blk: (lambda i: (n_t - 1 - i, blk))
    return pl.pallas_call(
        body, grid=(n_t,),
        in_specs=[pl.BlockSpec((ts, W), rev(0)), pl.BlockSpec((ts, W), rev(0)),
                  pl.BlockSpec((ts, LANES), rev(f_blk)), pl.BlockSpec((1, LANES), lambda i: (0, 0))],
        out_specs=[pl.BlockSpec((ts, LANES), rev(0)), pl.BlockSpec((1, LANES), lambda i: (0, 0))],
        out_shape=[jax.ShapeDtypeStruct((S, LANES), BF16), jax.ShapeDtypeStruct((1, LANES), F32)],
        scratch_shapes=[pltpu.VMEM((1, LANES), F32)],
        compiler_params=_params(1), name=name)(sp, rs, proj, fbias)


def _spare_lane(h):
    return HEAD_DIM if h == 0 else 0


def _head_operands(h, lane, q2, k2, ka2):
    act = (lane < HEAD_DIM) if h == 0 else (lane >= HEAD_DIM)
    base = HEAD_DIM if h == 0 else 0
    ones = ((lane >= base) & (lane < base + AUG)).astype(F32)
    qa = jnp.where(act, q2 * (1.0 / math.sqrt(HEAD_DIM)), ones).astype(BF16)
    ka = jnp.where(act, k2.astype(BF16), ka2)
    return act, qa, ka


def _attn_fwd(name, proj, ka, q_blk, k_blk, v_blk, n_pair):
    S = proj.shape[0]
    tq = _tile(S, 512)
    n_t = S // tq
    W = LANES * n_pair

    rc = _tile(tq, ROW_CHUNK)

    def body(q_ref, k_ref, v_ref, ka_ref, o_ref, o32_ref, lse_ref, m_ref, acc_ref, s_ref, p_ref):
        i, j = pl.program_id(1), pl.program_id(2)

        @pl.when(j == 0)
        def _():
            m_ref[...] = jnp.full_like(m_ref, NEG_INF)
            acc_ref[...] = jnp.zeros_like(acc_ref)

        def step(diagonal):
            lane = lax.broadcasted_iota(jnp.int32, (tq, LANES), 1)
            q2, k2, v2, ka2 = q_ref[...], k_ref[...], v_ref[...], ka_ref[...]
            for h in range(2):
                act, qa, kaug = _head_operands(h, lane, q2, k2, ka2)
                s = _dot_nt(qa, kaug)
                if diagonal:
                    row = lax.broadcasted_iota(jnp.int32, (tq, tq), 0)
                    col = lax.broadcasted_iota(jnp.int32, (tq, tq), 1)
                    s = jnp.where(row >= col, s, NEG_INF)
                s_ref[...] = s
                m_prev = m_ref[h]
                m_new = jnp.maximum(m_prev, jnp.max(s, axis=-1, keepdims=True))
                m_ref[h] = m_new
                for r in range(tq // rc):
                    rows = pl.ds(r * rc, rc)
                    p_ref[rows, :] = jnp.exp(s_ref[rows, :] - m_ref[h, rows, :]).astype(BF16)
                vm = jnp.where(act, v2, (lane == _spare_lane(h)).astype(F32)).astype(BF16)
                acc_ref[h] = jnp.exp(m_prev - m_new) * acc_ref[h] + _dot(p_ref[...], vm)

        @pl.when(j < i)
        def _():
            step(False)

        @pl.when(j == i)
        def _():
            step(True)
            lane = lax.broadcasted_iota(jnp.int32, (tq, LANES), 1)
            first = lane < HEAD_DIM
            acc = [acc_ref[0], acc_ref[1]]
            den = [acc[h][:, _spare_lane(h):_spare_lane(h) + 1] for h in range(2)]
            out = jnp.where(first, acc[0] / den[0], acc[1] / den[1])
            o_ref[...] = out.astype(BF16)
            o32_ref[...] = out
            lse_ref[...] = jnp.where(first, m_ref[0] + jnp.log(den[0]), m_ref[1] + jnp.log(den[1]))

    qspec = lambda blk: pl.BlockSpec((tq, LANES), lambda p, i, j: (i, blk + p))
    kspec = lambda blk: pl.BlockSpec((tq, LANES), lambda p, i, j: (jnp.minimum(j, i), blk + p))
    out = pl.BlockSpec((tq, LANES), lambda p, i, j: (i, p))
    return pl.pallas_call(
        body, grid=(n_pair, n_t, n_t),
        in_specs=[qspec(q_blk), kspec(k_blk), kspec(v_blk), kspec(0)],
        out_specs=[out, out, out],
        out_shape=[jax.ShapeDtypeStruct((S, W), BF16), jax.ShapeDtypeStruct((S, W), F32),
                   jax.ShapeDtypeStruct((S, W), F32)],
        scratch_shapes=[pltpu.VMEM((2, tq, 1), F32), pltpu.VMEM((2, tq, LANES), F32),
                        pltpu.VMEM((tq, tq), F32), pltpu.VMEM((tq, tq), BF16)],
        compiler_params=_params(3), name=name)(proj, proj, proj, ka)


def _attn_bwd(name, proj, ka, o, lse, dmix, q_blk, k_blk, v_blk, do_blk, n_pair):
    S = proj.shape[0]
    tq = _tile(S, 512)
    n_t = S // tq
    W = LANES * n_pair
    scale = 1.0 / math.sqrt(HEAD_DIM)
    rc = _tile(tq, ROW_CHUNK)

    def body(q_ref, k_ref, v_ref, ka_ref, o_ref, lse_ref, do_ref,
             dq_ref, dk_ref, dv_ref, sp_ref, rs_ref, dk_acc, dv_acc, s_ref, dp_ref, p_ref, ds_ref, d_ref):
        j, i = pl.program_id(1), pl.program_id(2)

        @pl.when((j == 0) & (i == 0))
        def _():
            dq_ref[...] = jnp.zeros_like(dq_ref)
            rs_ref[...] = jnp.zeros_like(rs_ref)

        @pl.when(i == 0)
        def _():
            dk_acc[...] = jnp.zeros_like(dk_acc)
            dv_acc[...] = jnp.zeros_like(dv_acc)

        def step(diagonal):
            lane = lax.broadcasted_iota(jnp.int32, (tq, LANES), 1)
            q2, k2, v2, ka2 = q_ref[...], k_ref[...], v_ref[...], ka_ref[...]
            o2, do2 = o_ref[...], do_ref[...]
            dq = []
            for h in range(2):
                act, qa, kaug = _head_operands(h, lane, q2, k2, ka2)
                dom = jnp.where(act, do2, 0.0)
                d_ref[...] = jnp.sum(dom * o2, axis=-1, keepdims=True)
                dob = dom.astype(BF16)
                s_ref[...] = _dot_nt(qa, kaug)
                dp_ref[...] = _dot_nt(dob, jnp.where(act, v2, 0.0).astype(BF16))
                for r in range(tq // rc):
                    rows = pl.ds(r * rc, rc)
                    sc = s_ref[rows, :]
                    if diagonal:
                        row = lax.broadcasted_iota(jnp.int32, (rc, tq), 0) + r * rc
                        col = lax.broadcasted_iota(jnp.int32, (rc, tq), 1)
                        sc = jnp.where(row >= col, sc, NEG_INF)
                    p = jnp.exp(sc - lse_ref[rows, :][:, h * HEAD_DIM:h * HEAD_DIM + 1])
                    p_ref[rows, :] = p.astype(BF16)
                    ds_ref[rows, :] = (p * (dp_ref[rows, :] - d_ref[rows, :])).astype(BF16)
                dv_acc[...] += _dot_tn(p_ref[...], dob)
                dk_acc[h] += _dot_tn(ds_ref[...], qa)
                dq.append(_dot(ds_ref[...], kaug))
            rows = pl.ds(pl.multiple_of(i * tq, tq), tq)
            first = lane < HEAD_DIM
            dq_ref[rows, :] += jnp.where(first, dq[0], dq[1])
            rs_ref[rows, :] += jnp.where(first, dq[1], dq[0])

        @pl.when(i > j)
        def _():
            step(False)

        @pl.when(i == j)
        def _():
            step(True)

        @pl.when(i == n_t - 1)
        def _():
            lane = lax.broadcasted_iota(jnp.int32, (tq, LANES), 1)
            first = lane < HEAD_DIM
            dk_ref[...] = jnp.where(first, dk_acc[0], dk_acc[1]).astype(BF16)
            sp_ref[...] = jnp.where(first, dk_acc[1], dk_acc[0])
            dv_ref[...] = dv_acc[...].astype(BF16)

        @pl.when((j == n_t - 1) & (i == n_t - 1))
        def _():
            dq_ref[...] = dq_ref[...] * scale

    qspec = lambda blk: pl.BlockSpec((tq, LANES), lambda p, j, i: (jnp.maximum(i, j), blk + p))
    kspec = lambda blk: pl.BlockSpec((tq, LANES), lambda p, j, i: (j, blk + p))
    kout = pl.BlockSpec((tq, LANES), lambda p, j, i: (j, p))
    qres = pl.BlockSpec((S, LANES), lambda p, j, i: (0, p))
    return pl.pallas_call(
        body, grid=(n_pair, n_t, n_t),
        in_specs=[qspec(q_blk), kspec(k_blk), kspec(v_blk), kspec(0),
                  qspec(0), qspec(0), qspec(do_blk)],
        out_specs=[qres, kout, kout, kout, qres],
        out_shape=[jax.ShapeDtypeStruct((S, W), F32), jax.ShapeDtypeStruct((S, W), BF16),
                   jax.ShapeDtypeStruct((S, W), BF16), jax.ShapeDtypeStruct((S, W), F32),
                   jax.ShapeDtypeStruct((S, W), F32)],
        scratch_shapes=[pltpu.VMEM((2, tq, LANES), F32), pltpu.VMEM((tq, LANES), F32),
                        pltpu.VMEM((tq, tq), F32), pltpu.VMEM((tq, tq), F32),
                        pltpu.VMEM((tq, tq), BF16), pltpu.VMEM((tq, tq), BF16), pltpu.VMEM((tq, 1), F32)],
        compiler_params=_params(3), name=name)(proj, proj, proj, ka, o, lse, dmix)


def _ffn_block_grad(name, act, rows, scale, dep=None):
    nb, S, Fs = act.shape
    D = rows.shape[1]
    tm = _tile(S, 512)
    return _mm_tn(name, act, rows, (nb, Fs, D),
                  pl.BlockSpec((1, tm, Fs), lambda j, i: (j, i, 0)),
                  pl.BlockSpec((tm, D), lambda j, i: (i, 0)),
                  pl.BlockSpec((1, Fs, D), lambda j, i: (j, 0, 0)), (nb, S // tm), scale, BF16, dep=dep)


def kernel(x, ffn1_norm, ffn1_w_gate, ffn1_w_up, ffn1_w_down, mix_norm, w_in, fgate_bias, conv_w, conv_b, conv_ln_g, conv_ln_b, w_out, ffn2_norm, ffn2_w_gate, ffn2_w_up, ffn2_w_down, final_norm, loss_target, m_ffn1_norm, m_ffn1_w_gate, m_ffn1_w_up, m_ffn1_w_down, m_mix_norm, m_w_in, m_fgate_bias, m_conv_w, m_conv_b, m_conv_ln_g, m_conv_ln_b, m_w_out, m_ffn2_norm, m_ffn2_w_gate, m_ffn2_w_up, m_ffn2_w_down, m_final_norm, v_ffn1_norm, v_ffn1_w_gate, v_ffn1_w_up, v_ffn1_w_down, v_mix_norm, v_w_in, v_fgate_bias, v_conv_w, v_conv_b, v_conv_ln_g, v_conv_ln_b, v_w_out, v_ffn2_norm, v_ffn2_w_gate, v_ffn2_w_up, v_ffn2_w_down, v_final_norm):
    xs = x[0]
    S, D = xs.shape
    C = conv_b.shape[0]
    n_heads = fgate_bias.shape[0]
    FW = n_heads * HEAD_DIM
    n_pair = n_heads // 2
    MIX = C + FW
    in_shard = w_in.shape[1]
    in_cols = in_shard * N_DEV
    NP = -(-in_cols // 512) * 512
    q_blk, k_blk, v_blk = 2 * C // LANES, (2 * C + FW) // LANES, (2 * C + 2 * FW) // LANES
    f_blk = (2 * C + 3 * FW) // LANES
    assert C % LANES == 0 and FW % LANES == 0 and n_heads % 2 == 0 and n_heads <= LANES
    assert in_cols == 2 * C + 3 * FW + n_heads and MIX == w_out.shape[0] * N_DEV

    vec = lambda a: a.reshape(1, -1)
    bf = lambda a: a.astype(BF16)
    tm = _tile(S, 512)

    wgt1, wut1, wd1 = _all_gather_two_level("ag_ffn1", [bf(ffn1_w_gate).T, bf(ffn1_w_up).T, bf(ffn1_w_down)])
    ag, _ = _xchg_start("ag_start", [bf(w_in).T, conv_w, bf(w_out), bf(ffn2_w_gate).T, bf(ffn2_w_up).T,
                                     bf(ffn2_w_down)], False, dep=wd1)
    fbias = jnp.pad(vec(fgate_bias), ((0, 0), (0, LANES - n_heads)))

    x1, h1, G1, U1 = _ffn_fwd("ffn1_fwd", xs, vec(ffn1_norm), wgt1, wut1, wd1)
    win_g, cw_g = _xchg_wait("ag_wait_in", ag[0:2], False, x1)
    wint = jnp.pad(win_g.reshape(in_cols, D), ((0, NP - in_cols), (0, 0)))
    cw = jnp.pad(cw_g.transpose(1, 0, 2).reshape(CONV_WIDTH, C), ((0, CONV_HALO - CONV_WIDTH), (0, 0)))
    proj, h2 = _norm_mm_nt("proj_in", x1, vec(mix_norm), wint, 512)
    yc, y_conv = _conv_fwd("conv_fwd", proj, C, cw, vec(conv_b), vec(conv_ln_g), vec(conv_ln_b))
    ka = _gate_prep("gate_prep", proj, f_blk, fbias, n_pair)
    o, o32, lse = _attn_fwd("attn_fwd", proj, ka, q_blk, k_blk, v_blk, n_pair)
    (wout_g,) = _xchg_wait("ag_wait_out", ag[2:3], False, lse)
    wout = wout_g.reshape(MIX, D)
    mix = jnp.concatenate([y_conv, o], axis=1)
    x2 = _mm_res("proj_out", mix, wout, x1, 512)
    wgt2, wut2, wd2 = _xchg_wait("ag_wait_ffn2", ag[3:6], False, x2)
    x3, h3, G2, U2 = _ffn_fwd("ffn2_fwd", x2, vec(ffn2_norm), wgt2, wut2, wd2)

    dx3, d_final_norm, loss_part = _final("final", x3, vec(final_norm), loss_target[0])
    dG2, dU2, A2 = _ffn_bwd_gate("ffn2_bwd_gate", dx3, G2, U2, wd2)
    dx2, d_ffn2_norm = _ffn_bwd_in("ffn2_bwd_in", dG2, dU2, wgt2, wut2, dx3, x2, vec(ffn2_norm))
    dwd2 = _ffn_block_grad("ffn2_dwd", A2, dx3, 0.5)
    s_d2, tok = _xchg_start("a2a_start_ffn2_wd", [dwd2], True)
    dwg2 = _ffn_block_grad("ffn2_dwg", dG2, h3, 1.0, dep=tok)
    s_g2, tok = _xchg_start("a2a_start_ffn2_wg", [dwg2], True)
    dwu2 = _ffn_block_grad("ffn2_dwu", dU2, h3, 1.0, dep=tok)
    s_u2, tok = _xchg_start("a2a_start_ffn2_wu", [dwu2], True)

    dmix = _mm_k("dmix", dx2, wout, 512, True, dep=tok)
    d_wout = _mm_tn("dwout", mix, dx2, (MIX, D),
                    pl.BlockSpec((tm, MIX), lambda j, i: (i, 0)), pl.BlockSpec((tm, 512), lambda j, i: (i, j)),
                    pl.BlockSpec((MIX, 512), lambda j, i: (0, j)), (D // 512, S // tm), 1.0, BF16)
    s_out, tok = _xchg_start("a2a_start_w_out", [d_wout.reshape(N_DEV, MIX // N_DEV, D)], True)
    dag, d_cw, d_cb, d_lg, d_lb = _conv_bwd("conv_bwd", dmix, yc, proj, C, cw, vec(conv_ln_g), vec(conv_ln_b))
    dq, dk, dv, sp, rs = _attn_bwd("attn_bwd", proj, ka, o32, lse, dmix, q_blk, k_blk, v_blk, C // LANES, n_pair)
    df, d_fb = _gate_bwd("gate_bwd", sp, rs, proj, f_blk, fbias, n_pair)
    dproj = jnp.concatenate([dag, bf(dq), dk, dv, df, jnp.zeros((S, NP - f_blk * LANES - LANES), BF16)], axis=1)
    d_wint = _mm_tn("dwin", dproj, h2, (NP, D),
                    pl.BlockSpec((tm, 512), lambda j, i: (i, j)), pl.BlockSpec((tm, D), lambda j, i: (i, 0)),
                    pl.BlockSpec((512, D), lambda j, i: (j, 0)), (NP // 512, S // tm), 1.0, BF16, dep=tok)
    s_in, tok = _xchg_start("a2a_start_w_in", [d_wint[:in_cols].reshape(N_DEV, in_shard, D)], True)
    dx1, d_mix_norm = _mm_k("dh2", dproj, wint, 512, False, norm_bwd=(x1, vec(mix_norm), dx2), dep=tok)
    dG1, dU1, A1 = _ffn_bwd_gate("ffn1_bwd_gate", dx1, G1, U1, wd1)

    rows = lambda a: a.reshape(-1, C)
    pad_row = lambda a: jnp.pad(a.reshape(1, -1), ((0, 0), (0, C - a.size)))
    pieces = [rows(d_mix_norm), rows(d_ffn2_norm), rows(d_final_norm),
              d_cw[:CONV_WIDTH], d_cb, d_lg, d_lb, pad_row(d_fb[0, :n_heads]), pad_row(loss_part[0, :1])]
    pack = jnp.concatenate(pieces, axis=0)
    n_rows = pack.shape[0]
    pack = jnp.pad(pack, ((0, -n_rows % SUBLANES), (0, 0)))
    (pack_g,) = _exchange("ag_small", [pack], False)
    tot = _reduce_adam("sum_small", pack_g)
    nd = D // C
    g_mix_norm, g_ffn2_norm, g_final_norm = (tot[k * nd:(k + 1) * nd].reshape(D) for k in range(3))
    r0 = 3 * nd
    me = _lin(_mesh_pos())
    cs = C // N_DEV
    g_conv_w = lax.dynamic_slice(tot[r0:r0 + CONV_WIDTH], (0, me * cs), (CONV_WIDTH, cs))
    g_conv_b, g_ln_g, g_ln_b = tot[r0 + CONV_WIDTH], tot[r0 + CONV_WIDTH + 1], tot[r0 + CONV_WIDTH + 2]
    g_fb = tot[r0 + CONV_WIDTH + 3, :n_heads]
    loss = tot[r0 + CONV_WIDTH + 4, 0]

    small = [(g_mix_norm, mix_norm, m_mix_norm, v_mix_norm),
             (g_fb, fgate_bias, m_fgate_bias, v_fgate_bias), (g_conv_w, conv_w, m_conv_w, v_conv_w),
             (g_conv_b, conv_b, m_conv_b, v_conv_b), (g_ln_g, conv_ln_g, m_conv_ln_g, v_conv_ln_g),
             (g_ln_b, conv_ln_b, m_conv_ln_b, v_conv_ln_b), (g_ffn2_norm, ffn2_norm, m_ffn2_norm, v_ffn2_norm),
             (g_final_norm, final_norm, m_final_norm, v_final_norm)]
    sizes = [g.size for g, _, _, _ in small]
    total = sum(sizes)
    padded = -(-total // (SUBLANES * LANES)) * (SUBLANES * LANES)

    def flat_pack(k, fill):
        flat = jnp.concatenate([t[k].reshape(-1) for t in small])
        return jnp.pad(flat, (0, padded - total), constant_values=fill).reshape(padded // LANES, LANES)

    sg, sd, sm, sv = _reduce_adam("adam_small", flat_pack(0, 0.0)[None], flat_pack(1, 0.0), flat_pack(2, 0.0),
                                  flat_pack(3, 1.0))

    def unpack(packed):
        flat = packed.reshape(-1)
        out, off = [], 0
        for (g, _, _, _), n in zip(small, sizes):
            out.append(flat[off:off + n].reshape(g.shape))
            off += n
        return out

    s_g, s_d, s_m, s_v = unpack(sg), unpack(sd), unpack(sm), unpack(sv)

    dwd1 = _ffn_block_grad("ffn1_dwd", A1, dx1, 0.5, dep=sd)
    s_d1, tok = _xchg_start("a2a_start_ffn1_wd", [dwd1], True)
    dwg1 = _ffn_block_grad("ffn1_dwg", dG1, h1, 1.0, dep=tok)
    s_g1, tok = _xchg_start("a2a_start_ffn1_wg", [dwg1], True)
    dwu1 = _ffn_block_grad("ffn1_dwu", dU1, h1, 1.0, dep=tok)
    s_u1, tok = _xchg_start("a2a_start_ffn1_wu", [dwu1], True)
    dx0, d_ffn1_norm = _ffn_bwd_in("ffn1_bwd_in", dG1, dU1, wgt1, wut1, dx1, xs, vec(ffn1_norm) + tok[:1, :1])

    r_d2, r_g2, r_u2, r_out, r_in = _xchg_wait("a2a_wait_a", s_d2 + s_g2 + s_u2 + s_out + s_in, True, dx0)
    tr = lambda a: a.T

    def adam_t(name, recv, w, m, v):
        return tuple(tr(r) for r in _reduce_adam(name, recv, tr(w), tr(m), tr(v)))

    res = {
        "ffn2_w_down": _reduce_adam("adam_ffn2_wd", r_d2, ffn2_w_down, m_ffn2_w_down, v_ffn2_w_down),
        "ffn2_w_gate": adam_t("adam_ffn2_wg", r_g2, ffn2_w_gate, m_ffn2_w_gate, v_ffn2_w_gate),
        "ffn2_w_up": adam_t("adam_ffn2_wu", r_u2, ffn2_w_up, m_ffn2_w_up, v_ffn2_w_up),
        "w_out": _reduce_adam("adam_w_out", r_out, w_out, m_w_out, v_w_out),
        "w_in": adam_t("adam_w_in", r_in, w_in, m_w_in, v_w_in),
    }
    (r_d1,) = _xchg_wait("a2a_wait_d1", s_d1, True, res["w_in"][0])
    res["ffn1_w_down"] = _reduce_adam("adam_ffn1_wd", r_d1, ffn1_w_down, m_ffn1_w_down, v_ffn1_w_down)
    (r_g1,) = _xchg_wait("a2a_wait_g1", s_g1, True, res["ffn1_w_down"][0])
    res["ffn1_w_gate"] = adam_t("adam_ffn1_wg", r_g1, ffn1_w_gate, m_ffn1_w_gate, v_ffn1_w_gate)
    (r_u1,) = _xchg_wait("a2a_wait_u1", s_u1, True, res["ffn1_w_gate"][0])
    res["ffn1_w_up"] = adam_t("adam_ffn1_wu", r_u1, ffn1_w_up, m_ffn1_w_up, v_ffn1_w_up)
    lanes = lambda a: a.reshape(-1, LANES)
    (norm1_g,) = _exchange("ag_norm1", [lanes(d_ffn1_norm)], False, dep=res["ffn1_w_up"][0])
    res["ffn1_norm"] = tuple(r.reshape(D) for r in _reduce_adam(
        "adam_norm1", norm1_g, lanes(ffn1_norm), lanes(m_ffn1_norm), lanes(v_ffn1_norm)))

    small_names = ["mix_norm", "fgate_bias", "conv_w", "conv_b", "conv_ln_g", "conv_ln_b",
                   "ffn2_norm", "final_norm"]
    for k, n in enumerate(small_names):
        res[n] = (s_g[k], s_d[k], s_m[k], s_v[k])
    order = ["ffn1_norm", "ffn1_w_gate", "ffn1_w_up", "ffn1_w_down", "mix_norm", "w_in", "fgate_bias",
             "conv_w", "conv_b", "conv_ln_g", "conv_ln_b", "w_out", "ffn2_norm", "ffn2_w_gate", "ffn2_w_up",
             "ffn2_w_down", "final_norm"]
    outs = [loss, dx0[None]]
    for k in range(4):
        outs += [res[n][k] for n in order]
    return tuple(outs)
```

```python
import math

import jax
import jax.numpy as jnp
from jax import lax
from jax.experimental import pallas as pl
from jax.experimental.pallas import tpu as pltpu

F32 = jnp.float32
BF16 = jnp.bfloat16

N_DEV = 8
MESH_ID = pl.DeviceIdType.MESH
HEAD_DIM = 64
CONV_WIDTH = 31
CONV_HALO = 32
NORM_EPS = 1e-6
LN_EPS = 1e-5
NEG_INF = -1e30
LANES = 128
SUBLANES = 8
V7X_VMEM_LIMIT = 52 * 1024 * 1024

ADAM_LR = 0.001
ADAM_B1 = 0.9
ADAM_B2 = 0.999
ADAM_EPS = 1e-08
ADAM_WD = 0.01
ADAM_STEP = 10


def _params(n_grid_axes):
    return pltpu.CompilerParams(dimension_semantics=("arbitrary",) * n_grid_axes,
                                vmem_limit_bytes=V7X_VMEM_LIMIT)


def _tile(n, pref, mult=8):
    t = min(pref, n)
    while t >= mult:
        if n % t == 0 and t % mult == 0:
            return t
        t -= mult
    return n


def _dot(a, b):
    return jnp.dot(a, b, preferred_element_type=F32)


def _dot_nt(a, b):
    return lax.dot_general(a, b, (((1,), (1,)), ((), ())), preferred_element_type=F32)


def _dot_tn(a, b):
    return lax.dot_general(a, b, (((0,), (0,)), ((), ())), preferred_element_type=F32)


def _sigmoid(x):
    return 1.0 / (1.0 + jnp.exp(-x))


def _rms_fwd(x, g):
    r = lax.rsqrt(jnp.mean(x * x, axis=-1, keepdims=True) + NORM_EPS)
    return x * r * g


def _rms_bwd(dh, x, g):
    r = lax.rsqrt(jnp.mean(x * x, axis=-1, keepdims=True) + NORM_EPS)
    xh = x * r
    dxh = dh * g
    dx = r * (dxh - xh * jnp.mean(dxh * xh, axis=-1, keepdims=True))
    return dx, jnp.sum(dh * xh, axis=0, keepdims=True)


def _split3(x):
    hi = x.astype(BF16)
    r = x - hi.astype(F32)
    mid = r.astype(BF16)
    lo = (r - mid.astype(F32)).astype(BF16)
    return hi, mid, lo


def _blk(ref):
    return ref[0] if len(ref.shape) == 3 else ref[...]


_DEP_SPEC = pl.BlockSpec(memory_space=pl.ANY)


def _mesh_pos():
    return lax.axis_index("x"), lax.axis_index("y"), lax.axis_index("c")


def _peer(pos, k):
    x, y, c = pos
    return (1 - x if k & 4 else x, 1 - y if k & 2 else y, 1 - c if k & 1 else c)


def _lin(pos):
    x, y, c = pos
    return 4 * x + 2 * y + c


def _remote_copy(src, land, send_sems, recv_sems, pos, k, all_to_all):
    peer = _peer(pos, k)
    return pltpu.make_async_remote_copy(
        src_ref=src.at[_lin(peer)] if all_to_all else src, dst_ref=land.at[_lin(pos)],
        send_sem=send_sems.at[k - 1], recv_sem=recv_sems.at[k - 1],
        device_id=peer, device_id_type=MESH_ID)


def _landing_shape(a, all_to_all):
    return a.shape if all_to_all else (N_DEV,) + a.shape


def _exchange(name, arrays, all_to_all, dep=None):
    n = len(arrays)
    deps = [] if dep is None else [dep]
    out_shapes = [jax.ShapeDtypeStruct(_landing_shape(a, all_to_all), a.dtype) for a in arrays]

    def body(*refs):
        ins, outs = refs[:n], refs[n + len(deps):2 * n + len(deps)]
        send_sems, recv_sems, local_sems = refs[2 * n + len(deps):]
        pos = _mesh_pos()
        me = _lin(pos)
        local = []
        for a in range(n):
            src = ins[a].at[me] if all_to_all else ins[a]
            cp = pltpu.make_async_copy(src, outs[a].at[me], local_sems.at[a])
            cp.start()
            local.append(cp)
        remote = [_remote_copy(ins[a], outs[a], send_sems.at[a], recv_sems.at[a], pos, k, all_to_all)
                  for a in range(n) for k in range(1, N_DEV)]
        for cp in remote:
            cp.start()
        for cp in remote:
            cp.wait()
        for cp in local:
            cp.wait()

    any_spec = pl.BlockSpec(memory_space=pl.ANY)
    return pl.pallas_call(
        body, out_shape=out_shapes, in_specs=[any_spec] * (n + len(deps)), out_specs=[any_spec] * n,
        scratch_shapes=[pltpu.SemaphoreType.DMA((n, N_DEV - 1)),
                        pltpu.SemaphoreType.DMA((n, N_DEV - 1)),
                        pltpu.SemaphoreType.DMA((n,))],
        name=name)(*arrays, *deps)


def _all_gather_two_level(name, shards):
    n = len(shards)
    out_shapes = [jax.ShapeDtypeStruct((N_DEV,) + a.shape, a.dtype) for a in shards]

    def body(*refs):
        ins, outs = refs[:n], refs[n:2 * n]
        send_sems, recv_sems, local_sems = refs[2 * n:]
        x, y, c = pos = _mesh_pos()
        sibling = (x, y, 1 - c)
        chips = [(1 - x, y), (x, 1 - y), (1 - x, 1 - y)]

        def copy(a, k, block, to, src=None):
            slot = outs[a].at[_lin(block)]
            return pltpu.make_async_remote_copy(
                src_ref=slot if src is None else src, dst_ref=slot,
                send_sem=send_sems.at[a, k], recv_sem=recv_sems.at[a, k],
                device_id=to, device_id_type=MESH_ID)

        local = [pltpu.make_async_copy(ins[a], outs[a].at[_lin(pos)], local_sems.at[a]) for a in range(n)]
        first = [copy(a, 1 + j, pos, (*chip, c), src=ins[a]) for j, chip in enumerate(chips) for a in range(n)]
        first += [copy(a, 0, pos, sibling, src=ins[a]) for a in range(n)]
        for cp in first + local:
            cp.start()
        passed = []
        for j, chip in enumerate(chips):
            for a in range(n):
                copy(a, 1 + j, (*chip, c), pos).wait_recv()
                cp = copy(a, 4 + j, (*chip, c), sibling)
                cp.start()
                passed.append(cp)
        for a in range(n):
            copy(a, 0, sibling, pos).wait_recv()
        for j, chip in enumerate(chips):
            for a in range(n):
                copy(a, 4 + j, (*chip, 1 - c), pos).wait_recv()
        for cp in first + passed:
            cp.wait_send()
        for cp in local:
            cp.wait()

    any_spec = pl.BlockSpec(memory_space=pl.ANY)
    return pl.pallas_call(
        body, out_shape=out_shapes, in_specs=[any_spec] * n, out_specs=[any_spec] * n,
        scratch_shapes=[pltpu.SemaphoreType.DMA((n, N_DEV - 1)),
                        pltpu.SemaphoreType.DMA((n, N_DEV - 1)),
                        pltpu.SemaphoreType.DMA((n,))],
        name=name)(*shards)


_HBM_SPEC = pl.BlockSpec(memory_space=pltpu.HBM)
_SEM_SPEC = pl.BlockSpec(memory_space=pltpu.SEMAPHORE)
_SIDE_EFFECT = pltpu.SideEffectType.DATAFLOW_SIDE_EFFECTING


def _xchg_start(name, arrays, all_to_all, dep=None):
    n = len(arrays)
    me = _lin(_mesh_pos())
    lands = [lax.dynamic_update_index_in_dim(
        lax.empty(_landing_shape(a, all_to_all), a.dtype),
        lax.dynamic_index_in_dim(a, me, 0, keepdims=False) if all_to_all else a, me, 0) for a in arrays]

    deps = [] if dep is None else [dep]

    def body(*refs):
        srcs, lnds = refs[:n], refs[n:2 * n]
        outs = refs[2 * n + len(deps):]
        send, recv = outs[:n], outs[n:2 * n]
        token = outs[4 * n]
        pos = _mesh_pos()
        for a in range(n):
            for k in range(1, N_DEV):
                _remote_copy(srcs[a], lnds[a], send[a], recv[a], pos, k, all_to_all).start()
        token[...] = jnp.zeros_like(token)

    hbm = lambda a: pltpu.HBM(a.shape, a.dtype)
    sems = [pltpu.SemaphoreType.DMA((N_DEV - 1,))] * (2 * n)
    res = pl.pallas_call(
        body, name=name,
        out_shape=sems + [hbm(a) for a in arrays] + [hbm(l) for l in lands]
        + [jax.ShapeDtypeStruct((SUBLANES, LANES), F32)],
        in_specs=[_HBM_SPEC] * (2 * n) + [_DEP_SPEC] * len(deps),
        out_specs=[_SEM_SPEC] * (2 * n) + [_HBM_SPEC] * (2 * n) + [pl.BlockSpec(memory_space=pltpu.VMEM)],
        input_output_aliases={a: 2 * n + a for a in range(2 * n)},
        compiler_params=pltpu.CompilerParams(has_side_effects=_SIDE_EFFECT),
    )(*[pltpu.with_memory_space_constraint(a, pltpu.HBM) for a in list(arrays) + lands], *deps)
    per_array = [(res[a], res[n + a], res[2 * n + a], res[3 * n + a]) for a in range(n)]
    return per_array, res[4 * n]


def _xchg_wait(name, started, all_to_all, after):
    n = len(started)

    def body(*refs):
        srcs, lnds, send, recv = refs[:n], refs[n:2 * n], refs[2 * n:3 * n], refs[3 * n:4 * n]
        pos = _mesh_pos()
        for a in range(n):
            for k in range(1, N_DEV):
                cp = _remote_copy(srcs[a], lnds[a], send[a], recv[a], pos, k, all_to_all)
                cp.wait_send()
                cp.wait_recv()

    hbm = lambda a: pltpu.HBM(a.shape, a.dtype)
    srcs = [s[2] for s in started]
    lands = [s[3] for s in started]
    res = pl.pallas_call(
        body, name=name,
        out_shape=[hbm(a) for a in srcs + lands],
        in_specs=[_HBM_SPEC] * (2 * n) + [_SEM_SPEC] * (2 * n) + [_DEP_SPEC],
        out_specs=[_HBM_SPEC] * (2 * n),
        input_output_aliases={a: a for a in range(2 * n)},
        compiler_params=pltpu.CompilerParams(has_side_effects=_SIDE_EFFECT),
    )(*srcs, *lands, *[s[0] for s in started], *[s[1] for s in started], after)
    return list(res[n:])


def _reduce_adam(name, parts, w=None, m=None, v=None):
    n, R, C = parts.shape
    tr = _tile(R, 256)
    do_adam = w is not None
    bc1 = 1.0 - ADAM_B1 ** ADAM_STEP
    bc2 = 1.0 - ADAM_B2 ** ADAM_STEP

    def body(*refs):
        p_ref = refs[0]
        g = p_ref[0].astype(F32)
        for d in range(1, n):
            g = g + p_ref[d].astype(F32)
        if not do_adam:
            refs[1][...] = g
            return
        w_ref, m_ref, v_ref, g_ref, d_ref, nm_ref, nv_ref = refs[1:]
        g_ref[...] = g
        nm = ADAM_B1 * m_ref[...] + (1.0 - ADAM_B1) * g
        nv = ADAM_B2 * v_ref[...] + (1.0 - ADAM_B2) * (g * g)
        m_hat = nm / bc1
        v_hat = nv / bc2
        d_ref[...] = -ADAM_LR * (m_hat / (jnp.sqrt(v_hat) + ADAM_EPS) + ADAM_WD * w_ref[...])
        nm_ref[...] = nm
        nv_ref[...] = nv

    tc = _tile(C, 512, LANES) if tr == R and R > 256 else C
    row = pl.BlockSpec((tr, tc), lambda i, j: (i, j))
    part = pl.BlockSpec((n, tr, tc), lambda i, j: (0, i, j))
    shard = jax.ShapeDtypeStruct((R, C), F32)
    grid = (R // tr, C // tc)
    if do_adam:
        return pl.pallas_call(body, grid=grid, in_specs=[part, row, row, row],
                              out_specs=[row] * 4, out_shape=[shard] * 4,
                              compiler_params=_params(2), name=name)(parts, w, m, v)
    return pl.pallas_call(body, grid=grid, in_specs=[part], out_specs=row, out_shape=shard,
                          compiler_params=_params(2), name=name)(parts)


def _ffn_fwd(name, x, gain, wgt, wut, wd):
    S, D = x.shape
    nb, Fs, _ = wgt.shape
    tm = _tile(S, 512)

    def body(x_ref, g_ref, wg_ref, wu_ref, wd_ref, xo_ref, h_ref, G_ref, U_ref, acc_ref):
        j = pl.program_id(1)

        @pl.when(j == 0)
        def _():
            h_ref[...] = _rms_fwd(x_ref[...], g_ref[...]).astype(BF16)
            acc_ref[...] = jnp.zeros_like(acc_ref)

        h = h_ref[...]
        G = _dot_nt(h, wg_ref[0])
        U = _dot_nt(h, wu_ref[0])
        G_ref[0] = G
        U_ref[0] = U
        a = G * _sigmoid(G) * U
        acc_ref[...] += _dot(a.astype(BF16), wd_ref[0])

        @pl.when(j == nb - 1)
        def _():
            xo_ref[...] = x_ref[...] + 0.5 * acc_ref[...]

    row = pl.BlockSpec((tm, D), lambda i, j: (i, 0))
    act = pl.BlockSpec((1, tm, Fs), lambda i, j: (j, i, 0))
    wblk = pl.BlockSpec((1, Fs, D), lambda i, j: (j, 0, 0))
    return pl.pallas_call(
        body, grid=(S // tm, nb),
        in_specs=[row, pl.BlockSpec((1, D), lambda i, j: (0, 0)), wblk, wblk, wblk],
        out_specs=[row, row, act, act],
        out_shape=[jax.ShapeDtypeStruct((S, D), F32), jax.ShapeDtypeStruct((S, D), BF16),
                   jax.ShapeDtypeStruct((nb, S, Fs), F32), jax.ShapeDtypeStruct((nb, S, Fs), F32)],
        scratch_shapes=[pltpu.VMEM((tm, D), F32)],
        compiler_params=_params(2), name=name)(x, gain, wgt, wut, wd)


def _ffn_bwd_gate(name, dxo, G, U, wd):
    S, D = dxo.shape
    nb, Fs, _ = wd.shape
    tm = _tile(S, 512)

    def gate_body(dxo_ref, G_ref, U_ref, wd_ref, dG_ref, dU_ref, A_ref, dxb_ref):
        @pl.when(pl.program_id(1) == 0)
        def _():
            dxb_ref[...] = dxo_ref[...].astype(BF16)

        dA = 0.5 * _dot_nt(dxb_ref[...], wd_ref[0])
        Gv = G_ref[0]
        Uv = U_ref[0]
        sg = _sigmoid(Gv)
        sl = Gv * sg
        dG_ref[0] = (dA * Uv * (sg * (1.0 + Gv * (1.0 - sg)))).astype(BF16)
        dU_ref[0] = (dA * sl).astype(BF16)
        A_ref[0] = (sl * Uv).astype(BF16)

    row = pl.BlockSpec((tm, D), lambda i, j: (i, 0))
    act = pl.BlockSpec((1, tm, Fs), lambda i, j: (j, i, 0))
    wblk = pl.BlockSpec((1, Fs, D), lambda i, j: (j, 0, 0))
    act_shape = jax.ShapeDtypeStruct((nb, S, Fs), BF16)
    return pl.pallas_call(
        gate_body, grid=(S // tm, nb), in_specs=[row, act, act, wblk], out_specs=[act, act, act],
        out_shape=[act_shape, act_shape, act_shape],
        scratch_shapes=[pltpu.VMEM((tm, D), BF16)],
        compiler_params=_params(2), name=name)(dxo, G, U, wd)


def _ffn_bwd_in(name, dG, dU, wgt, wut, dxo, x_in, gain):
    S, D = x_in.shape
    nb, Fs, _ = wgt.shape
    tm = _tile(S, 512)
    rows_per_chunk = _tile(tm, 128)

    def in_body(dG_ref, dU_ref, wg_ref, wu_ref, dxo_ref, x_ref, g_ref, dx_ref, dgain_ref, acc_ref):
        i, j = pl.program_id(0), pl.program_id(1)

        @pl.when(j == 0)
        def _():
            acc_ref[...] = jnp.zeros_like(acc_ref)

        @pl.when((i == 0) & (j == 0))
        def _():
            dgain_ref[...] = jnp.zeros_like(dgain_ref)

        acc_ref[...] += _dot(dG_ref[0], wg_ref[0]) + _dot(dU_ref[0], wu_ref[0])

        @pl.when(j == nb - 1)
        def _():
            def chunk(r, dg_sum):
                rows = pl.ds(pl.multiple_of(r * rows_per_chunk, rows_per_chunk), rows_per_chunk)
                dx, dg = _rms_bwd(acc_ref[rows, :], x_ref[rows, :], g_ref[...])
                dx_ref[rows, :] = dxo_ref[rows, :] + dx
                return dg_sum + dg

            dgain_ref[...] += lax.fori_loop(0, tm // rows_per_chunk, chunk, jnp.zeros((1, D), F32))

    row = pl.BlockSpec((tm, D), lambda i, j: (i, 0))
    vec = pl.BlockSpec((1, D), lambda i, j: (0, 0))
    act = pl.BlockSpec((1, tm, Fs), lambda i, j: (j, i, 0))
    wblk = pl.BlockSpec((1, Fs, D), lambda i, j: (j, 0, 0))
    return pl.pallas_call(
        in_body, grid=(S // tm, nb), in_specs=[act, act, wblk, wblk, row, row, vec],
        out_specs=[row, vec],
        out_shape=[jax.ShapeDtypeStruct((S, D), F32), jax.ShapeDtypeStruct((1, D), F32)],
        scratch_shapes=[pltpu.VMEM((tm, D), F32)],
        compiler_params=_params(2), name=name)(dG, dU, wgt, wut, dxo, x_in, gain)


def _mm_tn(name, lhs, rhs, out_shape, lhs_spec, rhs_spec, out_spec, grid, scale, out_dtype, dep=None):
    acc_shape = tuple(out_spec.block_shape[-2:])
    n_red = grid[-1]

    def body(l_ref, r_ref, *rest):
        o_ref, acc_ref = rest[-2:]
        i = pl.program_id(len(grid) - 1)

        @pl.when(i == 0)
        def _():
            acc_ref[...] = jnp.zeros_like(acc_ref)

        acc_ref[...] += _dot_tn(_blk(l_ref).astype(BF16), _blk(r_ref).astype(BF16))

        @pl.when(i == n_red - 1)
        def _():
            res = (scale * acc_ref[...]).astype(out_dtype)
            if len(o_ref.shape) == 3:
                o_ref[0] = res
            else:
                o_ref[...] = res

    deps = [] if dep is None else [dep]
    return pl.pallas_call(
        body, grid=grid, in_specs=[lhs_spec, rhs_spec] + [_DEP_SPEC] * len(deps), out_specs=out_spec,
        out_shape=jax.ShapeDtypeStruct(out_shape, out_dtype),
        scratch_shapes=[pltpu.VMEM(acc_shape, F32)],
        compiler_params=_params(len(grid)), name=name)(lhs, rhs, *deps)


def _norm_mm_nt(name, x, gain, wt, tn):
    S, D = x.shape
    N = wt.shape[0]
    tm = _tile(S, 1024)

    def body(x_ref, g_ref, w_ref, o_ref, h_ref):
        @pl.when(pl.program_id(1) == 0)
        def _():
            h_ref[...] = _rms_fwd(x_ref[...], g_ref[...]).astype(BF16)

        o_ref[...] = _dot_nt(h_ref[...], w_ref[...])

    row = pl.BlockSpec((tm, D), lambda i, j: (i, 0))
    return pl.pallas_call(
        body, grid=(S // tm, N // tn),
        in_specs=[row, pl.BlockSpec((1, D), lambda i, j: (0, 0)),
                  pl.BlockSpec((tn, D), lambda i, j: (j, 0))],
        out_specs=[pl.BlockSpec((tm, tn), lambda i, j: (i, j)), row],
        out_shape=[jax.ShapeDtypeStruct((S, N), F32), jax.ShapeDtypeStruct((S, D), BF16)],
        compiler_params=_params(2), name=name)(x, gain, wt)


def _mm_res(name, a, w, res, tn):
    S, K = a.shape
    N = w.shape[1]
    tm = _tile(S, 512)

    def body(a_ref, w_ref, r_ref, o_ref):
        o_ref[...] = r_ref[...] + _dot(a_ref[...], w_ref[...])

    tile = pl.BlockSpec((tm, tn), lambda i, j: (i, j))
    return pl.pallas_call(
        body, grid=(S // tm, N // tn),
        in_specs=[pl.BlockSpec((tm, K), lambda i, j: (i, 0)),
                  pl.BlockSpec((K, tn), lambda i, j: (0, j)), tile],
        out_specs=tile, out_shape=jax.ShapeDtypeStruct((S, N), F32),
        compiler_params=_params(2), name=name)(a, w, res)


def _mm_k(name, a, b, tk, transpose_b, norm_bwd=None, dep=None):
    S, K = a.shape
    N = b.shape[0] if transpose_b else b.shape[1]
    tm = _tile(S, 512)
    rows_per_chunk = _tile(tm, 128)
    nk = K // tk
    n_extra = 0 if norm_bwd is None else 3
    deps = [] if dep is None else [dep]

    def body(*refs):
        a_ref, b_ref = refs[:2]
        outs = refs[2 + n_extra + len(deps):]
        acc_ref = outs[-1]
        i, k = pl.program_id(0), pl.program_id(1)

        @pl.when(k == 0)
        def _():
            acc_ref[...] = jnp.zeros_like(acc_ref)

        av = a_ref[...].astype(BF16)
        acc_ref[...] += _dot_nt(av, b_ref[...]) if transpose_b else _dot(av, b_ref[...])

        if norm_bwd is None:
            @pl.when(k == nk - 1)
            def _():
                outs[0][...] = acc_ref[...]
        else:
            x_ref, g_ref, dres_ref = refs[2:5]
            o_ref, dgain_ref = outs[:2]

            @pl.when((i == 0) & (k == 0))
            def _():
                dgain_ref[...] = jnp.zeros_like(dgain_ref)

            @pl.when(k == nk - 1)
            def _():
                def chunk(r, dg_sum):
                    rows = pl.ds(pl.multiple_of(r * rows_per_chunk, rows_per_chunk), rows_per_chunk)
                    dx, dg = _rms_bwd(acc_ref[rows, :], x_ref[rows, :], g_ref[...])
                    o_ref[rows, :] = dres_ref[rows, :] + dx
                    return dg_sum + dg

                dgain_ref[...] += lax.fori_loop(0, tm // rows_per_chunk, chunk, jnp.zeros((1, N), F32))

    a_spec = pl.BlockSpec((tm, tk), lambda i, k: (i, k))
    b_spec = (pl.BlockSpec((N, tk), lambda i, k: (0, k)) if transpose_b
              else pl.BlockSpec((tk, N), lambda i, k: (k, 0)))
    row = pl.BlockSpec((tm, N), lambda i, k: (i, 0))
    vec = pl.BlockSpec((1, N), lambda i, k: (0, 0))
    out = jax.ShapeDtypeStruct((S, N), F32)
    scratch = [pltpu.VMEM((tm, N), F32)]
    dep_specs = [_DEP_SPEC] * len(deps)
    if norm_bwd is None:
        return pl.pallas_call(body, grid=(S // tm, nk), in_specs=[a_spec, b_spec] + dep_specs,
                              out_specs=row, out_shape=out, scratch_shapes=scratch,
                              compiler_params=_params(2), name=name)(a, b, *deps)
    x_in, gain, dres = norm_bwd
    return pl.pallas_call(body, grid=(S // tm, nk), in_specs=[a_spec, b_spec, row, vec, row] + dep_specs,
                          out_specs=[row, vec],
                          out_shape=[out, jax.ShapeDtypeStruct((1, N), F32)],
                          scratch_shapes=scratch,
                          compiler_params=_params(2), name=name)(a, b, x_in, gain, dres, *deps)


def _final(name, x, gain, target):
    S, D = x.shape
    tm = _tile(S, 512)

    def body(x_ref, g_ref, t_ref, dx_ref, dgain_ref, loss_ref):
        @pl.when(pl.program_id(0) == 0)
        def _():
            dgain_ref[...] = jnp.zeros_like(dgain_ref)
            loss_ref[...] = jnp.zeros_like(loss_ref)

        xv = x_ref[...]
        err = _rms_fwd(xv, g_ref[...]) - t_ref[...]
        per_tok = jnp.mean(err * err, axis=-1, keepdims=True)
        loss_ref[...] += 0.5 * jnp.sum(per_tok, axis=0, keepdims=True)
        dx, dg = _rms_bwd(err * (1.0 / D), xv, g_ref[...])
        dx_ref[...] = dx
        dgain_ref[...] += dg

    row = pl.BlockSpec((tm, D), lambda i: (i, 0))
    vec = pl.BlockSpec((1, D), lambda i: (0, 0))
    return pl.pallas_call(
        body, grid=(S // tm,), in_specs=[row, vec, row],
        out_specs=[row, vec, pl.BlockSpec((1, LANES), lambda i: (0, 0))],
        out_shape=[jax.ShapeDtypeStruct((S, D), F32), jax.ShapeDtypeStruct((1, D), F32),
                   jax.ShapeDtypeStruct((1, LANES), F32)],
        compiler_params=_params(1), name=name)(x, gain, target)


def _conv_tiles(S):
    ts = _tile(S, 256, CONV_HALO)
    return ts, ts // CONV_HALO


def _ln_stats(yc):
    mu = jnp.mean(yc, axis=-1, keepdims=True)
    d = yc - mu
    rs = lax.rsqrt(jnp.mean(d * d, axis=-1, keepdims=True) + LN_EPS)
    return d * rs, rs


def _conv_fwd(name, proj, C, cw, cb, lg, lb):
    S = proj.shape[0]
    ts, hb = _conv_tiles(S)

    def body(a_ref, g_ref, ah_ref, gh_ref, cw_ref, cb_ref, lg_ref, lb_ref, yc_ref, y_ref, ubuf):
        i = pl.program_id(0)
        uh = ah_ref[...] * _sigmoid(gh_ref[...])
        ubuf[pl.ds(0, CONV_HALO), :] = jnp.where(i > 0, uh, 0.0)
        ubuf[pl.ds(CONV_HALO, ts), :] = a_ref[...] * _sigmoid(g_ref[...])
        acc = jnp.zeros((ts, C), F32)
        for k in range(CONV_WIDTH):
            acc = acc + cw_ref[pl.ds(k, 1), :] * ubuf[pl.ds(k + CONV_HALO - CONV_WIDTH + 1, ts), :]
        yc = acc + cb_ref[...]
        yc_ref[...] = yc
        yn, _ = _ln_stats(yc)
        z = yn * lg_ref[...] + lb_ref[...]
        y_ref[...] = (z * _sigmoid(z)).astype(BF16)

    main = lambda col: pl.BlockSpec((ts, C), lambda i: (i, col))
    halo = lambda col: pl.BlockSpec((CONV_HALO, C), lambda i: (jnp.maximum(i * hb - 1, 0), col))
    vec = pl.BlockSpec((1, C), lambda i: (0, 0))
    return pl.pallas_call(
        body, grid=(S // ts,),
        in_specs=[main(0), main(1), halo(0), halo(1),
                  pl.BlockSpec((CONV_HALO, C), lambda i: (0, 0)), vec, vec, vec],
        out_specs=[pl.BlockSpec((ts, C), lambda i: (i, 0))] * 2,
        out_shape=[jax.ShapeDtypeStruct((S, C), F32), jax.ShapeDtypeStruct((S, C), BF16)],
        scratch_shapes=[pltpu.VMEM((ts + CONV_HALO, C), F32)],
        compiler_params=_params(1), name=name)(proj, proj, proj, proj, cw, cb, lg, lb)


def _conv_bwd(name, dmix, yc, proj, C, cw, lg, lb):
    S = proj.shape[0]
    ts, hb = _conv_tiles(S)
    n_t = S // ts

    def body(dy_ref, yc_ref, dyh_ref, ych_ref, a_ref, g_ref, ah_ref, gh_ref, cw_ref, lg_ref, lb_ref,
             dag_ref, dcw_ref, dcb_ref, dlg_ref, dlb_ref, ubuf, dbuf):
        i = pl.program_id(0)

        @pl.when(i == 0)
        def _():
            dcw_ref[...] = jnp.zeros_like(dcw_ref)
            dcb_ref[...] = jnp.zeros_like(dcb_ref)
            dlg_ref[...] = jnp.zeros_like(dlg_ref)
            dlb_ref[...] = jnp.zeros_like(dlb_ref)

        def ln_bwd(dy, ycv):
            yn, rs = _ln_stats(ycv)
            z = yn * lg_ref[...] + lb_ref[...]
            sg = _sigmoid(z)
            dz = dy * (sg * (1.0 + z * (1.0 - sg)))
            dyn = dz * lg_ref[...]
            dyc = rs * (dyn - jnp.mean(dyn, axis=-1, keepdims=True)
                        - yn * jnp.mean(dyn * yn, axis=-1, keepdims=True))
            return dyc, dz, yn

        dyc, dz, yn = ln_bwd(dy_ref[...], yc_ref[...])
        dlg_ref[...] += jnp.sum(dz * yn, axis=0, keepdims=True)
        dlb_ref[...] += jnp.sum(dz, axis=0, keepdims=True)
        dcb_ref[...] += jnp.sum(dyc, axis=0, keepdims=True)
        dych, _, _ = ln_bwd(dyh_ref[...], ych_ref[...])
        dbuf[pl.ds(0, ts), :] = dyc
        dbuf[pl.ds(ts, CONV_HALO), :] = jnp.where(i < n_t - 1, dych, 0.0)

        av = a_ref[...]
        sgm = _sigmoid(g_ref[...])
        uh = ah_ref[...] * _sigmoid(gh_ref[...])
        ubuf[pl.ds(0, CONV_HALO), :] = jnp.where(i > 0, uh, 0.0)
        ubuf[pl.ds(CONV_HALO, ts), :] = av * sgm

        du = jnp.zeros((ts, C), F32)
        for k in range(CONV_WIDTH):
            du = du + cw_ref[pl.ds(k, 1), :] * dbuf[pl.ds(CONV_WIDTH - 1 - k, ts), :]
            tap = ubuf[pl.ds(k + CONV_HALO - CONV_WIDTH + 1, ts), :]
            dcw_ref[pl.ds(k, 1), :] += jnp.sum(dyc * tap, axis=0, keepdims=True)
        dag_ref[:, pl.ds(0, C)] = (du * sgm).astype(BF16)
        dag_ref[:, pl.ds(C, C)] = (du * av * sgm * (1.0 - sgm)).astype(BF16)

    main = lambda col: pl.BlockSpec((ts, C), lambda i: (i, col))
    past = lambda col: pl.BlockSpec((CONV_HALO, C), lambda i: (jnp.maximum(i * hb - 1, 0), col))
    nxt = pl.BlockSpec((CONV_HALO, C), lambda i: (jnp.minimum((i + 1) * hb, n_t * hb - 1), 0))
    vec = pl.BlockSpec((1, C), lambda i: (0, 0))
    full = pl.BlockSpec((CONV_HALO, C), lambda i: (0, 0))
    vshape = jax.ShapeDtypeStruct((1, C), F32)
    return pl.pallas_call(
        body, grid=(n_t,),
        in_specs=[main(0), main(0), nxt, nxt, main(0), main(1), past(0), past(1), full, vec, vec],
        out_specs=[pl.BlockSpec((ts, 2 * C), lambda i: (i, 0)), full, vec, vec, vec],
        out_shape=[jax.ShapeDtypeStruct((S, 2 * C), BF16),
                   jax.ShapeDtypeStruct((CONV_HALO, C), F32), vshape, vshape, vshape],
        scratch_shapes=[pltpu.VMEM((ts + CONV_HALO, C), F32), pltpu.VMEM((ts + CONV_HALO, C), F32)],
        compiler_params=_params(1),
        name=name)(dmix, yc, dmix, yc, proj, proj, proj, proj, cw, lg, lb)


AUG = 3
ROW_CHUNK = 32
REDUCE_ROWS = 1024


def _gate_prep(name, proj, f_blk, fbias, n_pair):
    S = proj.shape[0]
    ts = _tile(S, 512)
    W = LANES * n_pair

    def body(pf_ref, fb_ref, ka_ref, carry):
        @pl.when(pl.program_id(0) == 0)
        def _():
            carry[...] = jnp.zeros_like(carry)

        f = pf_ref[...] + fb_ref[...]
        logf = jnp.minimum(f, 0.0) - jnp.log(1.0 + jnp.exp(-jnp.abs(f)))
        r = lax.broadcasted_iota(jnp.int32, (ts, ts), 0)
        c = lax.broadcasted_iota(jnp.int32, (ts, ts), 1)
        ltri = (c <= r).astype(BF16)
        hi, mid, lo = _split3(logf)
        cs = _dot(ltri, hi) + _dot(ltri, mid) + _dot(ltri, lo) + carry[...]
        carry[...] = cs[ts - 1:ts, :]
        hh = lax.broadcasted_iota(jnp.int32, (LANES, W), 0)
        ll = lax.broadcasted_iota(jnp.int32, (LANES, W), 1)
        pair, w = ll >> 7, ll & (LANES - 1)
        ka = jnp.zeros((ts, W), F32)
        for p, piece in enumerate(_split3(-cs)):
            e = (((w == HEAD_DIM + p) & (hh == 2 * pair)) | ((w == p) & (hh == 2 * pair + 1)))
            ka = ka + _dot(piece, e.astype(BF16))
        lw = lax.broadcasted_iota(jnp.int32, (1, W), 1) & (LANES - 1)
        ka = ka + ((lw == HEAD_DIM + AUG) | (lw == AUG)).astype(F32)
        ka_ref[...] = ka.astype(BF16)

    return pl.pallas_call(
        body, grid=(S // ts,),
        in_specs=[pl.BlockSpec((ts, LANES), lambda i: (i, f_blk)),
                  pl.BlockSpec((1, LANES), lambda i: (0, 0))],
        out_specs=pl.BlockSpec((ts, W), lambda i: (i, 0)),
        out_shape=jax.ShapeDtypeStruct((S, W), BF16),
        scratch_shapes=[pltpu.VMEM((1, LANES), F32)],
        compiler_params=_params(1), name=name)(proj, fbias)


def _gate_bwd(name, sp, rs, proj, f_blk, fbias, n_pair):
    S = proj.shape[0]
    ts = _tile(S, 512)
    n_t = S // ts
    W = LANES * n_pair

    def body(sp_ref, rs_ref, pf_ref, fb_ref, df_ref, dfb_ref, carry):
        @pl.when(pl.program_id(0) == 0)
        def _():
            carry[...] = jnp.zeros_like(carry)
            dfb_ref[...] = jnp.zeros_like(dfb_ref)

        ll = lax.broadcasted_iota(jnp.int32, (W, LANES), 0)
        hh = lax.broadcasted_iota(jnp.int32, (W, LANES), 1)
        pair, w = ll >> 7, ll & (LANES - 1)
        first, second = hh == 2 * pair, hh == 2 * pair + 1

        def pick(ref, lane_first, lane_second):
            sel = (((w == lane_first) & first) | ((w == lane_second) & second)).astype(BF16)
            hi, mid, lo = _split3(ref[...])
            return _dot(hi, sel) + _dot(mid, sel) + _dot(lo, sel)

        dc = pick(rs_ref, HEAD_DIM + AUG, AUG) - pick(sp_ref, HEAD_DIM, 0)
        r = lax.broadcasted_iota(jnp.int32, (ts, ts), 0)
        c = lax.broadcasted_iota(jnp.int32, (ts, ts), 1)
        utri = (c >= r).astype(BF16)
        hi, mid, lo = _split3(dc)
        dlogf = _dot(utri, hi) + _dot(utri, mid) + _dot(utri, lo) + carry[...]
        carry[...] = dlogf[0:1, :]
        f = pf_ref[...] + fb_ref[...]
        lane = lax.broadcasted_iota(jnp.int32, (ts, LANES), 1)
        df = jnp.where(lane < 2 * n_pair, dlogf * _sigmoid(-f), 0.0)
        df_ref[...] = df.astype(BF16)
        dfb_ref[...] += jnp.sum(df, axis=0, keepdims=True)

    rev = lambda blk: (lambda i: (n_t - 1 - i, blk))
    return pl.pallas_call(
        body, grid=(n_t,),
        in_specs=[pl.BlockSpec((ts, W), rev(0)), pl.BlockSpec((ts, W), rev(0)),
                  pl.BlockSpec((ts, LANES), rev(f_blk)), pl.BlockSpec((1, LANES), lambda i: (0, 0))],
        out_specs=[pl.BlockSpec((ts, LANES), rev(0)), pl.BlockSpec((1, LANES), lambda i: (0, 0))],
        out_shape=[jax.ShapeDtypeStruct((S, LANES), BF16), jax.ShapeDtypeStruct((1, LANES), F32)],
        scratch_shapes=[pltpu.VMEM((1, LANES), F32)],
        compiler_params=_params(1), name=name)(sp, rs, proj, fbias)


def _spare_lane(h):
    return HEAD_DIM if h == 0 else 0


def _head_operands(h, lane, q2, k2, ka2):
    act = (lane < HEAD_DIM) if h == 0 else (lane >= HEAD_DIM)
    base = HEAD_DIM if h == 0 else 0
    ones = ((lane >= base) & (lane < base + AUG)).astype(F32)
    qa = jnp.where(act, q2 * (1.0 / math.sqrt(HEAD_DIM)), ones).astype(BF16)
    ka = jnp.where(act, k2.astype(BF16), ka2)
    return act, qa, ka


def _attn_fwd(name, proj, ka, q_blk, k_blk, v_blk, n_pair):
    S = proj.shape[0]
    tq = _tile(S, 512)
    n_t = S // tq
    W = LANES * n_pair

    rc = _tile(tq, ROW_CHUNK)

    def body(q_ref, k_ref, v_ref, ka_ref, o_ref, o32_ref, lse_ref, m_ref, acc_ref, s_ref, p_ref):
        i, j = pl.program_id(1), pl.program_id(2)

        @pl.when(j == 0)
        def _():
            m_ref[...] = jnp.full_like(m_ref, NEG_INF)
            acc_ref[...] = jnp.zeros_like(acc_ref)

        def step(diagonal):
            lane = lax.broadcasted_iota(jnp.int32, (tq, LANES), 1)
            q2, k2, v2, ka2 = q_ref[...], k_ref[...], v_ref[...], ka_ref[...]
            for h in range(2):
                act, qa, kaug = _head_operands(h, lane, q2, k2, ka2)
                s = _dot_nt(qa, kaug)
                if diagonal:
                    row = lax.broadcasted_iota(jnp.int32, (tq, tq), 0)
                    col = lax.broadcasted_iota(jnp.int32, (tq, tq), 1)
                    s = jnp.where(row >= col, s, NEG_INF)
                s_ref[h] = s
                m_prev = m_ref[h]
                m_new = jnp.maximum(m_prev, jnp.max(s, axis=-1, keepdims=True))
                m_ref[h] = m_new
                for r in range(tq // rc):
                    rows = pl.ds(r * rc, rc)
                    p_ref[h, rows, :] = jnp.exp(s_ref[h, rows, :] - m_ref[h, rows, :]).astype(BF16)
                vm = jnp.where(act, v2, (lane == _spare_lane(h)).astype(F32)).astype(BF16)
                acc_ref[h] = jnp.exp(m_prev - m_new) * acc_ref[h] + _dot(p_ref[h], vm)

        @pl.when(j < i)
        def _():
            step(False)

        @pl.when(j == i)
        def _():
            step(True)
            lane = lax.broadcasted_iota(jnp.int32, (tq, LANES), 1)
            first = lane < HEAD_DIM
            acc = [acc_ref[0], acc_ref[1]]
            den = [acc[h][:, _spare_lane(h):_spare_lane(h) + 1] for h in range(2)]
            out = jnp.where(first, acc[0] / den[0], acc[1] / den[1])
            o_ref[...] = out.astype(BF16)
            o32_ref[...] = out
            lse_ref[...] = jnp.where(first, m_ref[0] + jnp.log(den[0]), m_ref[1] + jnp.log(den[1]))

    qspec = lambda blk: pl.BlockSpec((tq, LANES), lambda p, i, j: (i, blk + p))
    kspec = lambda blk: pl.BlockSpec((tq, LANES), lambda p, i, j: (jnp.minimum(j, i), blk + p))
    out = pl.BlockSpec((tq, LANES), lambda p, i, j: (i, p))
    return pl.pallas_call(
        body, grid=(n_pair, n_t, n_t),
        in_specs=[qspec(q_blk), kspec(k_blk), kspec(v_blk), kspec(0)],
        out_specs=[out, out, out],
        out_shape=[jax.ShapeDtypeStruct((S, W), BF16), jax.ShapeDtypeStruct((S, W), F32),
                   jax.ShapeDtypeStruct((S, W), F32)],
        scratch_shapes=[pltpu.VMEM((2, tq, 1), F32), pltpu.VMEM((2, tq, LANES), F32),
                        pltpu.VMEM((2, tq, tq), F32), pltpu.VMEM((2, tq, tq), BF16)],
        compiler_params=_params(3), name=name)(proj, proj, proj, ka)


def _attn_bwd(name, proj, ka, o, lse, dmix, q_blk, k_blk, v_blk, do_blk, n_pair):
    S = proj.shape[0]
    tq = _tile(S, 512)
    n_t = S // tq
    W = LANES * n_pair
    scale = 1.0 / math.sqrt(HEAD_DIM)
    rc = _tile(tq, ROW_CHUNK)

    def body(q_ref, k_ref, v_ref, ka_ref, o_ref, lse_ref, do_ref,
             dq_ref, dk_ref, dv_ref, sp_ref, rs_ref, dk_acc, dv_acc, s_ref, dp_ref, p_ref, ds_ref, d_ref):
        j, i = pl.program_id(1), pl.program_id(2)

        @pl.when((j == 0) & (i == 0))
        def _():
            dq_ref[...] = jnp.zeros_like(dq_ref)
            rs_ref[...] = jnp.zeros_like(rs_ref)

        @pl.when(i == 0)
        def _():
            dk_acc[...] = jnp.zeros_like(dk_acc)
            dv_acc[...] = jnp.zeros_like(dv_acc)

        def step(diagonal):
            lane = lax.broadcasted_iota(jnp.int32, (tq, LANES), 1)
            q2, k2, v2, ka2 = q_ref[...], k_ref[...], v_ref[...], ka_ref[...]
            o2, do2 = o_ref[...], do_ref[...]
            dq, heads = [], []
            for h in range(2):
                act, qa, kaug = _head_operands(h, lane, q2, k2, ka2)
                dom = jnp.where(act, do2, 0.0)
                d_ref[h] = jnp.sum(dom * o2, axis=-1, keepdims=True)
                dob = dom.astype(BF16)
                s_ref[h] = _dot_nt(qa, kaug)
                dp_ref[h] = _dot_nt(dob, jnp.where(act, v2, 0.0).astype(BF16))
                heads.append((qa, kaug, dob))
            for h in range(2):
                qa, kaug, dob = heads[h]
                for r in range(tq // rc):
                    rows = pl.ds(r * rc, rc)
                    sc = s_ref[h, rows, :]
                    if diagonal:
                        row = lax.broadcasted_iota(jnp.int32, (rc, tq), 0) + r * rc
                        col = lax.broadcasted_iota(jnp.int32, (rc, tq), 1)
                        sc = jnp.where(row >= col, sc, NEG_INF)
                    p = jnp.exp(sc - lse_ref[rows, :][:, h * HEAD_DIM:h * HEAD_DIM + 1])
                    p_ref[h, rows, :] = p.astype(BF16)
                    ds_ref[h, rows, :] = (p * (dp_ref[h, rows, :] - d_ref[h, rows, :])).astype(BF16)
                dv_acc[...] += _dot_tn(p_ref[h], dob)
                dk_acc[h] += _dot_tn(ds_ref[h], qa)
                dq.append(_dot(ds_ref[h], kaug))
            rows = pl.ds(pl.multiple_of(i * tq, tq), tq)
            first = lane < HEAD_DIM
            dq_ref[rows, :] += jnp.where(first, dq[0], dq[1])
            rs_ref[rows, :] += jnp.where(first, dq[1], dq[0])

        @pl.when(i > j)
        def _():
            step(False)

        @pl.when(i == j)
        def _():
            step(True)

        @pl.when(i == n_t - 1)
        def _():
            lane = lax.broadcasted_iota(jnp.int32, (tq, LANES), 1)
            first = lane < HEAD_DIM
            dk_ref[...] = jnp.where(first, dk_acc[0], dk_acc[1]).astype(BF16)
            sp_ref[...] = jnp.where(first, dk_acc[1], dk_acc[0])
            dv_ref[...] = dv_acc[...].astype(BF16)

        @pl.when((j == n_t - 1) & (i == n_t - 1))
        def _():
            dq_ref[...] = dq_ref[...] * scale

    qspec = lambda blk: pl.BlockSpec((tq, LANES), lambda p, j, i: (jnp.maximum(i, j), blk + p))
    kspec = lambda blk: pl.BlockSpec((tq, LANES), lambda p, j, i: (j, blk + p))
    kout = pl.BlockSpec((tq, LANES), lambda p, j, i: (j, p))
    qres = pl.BlockSpec((S, LANES), lambda p, j, i: (0, p))
    return pl.pallas_call(
        body, grid=(n_pair, n_t, n_t),
        in_specs=[qspec(q_blk), kspec(k_blk), kspec(v_blk), kspec(0),
                  qspec(0), qspec(0), qspec(do_blk)],
        out_specs=[qres, kout, kout, kout, qres],
        out_shape=[jax.ShapeDtypeStruct((S, W), F32), jax.ShapeDtypeStruct((S, W), BF16),
                   jax.ShapeDtypeStruct((S, W), BF16), jax.ShapeDtypeStruct((S, W), F32),
                   jax.ShapeDtypeStruct((S, W), F32)],
        scratch_shapes=[pltpu.VMEM((2, tq, LANES), F32), pltpu.VMEM((tq, LANES), F32),
                        pltpu.VMEM((2, tq, tq), F32), pltpu.VMEM((2, tq, tq), F32),
                        pltpu.VMEM((2, tq, tq), BF16), pltpu.VMEM((2, tq, tq), BF16),
                        pltpu.VMEM((2, tq, 1), F32)],
        compiler_params=_params(3), name=name)(proj, proj, proj, ka, o, lse, dmix)


def _ffn_block_grad(name, act, rows, scale, dep=None):
    nb, S, Fs = act.shape
    D = rows.shape[1]
    tm = _tile(S, REDUCE_ROWS)
    return _mm_tn(name, act, rows, (nb, Fs, D),
                  pl.BlockSpec((1, tm, Fs), lambda j, i: (j, i, 0)),
                  pl.BlockSpec((tm, D), lambda j, i: (i, 0)),
                  pl.BlockSpec((1, Fs, D), lambda j, i: (j, 0, 0)), (nb, S // tm), scale, BF16, dep=dep)


def kernel(x, ffn1_norm, ffn1_w_gate, ffn1_w_up, ffn1_w_down, mix_norm, w_in, fgate_bias, conv_w, conv_b, conv_ln_g, conv_ln_b, w_out, ffn2_norm, ffn2_w_gate, ffn2_w_up, ffn2_w_down, final_norm, loss_target, m_ffn1_norm, m_ffn1_w_gate, m_ffn1_w_up, m_ffn1_w_down, m_mix_norm, m_w_in, m_fgate_bias, m_conv_w, m_conv_b, m_conv_ln_g, m_conv_ln_b, m_w_out, m_ffn2_norm, m_ffn2_w_gate, m_ffn2_w_up, m_ffn2_w_down, m_final_norm, v_ffn1_norm, v_ffn1_w_gate, v_ffn1_w_up, v_ffn1_w_down, v_mix_norm, v_w_in, v_fgate_bias, v_conv_w, v_conv_b, v_conv_ln_g, v_conv_ln_b, v_w_out, v_ffn2_norm, v_ffn2_w_gate, v_ffn2_w_up, v_ffn2_w_down, v_final_norm):
    xs = x[0]
    S, D = xs.shape
    C = conv_b.shape[0]
    n_heads = fgate_bias.shape[0]
    FW = n_heads * HEAD_DIM
    n_pair = n_heads // 2
    MIX = C + FW
    in_shard = w_in.shape[1]
    in_cols = in_shard * N_DEV
    NP = -(-in_cols // 512) * 512
    q_blk, k_blk, v_blk = 2 * C // LANES, (2 * C + FW) // LANES, (2 * C + 2 * FW) // LANES
    f_blk = (2 * C + 3 * FW) // LANES
    assert C % LANES == 0 and FW % LANES == 0 and n_heads % 2 == 0 and n_heads <= LANES
    assert in_cols == 2 * C + 3 * FW + n_heads and MIX == w_out.shape[0] * N_DEV

    vec = lambda a: a.reshape(1, -1)
    bf = lambda a: a.astype(BF16)
    tm = _tile(S, REDUCE_ROWS)

    wgt1, wut1, wd1 = _all_gather_two_level("ag_ffn1", [bf(ffn1_w_gate).T, bf(ffn1_w_up).T, bf(ffn1_w_down)])
    ag, _ = _xchg_start("ag_start", [bf(w_in).T, conv_w, bf(w_out), bf(ffn2_w_gate).T, bf(ffn2_w_up).T,
                                     bf(ffn2_w_down)], False, dep=wd1)
    fbias = jnp.pad(vec(fgate_bias), ((0, 0), (0, LANES - n_heads)))

    x1, h1, G1, U1 = _ffn_fwd("ffn1_fwd", xs, vec(ffn1_norm), wgt1, wut1, wd1)
    win_g, cw_g = _xchg_wait("ag_wait_in", ag[0:2], False, x1)
    wint = jnp.pad(win_g.reshape(in_cols, D), ((0, NP - in_cols), (0, 0)))
    cw = jnp.pad(cw_g.transpose(1, 0, 2).reshape(CONV_WIDTH, C), ((0, CONV_HALO - CONV_WIDTH), (0, 0)))
    proj, h2 = _norm_mm_nt("proj_in", x1, vec(mix_norm), wint, 512)
    yc, y_conv = _conv_fwd("conv_fwd", proj, C, cw, vec(conv_b), vec(conv_ln_g), vec(conv_ln_b))
    ka = _gate_prep("gate_prep", proj, f_blk, fbias, n_pair)
    o, o32, lse = _attn_fwd("attn_fwd", proj, ka, q_blk, k_blk, v_blk, n_pair)
    (wout_g,) = _xchg_wait("ag_wait_out", ag[2:3], False, lse)
    wout = wout_g.reshape(MIX, D)
    mix = jnp.concatenate([y_conv, o], axis=1)
    x2 = _mm_res("proj_out", mix, wout, x1, 512)
    wgt2, wut2, wd2 = _xchg_wait("ag_wait_ffn2", ag[3:6], False, x2)
    x3, h3, G2, U2 = _ffn_fwd("ffn2_fwd", x2, vec(ffn2_norm), wgt2, wut2, wd2)

    dx3, d_final_norm, loss_part = _final("final", x3, vec(final_norm), loss_target[0])
    dG2, dU2, A2 = _ffn_bwd_gate("ffn2_bwd_gate", dx3, G2, U2, wd2)
    dx2, d_ffn2_norm = _ffn_bwd_in("ffn2_bwd_in", dG2, dU2, wgt2, wut2, dx3, x2, vec(ffn2_norm))
    dwd2 = _ffn_block_grad("ffn2_dwd", A2, dx3, 0.5)
    s_d2, tok = _xchg_start("a2a_start_ffn2_wd", [dwd2], True)
    dwg2 = _ffn_block_grad("ffn2_dwg", dG2, h3, 1.0, dep=tok)
    s_g2, tok = _xchg_start("a2a_start_ffn2_wg", [dwg2], True)
    dwu2 = _ffn_block_grad("ffn2_dwu", dU2, h3, 1.0, dep=tok)
    s_u2, tok = _xchg_start("a2a_start_ffn2_wu", [dwu2], True)

    dmix = _mm_k("dmix", dx2, wout, D, True, dep=tok)
    d_wout = _mm_tn("dwout", mix, dx2, (MIX, D),
                    pl.BlockSpec((tm, MIX), lambda j, i: (i, 0)), pl.BlockSpec((tm, 512), lambda j, i: (i, j)),
                    pl.BlockSpec((MIX, 512), lambda j, i: (0, j)), (D // 512, S // tm), 1.0, BF16)
    s_out, tok = _xchg_start("a2a_start_w_out", [d_wout.reshape(N_DEV, MIX // N_DEV, D)], True)
    dag, d_cw, d_cb, d_lg, d_lb = _conv_bwd("conv_bwd", dmix, yc, proj, C, cw, vec(conv_ln_g), vec(conv_ln_b))
    dq, dk, dv, sp, rs = _attn_bwd("attn_bwd", proj, ka, o32, lse, dmix, q_blk, k_blk, v_blk, C // LANES, n_pair)
    df, d_fb = _gate_bwd("gate_bwd", sp, rs, proj, f_blk, fbias, n_pair)
    dproj = jnp.concatenate([dag, bf(dq), dk, dv, df, jnp.zeros((S, NP - f_blk * LANES - LANES), BF16)], axis=1)
    d_wint = _mm_tn("dwin", dproj, h2, (NP, D),
                    pl.BlockSpec((tm, 512), lambda j, i: (i, j)), pl.BlockSpec((tm, D), lambda j, i: (i, 0)),
                    pl.BlockSpec((512, D), lambda j, i: (j, 0)), (NP // 512, S // tm), 1.0, BF16, dep=tok)
    s_in, tok = _xchg_start("a2a_start_w_in", [d_wint[:in_cols].reshape(N_DEV, in_shard, D)], True)
    dx1, d_mix_norm = _mm_k("dh2", dproj, wint, _tile(NP, NP // 4, LANES), False,
                            norm_bwd=(x1, vec(mix_norm), dx2), dep=tok)
    dG1, dU1, A1 = _ffn_bwd_gate("ffn1_bwd_gate", dx1, G1, U1, wd1)

    rows = lambda a: a.reshape(-1, C)
    pad_row = lambda a: jnp.pad(a.reshape(1, -1), ((0, 0), (0, C - a.size)))
    pieces = [rows(d_mix_norm), rows(d_ffn2_norm), rows(d_final_norm),
              d_cw[:CONV_WIDTH], d_cb, d_lg, d_lb, pad_row(d_fb[0, :n_heads]), pad_row(loss_part[0, :1])]
    pack = jnp.concatenate(pieces, axis=0)
    n_rows = pack.shape[0]
    pack = jnp.pad(pack, ((0, -n_rows % SUBLANES), (0, 0)))
    (pack_g,) = _exchange("ag_small", [pack], False)
    tot = _reduce_adam("sum_small", pack_g)
    nd = D // C
    g_mix_norm, g_ffn2_norm, g_final_norm = (tot[k * nd:(k + 1) * nd].reshape(D) for k in range(3))
    r0 = 3 * nd
    me = _lin(_mesh_pos())
    cs = C // N_DEV
    g_conv_w = lax.dynamic_slice(tot[r0:r0 + CONV_WIDTH], (0, me * cs), (CONV_WIDTH, cs))
    g_conv_b, g_ln_g, g_ln_b = tot[r0 + CONV_WIDTH], tot[r0 + CONV_WIDTH + 1], tot[r0 + CONV_WIDTH + 2]
    g_fb = tot[r0 + CONV_WIDTH + 3, :n_heads]
    loss = tot[r0 + CONV_WIDTH + 4, 0]

    small = [(g_mix_norm, mix_norm, m_mix_norm, v_mix_norm),
             (g_fb, fgate_bias, m_fgate_bias, v_fgate_bias), (g_conv_w, conv_w, m_conv_w, v_conv_w),
             (g_conv_b, conv_b, m_conv_b, v_conv_b), (g_ln_g, conv_ln_g, m_conv_ln_g, v_conv_ln_g),
             (g_ln_b, conv_ln_b, m_conv_ln_b, v_conv_ln_b), (g_ffn2_norm, ffn2_norm, m_ffn2_norm, v_ffn2_norm),
             (g_final_norm, final_norm, m_final_norm, v_final_norm)]
    sizes = [g.size for g, _, _, _ in small]
    total = sum(sizes)
    padded = -(-total // (SUBLANES * LANES)) * (SUBLANES * LANES)

    def flat_pack(k, fill):
        flat = jnp.concatenate([t[k].reshape(-1) for t in small])
        return jnp.pad(flat, (0, padded - total), constant_values=fill).reshape(padded // LANES, LANES)

    sg, sd, sm, sv = _reduce_adam("adam_small", flat_pack(0, 0.0)[None], flat_pack(1, 0.0), flat_pack(2, 0.0),
                                  flat_pack(3, 1.0))

    def unpack(packed):
        flat = packed.reshape(-1)
        out, off = [], 0
        for (g, _, _, _), n in zip(small, sizes):
            out.append(flat[off:off + n].reshape(g.shape))
            off += n
        return out

    s_g, s_d, s_m, s_v = unpack(sg), unpack(sd), unpack(sm), unpack(sv)

    dwd1 = _ffn_block_grad("ffn1_dwd", A1, dx1, 0.5, dep=sd)
    s_d1, tok = _xchg_start("a2a_start_ffn1_wd", [dwd1], True)
    dwg1 = _ffn_block_grad("ffn1_dwg", dG1, h1, 1.0, dep=tok)
    s_g1, tok = _xchg_start("a2a_start_ffn1_wg", [dwg1], True)
    dwu1 = _ffn_block_grad("ffn1_dwu", dU1, h1, 1.0, dep=tok)
    s_u1, tok = _xchg_start("a2a_start_ffn1_wu", [dwu1], True)
    dx0, d_ffn1_norm = _ffn_bwd_in("ffn1_bwd_in", dG1, dU1, wgt1, wut1, dx1, xs, vec(ffn1_norm) + tok[:1, :1])

    r_d2, r_g2, r_u2, r_out, r_in = _xchg_wait("a2a_wait_a", s_d2 + s_g2 + s_u2 + s_out + s_in, True, dx0)
    tr = lambda a: a.T

    def adam_t(name, recv, w, m, v):
        return tuple(tr(r) for r in _reduce_adam(name, recv, tr(w), tr(m), tr(v)))

    res = {
        "ffn2_w_down": _reduce_adam("adam_ffn2_wd", r_d2, ffn2_w_down, m_ffn2_w_down, v_ffn2_w_down),
        "ffn2_w_gate": adam_t("adam_ffn2_wg", r_g2, ffn2_w_gate, m_ffn2_w_gate, v_ffn2_w_gate),
        "ffn2_w_up": adam_t("adam_ffn2_wu", r_u2, ffn2_w_up, m_ffn2_w_up, v_ffn2_w_up),
        "w_out": _reduce_adam("adam_w_out", r_out, w_out, m_w_out, v_w_out),
        "w_in": adam_t("adam_w_in", r_in, w_in, m_w_in, v_w_in),
    }
    (r_d1,) = _xchg_wait("a2a_wait_d1", s_d1, True, res["w_in"][0])
    res["ffn1_w_down"] = _reduce_adam("adam_ffn1_wd", r_d1, ffn1_w_down, m_ffn1_w_down, v_ffn1_w_down)
    (r_g1,) = _xchg_wait("a2a_wait_g1", s_g1, True, res["ffn1_w_down"][0])
    res["ffn1_w_gate"] = adam_t("adam_ffn1_wg", r_g1, ffn1_w_gate, m_ffn1_w_gate, v_ffn1_w_gate)
    (r_u1,) = _xchg_wait("a2a_wait_u1", s_u1, True, res["ffn1_w_gate"][0])
    res["ffn1_w_up"] = adam_t("adam_ffn1_wu", r_u1, ffn1_w_up, m_ffn1_w_up, v_ffn1_w_up)
    lanes = lambda a: a.reshape(-1, LANES)
    (norm1_g,) = _exchange("ag_norm1", [lanes(d_ffn1_norm)], False, dep=res["ffn1_w_up"][0])
    res["ffn1_norm"] = tuple(r.reshape(D) for r in _reduce_adam(
        "adam_norm1", norm1_g, lanes(ffn1_norm), lanes(m_ffn1_norm), lanes(v_ffn1_norm)))

    small_names = ["mix_norm", "fgate_bias", "conv_w", "conv_b", "conv_ln_g", "conv_ln_b",
                   "ffn2_norm", "final_norm"]
    for k, n in enumerate(small_names):
        res[n] = (s_g[k], s_d[k], s_m[k], s_v[k])
    order = ["ffn1_norm", "ffn1_w_gate", "ffn1_w_up", "ffn1_w_down", "mix_norm", "w_in", "fgate_bias",
             "conv_w", "conv_b", "conv_ln_g", "conv_ln_b", "w_out", "ffn2_norm", "ffn2_w_gate", "ffn2_w_up",
             "ffn2_w_down", "final_norm"]
    outs = [loss, dx0[None]]
    for k in range(4):
        outs += [res[n][k] for n in order]
    return tuple(outs)
```

```python
import math

import jax
import jax.numpy as jnp
from jax import lax
from jax.experimental import pallas as pl
from jax.experimental.pallas import tpu as pltpu

F32 = jnp.float32
BF16 = jnp.bfloat16

N_DEV = 8
MESH_ID = pl.DeviceIdType.MESH
HEAD_DIM = 64
CONV_WIDTH = 31
CONV_HALO = 32
NORM_EPS = 1e-6
LN_EPS = 1e-5
NEG_INF = -1e30
LANES = 128
SUBLANES = 8
V7X_VMEM_LIMIT = 52 * 1024 * 1024

ADAM_LR = 0.001
ADAM_B1 = 0.9
ADAM_B2 = 0.999
ADAM_EPS = 1e-08
ADAM_WD = 0.01
ADAM_STEP = 10


def _params(n_grid_axes):
    return pltpu.CompilerParams(dimension_semantics=("arbitrary",) * n_grid_axes,
                                vmem_limit_bytes=V7X_VMEM_LIMIT)


def _tile(n, pref, mult=8):
    t = min(pref, n)
    while t >= mult:
        if n % t == 0 and t % mult == 0:
            return t
        t -= mult
    return n


def _dot(a, b):
    return jnp.dot(a, b, preferred_element_type=F32)


def _dot_nt(a, b):
    return lax.dot_general(a, b, (((1,), (1,)), ((), ())), preferred_element_type=F32)


def _dot_tn(a, b):
    return lax.dot_general(a, b, (((0,), (0,)), ((), ())), preferred_element_type=F32)


def _sigmoid(x):
    return 1.0 / (1.0 + jnp.exp(-x))


def _rms_fwd(x, g):
    r = lax.rsqrt(jnp.mean(x * x, axis=-1, keepdims=True) + NORM_EPS)
    return x * r * g


def _rms_bwd(dh, x, g):
    r = lax.rsqrt(jnp.mean(x * x, axis=-1, keepdims=True) + NORM_EPS)
    xh = x * r
    dxh = dh * g
    dx = r * (dxh - xh * jnp.mean(dxh * xh, axis=-1, keepdims=True))
    return dx, jnp.sum(dh * xh, axis=0, keepdims=True)


def _split3(x):
    hi = x.astype(BF16)
    r = x - hi.astype(F32)
    mid = r.astype(BF16)
    lo = (r - mid.astype(F32)).astype(BF16)
    return hi, mid, lo


def _blk(ref):
    return ref[0] if len(ref.shape) == 3 else ref[...]


_DEP_SPEC = pl.BlockSpec(memory_space=pl.ANY)


def _mesh_pos():
    return lax.axis_index("x"), lax.axis_index("y"), lax.axis_index("c")


def _peer(pos, k):
    x, y, c = pos
    return (1 - x if k & 4 else x, 1 - y if k & 2 else y, 1 - c if k & 1 else c)


def _lin(pos):
    x, y, c = pos
    return 4 * x + 2 * y + c


def _remote_copy(src, land, send_sems, recv_sems, pos, k, all_to_all):
    peer = _peer(pos, k)
    return pltpu.make_async_remote_copy(
        src_ref=src.at[_lin(peer)] if all_to_all else src, dst_ref=land.at[_lin(pos)],
        send_sem=send_sems.at[k - 1], recv_sem=recv_sems.at[k - 1],
        device_id=peer, device_id_type=MESH_ID)


def _landing_shape(a, all_to_all):
    return a.shape if all_to_all else (N_DEV,) + a.shape


def _exchange(name, arrays, all_to_all, dep=None):
    n = len(arrays)
    deps = [] if dep is None else [dep]
    out_shapes = [jax.ShapeDtypeStruct(_landing_shape(a, all_to_all), a.dtype) for a in arrays]

    def body(*refs):
        ins, outs = refs[:n], refs[n + len(deps):2 * n + len(deps)]
        send_sems, recv_sems, local_sems = refs[2 * n + len(deps):]
        pos = _mesh_pos()
        me = _lin(pos)
        local = []
        for a in range(n):
            src = ins[a].at[me] if all_to_all else ins[a]
            cp = pltpu.make_async_copy(src, outs[a].at[me], local_sems.at[a])
            cp.start()
            local.append(cp)
        remote = [_remote_copy(ins[a], outs[a], send_sems.at[a], recv_sems.at[a], pos, k, all_to_all)
                  for a in range(n) for k in range(1, N_DEV)]
        for cp in remote:
            cp.start()
        for cp in remote:
            cp.wait()
        for cp in local:
            cp.wait()

    any_spec = pl.BlockSpec(memory_space=pl.ANY)
    return pl.pallas_call(
        body, out_shape=out_shapes, in_specs=[any_spec] * (n + len(deps)), out_specs=[any_spec] * n,
        scratch_shapes=[pltpu.SemaphoreType.DMA((n, N_DEV - 1)),
                        pltpu.SemaphoreType.DMA((n, N_DEV - 1)),
                        pltpu.SemaphoreType.DMA((n,))],
        name=name)(*arrays, *deps)


def _all_gather_two_level(name, shards):
    n = len(shards)
    out_shapes = [jax.ShapeDtypeStruct((N_DEV,) + a.shape, a.dtype) for a in shards]

    def body(*refs):
        ins, outs = refs[:n], refs[n:2 * n]
        send_sems, recv_sems, local_sems = refs[2 * n:]
        x, y, c = pos = _mesh_pos()
        sibling = (x, y, 1 - c)
        chips = [(1 - x, y), (x, 1 - y), (1 - x, 1 - y)]

        def copy(a, k, block, to, src=None):
            slot = outs[a].at[_lin(block)]
            return pltpu.make_async_remote_copy(
                src_ref=slot if src is None else src, dst_ref=slot,
                send_sem=send_sems.at[a, k], recv_sem=recv_sems.at[a, k],
                device_id=to, device_id_type=MESH_ID)

        local = [pltpu.make_async_copy(ins[a], outs[a].at[_lin(pos)], local_sems.at[a]) for a in range(n)]
        first = [copy(a, 1 + j, pos, (*chip, c), src=ins[a]) for j, chip in enumerate(chips) for a in range(n)]
        first += [copy(a, 0, pos, sibling, src=ins[a]) for a in range(n)]
        for cp in first + local:
            cp.start()
        passed = []
        for j, chip in enumerate(chips):
            for a in range(n):
                copy(a, 1 + j, (*chip, c), pos).wait_recv()
                cp = copy(a, 4 + j, (*chip, c), sibling)
                cp.start()
                passed.append(cp)
        for a in range(n):
            copy(a, 0, sibling, pos).wait_recv()
        for j, chip in enumerate(chips):
            for a in range(n):
                copy(a, 4 + j, (*chip, 1 - c), pos).wait_recv()
        for cp in first + passed:
            cp.wait_send()
        for cp in local:
            cp.wait()

    any_spec = pl.BlockSpec(memory_space=pl.ANY)
    return pl.pallas_call(
        body, out_shape=out_shapes, in_specs=[any_spec] * n, out_specs=[any_spec] * n,
        scratch_shapes=[pltpu.SemaphoreType.DMA((n, N_DEV - 1)),
                        pltpu.SemaphoreType.DMA((n, N_DEV - 1)),
                        pltpu.SemaphoreType.DMA((n,))],
        name=name)(*shards)


_HBM_SPEC = pl.BlockSpec(memory_space=pltpu.HBM)
_SEM_SPEC = pl.BlockSpec(memory_space=pltpu.SEMAPHORE)
_SIDE_EFFECT = pltpu.SideEffectType.DATAFLOW_SIDE_EFFECTING


def _xchg_start(name, arrays, all_to_all, dep=None):
    n = len(arrays)
    me = _lin(_mesh_pos())
    lands = [lax.dynamic_update_index_in_dim(
        lax.empty(_landing_shape(a, all_to_all), a.dtype),
        lax.dynamic_index_in_dim(a, me, 0, keepdims=False) if all_to_all else a, me, 0) for a in arrays]

    deps = [] if dep is None else [dep]

    def body(*refs):
        srcs, lnds = refs[:n], refs[n:2 * n]
        outs = refs[2 * n + len(deps):]
        send, recv = outs[:n], outs[n:2 * n]
        token = outs[4 * n]
        pos = _mesh_pos()
        for a in range(n):
            for k in range(1, N_DEV):
                _remote_copy(srcs[a], lnds[a], send[a], recv[a], pos, k, all_to_all).start()
        token[...] = jnp.zeros_like(token)

    hbm = lambda a: pltpu.HBM(a.shape, a.dtype)
    sems = [pltpu.SemaphoreType.DMA((N_DEV - 1,))] * (2 * n)
    res = pl.pallas_call(
        body, name=name,
        out_shape=sems + [hbm(a) for a in arrays] + [hbm(l) for l in lands]
        + [jax.ShapeDtypeStruct((SUBLANES, LANES), F32)],
        in_specs=[_HBM_SPEC] * (2 * n) + [_DEP_SPEC] * len(deps),
        out_specs=[_SEM_SPEC] * (2 * n) + [_HBM_SPEC] * (2 * n) + [pl.BlockSpec(memory_space=pltpu.VMEM)],
        input_output_aliases={a: 2 * n + a for a in range(2 * n)},
        compiler_params=pltpu.CompilerParams(has_side_effects=_SIDE_EFFECT),
    )(*[pltpu.with_memory_space_constraint(a, pltpu.HBM) for a in list(arrays) + lands], *deps)
    per_array = [(res[a], res[n + a], res[2 * n + a], res[3 * n + a]) for a in range(n)]
    return per_array, res[4 * n]


def _xchg_wait(name, started, all_to_all, after):
    n = len(started)

    def body(*refs):
        srcs, lnds, send, recv = refs[:n], refs[n:2 * n], refs[2 * n:3 * n], refs[3 * n:4 * n]
        pos = _mesh_pos()
        for a in range(n):
            for k in range(1, N_DEV):
                cp = _remote_copy(srcs[a], lnds[a], send[a], recv[a], pos, k, all_to_all)
                cp.wait_send()
                cp.wait_recv()

    hbm = lambda a: pltpu.HBM(a.shape, a.dtype)
    srcs = [s[2] for s in started]
    lands = [s[3] for s in started]
    res = pl.pallas_call(
        body, name=name,
        out_shape=[hbm(a) for a in srcs + lands],
        in_specs=[_HBM_SPEC] * (2 * n) + [_SEM_SPEC] * (2 * n) + [_DEP_SPEC],
        out_specs=[_HBM_SPEC] * (2 * n),
        input_output_aliases={a: a for a in range(2 * n)},
        compiler_params=pltpu.CompilerParams(has_side_effects=_SIDE_EFFECT),
    )(*srcs, *lands, *[s[0] for s in started], *[s[1] for s in started], after)
    return list(res[n:])


def _reduce_adam(name, parts, w=None, m=None, v=None):
    n, R, C = parts.shape
    tr = _tile(R, 256)
    do_adam = w is not None
    bc1 = 1.0 - ADAM_B1 ** ADAM_STEP
    bc2 = 1.0 - ADAM_B2 ** ADAM_STEP

    def body(*refs):
        p_ref = refs[0]
        g = p_ref[0].astype(F32)
        for d in range(1, n):
            g = g + p_ref[d].astype(F32)
        if not do_adam:
            refs[1][...] = g
            return
        w_ref, m_ref, v_ref, g_ref, d_ref, nm_ref, nv_ref = refs[1:]
        g_ref[...] = g
        nm = ADAM_B1 * m_ref[...] + (1.0 - ADAM_B1) * g
        nv = ADAM_B2 * v_ref[...] + (1.0 - ADAM_B2) * (g * g)
        m_hat = nm / bc1
        v_hat = nv / bc2
        d_ref[...] = -ADAM_LR * (m_hat / (jnp.sqrt(v_hat) + ADAM_EPS) + ADAM_WD * w_ref[...])
        nm_ref[...] = nm
        nv_ref[...] = nv

    tc = _tile(C, 512, LANES) if tr == R and R > 256 else C
    row = pl.BlockSpec((tr, tc), lambda i, j: (i, j))
    part = pl.BlockSpec((n, tr, tc), lambda i, j: (0, i, j))
    shard = jax.ShapeDtypeStruct((R, C), F32)
    grid = (R // tr, C // tc)
    if do_adam:
        return pl.pallas_call(body, grid=grid, in_specs=[part, row, row, row],
                              out_specs=[row] * 4, out_shape=[shard] * 4,
                              compiler_params=_params(2), name=name)(parts, w, m, v)
    return pl.pallas_call(body, grid=grid, in_specs=[part], out_specs=row, out_shape=shard,
                          compiler_params=_params(2), name=name)(parts)


def _ffn_fwd(name, x, gain, wgt, wut, wd):
    S, D = x.shape
    nb, Fs, _ = wgt.shape
    tm = _tile(S, 512)

    def body(x_ref, g_ref, wg_ref, wu_ref, wd_ref, xo_ref, h_ref, G_ref, U_ref, acc_ref):
        j = pl.program_id(1)

        @pl.when(j == 0)
        def _():
            h_ref[...] = _rms_fwd(x_ref[...], g_ref[...]).astype(BF16)
            acc_ref[...] = jnp.zeros_like(acc_ref)

        h = h_ref[...]
        G = _dot_nt(h, wg_ref[0])
        U = _dot_nt(h, wu_ref[0])
        G_ref[0] = G
        U_ref[0] = U
        a = G * _sigmoid(G) * U
        acc_ref[...] += _dot(a.astype(BF16), wd_ref[0])

        @pl.when(j == nb - 1)
        def _():
            xo_ref[...] = x_ref[...] + 0.5 * acc_ref[...]

    row = pl.BlockSpec((tm, D), lambda i, j: (i, 0))
    act = pl.BlockSpec((1, tm, Fs), lambda i, j: (j, i, 0))
    wblk = pl.BlockSpec((1, Fs, D), lambda i, j: (j, 0, 0))
    return pl.pallas_call(
        body, grid=(S // tm, nb),
        in_specs=[row, pl.BlockSpec((1, D), lambda i, j: (0, 0)), wblk, wblk, wblk],
        out_specs=[row, row, act, act],
        out_shape=[jax.ShapeDtypeStruct((S, D), F32), jax.ShapeDtypeStruct((S, D), BF16),
                   jax.ShapeDtypeStruct((nb, S, Fs), F32), jax.ShapeDtypeStruct((nb, S, Fs), F32)],
        scratch_shapes=[pltpu.VMEM((tm, D), F32)],
        compiler_params=_params(2), name=name)(x, gain, wgt, wut, wd)


def _ffn_bwd_gate(name, dxo, G, U, wd):
    S, D = dxo.shape
    nb, Fs, _ = wd.shape
    tm = _tile(S, 512)

    def gate_body(dxo_ref, G_ref, U_ref, wd_ref, dG_ref, dU_ref, A_ref, dxb_ref):
        @pl.when(pl.program_id(1) == 0)
        def _():
            dxb_ref[...] = dxo_ref[...].astype(BF16)

        dA = 0.5 * _dot_nt(dxb_ref[...], wd_ref[0])
        Gv = G_ref[0]
        Uv = U_ref[0]
        sg = _sigmoid(Gv)
        sl = Gv * sg
        dG_ref[0] = (dA * Uv * (sg * (1.0 + Gv * (1.0 - sg)))).astype(BF16)
        dU_ref[0] = (dA * sl).astype(BF16)
        A_ref[0] = (sl * Uv).astype(BF16)

    row = pl.BlockSpec((tm, D), lambda i, j: (i, 0))
    act = pl.BlockSpec((1, tm, Fs), lambda i, j: (j, i, 0))
    wblk = pl.BlockSpec((1, Fs, D), lambda i, j: (j, 0, 0))
    act_shape = jax.ShapeDtypeStruct((nb, S, Fs), BF16)
    return pl.pallas_call(
        gate_body, grid=(S // tm, nb), in_specs=[row, act, act, wblk], out_specs=[act, act, act],
        out_shape=[act_shape, act_shape, act_shape],
        scratch_shapes=[pltpu.VMEM((tm, D), BF16)],
        compiler_params=_params(2), name=name)(dxo, G, U, wd)


def _ffn_bwd_in(name, dG, dU, wgt, wut, dxo, x_in, gain):
    S, D = x_in.shape
    nb, Fs, _ = wgt.shape
    tm = _tile(S, 512)
    rows_per_chunk = _tile(tm, 128)

    def in_body(dG_ref, dU_ref, wg_ref, wu_ref, dxo_ref, x_ref, g_ref, dx_ref, dgain_ref, acc_ref):
        i, j = pl.program_id(0), pl.program_id(1)

        @pl.when(j == 0)
        def _():
            acc_ref[...] = jnp.zeros_like(acc_ref)

        @pl.when((i == 0) & (j == 0))
        def _():
            dgain_ref[...] = jnp.zeros_like(dgain_ref)

        acc_ref[...] += _dot(dG_ref[0], wg_ref[0]) + _dot(dU_ref[0], wu_ref[0])

        @pl.when(j == nb - 1)
        def _():
            def chunk(r, dg_sum):
                rows = pl.ds(pl.multiple_of(r * rows_per_chunk, rows_per_chunk), rows_per_chunk)
                dx, dg = _rms_bwd(acc_ref[rows, :], x_ref[rows, :], g_ref[...])
                dx_ref[rows, :] = dxo_ref[rows, :] + dx
                return dg_sum + dg

            dgain_ref[...] += lax.fori_loop(0, tm // rows_per_chunk, chunk, jnp.zeros((1, D), F32))

    row = pl.BlockSpec((tm, D), lambda i, j: (i, 0))
    vec = pl.BlockSpec((1, D), lambda i, j: (0, 0))
    act = pl.BlockSpec((1, tm, Fs), lambda i, j: (j, i, 0))
    wblk = pl.BlockSpec((1, Fs, D), lambda i, j: (j, 0, 0))
    return pl.pallas_call(
        in_body, grid=(S // tm, nb), in_specs=[act, act, wblk, wblk, row, row, vec],
        out_specs=[row, vec],
        out_shape=[jax.ShapeDtypeStruct((S, D), F32), jax.ShapeDtypeStruct((1, D), F32)],
        scratch_shapes=[pltpu.VMEM((tm, D), F32)],
        compiler_params=_params(2), name=name)(dG, dU, wgt, wut, dxo, x_in, gain)


def _mm_tn(name, lhs, rhs, out_shape, lhs_spec, rhs_spec, out_spec, grid, scale, out_dtype, dep=None):
    acc_shape = tuple(out_spec.block_shape[-2:])
    n_red = grid[-1]

    def body(l_ref, r_ref, *rest):
        o_ref, acc_ref = rest[-2:]
        i = pl.program_id(len(grid) - 1)

        @pl.when(i == 0)
        def _():
            acc_ref[...] = jnp.zeros_like(acc_ref)

        acc_ref[...] += _dot_tn(_blk(l_ref).astype(BF16), _blk(r_ref).astype(BF16))

        @pl.when(i == n_red - 1)
        def _():
            res = (scale * acc_ref[...]).astype(out_dtype)
            if len(o_ref.shape) == 3:
                o_ref[0] = res
            else:
                o_ref[...] = res

    deps = [] if dep is None else [dep]
    return pl.pallas_call(
        body, grid=grid, in_specs=[lhs_spec, rhs_spec] + [_DEP_SPEC] * len(deps), out_specs=out_spec,
        out_shape=jax.ShapeDtypeStruct(out_shape, out_dtype),
        scratch_shapes=[pltpu.VMEM(acc_shape, F32)],
        compiler_params=_params(len(grid)), name=name)(lhs, rhs, *deps)


def _norm_mm_nt(name, x, gain, wt, tn):
    S, D = x.shape
    N = wt.shape[0]
    tm = _tile(S, 1024)

    def body(x_ref, g_ref, w_ref, o_ref, h_ref):
        @pl.when(pl.program_id(1) == 0)
        def _():
            h_ref[...] = _rms_fwd(x_ref[...], g_ref[...]).astype(BF16)

        o_ref[...] = _dot_nt(h_ref[...], w_ref[...])

    row = pl.BlockSpec((tm, D), lambda i, j: (i, 0))
    return pl.pallas_call(
        body, grid=(S // tm, N // tn),
        in_specs=[row, pl.BlockSpec((1, D), lambda i, j: (0, 0)),
                  pl.BlockSpec((tn, D), lambda i, j: (j, 0))],
        out_specs=[pl.BlockSpec((tm, tn), lambda i, j: (i, j)), row],
        out_shape=[jax.ShapeDtypeStruct((S, N), F32), jax.ShapeDtypeStruct((S, D), BF16)],
        compiler_params=_params(2), name=name)(x, gain, wt)


def _mm_res(name, a, w, res, tn):
    S, K = a.shape
    N = w.shape[1]
    tm = _tile(S, 512)

    def body(a_ref, w_ref, r_ref, o_ref):
        o_ref[...] = r_ref[...] + _dot(a_ref[...], w_ref[...])

    tile = pl.BlockSpec((tm, tn), lambda i, j: (i, j))
    return pl.pallas_call(
        body, grid=(S // tm, N // tn),
        in_specs=[pl.BlockSpec((tm, K), lambda i, j: (i, 0)),
                  pl.BlockSpec((K, tn), lambda i, j: (0, j)), tile],
        out_specs=tile, out_shape=jax.ShapeDtypeStruct((S, N), F32),
        compiler_params=_params(2), name=name)(a, w, res)


def _mm_k(name, a, b, tk, transpose_b, norm_bwd=None, dep=None):
    S, K = a.shape
    N = b.shape[0] if transpose_b else b.shape[1]
    tm = _tile(S, 512)
    rows_per_chunk = _tile(tm, 128)
    nk = K // tk
    n_extra = 0 if norm_bwd is None else 3
    deps = [] if dep is None else [dep]

    def body(*refs):
        a_ref, b_ref = refs[:2]
        outs = refs[2 + n_extra + len(deps):]
        acc_ref = outs[-1]
        i, k = pl.program_id(0), pl.program_id(1)

        @pl.when(k == 0)
        def _():
            acc_ref[...] = jnp.zeros_like(acc_ref)

        av = a_ref[...].astype(BF16)
        acc_ref[...] += _dot_nt(av, b_ref[...]) if transpose_b else _dot(av, b_ref[...])

        if norm_bwd is None:
            @pl.when(k == nk - 1)
            def _():
                outs[0][...] = acc_ref[...]
        else:
            x_ref, g_ref, dres_ref = refs[2:5]
            o_ref, dgain_ref = outs[:2]

            @pl.when((i == 0) & (k == 0))
            def _():
                dgain_ref[...] = jnp.zeros_like(dgain_ref)

            @pl.when(k == nk - 1)
            def _():
                def chunk(r, dg_sum):
                    rows = pl.ds(pl.multiple_of(r * rows_per_chunk, rows_per_chunk), rows_per_chunk)
                    dx, dg = _rms_bwd(acc_ref[rows, :], x_ref[rows, :], g_ref[...])
                    o_ref[rows, :] = dres_ref[rows, :] + dx
                    return dg_sum + dg

                dgain_ref[...] += lax.fori_loop(0, tm // rows_per_chunk, chunk, jnp.zeros((1, N), F32))

    a_spec = pl.BlockSpec((tm, tk), lambda i, k: (i, k))
    b_spec = (pl.BlockSpec((N, tk), lambda i, k: (0, k)) if transpose_b
              else pl.BlockSpec((tk, N), lambda i, k: (k, 0)))
    row = pl.BlockSpec((tm, N), lambda i, k: (i, 0))
    vec = pl.BlockSpec((1, N), lambda i, k: (0, 0))
    out = jax.ShapeDtypeStruct((S, N), F32)
    scratch = [pltpu.VMEM((tm, N), F32)]
    dep_specs = [_DEP_SPEC] * len(deps)
    if norm_bwd is None:
        return pl.pallas_call(body, grid=(S // tm, nk), in_specs=[a_spec, b_spec] + dep_specs,
                              out_specs=row, out_shape=out, scratch_shapes=scratch,
                              compiler_params=_params(2), name=name)(a, b, *deps)
    x_in, gain, dres = norm_bwd
    return pl.pallas_call(body, grid=(S // tm, nk), in_specs=[a_spec, b_spec, row, vec, row] + dep_specs,
                          out_specs=[row, vec],
                          out_shape=[out, jax.ShapeDtypeStruct((1, N), F32)],
                          scratch_shapes=scratch,
                          compiler_params=_params(2), name=name)(a, b, x_in, gain, dres, *deps)


def _final(name, x, gain, target):
    S, D = x.shape
    tm = _tile(S, 512)

    def body(x_ref, g_ref, t_ref, dx_ref, dgain_ref, loss_ref):
        @pl.when(pl.program_id(0) == 0)
        def _():
            dgain_ref[...] = jnp.zeros_like(dgain_ref)
            loss_ref[...] = jnp.zeros_like(loss_ref)

        xv = x_ref[...]
        err = _rms_fwd(xv, g_ref[...]) - t_ref[...]
        per_tok = jnp.mean(err * err, axis=-1, keepdims=True)
        loss_ref[...] += 0.5 * jnp.sum(per_tok, axis=0, keepdims=True)
        dx, dg = _rms_bwd(err * (1.0 / D), xv, g_ref[...])
        dx_ref[...] = dx
        dgain_ref[...] += dg

    row = pl.BlockSpec((tm, D), lambda i: (i, 0))
    vec = pl.BlockSpec((1, D), lambda i: (0, 0))
    return pl.pallas_call(
        body, grid=(S // tm,), in_specs=[row, vec, row],
        out_specs=[row, vec, pl.BlockSpec((1, LANES), lambda i: (0, 0))],
        out_shape=[jax.ShapeDtypeStruct((S, D), F32), jax.ShapeDtypeStruct((1, D), F32),
                   jax.ShapeDtypeStruct((1, LANES), F32)],
        compiler_params=_params(1), name=name)(x, gain, target)


def _conv_tiles(S):
    ts = _tile(S, 256, CONV_HALO)
    return ts, ts // CONV_HALO


def _ln_stats(yc):
    mu = jnp.mean(yc, axis=-1, keepdims=True)
    d = yc - mu
    rs = lax.rsqrt(jnp.mean(d * d, axis=-1, keepdims=True) + LN_EPS)
    return d * rs, rs


N_PHASE = SUBLANES
PHASE_ROWS = CONV_HALO - SUBLANES


def _shifted_copies(buf, shifted, ts):
    for p in range(1, N_PHASE):
        shifted[p - 1] = buf[pl.ds(p, ts + PHASE_ROWS), :]


def _tap(buf, shifted, offset, rows0, n_rows, cols):
    a, p = divmod(offset, N_PHASE)
    rows = pl.ds(a * N_PHASE + rows0, n_rows)
    return buf[rows, cols] if p == 0 else shifted[p - 1, rows, cols]


def _conv_fwd(name, proj, C, cw, cb, lg, lb):
    S = proj.shape[0]
    ts, hb = _conv_tiles(S)
    rb = _tile(ts, 128)
    first = CONV_HALO - CONV_WIDTH + 1

    def body(a_ref, g_ref, ah_ref, gh_ref, cw_ref, cb_ref, lg_ref, lb_ref, yc_ref, y_ref, ubuf, ushift):
        i = pl.program_id(0)
        uh = ah_ref[...] * _sigmoid(gh_ref[...])
        ubuf[pl.ds(0, CONV_HALO), :] = jnp.where(i > 0, uh, 0.0)
        ubuf[pl.ds(CONV_HALO, ts), :] = a_ref[...] * _sigmoid(g_ref[...])
        _shifted_copies(ubuf, ushift, ts)

        def col_block(c, carry):
            cols = pl.ds(pl.multiple_of(c * LANES, LANES), LANES)
            for r in range(ts // rb):
                acc = jnp.zeros((rb, LANES), F32)
                for k in range(CONV_WIDTH):
                    acc = acc + cw_ref[pl.ds(k, 1), cols] * _tap(ubuf, ushift, first + k, r * rb, rb, cols)
                yc_ref[pl.ds(r * rb, rb), cols] = acc + cb_ref[:, cols]
            return carry

        lax.fori_loop(0, C // LANES, col_block, 0)
        yn, _ = _ln_stats(yc_ref[...])
        z = yn * lg_ref[...] + lb_ref[...]
        y_ref[...] = (z * _sigmoid(z)).astype(BF16)

    main = lambda col: pl.BlockSpec((ts, C), lambda i: (i, col))
    halo = lambda col: pl.BlockSpec((CONV_HALO, C), lambda i: (jnp.maximum(i * hb - 1, 0), col))
    vec = pl.BlockSpec((1, C), lambda i: (0, 0))
    return pl.pallas_call(
        body, grid=(S // ts,),
        in_specs=[main(0), main(1), halo(0), halo(1),
                  pl.BlockSpec((CONV_HALO, C), lambda i: (0, 0)), vec, vec, vec],
        out_specs=[pl.BlockSpec((ts, C), lambda i: (i, 0))] * 2,
        out_shape=[jax.ShapeDtypeStruct((S, C), F32), jax.ShapeDtypeStruct((S, C), BF16)],
        scratch_shapes=[pltpu.VMEM((ts + CONV_HALO, C), F32),
                        pltpu.VMEM((N_PHASE - 1, ts + PHASE_ROWS, C), F32)],
        compiler_params=_params(1), name=name)(proj, proj, proj, proj, cw, cb, lg, lb)


def _conv_bwd(name, dmix, yc, proj, C, cw, lg, lb):
    S = proj.shape[0]
    ts, hb = _conv_tiles(S)
    n_t = S // ts
    rb = _tile(ts, 128)
    first = CONV_HALO - CONV_WIDTH + 1

    def body(dy_ref, yc_ref, dyh_ref, ych_ref, a_ref, g_ref, ah_ref, gh_ref, cw_ref, lg_ref, lb_ref,
             dag_ref, dcw_ref, dcb_ref, dlg_ref, dlb_ref, ubuf, dbuf, ushift, dshift, du_ref):
        i = pl.program_id(0)

        @pl.when(i == 0)
        def _():
            dcw_ref[...] = jnp.zeros_like(dcw_ref)
            dcb_ref[...] = jnp.zeros_like(dcb_ref)
            dlg_ref[...] = jnp.zeros_like(dlg_ref)
            dlb_ref[...] = jnp.zeros_like(dlb_ref)

        def ln_bwd(dy, ycv):
            yn, rs = _ln_stats(ycv)
            z = yn * lg_ref[...] + lb_ref[...]
            sg = _sigmoid(z)
            dz = dy * (sg * (1.0 + z * (1.0 - sg)))
            dyn = dz * lg_ref[...]
            dyc = rs * (dyn - jnp.mean(dyn, axis=-1, keepdims=True)
                        - yn * jnp.mean(dyn * yn, axis=-1, keepdims=True))
            return dyc, dz, yn

        dyc, dz, yn = ln_bwd(dy_ref[...], yc_ref[...])
        dlg_ref[...] += jnp.sum(dz * yn, axis=0, keepdims=True)
        dlb_ref[...] += jnp.sum(dz, axis=0, keepdims=True)
        dcb_ref[...] += jnp.sum(dyc, axis=0, keepdims=True)
        dych, _, _ = ln_bwd(dyh_ref[...], ych_ref[...])
        dbuf[pl.ds(0, ts), :] = dyc
        dbuf[pl.ds(ts, CONV_HALO), :] = jnp.where(i < n_t - 1, dych, 0.0)

        uh = ah_ref[...] * _sigmoid(gh_ref[...])
        ubuf[pl.ds(0, CONV_HALO), :] = jnp.where(i > 0, uh, 0.0)
        ubuf[pl.ds(CONV_HALO, ts), :] = a_ref[...] * _sigmoid(g_ref[...])
        _shifted_copies(ubuf, ushift, ts)
        _shifted_copies(dbuf, dshift, ts)

        def col_block(c, carry):
            cols = pl.ds(pl.multiple_of(c * LANES, LANES), LANES)
            dcw = [jnp.zeros((1, LANES), F32)] * CONV_WIDTH
            for r in range(ts // rb):
                du = jnp.zeros((rb, LANES), F32)
                dyc_blk = dbuf[pl.ds(r * rb, rb), cols]
                for k in range(CONV_WIDTH):
                    du = du + cw_ref[pl.ds(k, 1), cols] * _tap(dbuf, dshift, CONV_WIDTH - 1 - k, r * rb, rb, cols)
                    tap = _tap(ubuf, ushift, first + k, r * rb, rb, cols)
                    dcw[k] = dcw[k] + jnp.sum(dyc_blk * tap, axis=0, keepdims=True)
                du_ref[pl.ds(r * rb, rb), cols] = du
            for k in range(CONV_WIDTH):
                dcw_ref[pl.ds(k, 1), cols] += dcw[k]
            return carry

        lax.fori_loop(0, C // LANES, col_block, 0)
        du = du_ref[...]
        av = a_ref[...]
        sgm = _sigmoid(g_ref[...])
        dag_ref[:, pl.ds(0, C)] = (du * sgm).astype(BF16)
        dag_ref[:, pl.ds(C, C)] = (du * av * sgm * (1.0 - sgm)).astype(BF16)

    main = lambda col: pl.BlockSpec((ts, C), lambda i: (i, col))
    past = lambda col: pl.BlockSpec((CONV_HALO, C), lambda i: (jnp.maximum(i * hb - 1, 0), col))
    nxt = pl.BlockSpec((CONV_HALO, C), lambda i: (jnp.minimum((i + 1) * hb, n_t * hb - 1), 0))
    vec = pl.BlockSpec((1, C), lambda i: (0, 0))
    full = pl.BlockSpec((CONV_HALO, C), lambda i: (0, 0))
    vshape = jax.ShapeDtypeStruct((1, C), F32)
    buf = pltpu.VMEM((ts + CONV_HALO, C), F32)
    shifted = pltpu.VMEM((N_PHASE - 1, ts + PHASE_ROWS, C), F32)
    return pl.pallas_call(
        body, grid=(n_t,),
        in_specs=[main(0), main(0), nxt, nxt, main(0), main(1), past(0), past(1), full, vec, vec],
        out_specs=[pl.BlockSpec((ts, 2 * C), lambda i: (i, 0)), full, vec, vec, vec],
        out_shape=[jax.ShapeDtypeStruct((S, 2 * C), BF16),
                   jax.ShapeDtypeStruct((CONV_HALO, C), F32), vshape, vshape, vshape],
        scratch_shapes=[buf, buf, shifted, shifted, pltpu.VMEM((ts, C), F32)],
        compiler_params=_params(1),
        name=name)(dmix, yc, dmix, yc, proj, proj, proj, proj, cw, lg, lb)


AUG = 3
ROW_CHUNK = 32
REDUCE_ROWS = 1024


def _gate_prep(name, proj, f_blk, fbias, n_pair):
    S = proj.shape[0]
    ts = _tile(S, 512)
    W = LANES * n_pair

    def body(pf_ref, fb_ref, ka_ref, carry):
        @pl.when(pl.program_id(0) == 0)
        def _():
            carry[...] = jnp.zeros_like(carry)

        f = pf_ref[...] + fb_ref[...]
        logf = jnp.minimum(f, 0.0) - jnp.log(1.0 + jnp.exp(-jnp.abs(f)))
        r = lax.broadcasted_iota(jnp.int32, (ts, ts), 0)
        c = lax.broadcasted_iota(jnp.int32, (ts, ts), 1)
        ltri = (c <= r).astype(BF16)
        hi, mid, lo = _split3(logf)
        cs = _dot(ltri, hi) + _dot(ltri, mid) + _dot(ltri, lo) + carry[...]
        carry[...] = cs[ts - 1:ts, :]
        hh = lax.broadcasted_iota(jnp.int32, (LANES, W), 0)
        ll = lax.broadcasted_iota(jnp.int32, (LANES, W), 1)
        pair, w = ll >> 7, ll & (LANES - 1)
        ka = jnp.zeros((ts, W), F32)
        for p, piece in enumerate(_split3(-cs)):
            e = (((w == HEAD_DIM + p) & (hh == 2 * pair)) | ((w == p) & (hh == 2 * pair + 1)))
            ka = ka + _dot(piece, e.astype(BF16))
        lw = lax.broadcasted_iota(jnp.int32, (1, W), 1) & (LANES - 1)
        ka = ka + ((lw == HEAD_DIM + AUG) | (lw == AUG)).astype(F32)
        ka_ref[...] = ka.astype(BF16)

    return pl.pallas_call(
        body, grid=(S // ts,),
        in_specs=[pl.BlockSpec((ts, LANES), lambda i: (i, f_blk)),
                  pl.BlockSpec((1, LANES), lambda i: (0, 0))],
        out_specs=pl.BlockSpec((ts, W), lambda i: (i, 0)),
        out_shape=jax.ShapeDtypeStruct((S, W), BF16),
        scratch_shapes=[pltpu.VMEM((1, LANES), F32)],
        compiler_params=_params(1), name=name)(proj, fbias)


def _gate_bwd(name, sp, rs, proj, f_blk, fbias, n_pair):
    S = proj.shape[0]
    ts = _tile(S, 512)
    n_t = S // ts
    W = LANES * n_pair

    def body(sp_ref, rs_ref, pf_ref, fb_ref, df_ref, dfb_ref, carry):
        @pl.when(pl.program_id(0) == 0)
        def _():
            carry[...] = jnp.zeros_like(carry)
            dfb_ref[...] = jnp.zeros_like(dfb_ref)

        ll = lax.broadcasted_iota(jnp.int32, (W, LANES), 0)
        hh = lax.broadcasted_iota(jnp.int32, (W, LANES), 1)
        pair, w = ll >> 7, ll & (LANES - 1)
        first, second = hh == 2 * pair, hh == 2 * pair + 1

        def pick(ref, lane_first, lane_second):
            sel = (((w == lane_first) & first) | ((w == lane_second) & second)).astype(BF16)
            hi, mid, lo = _split3(ref[...])
            return _dot(hi, sel) + _dot(mid, sel) + _dot(lo, sel)

        dc = pick(rs_ref, HEAD_DIM + AUG, AUG) - pick(sp_ref, HEAD_DIM, 0)
        r = lax.broadcasted_iota(jnp.int32, (ts, ts), 0)
        c = lax.broadcasted_iota(jnp.int32, (ts, ts), 1)
        utri = (c >= r).astype(BF16)
        hi, mid, lo = _split3(dc)
        dlogf = _dot(utri, hi) + _dot(utri, mid) + _dot(utri, lo) + carry[...]
        carry[...] = dlogf[0:1, :]
        f = pf_ref[...] + fb_ref[...]
        lane = lax.broadcasted_iota(jnp.int32, (ts, LANES), 1)
        df = jnp.where(lane < 2 * n_pair, dlogf * _sigmoid(-f), 0.0)
        df_ref[...] = df.astype(BF16)
        dfb_ref[...] += jnp.sum(df, axis=0, keepdims=True)

    rev = lambda blk: (lambda i: (n_t - 1 - i, blk))
    return pl.pallas_call(
        body, grid=(n_t,),
        in_specs=[pl.BlockSpec((ts, W), rev(0)), pl.BlockSpec((ts, W), rev(0)),
                  pl.BlockSpec((ts, LANES), rev(f_blk)), pl.BlockSpec((1, LANES), lambda i: (0, 0))],
        out_specs=[pl.BlockSpec((ts, LANES), rev(0)), pl.BlockSpec((1, LANES), lambda i: (0, 0))],
        out_shape=[jax.ShapeDtypeStruct((S, LANES), BF16), jax.ShapeDtypeStruct((1, LANES), F32)],
        scratch_shapes=[pltpu.VMEM((1, LANES), F32)],
        compiler_params=_params(1), name=name)(sp, rs, proj, fbias)


def _spare_lane(h):
    return HEAD_DIM if h == 0 else 0


def _head_operands(h, lane, q2, k2, ka2):
    act = (lane < HEAD_DIM) if h == 0 else (lane >= HEAD_DIM)
    base = HEAD_DIM if h == 0 else 0
    ones = ((lane >= base) & (lane < base + AUG)).astype(F32)
    qa = jnp.where(act, q2 * (1.0 / math.sqrt(HEAD_DIM)), ones).astype(BF16)
    ka = jnp.where(act, k2.astype(BF16), ka2)
    return act, qa, ka


def _attn_fwd(name, proj, ka, q_blk, k_blk, v_blk, n_pair):
    S = proj.shape[0]
    tq = _tile(S, 512)
    n_t = S // tq
    W = LANES * n_pair

    rc = _tile(tq, ROW_CHUNK)

    def body(q_ref, k_ref, v_ref, ka_ref, o_ref, o32_ref, lse_ref, m_ref, acc_ref, s_ref, p_ref):
        i, j = pl.program_id(1), pl.program_id(2)

        @pl.when(j == 0)
        def _():
            m_ref[...] = jnp.full_like(m_ref, NEG_INF)
            acc_ref[...] = jnp.zeros_like(acc_ref)

        def step(diagonal):
            lane = lax.broadcasted_iota(jnp.int32, (tq, LANES), 1)
            q2, k2, v2, ka2 = q_ref[...], k_ref[...], v_ref[...], ka_ref[...]
            for h in range(2):
                act, qa, kaug = _head_operands(h, lane, q2, k2, ka2)
                s = _dot_nt(qa, kaug)
                if diagonal:
                    row = lax.broadcasted_iota(jnp.int32, (tq, tq), 0)
                    col = lax.broadcasted_iota(jnp.int32, (tq, tq), 1)
                    s = jnp.where(row >= col, s, NEG_INF)
                s_ref[h] = s
                m_prev = m_ref[h]
                m_new = jnp.maximum(m_prev, jnp.max(s, axis=-1, keepdims=True))
                m_ref[h] = m_new
                for r in range(tq // rc):
                    rows = pl.ds(r * rc, rc)
                    p_ref[h, rows, :] = jnp.exp(s_ref[h, rows, :] - m_ref[h, rows, :]).astype(BF16)
                vm = jnp.where(act, v2, (lane == _spare_lane(h)).astype(F32)).astype(BF16)
                acc_ref[h] = jnp.exp(m_prev - m_new) * acc_ref[h] + _dot(p_ref[h], vm)

        @pl.when(j < i)
        def _():
            step(False)

        @pl.when(j == i)
        def _():
            step(True)
            lane = lax.broadcasted_iota(jnp.int32, (tq, LANES), 1)
            first = lane < HEAD_DIM
            acc = [acc_ref[0], acc_ref[1]]
            den = [acc[h][:, _spare_lane(h):_spare_lane(h) + 1] for h in range(2)]
            out = jnp.where(first, acc[0] / den[0], acc[1] / den[1])
            o_ref[...] = out.astype(BF16)
            o32_ref[...] = out
            lse_ref[...] = jnp.where(first, m_ref[0] + jnp.log(den[0]), m_ref[1] + jnp.log(den[1]))

    qspec = lambda blk: pl.BlockSpec((tq, LANES), lambda p, i, j: (i, blk + p))
    kspec = lambda blk: pl.BlockSpec((tq, LANES), lambda p, i, j: (jnp.minimum(j, i), blk + p))
    out = pl.BlockSpec((tq, LANES), lambda p, i, j: (i, p))
    return pl.pallas_call(
        body, grid=(n_pair, n_t, n_t),
        in_specs=[qspec(q_blk), kspec(k_blk), kspec(v_blk), kspec(0)],
        out_specs=[out, out, out],
        out_shape=[jax.ShapeDtypeStruct((S, W), BF16), jax.ShapeDtypeStruct((S, W), F32),
                   jax.ShapeDtypeStruct((S, W), F32)],
        scratch_shapes=[pltpu.VMEM((2, tq, 1), F32), pltpu.VMEM((2, tq, LANES), F32),
                        pltpu.VMEM((2, tq, tq), F32), pltpu.VMEM((2, tq, tq), BF16)],
        compiler_params=_params(3), name=name)(proj, proj, proj, ka)


def _attn_bwd(name, proj, ka, o, lse, dmix, q_blk, k_blk, v_blk, do_blk, n_pair):
    S = proj.shape[0]
    tq = _tile(S, 512)
    n_t = S // tq
    W = LANES * n_pair
    scale = 1.0 / math.sqrt(HEAD_DIM)
    rc = _tile(tq, ROW_CHUNK)

    def body(q_ref, k_ref, v_ref, ka_ref, o_ref, lse_ref, do_ref,
             dq_ref, dk_ref, dv_ref, sp_ref, rs_ref, dk_acc, dv_acc, s_ref, dp_ref, p_ref, ds_ref, d_ref):
        j, i = pl.program_id(1), pl.program_id(2)

        @pl.when((j == 0) & (i == 0))
        def _():
            dq_ref[...] = jnp.zeros_like(dq_ref)
            rs_ref[...] = jnp.zeros_like(rs_ref)

        @pl.when(i == 0)
        def _():
            dk_acc[...] = jnp.zeros_like(dk_acc)
            dv_acc[...] = jnp.zeros_like(dv_acc)

        def step(diagonal):
            lane = lax.broadcasted_iota(jnp.int32, (tq, LANES), 1)
            q2, k2, v2, ka2 = q_ref[...], k_ref[...], v_ref[...], ka_ref[...]
            o2, do2 = o_ref[...], do_ref[...]
            dq, heads = [], []
            for h in range(2):
                act, qa, kaug = _head_operands(h, lane, q2, k2, ka2)
                dom = jnp.where(act, do2, 0.0)
                d_ref[h] = jnp.sum(dom * o2, axis=-1, keepdims=True)
                dob = dom.astype(BF16)
                s_ref[h] = _dot_nt(qa, kaug)
                dp_ref[h] = _dot_nt(dob, jnp.where(act, v2, 0.0).astype(BF16))
                heads.append((qa, kaug, dob))
            for h in range(2):
                qa, kaug, dob = heads[h]
                for r in range(tq // rc):
                    rows = pl.ds(r * rc, rc)
                    sc = s_ref[h, rows, :]
                    if diagonal:
                        row = lax.broadcasted_iota(jnp.int32, (rc, tq), 0) + r * rc
                        col = lax.broadcasted_iota(jnp.int32, (rc, tq), 1)
                        sc = jnp.where(row >= col, sc, NEG_INF)
                    p = jnp.exp(sc - lse_ref[rows, :][:, h * HEAD_DIM:h * HEAD_DIM + 1])
                    p_ref[h, rows, :] = p.astype(BF16)
                    ds_ref[h, rows, :] = (p * (dp_ref[h, rows, :] - d_ref[h, rows, :])).astype(BF16)
                dv_acc[...] += _dot_tn(p_ref[h], dob)
                dk_acc[h] += _dot_tn(ds_ref[h], qa)
                dq.append(_dot(ds_ref[h], kaug))
            rows = pl.ds(pl.multiple_of(i * tq, tq), tq)
            first = lane < HEAD_DIM
            dq_ref[rows, :] += jnp.where(first, dq[0], dq[1])
            rs_ref[rows, :] += jnp.where(first, dq[1], dq[0])

        @pl.when(i > j)
        def _():
            step(False)

        @pl.when(i == j)
        def _():
            step(True)

        @pl.when(i == n_t - 1)
        def _():
            lane = lax.broadcasted_iota(jnp.int32, (tq, LANES), 1)
            first = lane < HEAD_DIM
            dk_ref[...] = jnp.where(first, dk_acc[0], dk_acc[1]).astype(BF16)
            sp_ref[...] = jnp.where(first, dk_acc[1], dk_acc[0])
            dv_ref[...] = dv_acc[...].astype(BF16)

        @pl.when((j == n_t - 1) & (i == n_t - 1))
        def _():
            dq_ref[...] = dq_ref[...] * scale

    qspec = lambda blk: pl.BlockSpec((tq, LANES), lambda p, j, i: (jnp.maximum(i, j), blk + p))
    kspec = lambda blk: pl.BlockSpec((tq, LANES), lambda p, j, i: (j, blk + p))
    kout = pl.BlockSpec((tq, LANES), lambda p, j, i: (j, p))
    qres = pl.BlockSpec((S, LANES), lambda p, j, i: (0, p))
    return pl.pallas_call(
        body, grid=(n_pair, n_t, n_t),
        in_specs=[qspec(q_blk), kspec(k_blk), kspec(v_blk), kspec(0),
                  qspec(0), qspec(0), qspec(do_blk)],
        out_specs=[qres, kout, kout, kout, qres],
        out_shape=[jax.ShapeDtypeStruct((S, W), F32), jax.ShapeDtypeStruct((S, W), BF16),
                   jax.ShapeDtypeStruct((S, W), BF16), jax.ShapeDtypeStruct((S, W), F32),
                   jax.ShapeDtypeStruct((S, W), F32)],
        scratch_shapes=[pltpu.VMEM((2, tq, LANES), F32), pltpu.VMEM((tq, LANES), F32),
                        pltpu.VMEM((2, tq, tq), F32), pltpu.VMEM((2, tq, tq), F32),
                        pltpu.VMEM((2, tq, tq), BF16), pltpu.VMEM((2, tq, tq), BF16),
                        pltpu.VMEM((2, tq, 1), F32)],
        compiler_params=_params(3), name=name)(proj, proj, proj, ka, o, lse, dmix)


def _ffn_block_grad(name, act, rows, scale, dep=None):
    nb, S, Fs = act.shape
    D = rows.shape[1]
    tm = _tile(S, REDUCE_ROWS)
    return _mm_tn(name, act, rows, (nb, Fs, D),
                  pl.BlockSpec((1, tm, Fs), lambda j, i: (j, i, 0)),
                  pl.BlockSpec((tm, D), lambda j, i: (i, 0)),
                  pl.BlockSpec((1, Fs, D), lambda j, i: (j, 0, 0)), (nb, S // tm), scale, BF16, dep=dep)


def kernel(x, ffn1_norm, ffn1_w_gate, ffn1_w_up, ffn1_w_down, mix_norm, w_in, fgate_bias, conv_w, conv_b, conv_ln_g, conv_ln_b, w_out, ffn2_norm, ffn2_w_gate, ffn2_w_up, ffn2_w_down, final_norm, loss_target, m_ffn1_norm, m_ffn1_w_gate, m_ffn1_w_up, m_ffn1_w_down, m_mix_norm, m_w_in, m_fgate_bias, m_conv_w, m_conv_b, m_conv_ln_g, m_conv_ln_b, m_w_out, m_ffn2_norm, m_ffn2_w_gate, m_ffn2_w_up, m_ffn2_w_down, m_final_norm, v_ffn1_norm, v_ffn1_w_gate, v_ffn1_w_up, v_ffn1_w_down, v_mix_norm, v_w_in, v_fgate_bias, v_conv_w, v_conv_b, v_conv_ln_g, v_conv_ln_b, v_w_out, v_ffn2_norm, v_ffn2_w_gate, v_ffn2_w_up, v_ffn2_w_down, v_final_norm):
    xs = x[0]
    S, D = xs.shape
    C = conv_b.shape[0]
    n_heads = fgate_bias.shape[0]
    FW = n_heads * HEAD_DIM
    n_pair = n_heads // 2
    MIX = C + FW
    in_shard = w_in.shape[1]
    in_cols = in_shard * N_DEV
    NP = -(-in_cols // 512) * 512
    q_blk, k_blk, v_blk = 2 * C // LANES, (2 * C + FW) // LANES, (2 * C + 2 * FW) // LANES
    f_blk = (2 * C + 3 * FW) // LANES
    assert C % LANES == 0 and FW % LANES == 0 and n_heads % 2 == 0 and n_heads <= LANES
    assert in_cols == 2 * C + 3 * FW + n_heads and MIX == w_out.shape[0] * N_DEV

    vec = lambda a: a.reshape(1, -1)
    bf = lambda a: a.astype(BF16)
    tm = _tile(S, REDUCE_ROWS)

    wgt1, wut1, wd1 = _all_gather_two_level("ag_ffn1", [bf(ffn1_w_gate).T, bf(ffn1_w_up).T, bf(ffn1_w_down)])
    ag, _ = _xchg_start("ag_start", [bf(w_in).T, conv_w, bf(w_out), bf(ffn2_w_gate).T, bf(ffn2_w_up).T,
                                     bf(ffn2_w_down)], False, dep=wd1)
    fbias = jnp.pad(vec(fgate_bias), ((0, 0), (0, LANES - n_heads)))

    x1, h1, G1, U1 = _ffn_fwd("ffn1_fwd", xs, vec(ffn1_norm), wgt1, wut1, wd1)
    win_g, cw_g = _xchg_wait("ag_wait_in", ag[0:2], False, x1)
    wint = jnp.pad(win_g.reshape(in_cols, D), ((0, NP - in_cols), (0, 0)))
    cw = jnp.pad(cw_g.transpose(1, 0, 2).reshape(CONV_WIDTH, C), ((0, CONV_HALO - CONV_WIDTH), (0, 0)))
    proj, h2 = _norm_mm_nt("proj_in", x1, vec(mix_norm), wint, 512)
    yc, y_conv = _conv_fwd("conv_fwd", proj, C, cw, vec(conv_b), vec(conv_ln_g), vec(conv_ln_b))
    ka = _gate_prep("gate_prep", proj, f_blk, fbias, n_pair)
    o, o32, lse = _attn_fwd("attn_fwd", proj, ka, q_blk, k_blk, v_blk, n_pair)
    (wout_g,) = _xchg_wait("ag_wait_out", ag[2:3], False, lse)
    wout = wout_g.reshape(MIX, D)
    mix = jnp.concatenate([y_conv, o], axis=1)
    x2 = _mm_res("proj_out", mix, wout, x1, 512)
    wgt2, wut2, wd2 = _xchg_wait("ag_wait_ffn2", ag[3:6], False, x2)
    x3, h3, G2, U2 = _ffn_fwd("ffn2_fwd", x2, vec(ffn2_norm), wgt2, wut2, wd2)

    dx3, d_final_norm, loss_part = _final("final", x3, vec(final_norm), loss_target[0])
    dG2, dU2, A2 = _ffn_bwd_gate("ffn2_bwd_gate", dx3, G2, U2, wd2)
    dx2, d_ffn2_norm = _ffn_bwd_in("ffn2_bwd_in", dG2, dU2, wgt2, wut2, dx3, x2, vec(ffn2_norm))
    dwd2 = _ffn_block_grad("ffn2_dwd", A2, dx3, 0.5)
    s_d2, tok = _xchg_start("a2a_start_ffn2_wd", [dwd2], True)
    dwg2 = _ffn_block_grad("ffn2_dwg", dG2, h3, 1.0, dep=tok)
    s_g2, tok = _xchg_start("a2a_start_ffn2_wg", [dwg2], True)
    dwu2 = _ffn_block_grad("ffn2_dwu", dU2, h3, 1.0, dep=tok)
    s_u2, tok = _xchg_start("a2a_start_ffn2_wu", [dwu2], True)

    dmix = _mm_k("dmix", dx2, wout, D, True, dep=tok)
    d_wout = _mm_tn("dwout", mix, dx2, (MIX, D),
                    pl.BlockSpec((tm, MIX), lambda j, i: (i, 0)), pl.BlockSpec((tm, 512), lambda j, i: (i, j)),
                    pl.BlockSpec((MIX, 512), lambda j, i: (0, j)), (D // 512, S // tm), 1.0, BF16)
    s_out, tok = _xchg_start("a2a_start_w_out", [d_wout.reshape(N_DEV, MIX // N_DEV, D)], True)
    dag, d_cw, d_cb, d_lg, d_lb = _conv_bwd("conv_bwd", dmix, yc, proj, C, cw, vec(conv_ln_g), vec(conv_ln_b))
    dq, dk, dv, sp, rs = _attn_bwd("attn_bwd", proj, ka, o32, lse, dmix, q_blk, k_blk, v_blk, C // LANES, n_pair)
    df, d_fb = _gate_bwd("gate_bwd", sp, rs, proj, f_blk, fbias, n_pair)
    dproj = jnp.concatenate([dag, bf(dq), dk, dv, df, jnp.zeros((S, NP - f_blk * LANES - LANES), BF16)], axis=1)
    d_wint = _mm_tn("dwin", dproj, h2, (NP, D),
                    pl.BlockSpec((tm, 512), lambda j, i: (i, j)), pl.BlockSpec((tm, D), lambda j, i: (i, 0)),
                    pl.BlockSpec((512, D), lambda j, i: (j, 0)), (NP // 512, S // tm), 1.0, BF16, dep=tok)
    s_in, tok = _xchg_start("a2a_start_w_in", [d_wint[:in_cols].reshape(N_DEV, in_shard, D)], True)
    dx1, d_mix_norm = _mm_k("dh2", dproj, wint, _tile(NP, NP // 4, LANES), False,
                            norm_bwd=(x1, vec(mix_norm), dx2), dep=tok)
    dG1, dU1, A1 = _ffn_bwd_gate("ffn1_bwd_gate", dx1, G1, U1, wd1)

    rows = lambda a: a.reshape(-1, C)
    pad_row = lambda a: jnp.pad(a.reshape(1, -1), ((0, 0), (0, C - a.size)))
    pieces = [rows(d_mix_norm), rows(d_ffn2_norm), rows(d_final_norm),
              d_cw[:CONV_WIDTH], d_cb, d_lg, d_lb, pad_row(d_fb[0, :n_heads]), pad_row(loss_part[0, :1])]
    pack = jnp.concatenate(pieces, axis=0)
    n_rows = pack.shape[0]
    pack = jnp.pad(pack, ((0, -n_rows % SUBLANES), (0, 0)))
    (pack_g,) = _exchange("ag_small", [pack], False)
    tot = _reduce_adam("sum_small", pack_g)
    nd = D // C
    g_mix_norm, g_ffn2_norm, g_final_norm = (tot[k * nd:(k + 1) * nd].reshape(D) for k in range(3))
    r0 = 3 * nd
    me = _lin(_mesh_pos())
    cs = C // N_DEV
    g_conv_w = lax.dynamic_slice(tot[r0:r0 + CONV_WIDTH], (0, me * cs), (CONV_WIDTH, cs))
    g_conv_b, g_ln_g, g_ln_b = tot[r0 + CONV_WIDTH], tot[r0 + CONV_WIDTH + 1], tot[r0 + CONV_WIDTH + 2]
    g_fb = tot[r0 + CONV_WIDTH + 3, :n_heads]
    loss = tot[r0 + CONV_WIDTH + 4, 0]

    small = [(g_mix_norm, mix_norm, m_mix_norm, v_mix_norm),
             (g_fb, fgate_bias, m_fgate_bias, v_fgate_bias), (g_conv_w, conv_w, m_conv_w, v_conv_w),
             (g_conv_b, conv_b, m_conv_b, v_conv_b), (g_ln_g, conv_ln_g, m_conv_ln_g, v_conv_ln_g),
             (g_ln_b, conv_ln_b, m_conv_ln_b, v_conv_ln_b), (g_ffn2_norm, ffn2_norm, m_ffn2_norm, v_ffn2_norm),
             (g_final_norm, final_norm, m_final_norm, v_final_norm)]
    sizes = [g.size for g, _, _, _ in small]
    total = sum(sizes)
    padded = -(-total // (SUBLANES * LANES)) * (SUBLANES * LANES)

    def flat_pack(k, fill):
        flat = jnp.concatenate([t[k].reshape(-1) for t in small])
        return jnp.pad(flat, (0, padded - total), constant_values=fill).reshape(padded // LANES, LANES)

    sg, sd, sm, sv = _reduce_adam("adam_small", flat_pack(0, 0.0)[None], flat_pack(1, 0.0), flat_pack(2, 0.0),
                                  flat_pack(3, 1.0))

    def unpack(packed):
        flat = packed.reshape(-1)
        out, off = [], 0
        for (g, _, _, _), n in zip(small, sizes):
            out.append(flat[off:off + n].reshape(g.shape))
            off += n
        return out

    s_g, s_d, s_m, s_v = unpack(sg), unpack(sd), unpack(sm), unpack(sv)

    dwd1 = _ffn_block_grad("ffn1_dwd", A1, dx1, 0.5, dep=sd)
    s_d1, tok = _xchg_start("a2a_start_ffn1_wd", [dwd1], True)
    dwg1 = _ffn_block_grad("ffn1_dwg", dG1, h1, 1.0, dep=tok)
    s_g1, tok = _xchg_start("a2a_start_ffn1_wg", [dwg1], True)
    dwu1 = _ffn_block_grad("ffn1_dwu", dU1, h1, 1.0, dep=tok)
    s_u1, tok = _xchg_start("a2a_start_ffn1_wu", [dwu1], True)
    dx0, d_ffn1_norm = _ffn_bwd_in("ffn1_bwd_in", dG1, dU1, wgt1, wut1, dx1, xs, vec(ffn1_norm) + tok[:1, :1])

    r_d2, r_g2, r_u2, r_out, r_in = _xchg_wait("a2a_wait_a", s_d2 + s_g2 + s_u2 + s_out + s_in, True, dx0)
    tr = lambda a: a.T

    def adam_t(name, recv, w, m, v):
        return tuple(tr(r) for r in _reduce_adam(name, recv, tr(w), tr(m), tr(v)))

    res = {
        "ffn2_w_down": _reduce_adam("adam_ffn2_wd", r_d2, ffn2_w_down, m_ffn2_w_down, v_ffn2_w_down),
        "ffn2_w_gate": adam_t("adam_ffn2_wg", r_g2, ffn2_w_gate, m_ffn2_w_gate, v_ffn2_w_gate),
        "ffn2_w_up": adam_t("adam_ffn2_wu", r_u2, ffn2_w_up, m_ffn2_w_up, v_ffn2_w_up),
        "w_out": _reduce_adam("adam_w_out", r_out, w_out, m_w_out, v_w_out),
        "w_in": adam_t("adam_w_in", r_in, w_in, m_w_in, v_w_in),
    }
    (r_d1,) = _xchg_wait("a2a_wait_d1", s_d1, True, res["w_in"][0])
    res["ffn1_w_down"] = _reduce_adam("adam_ffn1_wd", r_d1, ffn1_w_down, m_ffn1_w_down, v_ffn1_w_down)
    (r_g1,) = _xchg_wait("a2a_wait_g1", s_g1, True, res["ffn1_w_down"][0])
    res["ffn1_w_gate"] = adam_t("adam_ffn1_wg", r_g1, ffn1_w_gate, m_ffn1_w_gate, v_ffn1_w_gate)
    (r_u1,) = _xchg_wait("a2a_wait_u1", s_u1, True, res["ffn1_w_gate"][0])
    res["ffn1_w_up"] = adam_t("adam_ffn1_wu", r_u1, ffn1_w_up, m_ffn1_w_up, v_ffn1_w_up)
    lanes = lambda a: a.reshape(-1, LANES)
    (norm1_g,) = _exchange("ag_norm1", [lanes(d_ffn1_norm)], False, dep=res["ffn1_w_up"][0])
    res["ffn1_norm"] = tuple(r.reshape(D) for r in _reduce_adam(
        "adam_norm1", norm1_g, lanes(ffn1_norm), lanes(m_ffn1_norm), lanes(v_ffn1_norm)))

    small_names = ["mix_norm", "fgate_bias", "conv_w", "conv_b", "conv_ln_g", "conv_ln_b",
                   "ffn2_norm", "final_norm"]
    for k, n in enumerate(small_names):
        res[n] = (s_g[k], s_d[k], s_m[k], s_v[k])
    order = ["ffn1_norm", "ffn1_w_gate", "ffn1_w_up", "ffn1_w_down", "mix_norm", "w_in", "fgate_bias",
             "conv_w", "conv_b", "conv_ln_g", "conv_ln_b", "w_out", "ffn2_norm", "ffn2_w_gate", "ffn2_w_up",
             "ffn2_w_down", "final_norm"]
    outs = [loss, dx0[None]]
    for k in range(4):
        outs += [res[n][k] for n in order]
    return tuple(outs)
```

```python
import math

import jax
import jax.numpy as jnp
from jax import lax
from jax.experimental import pallas as pl
from jax.experimental.pallas import tpu as pltpu

F32 = jnp.float32
BF16 = jnp.bfloat16

N_DEV = 8
MESH_ID = pl.DeviceIdType.MESH
HEAD_DIM = 64
CONV_WIDTH = 31
CONV_HALO = 32
NORM_EPS = 1e-6
LN_EPS = 1e-5
NEG_INF = -1e30
LANES = 128
SUBLANES = 8
V7X_VMEM_LIMIT = 52 * 1024 * 1024

ADAM_LR = 0.001
ADAM_B1 = 0.9
ADAM_B2 = 0.999
ADAM_EPS = 1e-08
ADAM_WD = 0.01
ADAM_STEP = 10


def _params(n_grid_axes):
    return pltpu.CompilerParams(dimension_semantics=("arbitrary",) * n_grid_axes,
                                vmem_limit_bytes=V7X_VMEM_LIMIT)


def _tile(n, pref, mult=8):
    t = min(pref, n)
    while t >= mult:
        if n % t == 0 and t % mult == 0:
            return t
        t -= mult
    return n


def _dot(a, b):
    return jnp.dot(a, b, preferred_element_type=F32)


def _dot_nt(a, b):
    return lax.dot_general(a, b, (((1,), (1,)), ((), ())), preferred_element_type=F32)


def _dot_tn(a, b):
    return lax.dot_general(a, b, (((0,), (0,)), ((), ())), preferred_element_type=F32)


def _sigmoid(x):
    return 1.0 / (1.0 + jnp.exp(-x))


def _rms_fwd(x, g):
    r = lax.rsqrt(jnp.mean(x * x, axis=-1, keepdims=True) + NORM_EPS)
    return x * r * g


def _rms_bwd(dh, x, g):
    r = lax.rsqrt(jnp.mean(x * x, axis=-1, keepdims=True) + NORM_EPS)
    xh = x * r
    dxh = dh * g
    dx = r * (dxh - xh * jnp.mean(dxh * xh, axis=-1, keepdims=True))
    return dx, jnp.sum(dh * xh, axis=0, keepdims=True)


def _split3(x):
    hi = x.astype(BF16)
    r = x - hi.astype(F32)
    mid = r.astype(BF16)
    lo = (r - mid.astype(F32)).astype(BF16)
    return hi, mid, lo


def _blk(ref):
    return ref[0] if len(ref.shape) == 3 else ref[...]


_DEP_SPEC = pl.BlockSpec(memory_space=pl.ANY)


def _mesh_pos():
    return lax.axis_index("x"), lax.axis_index("y"), lax.axis_index("c")


def _peer(pos, k):
    x, y, c = pos
    return (1 - x if k & 4 else x, 1 - y if k & 2 else y, 1 - c if k & 1 else c)


def _lin(pos):
    x, y, c = pos
    return 4 * x + 2 * y + c


def _remote_copy(src, land, send_sems, recv_sems, pos, k, all_to_all):
    peer = _peer(pos, k)
    return pltpu.make_async_remote_copy(
        src_ref=src.at[_lin(peer)] if all_to_all else src, dst_ref=land.at[_lin(pos)],
        send_sem=send_sems.at[k - 1], recv_sem=recv_sems.at[k - 1],
        device_id=peer, device_id_type=MESH_ID)


def _landing_shape(a, all_to_all):
    return a.shape if all_to_all else (N_DEV,) + a.shape


def _exchange(name, arrays, all_to_all, dep=None):
    n = len(arrays)
    deps = [] if dep is None else [dep]
    out_shapes = [jax.ShapeDtypeStruct(_landing_shape(a, all_to_all), a.dtype) for a in arrays]

    def body(*refs):
        ins, outs = refs[:n], refs[n + len(deps):2 * n + len(deps)]
        send_sems, recv_sems, local_sems = refs[2 * n + len(deps):]
        pos = _mesh_pos()
        me = _lin(pos)
        local = []
        for a in range(n):
            src = ins[a].at[me] if all_to_all else ins[a]
            cp = pltpu.make_async_copy(src, outs[a].at[me], local_sems.at[a])
            cp.start()
            local.append(cp)
        remote = [_remote_copy(ins[a], outs[a], send_sems.at[a], recv_sems.at[a], pos, k, all_to_all)
                  for a in range(n) for k in range(1, N_DEV)]
        for cp in remote:
            cp.start()
        for cp in remote:
            cp.wait()
        for cp in local:
            cp.wait()

    any_spec = pl.BlockSpec(memory_space=pl.ANY)
    return pl.pallas_call(
        body, out_shape=out_shapes, in_specs=[any_spec] * (n + len(deps)), out_specs=[any_spec] * n,
        scratch_shapes=[pltpu.SemaphoreType.DMA((n, N_DEV - 1)),
                        pltpu.SemaphoreType.DMA((n, N_DEV - 1)),
                        pltpu.SemaphoreType.DMA((n,))],
        name=name)(*arrays, *deps)


def _all_gather_two_level(name, shards):
    n = len(shards)
    out_shapes = [jax.ShapeDtypeStruct((N_DEV,) + a.shape, a.dtype) for a in shards]

    def body(*refs):
        ins, outs = refs[:n], refs[n:2 * n]
        send_sems, recv_sems, local_sems = refs[2 * n:]
        x, y, c = pos = _mesh_pos()
        sibling = (x, y, 1 - c)
        chips = [(1 - x, y), (x, 1 - y), (1 - x, 1 - y)]

        def copy(a, k, block, to, src=None):
            slot = outs[a].at[_lin(block)]
            return pltpu.make_async_remote_copy(
                src_ref=slot if src is None else src, dst_ref=slot,
                send_sem=send_sems.at[a, k], recv_sem=recv_sems.at[a, k],
                device_id=to, device_id_type=MESH_ID)

        local = [pltpu.make_async_copy(ins[a], outs[a].at[_lin(pos)], local_sems.at[a]) for a in range(n)]
        first = [copy(a, 1 + j, pos, (*chip, c), src=ins[a]) for j, chip in enumerate(chips) for a in range(n)]
        first += [copy(a, 0, pos, sibling, src=ins[a]) for a in range(n)]
        for cp in first + local:
            cp.start()
        passed = []
        for j, chip in enumerate(chips):
            for a in range(n):
                copy(a, 1 + j, (*chip, c), pos).wait_recv()
                cp = copy(a, 4 + j, (*chip, c), sibling)
                cp.start()
                passed.append(cp)
        for a in range(n):
            copy(a, 0, sibling, pos).wait_recv()
        for j, chip in enumerate(chips):
            for a in range(n):
                copy(a, 4 + j, (*chip, 1 - c), pos).wait_recv()
        for cp in first + passed:
            cp.wait_send()
        for cp in local:
            cp.wait()

    any_spec = pl.BlockSpec(memory_space=pl.ANY)
    return pl.pallas_call(
        body, out_shape=out_shapes, in_specs=[any_spec] * n, out_specs=[any_spec] * n,
        scratch_shapes=[pltpu.SemaphoreType.DMA((n, N_DEV - 1)),
                        pltpu.SemaphoreType.DMA((n, N_DEV - 1)),
                        pltpu.SemaphoreType.DMA((n,))],
        name=name)(*shards)


_HBM_SPEC = pl.BlockSpec(memory_space=pltpu.HBM)
_SEM_SPEC = pl.BlockSpec(memory_space=pltpu.SEMAPHORE)
_SIDE_EFFECT = pltpu.SideEffectType.DATAFLOW_SIDE_EFFECTING


def _xchg_start(name, arrays, all_to_all, dep=None):
    n = len(arrays)
    me = _lin(_mesh_pos())
    lands = [lax.dynamic_update_index_in_dim(
        lax.empty(_landing_shape(a, all_to_all), a.dtype),
        lax.dynamic_index_in_dim(a, me, 0, keepdims=False) if all_to_all else a, me, 0) for a in arrays]

    deps = [] if dep is None else [dep]

    def body(*refs):
        srcs, lnds = refs[:n], refs[n:2 * n]
        outs = refs[2 * n + len(deps):]
        send, recv = outs[:n], outs[n:2 * n]
        token = outs[4 * n]
        pos = _mesh_pos()
        for a in range(n):
            for k in range(1, N_DEV):
                _remote_copy(srcs[a], lnds[a], send[a], recv[a], pos, k, all_to_all).start()
        token[...] = jnp.zeros_like(token)

    hbm = lambda a: pltpu.HBM(a.shape, a.dtype)
    sems = [pltpu.SemaphoreType.DMA((N_DEV - 1,))] * (2 * n)
    res = pl.pallas_call(
        body, name=name,
        out_shape=sems + [hbm(a) for a in arrays] + [hbm(l) for l in lands]
        + [jax.ShapeDtypeStruct((SUBLANES, LANES), F32)],
        in_specs=[_HBM_SPEC] * (2 * n) + [_DEP_SPEC] * len(deps),
        out_specs=[_SEM_SPEC] * (2 * n) + [_HBM_SPEC] * (2 * n) + [pl.BlockSpec(memory_space=pltpu.VMEM)],
        input_output_aliases={a: 2 * n + a for a in range(2 * n)},
        compiler_params=pltpu.CompilerParams(has_side_effects=_SIDE_EFFECT),
    )(*[pltpu.with_memory_space_constraint(a, pltpu.HBM) for a in list(arrays) + lands], *deps)
    per_array = [(res[a], res[n + a], res[2 * n + a], res[3 * n + a]) for a in range(n)]
    return per_array, res[4 * n]


def _xchg_wait(name, started, all_to_all, after):
    n = len(started)

    def body(*refs):
        srcs, lnds, send, recv = refs[:n], refs[n:2 * n], refs[2 * n:3 * n], refs[3 * n:4 * n]
        pos = _mesh_pos()
        for a in range(n):
            for k in range(1, N_DEV):
                cp = _remote_copy(srcs[a], lnds[a], send[a], recv[a], pos, k, all_to_all)
                cp.wait_send()
                cp.wait_recv()

    hbm = lambda a: pltpu.HBM(a.shape, a.dtype)
    srcs = [s[2] for s in started]
    lands = [s[3] for s in started]
    res = pl.pallas_call(
        body, name=name,
        out_shape=[hbm(a) for a in srcs + lands],
        in_specs=[_HBM_SPEC] * (2 * n) + [_SEM_SPEC] * (2 * n) + [_DEP_SPEC],
        out_specs=[_HBM_SPEC] * (2 * n),
        input_output_aliases={a: a for a in range(2 * n)},
        compiler_params=pltpu.CompilerParams(has_side_effects=_SIDE_EFFECT),
    )(*srcs, *lands, *[s[0] for s in started], *[s[1] for s in started], after)
    return list(res[n:])


def _reduce_adam(name, parts, w=None, m=None, v=None):
    n, R, C = parts.shape
    tr = _tile(R, 256)
    do_adam = w is not None
    bc1 = 1.0 - ADAM_B1 ** ADAM_STEP
    bc2 = 1.0 - ADAM_B2 ** ADAM_STEP

    def body(*refs):
        p_ref = refs[0]
        g = p_ref[0].astype(F32)
        for d in range(1, n):
            g = g + p_ref[d].astype(F32)
        if not do_adam:
            refs[1][...] = g
            return
        w_ref, m_ref, v_ref, g_ref, d_ref, nm_ref, nv_ref = refs[1:]
        g_ref[...] = g
        nm = ADAM_B1 * m_ref[...] + (1.0 - ADAM_B1) * g
        nv = ADAM_B2 * v_ref[...] + (1.0 - ADAM_B2) * (g * g)
        m_hat = nm / bc1
        v_hat = nv / bc2
        d_ref[...] = -ADAM_LR * (m_hat / (jnp.sqrt(v_hat) + ADAM_EPS) + ADAM_WD * w_ref[...])
        nm_ref[...] = nm
        nv_ref[...] = nv

    tc = _tile(C, 512, LANES) if tr == R and R > 256 else C
    row = pl.BlockSpec((tr, tc), lambda i, j: (i, j))
    part = pl.BlockSpec((n, tr, tc), lambda i, j: (0, i, j))
    shard = jax.ShapeDtypeStruct((R, C), F32)
    grid = (R // tr, C // tc)
    if do_adam:
        return pl.pallas_call(body, grid=grid, in_specs=[part, row, row, row],
                              out_specs=[row] * 4, out_shape=[shard] * 4,
                              compiler_params=_params(2), name=name)(parts, w, m, v)
    return pl.pallas_call(body, grid=grid, in_specs=[part], out_specs=row, out_shape=shard,
                          compiler_params=_params(2), name=name)(parts)


def _ffn_fwd(name, x, gain, wgt, wut, wd):
    S, D = x.shape
    nb, Fs, _ = wgt.shape
    tm = _tile(S, 512)

    def body(x_ref, g_ref, wg_ref, wu_ref, wd_ref, xo_ref, h_ref, G_ref, U_ref, acc_ref):
        j = pl.program_id(1)

        @pl.when(j == 0)
        def _():
            h_ref[...] = _rms_fwd(x_ref[...], g_ref[...]).astype(BF16)
            acc_ref[...] = jnp.zeros_like(acc_ref)

        h = h_ref[...]
        G = _dot_nt(h, wg_ref[0])
        U = _dot_nt(h, wu_ref[0])
        G_ref[0] = G
        U_ref[0] = U
        a = G * _sigmoid(G) * U
        acc_ref[...] += _dot(a.astype(BF16), wd_ref[0])

        @pl.when(j == nb - 1)
        def _():
            xo_ref[...] = x_ref[...] + 0.5 * acc_ref[...]

    row = pl.BlockSpec((tm, D), lambda i, j: (i, 0))
    act = pl.BlockSpec((1, tm, Fs), lambda i, j: (j, i, 0))
    wblk = pl.BlockSpec((1, Fs, D), lambda i, j: (j, 0, 0))
    return pl.pallas_call(
        body, grid=(S // tm, nb),
        in_specs=[row, pl.BlockSpec((1, D), lambda i, j: (0, 0)), wblk, wblk, wblk],
        out_specs=[row, row, act, act],
        out_shape=[jax.ShapeDtypeStruct((S, D), F32), jax.ShapeDtypeStruct((S, D), BF16),
                   jax.ShapeDtypeStruct((nb, S, Fs), F32), jax.ShapeDtypeStruct((nb, S, Fs), F32)],
        scratch_shapes=[pltpu.VMEM((tm, D), F32)],
        compiler_params=_params(2), name=name)(x, gain, wgt, wut, wd)


def _ffn_bwd_gate(name, dxo, G, U, wd):
    S, D = dxo.shape
    nb, Fs, _ = wd.shape
    tm = _tile(S, 512)

    def gate_body(dxo_ref, G_ref, U_ref, wd_ref, dG_ref, dU_ref, A_ref, dxb_ref):
        @pl.when(pl.program_id(1) == 0)
        def _():
            dxb_ref[...] = dxo_ref[...].astype(BF16)

        dA = 0.5 * _dot_nt(dxb_ref[...], wd_ref[0])
        Gv = G_ref[0]
        Uv = U_ref[0]
        sg = _sigmoid(Gv)
        sl = Gv * sg
        dG_ref[0] = (dA * Uv * (sg * (1.0 + Gv * (1.0 - sg)))).astype(BF16)
        dU_ref[0] = (dA * sl).astype(BF16)
        A_ref[0] = (sl * Uv).astype(BF16)

    row = pl.BlockSpec((tm, D), lambda i, j: (i, 0))
    act = pl.BlockSpec((1, tm, Fs), lambda i, j: (j, i, 0))
    wblk = pl.BlockSpec((1, Fs, D), lambda i, j: (j, 0, 0))
    act_shape = jax.ShapeDtypeStruct((nb, S, Fs), BF16)
    return pl.pallas_call(
        gate_body, grid=(S // tm, nb), in_specs=[row, act, act, wblk], out_specs=[act, act, act],
        out_shape=[act_shape, act_shape, act_shape],
        scratch_shapes=[pltpu.VMEM((tm, D), BF16)],
        compiler_params=_params(2), name=name)(dxo, G, U, wd)


def _ffn_bwd_in(name, dG, dU, wgt, wut, dxo, x_in, gain):
    S, D = x_in.shape
    nb, Fs, _ = wgt.shape
    tm = _tile(S, 512)
    rows_per_chunk = _tile(tm, 128)

    def in_body(dG_ref, dU_ref, wg_ref, wu_ref, dxo_ref, x_ref, g_ref, dx_ref, dgain_ref, acc_ref):
        i, j = pl.program_id(0), pl.program_id(1)

        @pl.when(j == 0)
        def _():
            acc_ref[...] = jnp.zeros_like(acc_ref)

        @pl.when((i == 0) & (j == 0))
        def _():
            dgain_ref[...] = jnp.zeros_like(dgain_ref)

        acc_ref[...] += _dot(dG_ref[0], wg_ref[0]) + _dot(dU_ref[0], wu_ref[0])

        @pl.when(j == nb - 1)
        def _():
            def chunk(r, dg_sum):
                rows = pl.ds(pl.multiple_of(r * rows_per_chunk, rows_per_chunk), rows_per_chunk)
                dx, dg = _rms_bwd(acc_ref[rows, :], x_ref[rows, :], g_ref[...])
                dx_ref[rows, :] = dxo_ref[rows, :] + dx
                return dg_sum + dg

            dgain_ref[...] += lax.fori_loop(0, tm // rows_per_chunk, chunk, jnp.zeros((1, D), F32))

    row = pl.BlockSpec((tm, D), lambda i, j: (i, 0))
    vec = pl.BlockSpec((1, D), lambda i, j: (0, 0))
    act = pl.BlockSpec((1, tm, Fs), lambda i, j: (j, i, 0))
    wblk = pl.BlockSpec((1, Fs, D), lambda i, j: (j, 0, 0))
    return pl.pallas_call(
        in_body, grid=(S // tm, nb), in_specs=[act, act, wblk, wblk, row, row, vec],
        out_specs=[row, vec],
        out_shape=[jax.ShapeDtypeStruct((S, D), F32), jax.ShapeDtypeStruct((1, D), F32)],
        scratch_shapes=[pltpu.VMEM((tm, D), F32)],
        compiler_params=_params(2), name=name)(dG, dU, wgt, wut, dxo, x_in, gain)


def _mm_tn(name, lhs, rhs, out_shape, lhs_spec, rhs_spec, out_spec, grid, scale, out_dtype, dep=None):
    acc_shape = tuple(out_spec.block_shape[-2:])
    n_red = grid[-1]

    def body(l_ref, r_ref, *rest):
        o_ref, acc_ref = rest[-2:]
        i = pl.program_id(len(grid) - 1)

        @pl.when(i == 0)
        def _():
            acc_ref[...] = jnp.zeros_like(acc_ref)

        acc_ref[...] += _dot_tn(_blk(l_ref).astype(BF16), _blk(r_ref).astype(BF16))

        @pl.when(i == n_red - 1)
        def _():
            res = (scale * acc_ref[...]).astype(out_dtype)
            if len(o_ref.shape) == 3:
                o_ref[0] = res
            else:
                o_ref[...] = res

    deps = [] if dep is None else [dep]
    return pl.pallas_call(
        body, grid=grid, in_specs=[lhs_spec, rhs_spec] + [_DEP_SPEC] * len(deps), out_specs=out_spec,
        out_shape=jax.ShapeDtypeStruct(out_shape, out_dtype),
        scratch_shapes=[pltpu.VMEM(acc_shape, F32)],
        compiler_params=_params(len(grid)), name=name)(lhs, rhs, *deps)


def _norm_mm_nt(name, x, gain, wt, tn):
    S, D = x.shape
    N = wt.shape[0]
    tm = _tile(S, 1024)

    def body(x_ref, g_ref, w_ref, o_ref, h_ref):
        @pl.when(pl.program_id(1) == 0)
        def _():
            h_ref[...] = _rms_fwd(x_ref[...], g_ref[...]).astype(BF16)

        o_ref[...] = _dot_nt(h_ref[...], w_ref[...])

    row = pl.BlockSpec((tm, D), lambda i, j: (i, 0))
    return pl.pallas_call(
        body, grid=(S // tm, N // tn),
        in_specs=[row, pl.BlockSpec((1, D), lambda i, j: (0, 0)),
                  pl.BlockSpec((tn, D), lambda i, j: (j, 0))],
        out_specs=[pl.BlockSpec((tm, tn), lambda i, j: (i, j)), row],
        out_shape=[jax.ShapeDtypeStruct((S, N), F32), jax.ShapeDtypeStruct((S, D), BF16)],
        compiler_params=_params(2), name=name)(x, gain, wt)


def _mm_res(name, a, w, res, tn):
    S, K = a.shape
    N = w.shape[1]
    tm = _tile(S, 512)

    def body(a_ref, w_ref, r_ref, o_ref):
        o_ref[...] = r_ref[...] + _dot(a_ref[...], w_ref[...])

    tile = pl.BlockSpec((tm, tn), lambda i, j: (i, j))
    return pl.pallas_call(
        body, grid=(S // tm, N // tn),
        in_specs=[pl.BlockSpec((tm, K), lambda i, j: (i, 0)),
                  pl.BlockSpec((K, tn), lambda i, j: (0, j)), tile],
        out_specs=tile, out_shape=jax.ShapeDtypeStruct((S, N), F32),
        compiler_params=_params(2), name=name)(a, w, res)


def _mm_k(name, a, b, tk, transpose_b, norm_bwd=None, dep=None):
    S, K = a.shape
    N = b.shape[0] if transpose_b else b.shape[1]
    tm = _tile(S, 512)
    rows_per_chunk = _tile(tm, 128)
    nk = K // tk
    n_extra = 0 if norm_bwd is None else 3
    deps = [] if dep is None else [dep]

    def body(*refs):
        a_ref, b_ref = refs[:2]
        outs = refs[2 + n_extra + len(deps):]
        acc_ref = outs[-1]
        i, k = pl.program_id(0), pl.program_id(1)

        @pl.when(k == 0)
        def _():
            acc_ref[...] = jnp.zeros_like(acc_ref)

        av = a_ref[...].astype(BF16)
        acc_ref[...] += _dot_nt(av, b_ref[...]) if transpose_b else _dot(av, b_ref[...])

        if norm_bwd is None:
            @pl.when(k == nk - 1)
            def _():
                outs[0][...] = acc_ref[...]
        else:
            x_ref, g_ref, dres_ref = refs[2:5]
            o_ref, dgain_ref = outs[:2]

            @pl.when((i == 0) & (k == 0))
            def _():
                dgain_ref[...] = jnp.zeros_like(dgain_ref)

            @pl.when(k == nk - 1)
            def _():
                def chunk(r, dg_sum):
                    rows = pl.ds(pl.multiple_of(r * rows_per_chunk, rows_per_chunk), rows_per_chunk)
                    dx, dg = _rms_bwd(acc_ref[rows, :], x_ref[rows, :], g_ref[...])
                    o_ref[rows, :] = dres_ref[rows, :] + dx
                    return dg_sum + dg

                dgain_ref[...] += lax.fori_loop(0, tm // rows_per_chunk, chunk, jnp.zeros((1, N), F32))

    a_spec = pl.BlockSpec((tm, tk), lambda i, k: (i, k))
    b_spec = (pl.BlockSpec((N, tk), lambda i, k: (0, k)) if transpose_b
              else pl.BlockSpec((tk, N), lambda i, k: (k, 0)))
    row = pl.BlockSpec((tm, N), lambda i, k: (i, 0))
    vec = pl.BlockSpec((1, N), lambda i, k: (0, 0))
    out = jax.ShapeDtypeStruct((S, N), F32)
    scratch = [pltpu.VMEM((tm, N), F32)]
    dep_specs = [_DEP_SPEC] * len(deps)
    if norm_bwd is None:
        return pl.pallas_call(body, grid=(S // tm, nk), in_specs=[a_spec, b_spec] + dep_specs,
                              out_specs=row, out_shape=out, scratch_shapes=scratch,
                              compiler_params=_params(2), name=name)(a, b, *deps)
    x_in, gain, dres = norm_bwd
    return pl.pallas_call(body, grid=(S // tm, nk), in_specs=[a_spec, b_spec, row, vec, row] + dep_specs,
                          out_specs=[row, vec],
                          out_shape=[out, jax.ShapeDtypeStruct((1, N), F32)],
                          scratch_shapes=scratch,
                          compiler_params=_params(2), name=name)(a, b, x_in, gain, dres, *deps)


def _final(name, x, gain, target):
    S, D = x.shape
    tm = _tile(S, 512)

    def body(x_ref, g_ref, t_ref, dx_ref, dgain_ref, loss_ref):
        @pl.when(pl.program_id(0) == 0)
        def _():
            dgain_ref[...] = jnp.zeros_like(dgain_ref)
            loss_ref[...] = jnp.zeros_like(loss_ref)

        xv = x_ref[...]
        err = _rms_fwd(xv, g_ref[...]) - t_ref[...]
        per_tok = jnp.mean(err * err, axis=-1, keepdims=True)
        loss_ref[...] += 0.5 * jnp.sum(per_tok, axis=0, keepdims=True)
        dx, dg = _rms_bwd(err * (1.0 / D), xv, g_ref[...])
        dx_ref[...] = dx
        dgain_ref[...] += dg

    row = pl.BlockSpec((tm, D), lambda i: (i, 0))
    vec = pl.BlockSpec((1, D), lambda i: (0, 0))
    return pl.pallas_call(
        body, grid=(S // tm,), in_specs=[row, vec, row],
        out_specs=[row, vec, pl.BlockSpec((1, LANES), lambda i: (0, 0))],
        out_shape=[jax.ShapeDtypeStruct((S, D), F32), jax.ShapeDtypeStruct((1, D), F32),
                   jax.ShapeDtypeStruct((1, LANES), F32)],
        compiler_params=_params(1), name=name)(x, gain, target)


def _conv_tiles(S):
    ts = _tile(S, 256, CONV_HALO)
    return ts, ts // CONV_HALO


def _ln_stats(yc):
    mu = jnp.mean(yc, axis=-1, keepdims=True)
    d = yc - mu
    rs = lax.rsqrt(jnp.mean(d * d, axis=-1, keepdims=True) + LN_EPS)
    return d * rs, rs


N_PHASE = SUBLANES
PHASE_ROWS = CONV_HALO - SUBLANES


def _shifted_copies(buf, shifted, ts):
    for p in range(1, N_PHASE):
        shifted[p - 1] = buf[pl.ds(p, ts + PHASE_ROWS), :]


def _tap(buf, shifted, offset, rows0, n_rows, cols):
    a, p = divmod(offset, N_PHASE)
    rows = pl.ds(a * N_PHASE + rows0, n_rows)
    return buf[rows, cols] if p == 0 else shifted[p - 1, rows, cols]


def _conv_fwd(name, proj, C, cw, cb, lg, lb):
    S = proj.shape[0]
    ts, hb = _conv_tiles(S)
    rb = _tile(ts, 128)
    first = CONV_HALO - CONV_WIDTH + 1

    def body(a_ref, g_ref, ah_ref, gh_ref, cw_ref, cb_ref, lg_ref, lb_ref, yc_ref, y_ref, ubuf, ushift):
        i = pl.program_id(0)
        uh = ah_ref[...] * _sigmoid(gh_ref[...])
        ubuf[pl.ds(0, CONV_HALO), :] = jnp.where(i > 0, uh, 0.0)
        ubuf[pl.ds(CONV_HALO, ts), :] = a_ref[...] * _sigmoid(g_ref[...])
        _shifted_copies(ubuf, ushift, ts)

        def col_block(c, carry):
            cols = pl.ds(pl.multiple_of(c * LANES, LANES), LANES)
            for r in range(ts // rb):
                acc = jnp.zeros((rb, LANES), F32)
                for k in range(CONV_WIDTH):
                    acc = acc + cw_ref[pl.ds(k, 1), cols] * _tap(ubuf, ushift, first + k, r * rb, rb, cols)
                yc_ref[pl.ds(r * rb, rb), cols] = acc + cb_ref[:, cols]
            return carry

        lax.fori_loop(0, C // LANES, col_block, 0)
        yn, _ = _ln_stats(yc_ref[...])
        z = yn * lg_ref[...] + lb_ref[...]
        y_ref[...] = (z * _sigmoid(z)).astype(BF16)

    main = lambda col: pl.BlockSpec((ts, C), lambda i: (i, col))
    halo = lambda col: pl.BlockSpec((CONV_HALO, C), lambda i: (jnp.maximum(i * hb - 1, 0), col))
    vec = pl.BlockSpec((1, C), lambda i: (0, 0))
    return pl.pallas_call(
        body, grid=(S // ts,),
        in_specs=[main(0), main(1), halo(0), halo(1),
                  pl.BlockSpec((CONV_HALO, C), lambda i: (0, 0)), vec, vec, vec],
        out_specs=[pl.BlockSpec((ts, C), lambda i: (i, 0))] * 2,
        out_shape=[jax.ShapeDtypeStruct((S, C), F32), jax.ShapeDtypeStruct((S, C), BF16)],
        scratch_shapes=[pltpu.VMEM((ts + CONV_HALO, C), F32),
                        pltpu.VMEM((N_PHASE - 1, ts + PHASE_ROWS, C), F32)],
        compiler_params=_params(1), name=name)(proj, proj, proj, proj, cw, cb, lg, lb)


def _conv_bwd(name, dmix, yc, proj, C, cw, lg, lb):
    S = proj.shape[0]
    ts, hb = _conv_tiles(S)
    n_t = S // ts
    rb = _tile(ts, 128)
    first = CONV_HALO - CONV_WIDTH + 1

    def body(dy_ref, yc_ref, dyh_ref, ych_ref, a_ref, g_ref, ah_ref, gh_ref, cw_ref, lg_ref, lb_ref,
             dag_ref, dcw_ref, dcb_ref, dlg_ref, dlb_ref, ubuf, dbuf, ushift, dshift, du_ref):
        i = pl.program_id(0)

        @pl.when(i == 0)
        def _():
            dcw_ref[...] = jnp.zeros_like(dcw_ref)
            dcb_ref[...] = jnp.zeros_like(dcb_ref)
            dlg_ref[...] = jnp.zeros_like(dlg_ref)
            dlb_ref[...] = jnp.zeros_like(dlb_ref)

        def ln_bwd(dy, ycv):
            yn, rs = _ln_stats(ycv)
            z = yn * lg_ref[...] + lb_ref[...]
            sg = _sigmoid(z)
            dz = dy * (sg * (1.0 + z * (1.0 - sg)))
            dyn = dz * lg_ref[...]
            dyc = rs * (dyn - jnp.mean(dyn, axis=-1, keepdims=True)
                        - yn * jnp.mean(dyn * yn, axis=-1, keepdims=True))
            return dyc, dz, yn

        dyc, dz, yn = ln_bwd(dy_ref[...], yc_ref[...])
        dlg_ref[...] += jnp.sum(dz * yn, axis=0, keepdims=True)
        dlb_ref[...] += jnp.sum(dz, axis=0, keepdims=True)
        dcb_ref[...] += jnp.sum(dyc, axis=0, keepdims=True)
        dych, _, _ = ln_bwd(dyh_ref[...], ych_ref[...])
        dbuf[pl.ds(0, ts), :] = dyc
        dbuf[pl.ds(ts, CONV_HALO), :] = jnp.where(i < n_t - 1, dych, 0.0)

        uh = ah_ref[...] * _sigmoid(gh_ref[...])
        ubuf[pl.ds(0, CONV_HALO), :] = jnp.where(i > 0, uh, 0.0)
        ubuf[pl.ds(CONV_HALO, ts), :] = a_ref[...] * _sigmoid(g_ref[...])
        _shifted_copies(ubuf, ushift, ts)
        _shifted_copies(dbuf, dshift, ts)

        def col_block(c, carry):
            cols = pl.ds(pl.multiple_of(c * LANES, LANES), LANES)
            dcw = [jnp.zeros((1, LANES), F32)] * CONV_WIDTH
            for r in range(ts // rb):
                du = jnp.zeros((rb, LANES), F32)
                dyc_blk = dbuf[pl.ds(r * rb, rb), cols]
                for k in range(CONV_WIDTH):
                    du = du + cw_ref[pl.ds(k, 1), cols] * _tap(dbuf, dshift, CONV_WIDTH - 1 - k, r * rb, rb, cols)
                    tap = _tap(ubuf, ushift, first + k, r * rb, rb, cols)
                    dcw[k] = dcw[k] + jnp.sum(dyc_blk * tap, axis=0, keepdims=True)
                du_ref[pl.ds(r * rb, rb), cols] = du
            for k in range(CONV_WIDTH):
                dcw_ref[pl.ds(k, 1), cols] += dcw[k]
            return carry

        lax.fori_loop(0, C // LANES, col_block, 0)
        du = du_ref[...]
        av = a_ref[...]
        sgm = _sigmoid(g_ref[...])
        dag_ref[:, pl.ds(0, C)] = (du * sgm).astype(BF16)
        dag_ref[:, pl.ds(C, C)] = (du * av * sgm * (1.0 - sgm)).astype(BF16)

    main = lambda col: pl.BlockSpec((ts, C), lambda i: (i, col))
    past = lambda col: pl.BlockSpec((CONV_HALO, C), lambda i: (jnp.maximum(i * hb - 1, 0), col))
    nxt = pl.BlockSpec((CONV_HALO, C), lambda i: (jnp.minimum((i + 1) * hb, n_t * hb - 1), 0))
    vec = pl.BlockSpec((1, C), lambda i: (0, 0))
    full = pl.BlockSpec((CONV_HALO, C), lambda i: (0, 0))
    vshape = jax.ShapeDtypeStruct((1, C), F32)
    buf = pltpu.VMEM((ts + CONV_HALO, C), F32)
    shifted = pltpu.VMEM((N_PHASE - 1, ts + PHASE_ROWS, C), F32)
    return pl.pallas_call(
        body, grid=(n_t,),
        in_specs=[main(0), main(0), nxt, nxt, main(0), main(1), past(0), past(1), full, vec, vec],
        out_specs=[pl.BlockSpec((ts, 2 * C), lambda i: (i, 0)), full, vec, vec, vec],
        out_shape=[jax.ShapeDtypeStruct((S, 2 * C), BF16),
                   jax.ShapeDtypeStruct((CONV_HALO, C), F32), vshape, vshape, vshape],
        scratch_shapes=[buf, buf, shifted, shifted, pltpu.VMEM((ts, C), F32)],
        compiler_params=_params(1),
        name=name)(dmix, yc, dmix, yc, proj, proj, proj, proj, cw, lg, lb)


AUG = 3
ROW_CHUNK = 32
REDUCE_ROWS = 1024


def _gate_prep(name, proj, f_blk, fbias, n_pair):
    S = proj.shape[0]
    ts = _tile(S, 512)
    W = LANES * n_pair

    def body(pf_ref, fb_ref, ka_ref, carry):
        @pl.when(pl.program_id(0) == 0)
        def _():
            carry[...] = jnp.zeros_like(carry)

        f = pf_ref[...] + fb_ref[...]
        logf = jnp.minimum(f, 0.0) - jnp.log(1.0 + jnp.exp(-jnp.abs(f)))
        r = lax.broadcasted_iota(jnp.int32, (ts, ts), 0)
        c = lax.broadcasted_iota(jnp.int32, (ts, ts), 1)
        ltri = (c <= r).astype(BF16)
        hi, mid, lo = _split3(logf)
        cs = _dot(ltri, hi) + _dot(ltri, mid) + _dot(ltri, lo) + carry[...]
        carry[...] = cs[ts - 1:ts, :]
        hh = lax.broadcasted_iota(jnp.int32, (LANES, W), 0)
        ll = lax.broadcasted_iota(jnp.int32, (LANES, W), 1)
        pair, w = ll >> 7, ll & (LANES - 1)
        ka = jnp.zeros((ts, W), F32)
        for p, piece in enumerate(_split3(-cs)):
            e = (((w == HEAD_DIM + p) & (hh == 2 * pair)) | ((w == p) & (hh == 2 * pair + 1)))
            ka = ka + _dot(piece, e.astype(BF16))
        lw = lax.broadcasted_iota(jnp.int32, (1, W), 1) & (LANES - 1)
        ka = ka + ((lw == HEAD_DIM + AUG) | (lw == AUG)).astype(F32)
        ka_ref[...] = ka.astype(BF16)

    return pl.pallas_call(
        body, grid=(S // ts,),
        in_specs=[pl.BlockSpec((ts, LANES), lambda i: (i, f_blk)),
                  pl.BlockSpec((1, LANES), lambda i: (0, 0))],
        out_specs=pl.BlockSpec((ts, W), lambda i: (i, 0)),
        out_shape=jax.ShapeDtypeStruct((S, W), BF16),
        scratch_shapes=[pltpu.VMEM((1, LANES), F32)],
        compiler_params=_params(1), name=name)(proj, fbias)


def _gate_bwd(name, sp, rs, proj, f_blk, fbias, n_pair):
    S = proj.shape[0]
    ts = _tile(S, 512)
    n_t = S // ts
    W = LANES * n_pair

    def body(sp_ref, rs_ref, pf_ref, fb_ref, df_ref, dfb_ref, carry):
        @pl.when(pl.program_id(0) == 0)
        def _():
            carry[...] = jnp.zeros_like(carry)
            dfb_ref[...] = jnp.zeros_like(dfb_ref)

        ll = lax.broadcasted_iota(jnp.int32, (W, LANES), 0)
        hh = lax.broadcasted_iota(jnp.int32, (W, LANES), 1)
        pair, w = ll >> 7, ll & (LANES - 1)
        first, second = hh == 2 * pair, hh == 2 * pair + 1

        def pick(ref, lane_first, lane_second):
            sel = (((w == lane_first) & first) | ((w == lane_second) & second)).astype(BF16)
            hi, mid, lo = _split3(ref[...])
            return _dot(hi, sel) + _dot(mid, sel) + _dot(lo, sel)

        dc = pick(rs_ref, HEAD_DIM + AUG, AUG) - pick(sp_ref, HEAD_DIM, 0)
        r = lax.broadcasted_iota(jnp.int32, (ts, ts), 0)
        c = lax.broadcasted_iota(jnp.int32, (ts, ts), 1)
        utri = (c >= r).astype(BF16)
        hi, mid, lo = _split3(dc)
        dlogf = _dot(utri, hi) + _dot(utri, mid) + _dot(utri, lo) + carry[...]
        carry[...] = dlogf[0:1, :]
        f = pf_ref[...] + fb_ref[...]
        lane = lax.broadcasted_iota(jnp.int32, (ts, LANES), 1)
        df = jnp.where(lane < 2 * n_pair, dlogf * _sigmoid(-f), 0.0)
        df_ref[...] = df.astype(BF16)
        dfb_ref[...] += jnp.sum(df, axis=0, keepdims=True)

    rev = lambda blk: (lambda i: (n_t - 1 - i, blk))
    return pl.pallas_call(
        body, grid=(n_t,),
        in_specs=[pl.BlockSpec((ts, W), rev(0)), pl.BlockSpec((ts, W), rev(0)),
                  pl.BlockSpec((ts, LANES), rev(f_blk)), pl.BlockSpec((1, LANES), lambda i: (0, 0))],
        out_specs=[pl.BlockSpec((ts, LANES), rev(0)), pl.BlockSpec((1, LANES), lambda i: (0, 0))],
        out_shape=[jax.ShapeDtypeStruct((S, LANES), BF16), jax.ShapeDtypeStruct((1, LANES), F32)],
        scratch_shapes=[pltpu.VMEM((1, LANES), F32)],
        compiler_params=_params(1), name=name)(sp, rs, proj, fbias)


def _spare_lane(h):
    return HEAD_DIM if h == 0 else 0


def _head_operands(h, lane, q2, k2, ka2):
    act = (lane < HEAD_DIM) if h == 0 else (lane >= HEAD_DIM)
    base = HEAD_DIM if h == 0 else 0
    ones = ((lane >= base) & (lane < base + AUG)).astype(F32)
    qa = jnp.where(act, q2 * (1.0 / math.sqrt(HEAD_DIM)), ones).astype(BF16)
    ka = jnp.where(act, k2.astype(BF16), ka2)
    return act, qa, ka


def _attn_fwd(name, proj, ka, q_blk, k_blk, v_blk, n_pair):
    S = proj.shape[0]
    tq = _tile(S, 512)
    n_t = S // tq
    W = LANES * n_pair

    rc = _tile(tq, ROW_CHUNK)

    def body(q_ref, k_ref, v_ref, ka_ref, o_ref, o32_ref, lse_ref, m_ref, acc_ref, s_ref, p_ref):
        i, j = pl.program_id(1), pl.program_id(2)

        @pl.when(j == 0)
        def _():
            m_ref[...] = jnp.full_like(m_ref, NEG_INF)
            acc_ref[...] = jnp.zeros_like(acc_ref)

        def step(diagonal):
            lane = lax.broadcasted_iota(jnp.int32, (tq, LANES), 1)
            q2, k2, v2, ka2 = q_ref[...], k_ref[...], v_ref[...], ka_ref[...]
            for h in range(2):
                act, qa, kaug = _head_operands(h, lane, q2, k2, ka2)
                s = _dot_nt(qa, kaug)
                if diagonal:
                    row = lax.broadcasted_iota(jnp.int32, (tq, tq), 0)
                    col = lax.broadcasted_iota(jnp.int32, (tq, tq), 1)
                    s = jnp.where(row >= col, s, NEG_INF)
                s_ref[h] = s
                m_prev = m_ref[h]
                m_new = jnp.maximum(m_prev, jnp.max(s, axis=-1, keepdims=True))
                m_ref[h] = m_new
                for r in range(tq // rc):
                    rows = pl.ds(r * rc, rc)
                    p_ref[h, rows, :] = jnp.exp(s_ref[h, rows, :] - m_ref[h, rows, :]).astype(BF16)
                vm = jnp.where(act, v2, (lane == _spare_lane(h)).astype(F32)).astype(BF16)
                acc_ref[h] = jnp.exp(m_prev - m_new) * acc_ref[h] + _dot(p_ref[h], vm)

        @pl.when(j < i)
        def _():
            step(False)

        @pl.when(j == i)
        def _():
            step(True)
            lane = lax.broadcasted_iota(jnp.int32, (tq, LANES), 1)
            first = lane < HEAD_DIM
            acc = [acc_ref[0], acc_ref[1]]
            den = [acc[h][:, _spare_lane(h):_spare_lane(h) + 1] for h in range(2)]
            out = jnp.where(first, acc[0] / den[0], acc[1] / den[1])
            o_ref[...] = out.astype(BF16)
            o32_ref[...] = out
            lse_ref[...] = jnp.where(first, m_ref[0] + jnp.log(den[0]), m_ref[1] + jnp.log(den[1]))

    qspec = lambda blk: pl.BlockSpec((tq, LANES), lambda p, i, j: (i, blk + p))
    kspec = lambda blk: pl.BlockSpec((tq, LANES), lambda p, i, j: (jnp.minimum(j, i), blk + p))
    out = pl.BlockSpec((tq, LANES), lambda p, i, j: (i, p))
    return pl.pallas_call(
        body, grid=(n_pair, n_t, n_t),
        in_specs=[qspec(q_blk), kspec(k_blk), kspec(v_blk), kspec(0)],
        out_specs=[out, out, out],
        out_shape=[jax.ShapeDtypeStruct((S, W), BF16), jax.ShapeDtypeStruct((S, W), F32),
                   jax.ShapeDtypeStruct((S, W), F32)],
        scratch_shapes=[pltpu.VMEM((2, tq, 1), F32), pltpu.VMEM((2, tq, LANES), F32),
                        pltpu.VMEM((2, tq, tq), F32), pltpu.VMEM((2, tq, tq), BF16)],
        compiler_params=_params(3), name=name)(proj, proj, proj, ka)


def _attn_bwd(name, proj, ka, o, lse, dmix, q_blk, k_blk, v_blk, do_blk, n_pair):
    S = proj.shape[0]
    tq = _tile(S, 512)
    n_t = S // tq
    W = LANES * n_pair
    scale = 1.0 / math.sqrt(HEAD_DIM)
    rc = _tile(tq, ROW_CHUNK)

    def body(q_ref, k_ref, v_ref, ka_ref, o_ref, lse_ref, do_ref,
             dq_ref, dk_ref, dv_ref, sp_ref, rs_ref, dk_acc, dv_acc, s_ref, dp_ref, p_ref, ds_ref, d_ref):
        j, i = pl.program_id(1), pl.program_id(2)

        @pl.when((j == 0) & (i == 0))
        def _():
            dq_ref[...] = jnp.zeros_like(dq_ref)
            rs_ref[...] = jnp.zeros_like(rs_ref)

        @pl.when(i == 0)
        def _():
            dk_acc[...] = jnp.zeros_like(dk_acc)
            dv_acc[...] = jnp.zeros_like(dv_acc)

        def step(diagonal):
            lane = lax.broadcasted_iota(jnp.int32, (tq, LANES), 1)
            q2, k2, v2, ka2 = q_ref[...], k_ref[...], v_ref[...], ka_ref[...]
            o2, do2 = o_ref[...], do_ref[...]
            dq, heads = [], []
            for h in range(2):
                act, qa, kaug = _head_operands(h, lane, q2, k2, ka2)
                dom = jnp.where(act, do2, 0.0)
                d_ref[h] = jnp.sum(dom * o2, axis=-1, keepdims=True)
                dob = dom.astype(BF16)
                s_ref[h] = _dot_nt(qa, kaug)
                dp_ref[h] = _dot_nt(dob, jnp.where(act, v2, 0.0).astype(BF16))
                heads.append((qa, kaug, dob))
            for h in range(2):
                qa, kaug, dob = heads[h]
                for r in range(tq // rc):
                    rows = pl.ds(r * rc, rc)
                    sc = s_ref[h, rows, :]
                    if diagonal:
                        row = lax.broadcasted_iota(jnp.int32, (rc, tq), 0) + r * rc
                        col = lax.broadcasted_iota(jnp.int32, (rc, tq), 1)
                        sc = jnp.where(row >= col, sc, NEG_INF)
                    p = jnp.exp(sc - lse_ref[rows, :][:, h * HEAD_DIM:h * HEAD_DIM + 1])
                    p_ref[h, rows, :] = p.astype(BF16)
                    ds_ref[h, rows, :] = (p * (dp_ref[h, rows, :] - d_ref[h, rows, :])).astype(BF16)
                dv_acc[...] += _dot_tn(p_ref[h], dob)
                dk_acc[h] += _dot_tn(ds_ref[h], qa)
                dq.append(_dot(ds_ref[h], kaug))
            rows = pl.ds(pl.multiple_of(i * tq, tq), tq)
            first = lane < HEAD_DIM
            dq_ref[rows, :] += jnp.where(first, dq[0], dq[1])
            rs_ref[rows, :] += jnp.where(first, dq[1], dq[0])

        @pl.when(i > j)
        def _():
            step(False)

        @pl.when(i == j)
        def _():
            step(True)

        @pl.when(i == n_t - 1)
        def _():
            lane = lax.broadcasted_iota(jnp.int32, (tq, LANES), 1)
            first = lane < HEAD_DIM
            dk_ref[...] = jnp.where(first, dk_acc[0], dk_acc[1]).astype(BF16)
            sp_ref[...] = jnp.where(first, dk_acc[1], dk_acc[0])
            dv_ref[...] = dv_acc[...].astype(BF16)

        @pl.when((j == n_t - 1) & (i == n_t - 1))
        def _():
            dq_ref[...] = dq_ref[...] * scale

    qspec = lambda blk: pl.BlockSpec((tq, LANES), lambda p, j, i: (jnp.maximum(i, j), blk + p))
    kspec = lambda blk: pl.BlockSpec((tq, LANES), lambda p, j, i: (j, blk + p))
    kout = pl.BlockSpec((tq, LANES), lambda p, j, i: (j, p))
    qres = pl.BlockSpec((S, LANES), lambda p, j, i: (0, p))
    return pl.pallas_call(
        body, grid=(n_pair, n_t, n_t),
        in_specs=[qspec(q_blk), kspec(k_blk), kspec(v_blk), kspec(0),
                  qspec(0), qspec(0), qspec(do_blk)],
        out_specs=[qres, kout, kout, kout, qres],
        out_shape=[jax.ShapeDtypeStruct((S, W), F32), jax.ShapeDtypeStruct((S, W), BF16),
                   jax.ShapeDtypeStruct((S, W), BF16), jax.ShapeDtypeStruct((S, W), F32),
                   jax.ShapeDtypeStruct((S, W), F32)],
        scratch_shapes=[pltpu.VMEM((2, tq, LANES), F32), pltpu.VMEM((tq, LANES), F32),
                        pltpu.VMEM((2, tq, tq), F32), pltpu.VMEM((2, tq, tq), F32),
                        pltpu.VMEM((2, tq, tq), BF16), pltpu.VMEM((2, tq, tq), BF16),
                        pltpu.VMEM((2, tq, 1), F32)],
        compiler_params=_params(3), name=name)(proj, proj, proj, ka, o, lse, dmix)


def _ffn_block_grad(name, act, rows, scale, dep=None):
    nb, S, Fs = act.shape
    D = rows.shape[1]
    tm = _tile(S, REDUCE_ROWS)
    return _mm_tn(name, act, rows, (nb, Fs, D),
                  pl.BlockSpec((1, tm, Fs), lambda j, i: (j, i, 0)),
                  pl.BlockSpec((tm, D), lambda j, i: (i, 0)),
                  pl.BlockSpec((1, Fs, D), lambda j, i: (j, 0, 0)), (nb, S // tm), scale, BF16, dep=dep)


def kernel(x, ffn1_norm, ffn1_w_gate, ffn1_w_up, ffn1_w_down, mix_norm, w_in, fgate_bias, conv_w, conv_b, conv_ln_g, conv_ln_b, w_out, ffn2_norm, ffn2_w_gate, ffn2_w_up, ffn2_w_down, final_norm, loss_target, m_ffn1_norm, m_ffn1_w_gate, m_ffn1_w_up, m_ffn1_w_down, m_mix_norm, m_w_in, m_fgate_bias, m_conv_w, m_conv_b, m_conv_ln_g, m_conv_ln_b, m_w_out, m_ffn2_norm, m_ffn2_w_gate, m_ffn2_w_up, m_ffn2_w_down, m_final_norm, v_ffn1_norm, v_ffn1_w_gate, v_ffn1_w_up, v_ffn1_w_down, v_mix_norm, v_w_in, v_fgate_bias, v_conv_w, v_conv_b, v_conv_ln_g, v_conv_ln_b, v_w_out, v_ffn2_norm, v_ffn2_w_gate, v_ffn2_w_up, v_ffn2_w_down, v_final_norm):
    xs = x[0]
    S, D = xs.shape
    C = conv_b.shape[0]
    n_heads = fgate_bias.shape[0]
    FW = n_heads * HEAD_DIM
    n_pair = n_heads // 2
    MIX = C + FW
    in_shard = w_in.shape[1]
    in_cols = in_shard * N_DEV
    NP = -(-in_cols // 512) * 512
    q_blk, k_blk, v_blk = 2 * C // LANES, (2 * C + FW) // LANES, (2 * C + 2 * FW) // LANES
    f_blk = (2 * C + 3 * FW) // LANES
    assert C % LANES == 0 and FW % LANES == 0 and n_heads % 2 == 0 and n_heads <= LANES
    assert in_cols == 2 * C + 3 * FW + n_heads and MIX == w_out.shape[0] * N_DEV

    vec = lambda a: a.reshape(1, -1)
    bf = lambda a: a.astype(BF16)
    tm = _tile(S, REDUCE_ROWS)

    wgt1, wut1, wd1 = _all_gather_two_level("ag_ffn1", [bf(ffn1_w_gate).T, bf(ffn1_w_up).T, bf(ffn1_w_down)])
    ag, _ = _xchg_start("ag_start", [bf(w_in).T, conv_w, bf(w_out), bf(ffn2_w_gate).T, bf(ffn2_w_up).T,
                                     bf(ffn2_w_down)], False, dep=wd1)
    fbias = jnp.pad(vec(fgate_bias), ((0, 0), (0, LANES - n_heads)))

    x1, h1, G1, U1 = _ffn_fwd("ffn1_fwd", xs, vec(ffn1_norm), wgt1, wut1, wd1)
    win_g, cw_g = _xchg_wait("ag_wait_in", ag[0:2], False, x1)
    wint = jnp.pad(win_g.reshape(in_cols, D), ((0, NP - in_cols), (0, 0)))
    cw = jnp.pad(cw_g.transpose(1, 0, 2).reshape(CONV_WIDTH, C), ((0, CONV_HALO - CONV_WIDTH), (0, 0)))
    proj, h2 = _norm_mm_nt("proj_in", x1, vec(mix_norm), wint, 512)
    yc, y_conv = _conv_fwd("conv_fwd", proj, C, cw, vec(conv_b), vec(conv_ln_g), vec(conv_ln_b))
    ka = _gate_prep("gate_prep", proj, f_blk, fbias, n_pair)
    o, o32, lse = _attn_fwd("attn_fwd", proj, ka, q_blk, k_blk, v_blk, n_pair)
    (wout_g,) = _xchg_wait("ag_wait_out", ag[2:3], False, lse)
    wout = wout_g.reshape(MIX, D)
    mix = jnp.concatenate([y_conv, o], axis=1)
    x2 = _mm_res("proj_out", mix, wout, x1, 512)
    wgt2, wut2, wd2 = _xchg_wait("ag_wait_ffn2", ag[3:6], False, x2)
    x3, h3, G2, U2 = _ffn_fwd("ffn2_fwd", x2, vec(ffn2_norm), wgt2, wut2, wd2)

    dx3, d_final_norm, loss_part = _final("final", x3, vec(final_norm), loss_target[0])
    dG2, dU2, A2 = _ffn_bwd_gate("ffn2_bwd_gate", dx3, G2, U2, wd2)
    dx2, d_ffn2_norm = _ffn_bwd_in("ffn2_bwd_in", dG2, dU2, wgt2, wut2, dx3, x2, vec(ffn2_norm))
    dwd2 = _ffn_block_grad("ffn2_dwd", A2, dx3, 0.5)
    s_d2, tok = _xchg_start("a2a_start_ffn2_wd", [dwd2], True)
    dwg2 = _ffn_block_grad("ffn2_dwg", dG2, h3, 1.0, dep=tok)
    s_g2, tok = _xchg_start("a2a_start_ffn2_wg", [dwg2], True)
    dwu2 = _ffn_block_grad("ffn2_dwu", dU2, h3, 1.0, dep=tok)
    s_u2, tok = _xchg_start("a2a_start_ffn2_wu", [dwu2], True)

    dmix = _mm_k("dmix", dx2, wout, D, True, dep=tok)
    d_wout = _mm_tn("dwout", mix, dx2, (MIX, D),
                    pl.BlockSpec((tm, MIX), lambda j, i: (i, 0)), pl.BlockSpec((tm, 512), lambda j, i: (i, j)),
                    pl.BlockSpec((MIX, 512), lambda j, i: (0, j)), (D // 512, S // tm), 1.0, BF16)
    s_out, tok = _xchg_start("a2a_start_w_out", [d_wout.reshape(N_DEV, MIX // N_DEV, D)], True)
    dag, d_cw, d_cb, d_lg, d_lb = _conv_bwd("conv_bwd", dmix, yc, proj, C, cw, vec(conv_ln_g), vec(conv_ln_b))
    dq, dk, dv, sp, rs = _attn_bwd("attn_bwd", proj, ka, o32, lse, dmix, q_blk, k_blk, v_blk, C // LANES, n_pair)
    df, d_fb = _gate_bwd("gate_bwd", sp, rs, proj, f_blk, fbias, n_pair)
    dproj = jnp.concatenate([dag, bf(dq), dk, dv, df, jnp.zeros((S, NP - f_blk * LANES - LANES), BF16)], axis=1)
    d_wint = _mm_tn("dwin", dproj, h2, (NP, D),
                    pl.BlockSpec((tm, 512), lambda j, i: (i, j)), pl.BlockSpec((tm, D), lambda j, i: (i, 0)),
                    pl.BlockSpec((512, D), lambda j, i: (j, 0)), (NP // 512, S // tm), 1.0, BF16, dep=tok)
    s_in, tok = _xchg_start("a2a_start_w_in", [d_wint[:in_cols].reshape(N_DEV, in_shard, D)], True)
    dx1, d_mix_norm = _mm_k("dh2", dproj, wint, _tile(NP, NP // 4, LANES), False,
                            norm_bwd=(x1, vec(mix_norm), dx2), dep=tok)
    dG1, dU1, A1 = _ffn_bwd_gate("ffn1_bwd_gate", dx1, G1, U1, wd1)
    dwd1 = _ffn_block_grad("ffn1_dwd", A1, dx1, 0.5)
    s_d1, tok = _xchg_start("a2a_start_ffn1_wd", [dwd1], True)
    dwg1 = _ffn_block_grad("ffn1_dwg", dG1, h1, 1.0, dep=tok)
    s_g1, tok = _xchg_start("a2a_start_ffn1_wg", [dwg1], True)
    dwu1 = _ffn_block_grad("ffn1_dwu", dU1, h1, 1.0, dep=tok)
    s_u1, tok = _xchg_start("a2a_start_ffn1_wu", [dwu1], True)
    dx0, d_ffn1_norm = _ffn_bwd_in("ffn1_bwd_in", dG1, dU1, wgt1, wut1, dx1, xs, vec(ffn1_norm) + tok[:1, :1])

    r_d2, r_g2, r_u2, r_out, r_in = _xchg_wait("a2a_wait_a", s_d2 + s_g2 + s_u2 + s_out + s_in, True, dx0)
    tr = lambda a: a.T

    def adam_t(name, recv, w, m, v):
        return tuple(tr(r) for r in _reduce_adam(name, recv, tr(w), tr(m), tr(v)))

    res = {
        "ffn2_w_down": _reduce_adam("adam_ffn2_wd", r_d2, ffn2_w_down, m_ffn2_w_down, v_ffn2_w_down),
        "ffn2_w_gate": adam_t("adam_ffn2_wg", r_g2, ffn2_w_gate, m_ffn2_w_gate, v_ffn2_w_gate),
        "ffn2_w_up": adam_t("adam_ffn2_wu", r_u2, ffn2_w_up, m_ffn2_w_up, v_ffn2_w_up),
        "w_out": _reduce_adam("adam_w_out", r_out, w_out, m_w_out, v_w_out),
        "w_in": adam_t("adam_w_in", r_in, w_in, m_w_in, v_w_in),
    }
    (r_d1,) = _xchg_wait("a2a_wait_d1", s_d1, True, res["w_in"][0])
    res["ffn1_w_down"] = _reduce_adam("adam_ffn1_wd", r_d1, ffn1_w_down, m_ffn1_w_down, v_ffn1_w_down)
    (r_g1,) = _xchg_wait("a2a_wait_g1", s_g1, True, res["ffn1_w_down"][0])
    res["ffn1_w_gate"] = adam_t("adam_ffn1_wg", r_g1, ffn1_w_gate, m_ffn1_w_gate, v_ffn1_w_gate)
    (r_u1,) = _xchg_wait("a2a_wait_u1", s_u1, True, res["ffn1_w_gate"][0])
    res["ffn1_w_up"] = adam_t("adam_ffn1_wu", r_u1, ffn1_w_up, m_ffn1_w_up, v_ffn1_w_up)

    lanes = lambda a: a.reshape(-1, LANES)
    (norm1_g,) = _exchange("ag_norm1", [lanes(d_ffn1_norm)], False, dep=res["ffn1_w_up"][0])
    res["ffn1_norm"] = tuple(r.reshape(D) for r in _reduce_adam(
        "adam_norm1", norm1_g, lanes(ffn1_norm), lanes(m_ffn1_norm), lanes(v_ffn1_norm)))
    rows = lambda a: a.reshape(-1, C)
    pad_row = lambda a: jnp.pad(a.reshape(1, -1), ((0, 0), (0, C - a.size)))
    pieces = [rows(d_mix_norm), rows(d_ffn2_norm), rows(d_final_norm),
              d_cw[:CONV_WIDTH], d_cb, d_lg, d_lb, pad_row(d_fb[0, :n_heads]), pad_row(loss_part[0, :1])]
    pack = jnp.concatenate(pieces, axis=0)
    n_rows = pack.shape[0]
    pack = jnp.pad(pack, ((0, -n_rows % SUBLANES), (0, 0)))
    (pack_g,) = _exchange("ag_small", [pack], False, dep=res["ffn1_norm"][0])
    tot = _reduce_adam("sum_small", pack_g)
    nd = D // C
    g_mix_norm, g_ffn2_norm, g_final_norm = (tot[k * nd:(k + 1) * nd].reshape(D) for k in range(3))
    r0 = 3 * nd
    me = _lin(_mesh_pos())
    cs = C // N_DEV
    g_conv_w = lax.dynamic_slice(tot[r0:r0 + CONV_WIDTH], (0, me * cs), (CONV_WIDTH, cs))
    g_conv_b, g_ln_g, g_ln_b = tot[r0 + CONV_WIDTH], tot[r0 + CONV_WIDTH + 1], tot[r0 + CONV_WIDTH + 2]
    g_fb = tot[r0 + CONV_WIDTH + 3, :n_heads]
    loss = tot[r0 + CONV_WIDTH + 4, 0]

    small = [(g_mix_norm, mix_norm, m_mix_norm, v_mix_norm),
             (g_fb, fgate_bias, m_fgate_bias, v_fgate_bias), (g_conv_w, conv_w, m_conv_w, v_conv_w),
             (g_conv_b, conv_b, m_conv_b, v_conv_b), (g_ln_g, conv_ln_g, m_conv_ln_g, v_conv_ln_g),
             (g_ln_b, conv_ln_b, m_conv_ln_b, v_conv_ln_b), (g_ffn2_norm, ffn2_norm, m_ffn2_norm, v_ffn2_norm),
             (g_final_norm, final_norm, m_final_norm, v_final_norm)]
    sizes = [g.size for g, _, _, _ in small]
    total = sum(sizes)
    padded = -(-total // (SUBLANES * LANES)) * (SUBLANES * LANES)

    def flat_pack(k, fill):
        flat = jnp.concatenate([t[k].reshape(-1) for t in small])
        return jnp.pad(flat, (0, padded - total), constant_values=fill).reshape(padded // LANES, LANES)

    sg, sd, sm, sv = _reduce_adam("adam_small", flat_pack(0, 0.0)[None], flat_pack(1, 0.0), flat_pack(2, 0.0),
                                  flat_pack(3, 1.0))

    def unpack(packed):
        flat = packed.reshape(-1)
        out, off = [], 0
        for (g, _, _, _), n in zip(small, sizes):
            out.append(flat[off:off + n].reshape(g.shape))
            off += n
        return out

    s_g, s_d, s_m, s_v = unpack(sg), unpack(sd), unpack(sm), unpack(sv)

    small_names = ["mix_norm", "fgate_bias", "conv_w", "conv_b", "conv_ln_g", "conv_ln_b",
                   "ffn2_norm", "final_norm"]
    for k, n in enumerate(small_names):
        res[n] = (s_g[k], s_d[k], s_m[k], s_v[k])
    order = ["ffn1_norm", "ffn1_w_gate", "ffn1_w_up", "ffn1_w_down", "mix_norm", "w_in", "fgate_bias",
             "conv_w", "conv_b", "conv_ln_g", "conv_ln_b", "w_out", "ffn2_norm", "ffn2_w_gate", "ffn2_w_up",
             "ffn2_w_down", "final_norm"]
    outs = [loss, dx0[None]]
    for k in range(4):
        outs += [res[n][k] for n in order]
    return tuple(outs)
```

```python
import math

import jax
import jax.numpy as jnp
from jax import lax
from jax.experimental import pallas as pl
from jax.experimental.pallas import tpu as pltpu

F32 = jnp.float32
BF16 = jnp.bfloat16

N_DEV = 8
MESH_ID = pl.DeviceIdType.MESH
HEAD_DIM = 64
CONV_WIDTH = 31
CONV_HALO = 32
NORM_EPS = 1e-6
LN_EPS = 1e-5
NEG_INF = -1e30
LANES = 128
SUBLANES = 8
V7X_VMEM_LIMIT = 52 * 1024 * 1024

ADAM_LR = 0.001
ADAM_B1 = 0.9
ADAM_B2 = 0.999
ADAM_EPS = 1e-08
ADAM_WD = 0.01
ADAM_STEP = 10


def _params(n_grid_axes):
    return pltpu.CompilerParams(dimension_semantics=("arbitrary",) * n_grid_axes,
                                vmem_limit_bytes=V7X_VMEM_LIMIT)


def _tile(n, pref, mult=8):
    t = min(pref, n)
    while t >= mult:
        if n % t == 0 and t % mult == 0:
            return t
        t -= mult
    return n


def _dot(a, b):
    return jnp.dot(a, b, preferred_element_type=F32)


def _dot_nt(a, b):
    return lax.dot_general(a, b, (((1,), (1,)), ((), ())), preferred_element_type=F32)


def _dot_tn(a, b):
    return lax.dot_general(a, b, (((0,), (0,)), ((), ())), preferred_element_type=F32)


def _sigmoid(x):
    return 1.0 / (1.0 + jnp.exp(-x))


def _rms_fwd(x, g):
    r = lax.rsqrt(jnp.mean(x * x, axis=-1, keepdims=True) + NORM_EPS)
    return x * r * g


def _rms_bwd(dh, x, g):
    r = lax.rsqrt(jnp.mean(x * x, axis=-1, keepdims=True) + NORM_EPS)
    xh = x * r
    dxh = dh * g
    dx = r * (dxh - xh * jnp.mean(dxh * xh, axis=-1, keepdims=True))
    return dx, jnp.sum(dh * xh, axis=0, keepdims=True)


def _split3(x):
    hi = x.astype(BF16)
    r = x - hi.astype(F32)
    mid = r.astype(BF16)
    lo = (r - mid.astype(F32)).astype(BF16)
    return hi, mid, lo


def _blk(ref):
    return ref[0] if len(ref.shape) == 3 else ref[...]


_DEP_SPEC = pl.BlockSpec(memory_space=pl.ANY)


def _mesh_pos():
    return lax.axis_index("x"), lax.axis_index("y"), lax.axis_index("c")


def _peer(pos, k):
    x, y, c = pos
    return (1 - x if k & 4 else x, 1 - y if k & 2 else y, 1 - c if k & 1 else c)


def _lin(pos):
    x, y, c = pos
    return 4 * x + 2 * y + c


def _remote_copy(src, land, send_sems, recv_sems, pos, k, all_to_all):
    peer = _peer(pos, k)
    return pltpu.make_async_remote_copy(
        src_ref=src.at[_lin(peer)] if all_to_all else src, dst_ref=land.at[_lin(pos)],
        send_sem=send_sems.at[k - 1], recv_sem=recv_sems.at[k - 1],
        device_id=peer, device_id_type=MESH_ID)


def _landing_shape(a, all_to_all):
    return a.shape if all_to_all else (N_DEV,) + a.shape


def _exchange(name, arrays, all_to_all, dep=None):
    n = len(arrays)
    deps = [] if dep is None else [dep]
    out_shapes = [jax.ShapeDtypeStruct(_landing_shape(a, all_to_all), a.dtype) for a in arrays]

    def body(*refs):
        ins, outs = refs[:n], refs[n + len(deps):2 * n + len(deps)]
        send_sems, recv_sems, local_sems = refs[2 * n + len(deps):]
        pos = _mesh_pos()
        me = _lin(pos)
        local = []
        for a in range(n):
            src = ins[a].at[me] if all_to_all else ins[a]
            cp = pltpu.make_async_copy(src, outs[a].at[me], local_sems.at[a])
            cp.start()
            local.append(cp)
        remote = [_remote_copy(ins[a], outs[a], send_sems.at[a], recv_sems.at[a], pos, k, all_to_all)
                  for a in range(n) for k in range(1, N_DEV)]
        for cp in remote:
            cp.start()
        for cp in remote:
            cp.wait()
        for cp in local:
            cp.wait()

    any_spec = pl.BlockSpec(memory_space=pl.ANY)
    return pl.pallas_call(
        body, out_shape=out_shapes, in_specs=[any_spec] * (n + len(deps)), out_specs=[any_spec] * n,
        scratch_shapes=[pltpu.SemaphoreType.DMA((n, N_DEV - 1)),
                        pltpu.SemaphoreType.DMA((n, N_DEV - 1)),
                        pltpu.SemaphoreType.DMA((n,))],
        name=name)(*arrays, *deps)


def _all_gather_two_level(name, shards):
    n = len(shards)
    out_shapes = [jax.ShapeDtypeStruct((N_DEV,) + a.shape, a.dtype) for a in shards]

    def body(*refs):
        ins, outs = refs[:n], refs[n:2 * n]
        send_sems, recv_sems, local_sems = refs[2 * n:]
        x, y, c = pos = _mesh_pos()
        sibling = (x, y, 1 - c)
        chips = [(1 - x, y), (x, 1 - y), (1 - x, 1 - y)]

        def copy(a, k, block, to, src=None):
            slot = outs[a].at[_lin(block)]
            return pltpu.make_async_remote_copy(
                src_ref=slot if src is None else src, dst_ref=slot,
                send_sem=send_sems.at[a, k], recv_sem=recv_sems.at[a, k],
                device_id=to, device_id_type=MESH_ID)

        local = [pltpu.make_async_copy(ins[a], outs[a].at[_lin(pos)], local_sems.at[a]) for a in range(n)]
        first = [copy(a, 1 + j, pos, (*chip, c), src=ins[a]) for j, chip in enumerate(chips) for a in range(n)]
        first += [copy(a, 0, pos, sibling, src=ins[a]) for a in range(n)]
        for cp in first + local:
            cp.start()
        passed = []
        for j, chip in enumerate(chips):
            for a in range(n):
                copy(a, 1 + j, (*chip, c), pos).wait_recv()
                cp = copy(a, 4 + j, (*chip, c), sibling)
                cp.start()
                passed.append(cp)
        for a in range(n):
            copy(a, 0, sibling, pos).wait_recv()
        for j, chip in enumerate(chips):
            for a in range(n):
                copy(a, 4 + j, (*chip, 1 - c), pos).wait_recv()
        for cp in first + passed:
            cp.wait_send()
        for cp in local:
            cp.wait()

    any_spec = pl.BlockSpec(memory_space=pl.ANY)
    return pl.pallas_call(
        body, out_shape=out_shapes, in_specs=[any_spec] * n, out_specs=[any_spec] * n,
        scratch_shapes=[pltpu.SemaphoreType.DMA((n, N_DEV - 1)),
                        pltpu.SemaphoreType.DMA((n, N_DEV - 1)),
                        pltpu.SemaphoreType.DMA((n,))],
        name=name)(*shards)


_HBM_SPEC = pl.BlockSpec(memory_space=pltpu.HBM)
_SEM_SPEC = pl.BlockSpec(memory_space=pltpu.SEMAPHORE)
_SIDE_EFFECT = pltpu.SideEffectType.DATAFLOW_SIDE_EFFECTING


def _xchg_start(name, arrays, all_to_all, dep=None):
    n = len(arrays)
    me = _lin(_mesh_pos())
    lands = [lax.dynamic_update_index_in_dim(
        lax.empty(_landing_shape(a, all_to_all), a.dtype),
        lax.dynamic_index_in_dim(a, me, 0, keepdims=False) if all_to_all else a, me, 0) for a in arrays]

    deps = [] if dep is None else [dep]

    def body(*refs):
        srcs, lnds = refs[:n], refs[n:2 * n]
        outs = refs[2 * n + len(deps):]
        send, recv = outs[:n], outs[n:2 * n]
        token = outs[4 * n]
        pos = _mesh_pos()
        for a in range(n):
            for k in range(1, N_DEV):
                _remote_copy(srcs[a], lnds[a], send[a], recv[a], pos, k, all_to_all).start()
        token[...] = jnp.zeros_like(token)

    hbm = lambda a: pltpu.HBM(a.shape, a.dtype)
    sems = [pltpu.SemaphoreType.DMA((N_DEV - 1,))] * (2 * n)
    res = pl.pallas_call(
        body, name=name,
        out_shape=sems + [hbm(a) for a in arrays] + [hbm(l) for l in lands]
        + [jax.ShapeDtypeStruct((SUBLANES, LANES), F32)],
        in_specs=[_HBM_SPEC] * (2 * n) + [_DEP_SPEC] * len(deps),
        out_specs=[_SEM_SPEC] * (2 * n) + [_HBM_SPEC] * (2 * n) + [pl.BlockSpec(memory_space=pltpu.VMEM)],
        input_output_aliases={a: 2 * n + a for a in range(2 * n)},
        compiler_params=pltpu.CompilerParams(has_side_effects=_SIDE_EFFECT),
    )(*[pltpu.with_memory_space_constraint(a, pltpu.HBM) for a in list(arrays) + lands], *deps)
    per_array = [(res[a], res[n + a], res[2 * n + a], res[3 * n + a]) for a in range(n)]
    return per_array, res[4 * n]


def _xchg_wait(name, started, all_to_all, after):
    n = len(started)

    def body(*refs):
        srcs, lnds, send, recv = refs[:n], refs[n:2 * n], refs[2 * n:3 * n], refs[3 * n:4 * n]
        pos = _mesh_pos()
        for a in range(n):
            for k in range(1, N_DEV):
                cp = _remote_copy(srcs[a], lnds[a], send[a], recv[a], pos, k, all_to_all)
                cp.wait_send()
                cp.wait_recv()

    hbm = lambda a: pltpu.HBM(a.shape, a.dtype)
    srcs = [s[2] for s in started]
    lands = [s[3] for s in started]
    res = pl.pallas_call(
        body, name=name,
        out_shape=[hbm(a) for a in srcs + lands],
        in_specs=[_HBM_SPEC] * (2 * n) + [_SEM_SPEC] * (2 * n) + [_DEP_SPEC],
        out_specs=[_HBM_SPEC] * (2 * n),
        input_output_aliases={a: a for a in range(2 * n)},
        compiler_params=pltpu.CompilerParams(has_side_effects=_SIDE_EFFECT),
    )(*srcs, *lands, *[s[0] for s in started], *[s[1] for s in started], after)
    return list(res[n:])


def _reduce_adam(name, parts, w=None, m=None, v=None):
    n, R, C = parts.shape
    tr = _tile(R, 256)
    do_adam = w is not None
    bc1 = 1.0 - ADAM_B1 ** ADAM_STEP
    bc2 = 1.0 - ADAM_B2 ** ADAM_STEP

    def body(*refs):
        p_ref = refs[0]
        g = p_ref[0].astype(F32)
        for d in range(1, n):
            g = g + p_ref[d].astype(F32)
        if not do_adam:
            refs[1][...] = g
            return
        w_ref, m_ref, v_ref, g_ref, d_ref, nm_ref, nv_ref = refs[1:]
        g_ref[...] = g
        nm = ADAM_B1 * m_ref[...] + (1.0 - ADAM_B1) * g
        nv = ADAM_B2 * v_ref[...] + (1.0 - ADAM_B2) * (g * g)
        m_hat = nm / bc1
        v_hat = nv / bc2
        d_ref[...] = -ADAM_LR * (m_hat / (jnp.sqrt(v_hat) + ADAM_EPS) + ADAM_WD * w_ref[...])
        nm_ref[...] = nm
        nv_ref[...] = nv

    tc = _tile(C, 512, LANES) if tr == R and R > 256 else C
    row = pl.BlockSpec((tr, tc), lambda i, j: (i, j))
    part = pl.BlockSpec((n, tr, tc), lambda i, j: (0, i, j))
    shard = jax.ShapeDtypeStruct((R, C), F32)
    grid = (R // tr, C // tc)
    if do_adam:
        return pl.pallas_call(body, grid=grid, in_specs=[part, row, row, row],
                              out_specs=[row] * 4, out_shape=[shard] * 4,
                              compiler_params=_params(2), name=name)(parts, w, m, v)
    return pl.pallas_call(body, grid=grid, in_specs=[part], out_specs=row, out_shape=shard,
                          compiler_params=_params(2), name=name)(parts)


def _ffn_fwd(name, x, gain, wgt, wut, wd):
    S, D = x.shape
    nb, Fs, _ = wgt.shape
    tm = _tile(S, 512)

    def body(x_ref, g_ref, wg_ref, wu_ref, wd_ref, xo_ref, h_ref, G_ref, U_ref, acc_ref):
        j = pl.program_id(1)

        @pl.when(j == 0)
        def _():
            h_ref[...] = _rms_fwd(x_ref[...], g_ref[...]).astype(BF16)
            acc_ref[...] = jnp.zeros_like(acc_ref)

        h = h_ref[...]
        G = _dot_nt(h, wg_ref[0])
        U = _dot_nt(h, wu_ref[0])
        G_ref[0] = G
        U_ref[0] = U
        a = G * _sigmoid(G) * U
        acc_ref[...] += _dot(a.astype(BF16), wd_ref[0])

        @pl.when(j == nb - 1)
        def _():
            xo_ref[...] = x_ref[...] + 0.5 * acc_ref[...]

    row = pl.BlockSpec((tm, D), lambda i, j: (i, 0))
    act = pl.BlockSpec((1, tm, Fs), lambda i, j: (j, i, 0))
    wblk = pl.BlockSpec((1, Fs, D), lambda i, j: (j, 0, 0))
    return pl.pallas_call(
        body, grid=(S // tm, nb),
        in_specs=[row, pl.BlockSpec((1, D), lambda i, j: (0, 0)), wblk, wblk, wblk],
        out_specs=[row, row, act, act],
        out_shape=[jax.ShapeDtypeStruct((S, D), F32), jax.ShapeDtypeStruct((S, D), BF16),
                   jax.ShapeDtypeStruct((nb, S, Fs), F32), jax.ShapeDtypeStruct((nb, S, Fs), F32)],
        scratch_shapes=[pltpu.VMEM((tm, D), F32)],
        compiler_params=_params(2), name=name)(x, gain, wgt, wut, wd)


def _ffn_bwd_gate(name, dxo, G, U, wd):
    S, D = dxo.shape
    nb, Fs, _ = wd.shape
    tm = _tile(S, 512)

    def gate_body(dxo_ref, G_ref, U_ref, wd_ref, dG_ref, dU_ref, A_ref, dxb_ref):
        @pl.when(pl.program_id(1) == 0)
        def _():
            dxb_ref[...] = dxo_ref[...].astype(BF16)

        for c0 in range(0, Fs, MXU_COLS):
            cols = pl.ds(c0, min(MXU_COLS, Fs - c0))
            dA = 0.5 * _dot_nt(dxb_ref[...], wd_ref[0, cols, :])
            Gv = G_ref[0, :, cols]
            Uv = U_ref[0, :, cols]
            sg = _sigmoid(Gv)
            sl = Gv * sg
            dG_ref[0, :, cols] = (dA * Uv * (sg * (1.0 + Gv * (1.0 - sg)))).astype(BF16)
            dU_ref[0, :, cols] = (dA * sl).astype(BF16)
            A_ref[0, :, cols] = (sl * Uv).astype(BF16)

    row = pl.BlockSpec((tm, D), lambda i, j: (i, 0))
    act = pl.BlockSpec((1, tm, Fs), lambda i, j: (j, i, 0))
    wblk = pl.BlockSpec((1, Fs, D), lambda i, j: (j, 0, 0))
    act_shape = jax.ShapeDtypeStruct((nb, S, Fs), BF16)
    return pl.pallas_call(
        gate_body, grid=(S // tm, nb), in_specs=[row, act, act, wblk], out_specs=[act, act, act],
        out_shape=[act_shape, act_shape, act_shape],
        scratch_shapes=[pltpu.VMEM((tm, D), BF16)],
        compiler_params=_params(2), name=name)(dxo, G, U, wd)


def _ffn_bwd_in(name, dG, dU, wgt, wut, dxo, x_in, gain):
    S, D = x_in.shape
    nb, Fs, _ = wgt.shape
    tm = _tile(S, 512)
    rows_per_chunk = _tile(tm, 128)

    def in_body(dG_ref, dU_ref, wg_ref, wu_ref, dxo_ref, x_ref, g_ref, dx_ref, dgain_ref, acc_ref):
        i, j = pl.program_id(0), pl.program_id(1)

        @pl.when(j == 0)
        def _():
            acc_ref[...] = jnp.zeros_like(acc_ref)

        @pl.when((i == 0) & (j == 0))
        def _():
            dgain_ref[...] = jnp.zeros_like(dgain_ref)

        acc_ref[...] += _dot(dG_ref[0], wg_ref[0]) + _dot(dU_ref[0], wu_ref[0])

        @pl.when(j == nb - 1)
        def _():
            def chunk(r, dg_sum):
                rows = pl.ds(pl.multiple_of(r * rows_per_chunk, rows_per_chunk), rows_per_chunk)
                dx, dg = _rms_bwd(acc_ref[rows, :], x_ref[rows, :], g_ref[...])
                dx_ref[rows, :] = dxo_ref[rows, :] + dx
                return dg_sum + dg

            dgain_ref[...] += lax.fori_loop(0, tm // rows_per_chunk, chunk, jnp.zeros((1, D), F32))

    row = pl.BlockSpec((tm, D), lambda i, j: (i, 0))
    vec = pl.BlockSpec((1, D), lambda i, j: (0, 0))
    act = pl.BlockSpec((1, tm, Fs), lambda i, j: (j, i, 0))
    wblk = pl.BlockSpec((1, Fs, D), lambda i, j: (j, 0, 0))
    return pl.pallas_call(
        in_body, grid=(S // tm, nb), in_specs=[act, act, wblk, wblk, row, row, vec],
        out_specs=[row, vec],
        out_shape=[jax.ShapeDtypeStruct((S, D), F32), jax.ShapeDtypeStruct((1, D), F32)],
        scratch_shapes=[pltpu.VMEM((tm, D), F32)],
        compiler_params=_params(2), name=name)(dG, dU, wgt, wut, dxo, x_in, gain)


def _mm_tn(name, lhs, rhs, out_shape, lhs_spec, rhs_spec, out_spec, grid, scale, out_dtype, dep=None):
    acc_shape = tuple(out_spec.block_shape[-2:])
    n_red = grid[-1]

    def body(l_ref, r_ref, *rest):
        o_ref, acc_ref = rest[-2:]
        i = pl.program_id(len(grid) - 1)

        @pl.when(i == 0)
        def _():
            acc_ref[...] = jnp.zeros_like(acc_ref)

        acc_ref[...] += _dot_tn(_blk(l_ref).astype(BF16), _blk(r_ref).astype(BF16))

        @pl.when(i == n_red - 1)
        def _():
            res = (scale * acc_ref[...]).astype(out_dtype)
            if len(o_ref.shape) == 3:
                o_ref[0] = res
            else:
                o_ref[...] = res

    deps = [] if dep is None else [dep]
    return pl.pallas_call(
        body, grid=grid, in_specs=[lhs_spec, rhs_spec] + [_DEP_SPEC] * len(deps), out_specs=out_spec,
        out_shape=jax.ShapeDtypeStruct(out_shape, out_dtype),
        scratch_shapes=[pltpu.VMEM(acc_shape, F32)],
        compiler_params=_params(len(grid)), name=name)(lhs, rhs, *deps)


def _norm_mm_nt(name, x, gain, wt, tn):
    S, D = x.shape
    N = wt.shape[0]
    tm = _tile(S, 1024)

    def body(x_ref, g_ref, w_ref, o_ref, h_ref):
        @pl.when(pl.program_id(1) == 0)
        def _():
            h_ref[...] = _rms_fwd(x_ref[...], g_ref[...]).astype(BF16)

        o_ref[...] = _dot_nt(h_ref[...], w_ref[...])

    row = pl.BlockSpec((tm, D), lambda i, j: (i, 0))
    return pl.pallas_call(
        body, grid=(S // tm, N // tn),
        in_specs=[row, pl.BlockSpec((1, D), lambda i, j: (0, 0)),
                  pl.BlockSpec((tn, D), lambda i, j: (j, 0))],
        out_specs=[pl.BlockSpec((tm, tn), lambda i, j: (i, j)), row],
        out_shape=[jax.ShapeDtypeStruct((S, N), F32), jax.ShapeDtypeStruct((S, D), BF16)],
        compiler_params=_params(2), name=name)(x, gain, wt)


def _mm_res(name, a, w, res, tn):
    S, K = a.shape
    N = w.shape[1]
    tm = _tile(S, 512)

    def body(a_ref, w_ref, r_ref, o_ref):
        o_ref[...] = r_ref[...] + _dot(a_ref[...], w_ref[...])

    tile = pl.BlockSpec((tm, tn), lambda i, j: (i, j))
    return pl.pallas_call(
        body, grid=(S // tm, N // tn),
        in_specs=[pl.BlockSpec((tm, K), lambda i, j: (i, 0)),
                  pl.BlockSpec((K, tn), lambda i, j: (0, j)), tile],
        out_specs=tile, out_shape=jax.ShapeDtypeStruct((S, N), F32),
        compiler_params=_params(2), name=name)(a, w, res)


def _mm_k(name, a, b, tk, transpose_b, norm_bwd=None, dep=None):
    S, K = a.shape
    N = b.shape[0] if transpose_b else b.shape[1]
    tm = _tile(S, 512)
    rows_per_chunk = _tile(tm, 128)
    nk = K // tk
    n_extra = 0 if norm_bwd is None else 3
    deps = [] if dep is None else [dep]

    def body(*refs):
        a_ref, b_ref = refs[:2]
        outs = refs[2 + n_extra + len(deps):]
        acc_ref = outs[-1]
        i, k = pl.program_id(0), pl.program_id(1)

        @pl.when(k == 0)
        def _():
            acc_ref[...] = jnp.zeros_like(acc_ref)

        av = a_ref[...].astype(BF16)
        acc_ref[...] += _dot_nt(av, b_ref[...]) if transpose_b else _dot(av, b_ref[...])

        if norm_bwd is None:
            @pl.when(k == nk - 1)
            def _():
                outs[0][...] = acc_ref[...]
        else:
            x_ref, g_ref, dres_ref = refs[2:5]
            o_ref, dgain_ref = outs[:2]

            @pl.when((i == 0) & (k == 0))
            def _():
                dgain_ref[...] = jnp.zeros_like(dgain_ref)

            @pl.when(k == nk - 1)
            def _():
                def chunk(r, dg_sum):
                    rows = pl.ds(pl.multiple_of(r * rows_per_chunk, rows_per_chunk), rows_per_chunk)
                    dx, dg = _rms_bwd(acc_ref[rows, :], x_ref[rows, :], g_ref[...])
                    o_ref[rows, :] = dres_ref[rows, :] + dx
                    return dg_sum + dg

                dgain_ref[...] += lax.fori_loop(0, tm // rows_per_chunk, chunk, jnp.zeros((1, N), F32))

    a_spec = pl.BlockSpec((tm, tk), lambda i, k: (i, k))
    b_spec = (pl.BlockSpec((N, tk), lambda i, k: (0, k)) if transpose_b
              else pl.BlockSpec((tk, N), lambda i, k: (k, 0)))
    row = pl.BlockSpec((tm, N), lambda i, k: (i, 0))
    vec = pl.BlockSpec((1, N), lambda i, k: (0, 0))
    out = jax.ShapeDtypeStruct((S, N), F32)
    scratch = [pltpu.VMEM((tm, N), F32)]
    dep_specs = [_DEP_SPEC] * len(deps)
    if norm_bwd is None:
        return pl.pallas_call(body, grid=(S // tm, nk), in_specs=[a_spec, b_spec] + dep_specs,
                              out_specs=row, out_shape=out, scratch_shapes=scratch,
                              compiler_params=_params(2), name=name)(a, b, *deps)
    x_in, gain, dres = norm_bwd
    return pl.pallas_call(body, grid=(S // tm, nk), in_specs=[a_spec, b_spec, row, vec, row] + dep_specs,
                          out_specs=[row, vec],
                          out_shape=[out, jax.ShapeDtypeStruct((1, N), F32)],
                          scratch_shapes=scratch,
                          compiler_params=_params(2), name=name)(a, b, x_in, gain, dres, *deps)


def _final(name, x, gain, target):
    S, D = x.shape
    tm = _tile(S, 512)

    def body(x_ref, g_ref, t_ref, dx_ref, dgain_ref, loss_ref):
        @pl.when(pl.program_id(0) == 0)
        def _():
            dgain_ref[...] = jnp.zeros_like(dgain_ref)
            loss_ref[...] = jnp.zeros_like(loss_ref)

        xv = x_ref[...]
        err = _rms_fwd(xv, g_ref[...]) - t_ref[...]
        per_tok = jnp.mean(err * err, axis=-1, keepdims=True)
        loss_ref[...] += 0.5 * jnp.sum(per_tok, axis=0, keepdims=True)
        dx, dg = _rms_bwd(err * (1.0 / D), xv, g_ref[...])
        dx_ref[...] = dx
        dgain_ref[...] += dg

    row = pl.BlockSpec((tm, D), lambda i: (i, 0))
    vec = pl.BlockSpec((1, D), lambda i: (0, 0))
    return pl.pallas_call(
        body, grid=(S // tm,), in_specs=[row, vec, row],
        out_specs=[row, vec, pl.BlockSpec((1, LANES), lambda i: (0, 0))],
        out_shape=[jax.ShapeDtypeStruct((S, D), F32), jax.ShapeDtypeStruct((1, D), F32),
                   jax.ShapeDtypeStruct((1, LANES), F32)],
        compiler_params=_params(1), name=name)(x, gain, target)


def _conv_tiles(S):
    ts = _tile(S, 256, CONV_HALO)
    return ts, ts // CONV_HALO


def _ln_stats(yc):
    mu = jnp.mean(yc, axis=-1, keepdims=True)
    d = yc - mu
    rs = lax.rsqrt(jnp.mean(d * d, axis=-1, keepdims=True) + LN_EPS)
    return d * rs, rs


N_PHASE = SUBLANES
PHASE_ROWS = CONV_HALO - SUBLANES


def _shifted_copies(buf, shifted, ts):
    for p in range(1, N_PHASE):
        shifted[p - 1] = buf[pl.ds(p, ts + PHASE_ROWS), :]


def _tap(buf, shifted, offset, rows0, n_rows, cols):
    a, p = divmod(offset, N_PHASE)
    rows = pl.ds(a * N_PHASE + rows0, n_rows)
    return buf[rows, cols] if p == 0 else shifted[p - 1, rows, cols]


def _conv_fwd(name, proj, C, cw, cb, lg, lb):
    S = proj.shape[0]
    ts, hb = _conv_tiles(S)
    rb = _tile(ts, 128)
    first = CONV_HALO - CONV_WIDTH + 1

    def body(a_ref, g_ref, ah_ref, gh_ref, cw_ref, cb_ref, lg_ref, lb_ref, yc_ref, y_ref, ubuf, ushift):
        i = pl.program_id(0)
        uh = ah_ref[...] * _sigmoid(gh_ref[...])
        ubuf[pl.ds(0, CONV_HALO), :] = jnp.where(i > 0, uh, 0.0)
        ubuf[pl.ds(CONV_HALO, ts), :] = a_ref[...] * _sigmoid(g_ref[...])
        _shifted_copies(ubuf, ushift, ts)

        def col_block(c, carry):
            cols = pl.ds(pl.multiple_of(c * LANES, LANES), LANES)
            for r in range(ts // rb):
                acc = jnp.zeros((rb, LANES), F32)
                for k in range(CONV_WIDTH):
                    acc = acc + cw_ref[pl.ds(k, 1), cols] * _tap(ubuf, ushift, first + k, r * rb, rb, cols)
                yc_ref[pl.ds(r * rb, rb), cols] = acc + cb_ref[:, cols]
            return carry

        lax.fori_loop(0, C // LANES, col_block, 0)
        yn, _ = _ln_stats(yc_ref[...])
        z = yn * lg_ref[...] + lb_ref[...]
        y_ref[...] = (z * _sigmoid(z)).astype(BF16)

    main = lambda col: pl.BlockSpec((ts, C), lambda i: (i, col))
    halo = lambda col: pl.BlockSpec((CONV_HALO, C), lambda i: (jnp.maximum(i * hb - 1, 0), col))
    vec = pl.BlockSpec((1, C), lambda i: (0, 0))
    return pl.pallas_call(
        body, grid=(S // ts,),
        in_specs=[main(0), main(1), halo(0), halo(1),
                  pl.BlockSpec((CONV_HALO, C), lambda i: (0, 0)), vec, vec, vec],
        out_specs=[pl.BlockSpec((ts, C), lambda i: (i, 0))] * 2,
        out_shape=[jax.ShapeDtypeStruct((S, C), F32), jax.ShapeDtypeStruct((S, C), BF16)],
        scratch_shapes=[pltpu.VMEM((ts + CONV_HALO, C), F32),
                        pltpu.VMEM((N_PHASE - 1, ts + PHASE_ROWS, C), F32)],
        compiler_params=_params(1), name=name)(proj, proj, proj, proj, cw, cb, lg, lb)


def _conv_bwd(name, dmix, yc, proj, C, cw, lg, lb):
    S = proj.shape[0]
    ts, hb = _conv_tiles(S)
    n_t = S // ts
    rb = _tile(ts, 128)
    first = CONV_HALO - CONV_WIDTH + 1

    def body(dy_ref, yc_ref, dyh_ref, ych_ref, a_ref, g_ref, ah_ref, gh_ref, cw_ref, lg_ref, lb_ref,
             dag_ref, dcw_ref, dcb_ref, dlg_ref, dlb_ref, ubuf, dbuf, ushift, dshift, du_ref):
        i = pl.program_id(0)

        @pl.when(i == 0)
        def _():
            dcw_ref[...] = jnp.zeros_like(dcw_ref)
            dcb_ref[...] = jnp.zeros_like(dcb_ref)
            dlg_ref[...] = jnp.zeros_like(dlg_ref)
            dlb_ref[...] = jnp.zeros_like(dlb_ref)

        def ln_bwd(dy, ycv):
            yn, rs = _ln_stats(ycv)
            z = yn * lg_ref[...] + lb_ref[...]
            sg = _sigmoid(z)
            dz = dy * (sg * (1.0 + z * (1.0 - sg)))
            dyn = dz * lg_ref[...]
            dyc = rs * (dyn - jnp.mean(dyn, axis=-1, keepdims=True)
                        - yn * jnp.mean(dyn * yn, axis=-1, keepdims=True))
            return dyc, dz, yn

        dyc, dz, yn = ln_bwd(dy_ref[...], yc_ref[...])
        dlg_ref[...] += jnp.sum(dz * yn, axis=0, keepdims=True)
        dlb_ref[...] += jnp.sum(dz, axis=0, keepdims=True)
        dcb_ref[...] += jnp.sum(dyc, axis=0, keepdims=True)
        dych, _, _ = ln_bwd(dyh_ref[...], ych_ref[...])
        dbuf[pl.ds(0, ts), :] = dyc
        dbuf[pl.ds(ts, CONV_HALO), :] = jnp.where(i < n_t - 1, dych, 0.0)

        uh = ah_ref[...] * _sigmoid(gh_ref[...])
        ubuf[pl.ds(0, CONV_HALO), :] = jnp.where(i > 0, uh, 0.0)
        ubuf[pl.ds(CONV_HALO, ts), :] = a_ref[...] * _sigmoid(g_ref[...])
        _shifted_copies(ubuf, ushift, ts)
        _shifted_copies(dbuf, dshift, ts)

        def col_block(c, carry):
            cols = pl.ds(pl.multiple_of(c * LANES, LANES), LANES)
            dcw = [jnp.zeros((1, LANES), F32)] * CONV_WIDTH
            for r in range(ts // rb):
                du = jnp.zeros((rb, LANES), F32)
                dyc_blk = dbuf[pl.ds(r * rb, rb), cols]
                for k in range(CONV_WIDTH):
                    du = du + cw_ref[pl.ds(k, 1), cols] * _tap(dbuf, dshift, CONV_WIDTH - 1 - k, r * rb, rb, cols)
                    tap = _tap(ubuf, ushift, first + k, r * rb, rb, cols)
                    dcw[k] = dcw[k] + jnp.sum(dyc_blk * tap, axis=0, keepdims=True)
                du_ref[pl.ds(r * rb, rb), cols] = du
            for k in range(CONV_WIDTH):
                dcw_ref[pl.ds(k, 1), cols] += dcw[k]
            return carry

        lax.fori_loop(0, C // LANES, col_block, 0)
        du = du_ref[...]
        av = a_ref[...]
        sgm = _sigmoid(g_ref[...])
        dag_ref[:, pl.ds(0, C)] = (du * sgm).astype(BF16)
        dag_ref[:, pl.ds(C, C)] = (du * av * sgm * (1.0 - sgm)).astype(BF16)

    main = lambda col: pl.BlockSpec((ts, C), lambda i: (i, col))
    past = lambda col: pl.BlockSpec((CONV_HALO, C), lambda i: (jnp.maximum(i * hb - 1, 0), col))
    nxt = pl.BlockSpec((CONV_HALO, C), lambda i: (jnp.minimum((i + 1) * hb, n_t * hb - 1), 0))
    vec = pl.BlockSpec((1, C), lambda i: (0, 0))
    full = pl.BlockSpec((CONV_HALO, C), lambda i: (0, 0))
    vshape = jax.ShapeDtypeStruct((1, C), F32)
    buf = pltpu.VMEM((ts + CONV_HALO, C), F32)
    shifted = pltpu.VMEM((N_PHASE - 1, ts + PHASE_ROWS, C), F32)
    return pl.pallas_call(
        body, grid=(n_t,),
        in_specs=[main(0), main(0), nxt, nxt, main(0), main(1), past(0), past(1), full, vec, vec],
        out_specs=[pl.BlockSpec((ts, 2 * C), lambda i: (i, 0)), full, vec, vec, vec],
        out_shape=[jax.ShapeDtypeStruct((S, 2 * C), BF16),
                   jax.ShapeDtypeStruct((CONV_HALO, C), F32), vshape, vshape, vshape],
        scratch_shapes=[buf, buf, shifted, shifted, pltpu.VMEM((ts, C), F32)],
        compiler_params=_params(1),
        name=name)(dmix, yc, dmix, yc, proj, proj, proj, proj, cw, lg, lb)


AUG = 3
ROW_CHUNK = 32
Q_TILES_PER_KV = 1
MXU_COLS = 256
REDUCE_ROWS = 1024


def _gate_prep(name, proj, f_blk, fbias, n_pair):
    S = proj.shape[0]
    ts = _tile(S, 512)
    W = LANES * n_pair

    def body(pf_ref, fb_ref, ka_ref, carry):
        @pl.when(pl.program_id(0) == 0)
        def _():
            carry[...] = jnp.zeros_like(carry)

        f = pf_ref[...] + fb_ref[...]
        logf = jnp.minimum(f, 0.0) - jnp.log(1.0 + jnp.exp(-jnp.abs(f)))
        r = lax.broadcasted_iota(jnp.int32, (ts, ts), 0)
        c = lax.broadcasted_iota(jnp.int32, (ts, ts), 1)
        ltri = (c <= r).astype(BF16)
        hi, mid, lo = _split3(logf)
        cs = _dot(ltri, hi) + _dot(ltri, mid) + _dot(ltri, lo) + carry[...]
        carry[...] = cs[ts - 1:ts, :]
        hh = lax.broadcasted_iota(jnp.int32, (LANES, W), 0)
        ll = lax.broadcasted_iota(jnp.int32, (LANES, W), 1)
        pair, w = ll >> 7, ll & (LANES - 1)
        ka = jnp.zeros((ts, W), F32)
        for p, piece in enumerate(_split3(-cs)):
            e = (((w == HEAD_DIM + p) & (hh == 2 * pair)) | ((w == p) & (hh == 2 * pair + 1)))
            ka = ka + _dot(piece, e.astype(BF16))
        lw = lax.broadcasted_iota(jnp.int32, (1, W), 1) & (LANES - 1)
        ka = ka + ((lw == HEAD_DIM + AUG) | (lw == AUG)).astype(F32)
        ka_ref[...] = ka.astype(BF16)

    return pl.pallas_call(
        body, grid=(S // ts,),
        in_specs=[pl.BlockSpec((ts, LANES), lambda i: (i, f_blk)),
                  pl.BlockSpec((1, LANES), lambda i: (0, 0))],
        out_specs=pl.BlockSpec((ts, W), lambda i: (i, 0)),
        out_shape=jax.ShapeDtypeStruct((S, W), BF16),
        scratch_shapes=[pltpu.VMEM((1, LANES), F32)],
        compiler_params=_params(1), name=name)(proj, fbias)


def _gate_bwd(name, sp, rs, proj, f_blk, fbias, n_pair):
    S = proj.shape[0]
    ts = _tile(S, 512)
    n_t = S // ts
    W = LANES * n_pair

    def body(sp_ref, rs_ref, pf_ref, fb_ref, df_ref, dfb_ref, carry):
        @pl.when(pl.program_id(0) == 0)
        def _():
            carry[...] = jnp.zeros_like(carry)
            dfb_ref[...] = jnp.zeros_like(dfb_ref)

        ll = lax.broadcasted_iota(jnp.int32, (W, LANES), 0)
        hh = lax.broadcasted_iota(jnp.int32, (W, LANES), 1)
        pair, w = ll >> 7, ll & (LANES - 1)
        first, second = hh == 2 * pair, hh == 2 * pair + 1

        def pick(ref, lane_first, lane_second):
            sel = (((w == lane_first) & first) | ((w == lane_second) & second)).astype(BF16)
            hi, mid, lo = _split3(ref[...])
            return _dot(hi, sel) + _dot(mid, sel) + _dot(lo, sel)

        dc = pick(rs_ref, HEAD_DIM + AUG, AUG) - pick(sp_ref, HEAD_DIM, 0)
        r = lax.broadcasted_iota(jnp.int32, (ts, ts), 0)
        c = lax.broadcasted_iota(jnp.int32, (ts, ts), 1)
        utri = (c >= r).astype(BF16)
        hi, mid, lo = _split3(dc)
        dlogf = _dot(utri, hi) + _dot(utri, mid) + _dot(utri, lo) + carry[...]
        carry[...] = dlogf[0:1, :]
        f = pf_ref[...] + fb_ref[...]
        lane = lax.broadcasted_iota(jnp.int32, (ts, LANES), 1)
        df = jnp.where(lane < 2 * n_pair, dlogf * _sigmoid(-f), 0.0)
        df_ref[...] = df.astype(BF16)
        dfb_ref[...] += jnp.sum(df, axis=0, keepdims=True)

    rev = lambda blk: (lambda i: (n_t - 1 - i, blk))
    return pl.pallas_call(
        body, grid=(n_t,),
        in_specs=[pl.BlockSpec((ts, W), rev(0)), pl.BlockSpec((ts, W), rev(0)),
                  pl.BlockSpec((ts, LANES), rev(f_blk)), pl.BlockSpec((1, LANES), lambda i: (0, 0))],
        out_specs=[pl.BlockSpec((ts, LANES), rev(0)), pl.BlockSpec((1, LANES), lambda i: (0, 0))],
        out_shape=[jax.ShapeDtypeStruct((S, LANES), BF16), jax.ShapeDtypeStruct((1, LANES), F32)],
        scratch_shapes=[pltpu.VMEM((1, LANES), F32)],
        compiler_params=_params(1), name=name)(sp, rs, proj, fbias)


def _spare_lane(h):
    return HEAD_DIM if h == 0 else 0


def _head_operands(h, lane, q2, k2, ka2):
    act = (lane < HEAD_DIM) if h == 0 else (lane >= HEAD_DIM)
    base = HEAD_DIM if h == 0 else 0
    ones = ((lane >= base) & (lane < base + AUG)).astype(F32)
    qa = jnp.where(act, q2 * (1.0 / math.sqrt(HEAD_DIM)), ones).astype(BF16)
    ka = jnp.where(act, k2.astype(BF16), ka2)
    return act, qa, ka


def _attn_fwd(name, proj, ka, q_blk, k_blk, v_blk, n_pair):
    S = proj.shape[0]
    tk = _tile(S, 512)
    tq = _tile(S, Q_TILES_PER_KV * tk, tk)
    ratio = tq // tk
    n_q, n_k = S // tq, S // tk
    W = LANES * n_pair
    rc = _tile(tq, ROW_CHUNK)

    def body(q_ref, k_ref, v_ref, ka_ref, o_ref, o32_ref, lse_ref, m_ref, acc_ref, s_ref, p_ref):
        i, j = pl.program_id(1), pl.program_id(2)
        last = ratio * i + ratio - 1

        @pl.when(j == 0)
        def _():
            m_ref[...] = jnp.full_like(m_ref, NEG_INF)
            acc_ref[...] = jnp.zeros_like(acc_ref)

        def step(diagonal):
            lane = lax.broadcasted_iota(jnp.int32, (1, LANES), 1)
            q2, k2, v2, ka2 = q_ref[...], k_ref[...], v_ref[...], ka_ref[...]
            for h in range(2):
                act, qa, kaug = _head_operands(h, lane, q2, k2, ka2)
                s = _dot_nt(qa, kaug)
                if diagonal:
                    row = lax.broadcasted_iota(jnp.int32, (tq, tk), 0)
                    col = lax.broadcasted_iota(jnp.int32, (tq, tk), 1)
                    s = jnp.where(row - col >= j * tk - i * tq, s, NEG_INF)
                s_ref[h] = s
                m_prev = m_ref[h]
                m_new = jnp.maximum(m_prev, jnp.max(s, axis=-1, keepdims=True))
                m_ref[h] = m_new
                for r in range(tq // rc):
                    rows = pl.ds(r * rc, rc)
                    p_ref[h, rows, :] = jnp.exp(s_ref[h, rows, :] - m_ref[h, rows, :]).astype(BF16)
                vm = jnp.where(act, v2, (lane == _spare_lane(h)).astype(F32)).astype(BF16)
                acc_ref[h] = jnp.exp(m_prev - m_new) * acc_ref[h] + _dot(p_ref[h], vm)

        @pl.when(j < ratio * i)
        def _():
            step(False)

        @pl.when((j >= ratio * i) & (j <= last))
        def _():
            step(True)

        @pl.when(j == last)
        def _():
            lane = lax.broadcasted_iota(jnp.int32, (tq, LANES), 1)
            first = lane < HEAD_DIM
            acc = [acc_ref[0], acc_ref[1]]
            den = [acc[h][:, _spare_lane(h):_spare_lane(h) + 1] for h in range(2)]
            out = jnp.where(first, acc[0] / den[0], acc[1] / den[1])
            o_ref[...] = out.astype(BF16)
            o32_ref[...] = out
            lse_ref[...] = jnp.where(first, m_ref[0] + jnp.log(den[0]), m_ref[1] + jnp.log(den[1]))

    qspec = lambda blk: pl.BlockSpec((tq, LANES), lambda p, i, j: (i, blk + p))
    kspec = lambda blk: pl.BlockSpec((tk, LANES),
                                     lambda p, i, j: (jnp.minimum(j, ratio * i + ratio - 1), blk + p))
    out = pl.BlockSpec((tq, LANES), lambda p, i, j: (i, p))
    return pl.pallas_call(
        body, grid=(n_pair, n_q, n_k),
        in_specs=[qspec(q_blk), kspec(k_blk), kspec(v_blk), kspec(0)],
        out_specs=[out, out, out],
        out_shape=[jax.ShapeDtypeStruct((S, W), BF16), jax.ShapeDtypeStruct((S, W), F32),
                   jax.ShapeDtypeStruct((S, W), F32)],
        scratch_shapes=[pltpu.VMEM((2, tq, 1), F32), pltpu.VMEM((2, tq, LANES), F32),
                        pltpu.VMEM((2, tq, tk), F32), pltpu.VMEM((2, tq, tk), BF16)],
        compiler_params=_params(3), name=name)(proj, proj, proj, ka)


def _attn_bwd(name, proj, ka, o, lse, dmix, q_blk, k_blk, v_blk, do_blk, n_pair):
    S = proj.shape[0]
    tq = _tile(S, 512)
    n_t = S // tq
    W = LANES * n_pair
    scale = 1.0 / math.sqrt(HEAD_DIM)
    rc = _tile(tq, ROW_CHUNK)

    def body(q_ref, k_ref, v_ref, ka_ref, o_ref, lse_ref, do_ref,
             dq_ref, dk_ref, dv_ref, sp_ref, rs_ref, dk_acc, dv_acc, s_ref, dp_ref, p_ref, ds_ref, d_ref):
        j, i = pl.program_id(1), pl.program_id(2)

        @pl.when((j == 0) & (i == 0))
        def _():
            dq_ref[...] = jnp.zeros_like(dq_ref)
            rs_ref[...] = jnp.zeros_like(rs_ref)

        @pl.when(i == 0)
        def _():
            dk_acc[...] = jnp.zeros_like(dk_acc)
            dv_acc[...] = jnp.zeros_like(dv_acc)

        def step(diagonal):
            lane = lax.broadcasted_iota(jnp.int32, (tq, LANES), 1)
            q2, k2, v2, ka2 = q_ref[...], k_ref[...], v_ref[...], ka_ref[...]
            o2, do2 = o_ref[...], do_ref[...]
            dq, heads = [], []
            for h in range(2):
                act, qa, kaug = _head_operands(h, lane, q2, k2, ka2)
                dom = jnp.where(act, do2, 0.0)
                d_ref[h] = jnp.sum(dom * o2, axis=-1, keepdims=True)
                dob = dom.astype(BF16)
                s_ref[h] = _dot_nt(qa, kaug)
                dp_ref[h] = _dot_nt(dob, jnp.where(act, v2, 0.0).astype(BF16))
                heads.append((qa, kaug, dob))
            for h in range(2):
                qa, kaug, dob = heads[h]
                for r in range(tq // rc):
                    rows = pl.ds(r * rc, rc)
                    sc = s_ref[h, rows, :]
                    if diagonal:
                        row = lax.broadcasted_iota(jnp.int32, (rc, tq), 0) + r * rc
                        col = lax.broadcasted_iota(jnp.int32, (rc, tq), 1)
                        sc = jnp.where(row >= col, sc, NEG_INF)
                    p = jnp.exp(sc - lse_ref[rows, :][:, h * HEAD_DIM:h * HEAD_DIM + 1])
                    p_ref[h, rows, :] = p.astype(BF16)
                    ds_ref[h, rows, :] = (p * (dp_ref[h, rows, :] - d_ref[h, rows, :])).astype(BF16)
                dv_acc[...] += _dot_tn(p_ref[h], dob)
                dk_acc[h] += _dot_tn(ds_ref[h], qa)
                dq.append(_dot(ds_ref[h], kaug))
            rows = pl.ds(pl.multiple_of(i * tq, tq), tq)
            first = lane < HEAD_DIM
            dq_ref[rows, :] += jnp.where(first, dq[0], dq[1])
            rs_ref[rows, :] += jnp.where(first, dq[1], dq[0])

        @pl.when(i > j)
        def _():
            step(False)

        @pl.when(i == j)
        def _():
            step(True)

        @pl.when(i == n_t - 1)
        def _():
            lane = lax.broadcasted_iota(jnp.int32, (tq, LANES), 1)
            first = lane < HEAD_DIM
            dk_ref[...] = jnp.where(first, dk_acc[0], dk_acc[1]).astype(BF16)
            sp_ref[...] = jnp.where(first, dk_acc[1], dk_acc[0])
            dv_ref[...] = dv_acc[...].astype(BF16)

        @pl.when((j == n_t - 1) & (i == n_t - 1))
        def _():
            dq_ref[...] = dq_ref[...] * scale

    qspec = lambda blk: pl.BlockSpec((tq, LANES), lambda p, j, i: (jnp.maximum(i, j), blk + p))
    kspec = lambda blk: pl.BlockSpec((tq, LANES), lambda p, j, i: (j, blk + p))
    kout = pl.BlockSpec((tq, LANES), lambda p, j, i: (j, p))
    qres = pl.BlockSpec((S, LANES), lambda p, j, i: (0, p))
    return pl.pallas_call(
        body, grid=(n_pair, n_t, n_t),
        in_specs=[qspec(q_blk), kspec(k_blk), kspec(v_blk), kspec(0),
                  qspec(0), qspec(0), qspec(do_blk)],
        out_specs=[qres, kout, kout, kout, qres],
        out_shape=[jax.ShapeDtypeStruct((S, W), F32), jax.ShapeDtypeStruct((S, W), BF16),
                   jax.ShapeDtypeStruct((S, W), BF16), jax.ShapeDtypeStruct((S, W), F32),
                   jax.ShapeDtypeStruct((S, W), F32)],
        scratch_shapes=[pltpu.VMEM((2, tq, LANES), F32), pltpu.VMEM((tq, LANES), F32),
                        pltpu.VMEM((2, tq, tq), F32), pltpu.VMEM((2, tq, tq), F32),
                        pltpu.VMEM((2, tq, tq), BF16), pltpu.VMEM((2, tq, tq), BF16),
                        pltpu.VMEM((2, tq, 1), F32)],
        compiler_params=_params(3), name=name)(proj, proj, proj, ka, o, lse, dmix)


def _ffn_block_grad(name, act, rows, scale, dep=None):
    nb, S, Fs = act.shape
    D = rows.shape[1]
    tm = _tile(S, REDUCE_ROWS)
    return _mm_tn(name, act, rows, (nb, Fs, D),
                  pl.BlockSpec((1, tm, Fs), lambda j, i: (j, i, 0)),
                  pl.BlockSpec((tm, D), lambda j, i: (i, 0)),
                  pl.BlockSpec((1, Fs, D), lambda j, i: (j, 0, 0)), (nb, S // tm), scale, BF16, dep=dep)


def kernel(x, ffn1_norm, ffn1_w_gate, ffn1_w_up, ffn1_w_down, mix_norm, w_in, fgate_bias, conv_w, conv_b, conv_ln_g, conv_ln_b, w_out, ffn2_norm, ffn2_w_gate, ffn2_w_up, ffn2_w_down, final_norm, loss_target, m_ffn1_norm, m_ffn1_w_gate, m_ffn1_w_up, m_ffn1_w_down, m_mix_norm, m_w_in, m_fgate_bias, m_conv_w, m_conv_b, m_conv_ln_g, m_conv_ln_b, m_w_out, m_ffn2_norm, m_ffn2_w_gate, m_ffn2_w_up, m_ffn2_w_down, m_final_norm, v_ffn1_norm, v_ffn1_w_gate, v_ffn1_w_up, v_ffn1_w_down, v_mix_norm, v_w_in, v_fgate_bias, v_conv_w, v_conv_b, v_conv_ln_g, v_conv_ln_b, v_w_out, v_ffn2_norm, v_ffn2_w_gate, v_ffn2_w_up, v_ffn2_w_down, v_final_norm):
    xs = x[0]
    S, D = xs.shape
    C = conv_b.shape[0]
    n_heads = fgate_bias.shape[0]
    FW = n_heads * HEAD_DIM
    n_pair = n_heads // 2
    MIX = C + FW
    in_shard = w_in.shape[1]
    in_cols = in_shard * N_DEV
    NP = -(-in_cols // 512) * 512
    q_blk, k_blk, v_blk = 2 * C // LANES, (2 * C + FW) // LANES, (2 * C + 2 * FW) // LANES
    f_blk = (2 * C + 3 * FW) // LANES
    assert C % LANES == 0 and FW % LANES == 0 and n_heads % 2 == 0 and n_heads <= LANES
    assert in_cols == 2 * C + 3 * FW + n_heads and MIX == w_out.shape[0] * N_DEV

    vec = lambda a: a.reshape(1, -1)
    bf = lambda a: a.astype(BF16)
    tm = _tile(S, REDUCE_ROWS)

    wgt1, wut1, wd1 = _all_gather_two_level("ag_ffn1", [bf(ffn1_w_gate).T, bf(ffn1_w_up).T, bf(ffn1_w_down)])
    ag, _ = _xchg_start("ag_start", [bf(w_in).T, conv_w, bf(w_out), bf(ffn2_w_gate).T, bf(ffn2_w_up).T,
                                     bf(ffn2_w_down)], False, dep=wd1)
    fbias = jnp.pad(vec(fgate_bias), ((0, 0), (0, LANES - n_heads)))

    x1, h1, G1, U1 = _ffn_fwd("ffn1_fwd", xs, vec(ffn1_norm), wgt1, wut1, wd1)
    win_g, cw_g = _xchg_wait("ag_wait_in", ag[0:2], False, x1)
    wint = jnp.pad(win_g.reshape(in_cols, D), ((0, NP - in_cols), (0, 0)))
    cw = jnp.pad(cw_g.transpose(1, 0, 2).reshape(CONV_WIDTH, C), ((0, CONV_HALO - CONV_WIDTH), (0, 0)))
    proj, h2 = _norm_mm_nt("proj_in", x1, vec(mix_norm), wint, 512)
    yc, y_conv = _conv_fwd("conv_fwd", proj, C, cw, vec(conv_b), vec(conv_ln_g), vec(conv_ln_b))
    ka = _gate_prep("gate_prep", proj, f_blk, fbias, n_pair)
    o, o32, lse = _attn_fwd("attn_fwd", proj, ka, q_blk, k_blk, v_blk, n_pair)
    (wout_g,) = _xchg_wait("ag_wait_out", ag[2:3], False, lse)
    wout = wout_g.reshape(MIX, D)
    mix = jnp.concatenate([y_conv, o], axis=1)
    x2 = _mm_res("proj_out", mix, wout, x1, 512)
    wgt2, wut2, wd2 = _xchg_wait("ag_wait_ffn2", ag[3:6], False, x2)
    x3, h3, G2, U2 = _ffn_fwd("ffn2_fwd", x2, vec(ffn2_norm), wgt2, wut2, wd2)

    dx3, d_final_norm, loss_part = _final("final", x3, vec(final_norm), loss_target[0])
    dG2, dU2, A2 = _ffn_bwd_gate("ffn2_bwd_gate", dx3, G2, U2, wd2)
    dx2, d_ffn2_norm = _ffn_bwd_in("ffn2_bwd_in", dG2, dU2, wgt2, wut2, dx3, x2, vec(ffn2_norm))
    dwd2 = _ffn_block_grad("ffn2_dwd", A2, dx3, 0.5)
    s_d2, tok = _xchg_start("a2a_start_ffn2_wd", [dwd2], True)
    dwg2 = _ffn_block_grad("ffn2_dwg", dG2, h3, 1.0, dep=tok)
    s_g2, tok = _xchg_start("a2a_start_ffn2_wg", [dwg2], True)
    dwu2 = _ffn_block_grad("ffn2_dwu", dU2, h3, 1.0, dep=tok)
    s_u2, tok = _xchg_start("a2a_start_ffn2_wu", [dwu2], True)

    dmix = _mm_k("dmix", dx2, wout, D, True, dep=tok)
    d_wout = _mm_tn("dwout", mix, dx2, (MIX, D),
                    pl.BlockSpec((tm, MIX), lambda j, i: (i, 0)), pl.BlockSpec((tm, 512), lambda j, i: (i, j)),
                    pl.BlockSpec((MIX, 512), lambda j, i: (0, j)), (D // 512, S // tm), 1.0, BF16)
    s_out, tok = _xchg_start("a2a_start_w_out", [d_wout.reshape(N_DEV, MIX // N_DEV, D)], True)
    dag, d_cw, d_cb, d_lg, d_lb = _conv_bwd("conv_bwd", dmix, yc, proj, C, cw, vec(conv_ln_g), vec(conv_ln_b))
    dq, dk, dv, sp, rs = _attn_bwd("attn_bwd", proj, ka, o32, lse, dmix, q_blk, k_blk, v_blk, C // LANES, n_pair)
    df, d_fb = _gate_bwd("gate_bwd", sp, rs, proj, f_blk, fbias, n_pair)
    dproj = jnp.concatenate([dag, bf(dq), dk, dv, df, jnp.zeros((S, NP - f_blk * LANES - LANES), BF16)], axis=1)
    d_wint = _mm_tn("dwin", dproj, h2, (NP, D),
                    pl.BlockSpec((tm, 512), lambda j, i: (i, j)), pl.BlockSpec((tm, D), lambda j, i: (i, 0)),
                    pl.BlockSpec((512, D), lambda j, i: (j, 0)), (NP // 512, S // tm), 1.0, BF16, dep=tok)
    s_in, tok = _xchg_start("a2a_start_w_in", [d_wint[:in_cols].reshape(N_DEV, in_shard, D)], True)
    dx1, d_mix_norm = _mm_k("dh2", dproj, wint, _tile(NP, NP // 4, LANES), False,
                            norm_bwd=(x1, vec(mix_norm), dx2), dep=tok)
    dG1, dU1, A1 = _ffn_bwd_gate("ffn1_bwd_gate", dx1, G1, U1, wd1)
    dwd1 = _ffn_block_grad("ffn1_dwd", A1, dx1, 0.5)
    s_d1, tok = _xchg_start("a2a_start_ffn1_wd", [dwd1], True)
    dwg1 = _ffn_block_grad("ffn1_dwg", dG1, h1, 1.0, dep=tok)
    s_g1, tok = _xchg_start("a2a_start_ffn1_wg", [dwg1], True)
    dwu1 = _ffn_block_grad("ffn1_dwu", dU1, h1, 1.0, dep=tok)
    s_u1, tok = _xchg_start("a2a_start_ffn1_wu", [dwu1], True)
    dx0, d_ffn1_norm = _ffn_bwd_in("ffn1_bwd_in", dG1, dU1, wgt1, wut1, dx1, xs, vec(ffn1_norm) + tok[:1, :1])

    r_d2, r_g2, r_u2, r_out, r_in = _xchg_wait("a2a_wait_a", s_d2 + s_g2 + s_u2 + s_out + s_in, True, dx0)
    tr = lambda a: a.T

    def adam_t(name, recv, w, m, v):
        return tuple(tr(r) for r in _reduce_adam(name, recv, tr(w), tr(m), tr(v)))

    res = {
        "ffn2_w_down": _reduce_adam("adam_ffn2_wd", r_d2, ffn2_w_down, m_ffn2_w_down, v_ffn2_w_down),
        "ffn2_w_gate": adam_t("adam_ffn2_wg", r_g2, ffn2_w_gate, m_ffn2_w_gate, v_ffn2_w_gate),
        "ffn2_w_up": adam_t("adam_ffn2_wu", r_u2, ffn2_w_up, m_ffn2_w_up, v_ffn2_w_up),
        "w_out": _reduce_adam("adam_w_out", r_out, w_out, m_w_out, v_w_out),
        "w_in": adam_t("adam_w_in", r_in, w_in, m_w_in, v_w_in),
    }
    (r_d1,) = _xchg_wait("a2a_wait_d1", s_d1, True, res["w_in"][0])
    res["ffn1_w_down"] = _reduce_adam("adam_ffn1_wd", r_d1, ffn1_w_down, m_ffn1_w_down, v_ffn1_w_down)
    (r_g1,) = _xchg_wait("a2a_wait_g1", s_g1, True, res["ffn1_w_down"][0])
    res["ffn1_w_gate"] = adam_t("adam_ffn1_wg", r_g1, ffn1_w_gate, m_ffn1_w_gate, v_ffn1_w_gate)
    (r_u1,) = _xchg_wait("a2a_wait_u1", s_u1, True, res["ffn1_w_gate"][0])
    res["ffn1_w_up"] = adam_t("adam_ffn1_wu", r_u1, ffn1_w_up, m_ffn1_w_up, v_ffn1_w_up)

    lanes = lambda a: a.reshape(-1, LANES)
    (norm1_g,) = _exchange("ag_norm1", [lanes(d_ffn1_norm)], False, dep=res["ffn1_w_up"][0])
    res["ffn1_norm"] = tuple(r.reshape(D) for r in _reduce_adam(
        "adam_norm1", norm1_g, lanes(ffn1_norm), lanes(m_ffn1_norm), lanes(v_ffn1_norm)))
    rows = lambda a: a.reshape(-1, C)
    pad_row = lambda a: jnp.pad(a.reshape(1, -1), ((0, 0), (0, C - a.size)))
    pieces = [rows(d_mix_norm), rows(d_ffn2_norm), rows(d_final_norm),
              d_cw[:CONV_WIDTH], d_cb, d_lg, d_lb, pad_row(d_fb[0, :n_heads]), pad_row(loss_part[0, :1])]
    pack = jnp.concatenate(pieces, axis=0)
    n_rows = pack.shape[0]
    pack = jnp.pad(pack, ((0, -n_rows % SUBLANES), (0, 0)))
    (pack_g,) = _exchange("ag_small", [pack], False, dep=res["ffn1_norm"][0])
    tot = _reduce_adam("sum_small", pack_g)
    nd = D // C
    g_mix_norm, g_ffn2_norm, g_final_norm = (tot[k * nd:(k + 1) * nd].reshape(D) for k in range(3))
    r0 = 3 * nd
    me = _lin(_mesh_pos())
    cs = C // N_DEV
    g_conv_w = lax.dynamic_slice(tot[r0:r0 + CONV_WIDTH], (0, me * cs), (CONV_WIDTH, cs))
    g_conv_b, g_ln_g, g_ln_b = tot[r0 + CONV_WIDTH], tot[r0 + CONV_WIDTH + 1], tot[r0 + CONV_WIDTH + 2]
    g_fb = tot[r0 + CONV_WIDTH + 3, :n_heads]
    loss = tot[r0 + CONV_WIDTH + 4, 0]

    small = [(g_mix_norm, mix_norm, m_mix_norm, v_mix_norm),
             (g_fb, fgate_bias, m_fgate_bias, v_fgate_bias), (g_conv_w, conv_w, m_conv_w, v_conv_w),
             (g_conv_b, conv_b, m_conv_b, v_conv_b), (g_ln_g, conv_ln_g, m_conv_ln_g, v_conv_ln_g),
             (g_ln_b, conv_ln_b, m_conv_ln_b, v_conv_ln_b), (g_ffn2_norm, ffn2_norm, m_ffn2_norm, v_ffn2_norm),
             (g_final_norm, final_norm, m_final_norm, v_final_norm)]
    sizes = [g.size for g, _, _, _ in small]
    total = sum(sizes)
    padded = -(-total // (SUBLANES * LANES)) * (SUBLANES * LANES)

    def flat_pack(k, fill):
        flat = jnp.concatenate([t[k].reshape(-1) for t in small])
        return jnp.pad(flat, (0, padded - total), constant_values=fill).reshape(padded // LANES, LANES)

    sg, sd, sm, sv = _reduce_adam("adam_small", flat_pack(0, 0.0)[None], flat_pack(1, 0.0), flat_pack(2, 0.0),
                                  flat_pack(3, 1.0))

    def unpack(packed):
        flat = packed.reshape(-1)
        out, off = [], 0
        for (g, _, _, _), n in zip(small, sizes):
            out.append(flat[off:off + n].reshape(g.shape))
            off += n
        return out

    s_g, s_d, s_m, s_v = unpack(sg), unpack(sd), unpack(sm), unpack(sv)

    small_names = ["mix_norm", "fgate_bias", "conv_w", "conv_b", "conv_ln_g", "conv_ln_b",
                   "ffn2_norm", "final_norm"]
    for k, n in enumerate(small_names):
        res[n] = (s_g[k], s_d[k], s_m[k], s_v[k])
    order = ["ffn1_norm", "ffn1_w_gate", "ffn1_w_up", "ffn1_w_down", "mix_norm", "w_in", "fgate_bias",
             "conv_w", "conv_b", "conv_ln_g", "conv_ln_b", "w_out", "ffn2_norm", "ffn2_w_gate", "ffn2_w_up",
             "ffn2_w_down", "final_norm"]
    outs = [loss, dx0[None]]
    for k in range(4):
        outs += [res[n][k] for n in order]
    return tuple(outs)
```

```python
import math

import jax
import jax.numpy as jnp
from jax import lax
from jax.experimental import pallas as pl
from jax.experimental.pallas import tpu as pltpu

F32 = jnp.float32
BF16 = jnp.bfloat16

N_DEV = 8
MESH_ID = pl.DeviceIdType.MESH
HEAD_DIM = 64
CONV_WIDTH = 31
CONV_HALO = 32
NORM_EPS = 1e-6
LN_EPS = 1e-5
NEG_INF = -1e30
LANES = 128
SUBLANES = 8
V7X_VMEM_LIMIT = 52 * 1024 * 1024

ADAM_LR = 0.001
ADAM_B1 = 0.9
ADAM_B2 = 0.999
ADAM_EPS = 1e-08
ADAM_WD = 0.01
ADAM_STEP = 10


def _params(n_grid_axes):
    return pltpu.CompilerParams(dimension_semantics=("arbitrary",) * n_grid_axes,
                                vmem_limit_bytes=V7X_VMEM_LIMIT)


def _tile(n, pref, mult=8):
    t = min(pref, n)
    while t >= mult:
        if n % t == 0 and t % mult == 0:
            return t
        t -= mult
    return n


def _dot(a, b):
    return jnp.dot(a, b, preferred_element_type=F32)


def _dot_nt(a, b):
    return lax.dot_general(a, b, (((1,), (1,)), ((), ())), preferred_element_type=F32)


def _dot_tn(a, b):
    return lax.dot_general(a, b, (((0,), (0,)), ((), ())), preferred_element_type=F32)


def _sigmoid(x):
    return 1.0 / (1.0 + jnp.exp(-x))


def _rms_fwd(x, g):
    r = lax.rsqrt(jnp.mean(x * x, axis=-1, keepdims=True) + NORM_EPS)
    return x * r * g


def _rms_bwd(dh, x, g):
    r = lax.rsqrt(jnp.mean(x * x, axis=-1, keepdims=True) + NORM_EPS)
    xh = x * r
    dxh = dh * g
    dx = r * (dxh - xh * jnp.mean(dxh * xh, axis=-1, keepdims=True))
    return dx, jnp.sum(dh * xh, axis=0, keepdims=True)


def _split3(x):
    hi = x.astype(BF16)
    r = x - hi.astype(F32)
    mid = r.astype(BF16)
    lo = (r - mid.astype(F32)).astype(BF16)
    return hi, mid, lo


def _blk(ref):
    return ref[0] if len(ref.shape) == 3 else ref[...]


_DEP_SPEC = pl.BlockSpec(memory_space=pl.ANY)


def _mesh_pos():
    return lax.axis_index("x"), lax.axis_index("y"), lax.axis_index("c")


def _peer(pos, k):
    x, y, c = pos
    return (1 - x if k & 4 else x, 1 - y if k & 2 else y, 1 - c if k & 1 else c)


def _lin(pos):
    x, y, c = pos
    return 4 * x + 2 * y + c


def _remote_copy(src, land, send_sems, recv_sems, pos, k, all_to_all):
    peer = _peer(pos, k)
    return pltpu.make_async_remote_copy(
        src_ref=src.at[_lin(peer)] if all_to_all else src, dst_ref=land.at[_lin(pos)],
        send_sem=send_sems.at[k - 1], recv_sem=recv_sems.at[k - 1],
        device_id=peer, device_id_type=MESH_ID)


def _landing_shape(a, all_to_all):
    return a.shape if all_to_all else (N_DEV,) + a.shape


def _exchange(name, arrays, all_to_all, dep=None):
    n = len(arrays)
    deps = [] if dep is None else [dep]
    out_shapes = [jax.ShapeDtypeStruct(_landing_shape(a, all_to_all), a.dtype) for a in arrays]

    def body(*refs):
        ins, outs = refs[:n], refs[n + len(deps):2 * n + len(deps)]
        send_sems, recv_sems, local_sems = refs[2 * n + len(deps):]
        pos = _mesh_pos()
        me = _lin(pos)
        local = []
        for a in range(n):
            src = ins[a].at[me] if all_to_all else ins[a]
            cp = pltpu.make_async_copy(src, outs[a].at[me], local_sems.at[a])
            cp.start()
            local.append(cp)
        remote = [_remote_copy(ins[a], outs[a], send_sems.at[a], recv_sems.at[a], pos, k, all_to_all)
                  for a in range(n) for k in range(1, N_DEV)]
        for cp in remote:
            cp.start()
        for cp in remote:
            cp.wait()
        for cp in local:
            cp.wait()

    any_spec = pl.BlockSpec(memory_space=pl.ANY)
    return pl.pallas_call(
        body, out_shape=out_shapes, in_specs=[any_spec] * (n + len(deps)), out_specs=[any_spec] * n,
        scratch_shapes=[pltpu.SemaphoreType.DMA((n, N_DEV - 1)),
                        pltpu.SemaphoreType.DMA((n, N_DEV - 1)),
                        pltpu.SemaphoreType.DMA((n,))],
        name=name)(*arrays, *deps)


def _all_gather_two_level(name, shards):
    n = len(shards)
    out_shapes = [jax.ShapeDtypeStruct((N_DEV,) + a.shape, a.dtype) for a in shards]

    def body(*refs):
        ins, outs = refs[:n], refs[n:2 * n]
        send_sems, recv_sems, local_sems = refs[2 * n:]
        x, y, c = pos = _mesh_pos()
        sibling = (x, y, 1 - c)
        chips = [(1 - x, y), (x, 1 - y), (1 - x, 1 - y)]

        def copy(a, k, block, to, src=None):
            slot = outs[a].at[_lin(block)]
            return pltpu.make_async_remote_copy(
                src_ref=slot if src is None else src, dst_ref=slot,
                send_sem=send_sems.at[a, k], recv_sem=recv_sems.at[a, k],
                device_id=to, device_id_type=MESH_ID)

        local = [pltpu.make_async_copy(ins[a], outs[a].at[_lin(pos)], local_sems.at[a]) for a in range(n)]
        first = [copy(a, 1 + j, pos, (*chip, c), src=ins[a]) for j, chip in enumerate(chips) for a in range(n)]
        first += [copy(a, 0, pos, sibling, src=ins[a]) for a in range(n)]
        for cp in first + local:
            cp.start()
        passed = []
        for j, chip in enumerate(chips):
            for a in range(n):
                copy(a, 1 + j, (*chip, c), pos).wait_recv()
                cp = copy(a, 4 + j, (*chip, c), sibling)
                cp.start()
                passed.append(cp)
        for a in range(n):
            copy(a, 0, sibling, pos).wait_recv()
        for j, chip in enumerate(chips):
            for a in range(n):
                copy(a, 4 + j, (*chip, 1 - c), pos).wait_recv()
        for cp in first + passed:
            cp.wait_send()
        for cp in local:
            cp.wait()

    any_spec = pl.BlockSpec(memory_space=pl.ANY)
    return pl.pallas_call(
        body, out_shape=out_shapes, in_specs=[any_spec] * n, out_specs=[any_spec] * n,
        scratch_shapes=[pltpu.SemaphoreType.DMA((n, N_DEV - 1)),
                        pltpu.SemaphoreType.DMA((n, N_DEV - 1)),
                        pltpu.SemaphoreType.DMA((n,))],
        name=name)(*shards)


_HBM_SPEC = pl.BlockSpec(memory_space=pltpu.HBM)
_SEM_SPEC = pl.BlockSpec(memory_space=pltpu.SEMAPHORE)
_SIDE_EFFECT = pltpu.SideEffectType.DATAFLOW_SIDE_EFFECTING


def _xchg_start(name, arrays, all_to_all, dep=None):
    n = len(arrays)
    me = _lin(_mesh_pos())
    lands = [lax.dynamic_update_index_in_dim(
        lax.empty(_landing_shape(a, all_to_all), a.dtype),
        lax.dynamic_index_in_dim(a, me, 0, keepdims=False) if all_to_all else a, me, 0) for a in arrays]

    deps = [] if dep is None else [dep]

    def body(*refs):
        srcs, lnds = refs[:n], refs[n:2 * n]
        outs = refs[2 * n + len(deps):]
        send, recv = outs[:n], outs[n:2 * n]
        token = outs[4 * n]
        pos = _mesh_pos()
        for a in range(n):
            for k in range(1, N_DEV):
                _remote_copy(srcs[a], lnds[a], send[a], recv[a], pos, k, all_to_all).start()
        token[...] = jnp.zeros_like(token)

    hbm = lambda a: pltpu.HBM(a.shape, a.dtype)
    sems = [pltpu.SemaphoreType.DMA((N_DEV - 1,))] * (2 * n)
    res = pl.pallas_call(
        body, name=name,
        out_shape=sems + [hbm(a) for a in arrays] + [hbm(l) for l in lands]
        + [jax.ShapeDtypeStruct((SUBLANES, LANES), F32)],
        in_specs=[_HBM_SPEC] * (2 * n) + [_DEP_SPEC] * len(deps),
        out_specs=[_SEM_SPEC] * (2 * n) + [_HBM_SPEC] * (2 * n) + [pl.BlockSpec(memory_space=pltpu.VMEM)],
        input_output_aliases={a: 2 * n + a for a in range(2 * n)},
        compiler_params=pltpu.CompilerParams(has_side_effects=_SIDE_EFFECT),
    )(*[pltpu.with_memory_space_constraint(a, pltpu.HBM) for a in list(arrays) + lands], *deps)
    per_array = [(res[a], res[n + a], res[2 * n + a], res[3 * n + a]) for a in range(n)]
    return per_array, res[4 * n]


def _xchg_wait(name, started, all_to_all, after):
    n = len(started)

    def body(*refs):
        srcs, lnds, send, recv = refs[:n], refs[n:2 * n], refs[2 * n:3 * n], refs[3 * n:4 * n]
        pos = _mesh_pos()
        for a in range(n):
            for k in range(1, N_DEV):
                cp = _remote_copy(srcs[a], lnds[a], send[a], recv[a], pos, k, all_to_all)
                cp.wait_send()
                cp.wait_recv()

    hbm = lambda a: pltpu.HBM(a.shape, a.dtype)
    srcs = [s[2] for s in started]
    lands = [s[3] for s in started]
    res = pl.pallas_call(
        body, name=name,
        out_shape=[hbm(a) for a in srcs + lands],
        in_specs=[_HBM_SPEC] * (2 * n) + [_SEM_SPEC] * (2 * n) + [_DEP_SPEC],
        out_specs=[_HBM_SPEC] * (2 * n),
        input_output_aliases={a: a for a in range(2 * n)},
        compiler_params=pltpu.CompilerParams(has_side_effects=_SIDE_EFFECT),
    )(*srcs, *lands, *[s[0] for s in started], *[s[1] for s in started], after)
    return list(res[n:])


def _reduce_adam(name, parts, w=None, m=None, v=None):
    n, R, C = parts.shape
    tr = _tile(R, 256)
    do_adam = w is not None
    bc1 = 1.0 - ADAM_B1 ** ADAM_STEP
    bc2 = 1.0 - ADAM_B2 ** ADAM_STEP

    def body(*refs):
        p_ref = refs[0]
        g = p_ref[0].astype(F32)
        for d in range(1, n):
            g = g + p_ref[d].astype(F32)
        if not do_adam:
            refs[1][...] = g
            return
        w_ref, m_ref, v_ref, g_ref, d_ref, nm_ref, nv_ref = refs[1:]
        g_ref[...] = g
        nm = ADAM_B1 * m_ref[...] + (1.0 - ADAM_B1) * g
        nv = ADAM_B2 * v_ref[...] + (1.0 - ADAM_B2) * (g * g)
        m_hat = nm / bc1
        v_hat = nv / bc2
        d_ref[...] = -ADAM_LR * (m_hat / (jnp.sqrt(v_hat) + ADAM_EPS) + ADAM_WD * w_ref[...])
        nm_ref[...] = nm
        nv_ref[...] = nv

    tc = _tile(C, 512, LANES) if tr == R and R > 256 else C
    row = pl.BlockSpec((tr, tc), lambda i, j: (i, j))
    part = pl.BlockSpec((n, tr, tc), lambda i, j: (0, i, j))
    shard = jax.ShapeDtypeStruct((R, C), F32)
    grid = (R // tr, C // tc)
    if do_adam:
        return pl.pallas_call(body, grid=grid, in_specs=[part, row, row, row],
                              out_specs=[row] * 4, out_shape=[shard] * 4,
                              compiler_params=_params(2), name=name)(parts, w, m, v)
    return pl.pallas_call(body, grid=grid, in_specs=[part], out_specs=row, out_shape=shard,
                          compiler_params=_params(2), name=name)(parts)


def _ffn_fwd(name, x, gain, wgt, wut, wd):
    S, D = x.shape
    nb, Fs, _ = wgt.shape
    tm = _tile(S, 512)

    def body(x_ref, g_ref, wg_ref, wu_ref, wd_ref, xo_ref, h_ref, G_ref, U_ref, acc_ref):
        j = pl.program_id(1)

        @pl.when(j == 0)
        def _():
            h_ref[...] = _rms_fwd(x_ref[...], g_ref[...]).astype(BF16)
            acc_ref[...] = jnp.zeros_like(acc_ref)

        h = h_ref[...]
        G = _dot_nt(h, wg_ref[0])
        U = _dot_nt(h, wu_ref[0])
        G_ref[0] = G
        U_ref[0] = U
        a = G * _sigmoid(G) * U
        acc_ref[...] += _dot(a.astype(BF16), wd_ref[0])

        @pl.when(j == nb - 1)
        def _():
            xo_ref[...] = x_ref[...] + 0.5 * acc_ref[...]

    row = pl.BlockSpec((tm, D), lambda i, j: (i, 0))
    act = pl.BlockSpec((1, tm, Fs), lambda i, j: (j, i, 0))
    wblk = pl.BlockSpec((1, Fs, D), lambda i, j: (j, 0, 0))
    return pl.pallas_call(
        body, grid=(S // tm, nb),
        in_specs=[row, pl.BlockSpec((1, D), lambda i, j: (0, 0)), wblk, wblk, wblk],
        out_specs=[row, row, act, act],
        out_shape=[jax.ShapeDtypeStruct((S, D), F32), jax.ShapeDtypeStruct((S, D), BF16),
                   jax.ShapeDtypeStruct((nb, S, Fs), F32), jax.ShapeDtypeStruct((nb, S, Fs), F32)],
        scratch_shapes=[pltpu.VMEM((tm, D), F32)],
        compiler_params=_params(2), name=name)(x, gain, wgt, wut, wd)


def _ffn_up(name, x, gain, wgt, wut):
    S, D = x.shape
    nb, Fs, _ = wgt.shape
    tm = _tile(S, 512)

    def body(x_ref, g_ref, wg_ref, wu_ref, h_ref, G_ref, U_ref, A_ref):
        @pl.when(pl.program_id(1) == 0)
        def _():
            h_ref[...] = _rms_fwd(x_ref[...], g_ref[...]).astype(BF16)

        h = h_ref[...]
        G = _dot_nt(h, wg_ref[0])
        U = _dot_nt(h, wu_ref[0])
        G_ref[0] = G
        U_ref[0] = U
        A_ref[0] = (G * _sigmoid(G) * U).astype(BF16)

    row = pl.BlockSpec((tm, D), lambda i, j: (i, 0))
    act = pl.BlockSpec((1, tm, Fs), lambda i, j: (j, i, 0))
    wblk = pl.BlockSpec((1, Fs, D), lambda i, j: (j, 0, 0))
    return pl.pallas_call(
        body, grid=(S // tm, nb),
        in_specs=[row, pl.BlockSpec((1, D), lambda i, j: (0, 0)), wblk, wblk],
        out_specs=[row, act, act, act],
        out_shape=[jax.ShapeDtypeStruct((S, D), BF16), jax.ShapeDtypeStruct((nb, S, Fs), F32),
                   jax.ShapeDtypeStruct((nb, S, Fs), F32), jax.ShapeDtypeStruct((nb, S, Fs), BF16)],
        compiler_params=_params(2), name=name)(x, gain, wgt, wut)


def _ffn_down(name, x, A, wd):
    S, D = x.shape
    nb, Fs, _ = wd.shape
    tm = _tile(S, 512)

    def body(x_ref, A_ref, wd_ref, xo_ref, acc_ref):
        j = pl.program_id(1)

        @pl.when(j == 0)
        def _():
            acc_ref[...] = jnp.zeros_like(acc_ref)

        acc_ref[...] += _dot(A_ref[0], wd_ref[0])

        @pl.when(j == nb - 1)
        def _():
            xo_ref[...] = x_ref[...] + 0.5 * acc_ref[...]

    row = pl.BlockSpec((tm, D), lambda i, j: (i, 0))
    return pl.pallas_call(
        body, grid=(S // tm, nb),
        in_specs=[row, pl.BlockSpec((1, tm, Fs), lambda i, j: (j, i, 0)),
                  pl.BlockSpec((1, Fs, D), lambda i, j: (j, 0, 0))],
        out_specs=row, out_shape=jax.ShapeDtypeStruct((S, D), F32),
        scratch_shapes=[pltpu.VMEM((tm, D), F32)],
        compiler_params=_params(2), name=name)(x, A, wd)


def _ffn_bwd_gate(name, dxo, G, U, wd):
    S, D = dxo.shape
    nb, Fs, _ = wd.shape
    tm = _tile(S, 512)

    def gate_body(dxo_ref, G_ref, U_ref, wd_ref, dG_ref, dU_ref, A_ref, dxb_ref):
        @pl.when(pl.program_id(1) == 0)
        def _():
            dxb_ref[...] = dxo_ref[...].astype(BF16)

        dA = 0.5 * _dot_nt(dxb_ref[...], wd_ref[0])
        Gv = G_ref[0]
        Uv = U_ref[0]
        sg = _sigmoid(Gv)
        sl = Gv * sg
        dG_ref[0] = (dA * Uv * (sg * (1.0 + Gv * (1.0 - sg)))).astype(BF16)
        dU_ref[0] = (dA * sl).astype(BF16)
        A_ref[0] = (sl * Uv).astype(BF16)

    row = pl.BlockSpec((tm, D), lambda i, j: (i, 0))
    act = pl.BlockSpec((1, tm, Fs), lambda i, j: (j, i, 0))
    wblk = pl.BlockSpec((1, Fs, D), lambda i, j: (j, 0, 0))
    act_shape = jax.ShapeDtypeStruct((nb, S, Fs), BF16)
    return pl.pallas_call(
        gate_body, grid=(S // tm, nb), in_specs=[row, act, act, wblk], out_specs=[act, act, act],
        out_shape=[act_shape, act_shape, act_shape],
        scratch_shapes=[pltpu.VMEM((tm, D), BF16)],
        compiler_params=_params(2), name=name)(dxo, G, U, wd)


def _ffn_bwd_in(name, dG, dU, wgt, wut, dxo, x_in, gain):
    S, D = x_in.shape
    nb, Fs, _ = wgt.shape
    tm = _tile(S, 512)
    rows_per_chunk = _tile(tm, 128)

    def in_body(dG_ref, dU_ref, wg_ref, wu_ref, dxo_ref, x_ref, g_ref, dx_ref, dgain_ref, acc_ref):
        i, j = pl.program_id(0), pl.program_id(1)

        @pl.when(j == 0)
        def _():
            acc_ref[...] = jnp.zeros_like(acc_ref)

        @pl.when((i == 0) & (j == 0))
        def _():
            dgain_ref[...] = jnp.zeros_like(dgain_ref)

        acc_ref[...] += _dot(dG_ref[0], wg_ref[0]) + _dot(dU_ref[0], wu_ref[0])

        @pl.when(j == nb - 1)
        def _():
            def chunk(r, dg_sum):
                rows = pl.ds(pl.multiple_of(r * rows_per_chunk, rows_per_chunk), rows_per_chunk)
                dx, dg = _rms_bwd(acc_ref[rows, :], x_ref[rows, :], g_ref[...])
                dx_ref[rows, :] = dxo_ref[rows, :] + dx
                return dg_sum + dg

            dgain_ref[...] += lax.fori_loop(0, tm // rows_per_chunk, chunk, jnp.zeros((1, D), F32))

    row = pl.BlockSpec((tm, D), lambda i, j: (i, 0))
    vec = pl.BlockSpec((1, D), lambda i, j: (0, 0))
    act = pl.BlockSpec((1, tm, Fs), lambda i, j: (j, i, 0))
    wblk = pl.BlockSpec((1, Fs, D), lambda i, j: (j, 0, 0))
    return pl.pallas_call(
        in_body, grid=(S // tm, nb), in_specs=[act, act, wblk, wblk, row, row, vec],
        out_specs=[row, vec],
        out_shape=[jax.ShapeDtypeStruct((S, D), F32), jax.ShapeDtypeStruct((1, D), F32)],
        scratch_shapes=[pltpu.VMEM((tm, D), F32)],
        compiler_params=_params(2), name=name)(dG, dU, wgt, wut, dxo, x_in, gain)


def _mm_tn(name, lhs, rhs, out_shape, lhs_spec, rhs_spec, out_spec, grid, scale, out_dtype, dep=None):
    acc_shape = tuple(out_spec.block_shape[-2:])
    n_red = grid[-1]

    def body(l_ref, r_ref, *rest):
        o_ref, acc_ref = rest[-2:]
        i = pl.program_id(len(grid) - 1)

        @pl.when(i == 0)
        def _():
            acc_ref[...] = jnp.zeros_like(acc_ref)

        acc_ref[...] += _dot_tn(_blk(l_ref).astype(BF16), _blk(r_ref).astype(BF16))

        @pl.when(i == n_red - 1)
        def _():
            res = (scale * acc_ref[...]).astype(out_dtype)
            if len(o_ref.shape) == 3:
                o_ref[0] = res
            else:
                o_ref[...] = res

    deps = [] if dep is None else [dep]
    return pl.pallas_call(
        body, grid=grid, in_specs=[lhs_spec, rhs_spec] + [_DEP_SPEC] * len(deps), out_specs=out_spec,
        out_shape=jax.ShapeDtypeStruct(out_shape, out_dtype),
        scratch_shapes=[pltpu.VMEM(acc_shape, F32)],
        compiler_params=_params(len(grid)), name=name)(lhs, rhs, *deps)


def _norm_mm_nt(name, x, gain, wt, tn):
    S, D = x.shape
    N = wt.shape[0]
    tm = _tile(S, 1024)

    def body(x_ref, g_ref, w_ref, o_ref, h_ref):
        @pl.when(pl.program_id(1) == 0)
        def _():
            h_ref[...] = _rms_fwd(x_ref[...], g_ref[...]).astype(BF16)

        o_ref[...] = _dot_nt(h_ref[...], w_ref[...])

    row = pl.BlockSpec((tm, D), lambda i, j: (i, 0))
    return pl.pallas_call(
        body, grid=(S // tm, N // tn),
        in_specs=[row, pl.BlockSpec((1, D), lambda i, j: (0, 0)),
                  pl.BlockSpec((tn, D), lambda i, j: (j, 0))],
        out_specs=[pl.BlockSpec((tm, tn), lambda i, j: (i, j)), row],
        out_shape=[jax.ShapeDtypeStruct((S, N), F32), jax.ShapeDtypeStruct((S, D), BF16)],
        compiler_params=_params(2), name=name)(x, gain, wt)


def _mm_res(name, a, w, res, tn):
    S, K = a.shape
    N = w.shape[1]
    tm = _tile(S, 512)

    def body(a_ref, w_ref, r_ref, o_ref):
        o_ref[...] = r_ref[...] + _dot(a_ref[...], w_ref[...])

    tile = pl.BlockSpec((tm, tn), lambda i, j: (i, j))
    return pl.pallas_call(
        body, grid=(S // tm, N // tn),
        in_specs=[pl.BlockSpec((tm, K), lambda i, j: (i, 0)),
                  pl.BlockSpec((K, tn), lambda i, j: (0, j)), tile],
        out_specs=tile, out_shape=jax.ShapeDtypeStruct((S, N), F32),
        compiler_params=_params(2), name=name)(a, w, res)


def _mm_k(name, a, b, tk, transpose_b, norm_bwd=None, dep=None):
    S, K = a.shape
    N = b.shape[0] if transpose_b else b.shape[1]
    tm = _tile(S, 512)
    rows_per_chunk = _tile(tm, 128)
    nk = K // tk
    n_extra = 0 if norm_bwd is None else 3
    deps = [] if dep is None else [dep]

    def body(*refs):
        a_ref, b_ref = refs[:2]
        outs = refs[2 + n_extra + len(deps):]
        acc_ref = outs[-1]
        i, k = pl.program_id(0), pl.program_id(1)

        @pl.when(k == 0)
        def _():
            acc_ref[...] = jnp.zeros_like(acc_ref)

        av = a_ref[...].astype(BF16)
        acc_ref[...] += _dot_nt(av, b_ref[...]) if transpose_b else _dot(av, b_ref[...])

        if norm_bwd is None:
            @pl.when(k == nk - 1)
            def _():
                outs[0][...] = acc_ref[...]
        else:
            x_ref, g_ref, dres_ref = refs[2:5]
            o_ref, dgain_ref = outs[:2]

            @pl.when((i == 0) & (k == 0))
            def _():
                dgain_ref[...] = jnp.zeros_like(dgain_ref)

            @pl.when(k == nk - 1)
            def _():
                def chunk(r, dg_sum):
                    rows = pl.ds(pl.multiple_of(r * rows_per_chunk, rows_per_chunk), rows_per_chunk)
                    dx, dg = _rms_bwd(acc_ref[rows, :], x_ref[rows, :], g_ref[...])
                    o_ref[rows, :] = dres_ref[rows, :] + dx
                    return dg_sum + dg

                dgain_ref[...] += lax.fori_loop(0, tm // rows_per_chunk, chunk, jnp.zeros((1, N), F32))

    a_spec = pl.BlockSpec((tm, tk), lambda i, k: (i, k))
    b_spec = (pl.BlockSpec((N, tk), lambda i, k: (0, k)) if transpose_b
              else pl.BlockSpec((tk, N), lambda i, k: (k, 0)))
    row = pl.BlockSpec((tm, N), lambda i, k: (i, 0))
    vec = pl.BlockSpec((1, N), lambda i, k: (0, 0))
    out = jax.ShapeDtypeStruct((S, N), F32)
    scratch = [pltpu.VMEM((tm, N), F32)]
    dep_specs = [_DEP_SPEC] * len(deps)
    if norm_bwd is None:
        return pl.pallas_call(body, grid=(S // tm, nk), in_specs=[a_spec, b_spec] + dep_specs,
                              out_specs=row, out_shape=out, scratch_shapes=scratch,
                              compiler_params=_params(2), name=name)(a, b, *deps)
    x_in, gain, dres = norm_bwd
    return pl.pallas_call(body, grid=(S // tm, nk), in_specs=[a_spec, b_spec, row, vec, row] + dep_specs,
                          out_specs=[row, vec],
                          out_shape=[out, jax.ShapeDtypeStruct((1, N), F32)],
                          scratch_shapes=scratch,
                          compiler_params=_params(2), name=name)(a, b, x_in, gain, dres, *deps)


def _final(name, x, gain, target):
    S, D = x.shape
    tm = _tile(S, 512)

    def body(x_ref, g_ref, t_ref, dx_ref, dgain_ref, loss_ref):
        @pl.when(pl.program_id(0) == 0)
        def _():
            dgain_ref[...] = jnp.zeros_like(dgain_ref)
            loss_ref[...] = jnp.zeros_like(loss_ref)

        xv = x_ref[...]
        err = _rms_fwd(xv, g_ref[...]) - t_ref[...]
        per_tok = jnp.mean(err * err, axis=-1, keepdims=True)
        loss_ref[...] += 0.5 * jnp.sum(per_tok, axis=0, keepdims=True)
        dx, dg = _rms_bwd(err * (1.0 / D), xv, g_ref[...])
        dx_ref[...] = dx
        dgain_ref[...] += dg

    row = pl.BlockSpec((tm, D), lambda i: (i, 0))
    vec = pl.BlockSpec((1, D), lambda i: (0, 0))
    return pl.pallas_call(
        body, grid=(S // tm,), in_specs=[row, vec, row],
        out_specs=[row, vec, pl.BlockSpec((1, LANES), lambda i: (0, 0))],
        out_shape=[jax.ShapeDtypeStruct((S, D), F32), jax.ShapeDtypeStruct((1, D), F32),
                   jax.ShapeDtypeStruct((1, LANES), F32)],
        compiler_params=_params(1), name=name)(x, gain, target)


def _conv_tiles(S):
    ts = _tile(S, 256, CONV_HALO)
    return ts, ts // CONV_HALO


def _ln_stats(yc):
    mu = jnp.mean(yc, axis=-1, keepdims=True)
    d = yc - mu
    rs = lax.rsqrt(jnp.mean(d * d, axis=-1, keepdims=True) + LN_EPS)
    return d * rs, rs


N_PHASE = SUBLANES
PHASE_ROWS = CONV_HALO - SUBLANES


def _shifted_copies(buf, shifted, ts):
    for p in range(1, N_PHASE):
        shifted[p - 1] = buf[pl.ds(p, ts + PHASE_ROWS), :]


def _tap(buf, shifted, offset, rows0, n_rows, cols):
    a, p = divmod(offset, N_PHASE)
    rows = pl.ds(a * N_PHASE + rows0, n_rows)
    return buf[rows, cols] if p == 0 else shifted[p - 1, rows, cols]


def _conv_fwd(name, proj, C, cw, cb, lg, lb):
    S = proj.shape[0]
    ts, hb = _conv_tiles(S)
    rb = _tile(ts, 128)
    first = CONV_HALO - CONV_WIDTH + 1

    def body(a_ref, g_ref, ah_ref, gh_ref, cw_ref, cb_ref, lg_ref, lb_ref, yc_ref, y_ref, ubuf, ushift):
        i = pl.program_id(0)
        uh = ah_ref[...] * _sigmoid(gh_ref[...])
        ubuf[pl.ds(0, CONV_HALO), :] = jnp.where(i > 0, uh, 0.0)
        ubuf[pl.ds(CONV_HALO, ts), :] = a_ref[...] * _sigmoid(g_ref[...])
        _shifted_copies(ubuf, ushift, ts)

        def col_block(c, carry):
            cols = pl.ds(pl.multiple_of(c * LANES, LANES), LANES)
            for r in range(ts // rb):
                acc = jnp.zeros((rb, LANES), F32)
                for k in range(CONV_WIDTH):
                    acc = acc + cw_ref[pl.ds(k, 1), cols] * _tap(ubuf, ushift, first + k, r * rb, rb, cols)
                yc_ref[pl.ds(r * rb, rb), cols] = acc + cb_ref[:, cols]
            return carry

        lax.fori_loop(0, C // LANES, col_block, 0)
        yn, _ = _ln_stats(yc_ref[...])
        z = yn * lg_ref[...] + lb_ref[...]
        y_ref[...] = (z * _sigmoid(z)).astype(BF16)

    main = lambda col: pl.BlockSpec((ts, C), lambda i: (i, col))
    halo = lambda col: pl.BlockSpec((CONV_HALO, C), lambda i: (jnp.maximum(i * hb - 1, 0), col))
    vec = pl.BlockSpec((1, C), lambda i: (0, 0))
    return pl.pallas_call(
        body, grid=(S // ts,),
        in_specs=[main(0), main(1), halo(0), halo(1),
                  pl.BlockSpec((CONV_HALO, C), lambda i: (0, 0)), vec, vec, vec],
        out_specs=[pl.BlockSpec((ts, C), lambda i: (i, 0))] * 2,
        out_shape=[jax.ShapeDtypeStruct((S, C), F32), jax.ShapeDtypeStruct((S, C), BF16)],
        scratch_shapes=[pltpu.VMEM((ts + CONV_HALO, C), F32),
                        pltpu.VMEM((N_PHASE - 1, ts + PHASE_ROWS, C), F32)],
        compiler_params=_params(1), name=name)(proj, proj, proj, proj, cw, cb, lg, lb)


def _conv_bwd(name, dmix, yc, proj, C, cw, lg, lb):
    S = proj.shape[0]
    ts, hb = _conv_tiles(S)
    n_t = S // ts
    rb = _tile(ts, 128)
    first = CONV_HALO - CONV_WIDTH + 1

    def body(dy_ref, yc_ref, dyh_ref, ych_ref, a_ref, g_ref, ah_ref, gh_ref, cw_ref, lg_ref, lb_ref,
             dag_ref, dcw_ref, dcb_ref, dlg_ref, dlb_ref, ubuf, dbuf, ushift, dshift, du_ref):
        i = pl.program_id(0)

        @pl.when(i == 0)
        def _():
            dcw_ref[...] = jnp.zeros_like(dcw_ref)
            dcb_ref[...] = jnp.zeros_like(dcb_ref)
            dlg_ref[...] = jnp.zeros_like(dlg_ref)
            dlb_ref[...] = jnp.zeros_like(dlb_ref)

        def ln_bwd(dy, ycv):
            yn, rs = _ln_stats(ycv)
            z = yn * lg_ref[...] + lb_ref[...]
            sg = _sigmoid(z)
            dz = dy * (sg * (1.0 + z * (1.0 - sg)))
            dyn = dz * lg_ref[...]
            dyc = rs * (dyn - jnp.mean(dyn, axis=-1, keepdims=True)
                        - yn * jnp.mean(dyn * yn, axis=-1, keepdims=True))
            return dyc, dz, yn

        dyc, dz, yn = ln_bwd(dy_ref[...], yc_ref[...])
        dlg_ref[...] += jnp.sum(dz * yn, axis=0, keepdims=True)
        dlb_ref[...] += jnp.sum(dz, axis=0, keepdims=True)
        dcb_ref[...] += jnp.sum(dyc, axis=0, keepdims=True)
        dych, _, _ = ln_bwd(dyh_ref[...], ych_ref[...])
        dbuf[pl.ds(0, ts), :] = dyc
        dbuf[pl.ds(ts, CONV_HALO), :] = jnp.where(i < n_t - 1, dych, 0.0)

        uh = ah_ref[...] * _sigmoid(gh_ref[...])
        ubuf[pl.ds(0, CONV_HALO), :] = jnp.where(i > 0, uh, 0.0)
        ubuf[pl.ds(CONV_HALO, ts), :] = a_ref[...] * _sigmoid(g_ref[...])
        _shifted_copies(ubuf, ushift, ts)
        _shifted_copies(dbuf, dshift, ts)

        def col_block(c, carry):
            cols = pl.ds(pl.multiple_of(c * LANES, LANES), LANES)
            dcw = [jnp.zeros((1, LANES), F32)] * CONV_WIDTH
            for r in range(ts // rb):
                du = jnp.zeros((rb, LANES), F32)
                dyc_blk = dbuf[pl.ds(r * rb, rb), cols]
                for k in range(CONV_WIDTH):
                    du = du + cw_ref[pl.ds(k, 1), cols] * _tap(dbuf, dshift, CONV_WIDTH - 1 - k, r * rb, rb, cols)
                    tap = _tap(ubuf, ushift, first + k, r * rb, rb, cols)
                    dcw[k] = dcw[k] + jnp.sum(dyc_blk * tap, axis=0, keepdims=True)
                du_ref[pl.ds(r * rb, rb), cols] = du
            for k in range(CONV_WIDTH):
                dcw_ref[pl.ds(k, 1), cols] += dcw[k]
            return carry

        lax.fori_loop(0, C // LANES, col_block, 0)
        du = du_ref[...]
        av = a_ref[...]
        sgm = _sigmoid(g_ref[...])
        dag_ref[:, pl.ds(0, C)] = (du * sgm).astype(BF16)
        dag_ref[:, pl.ds(C, C)] = (du * av * sgm * (1.0 - sgm)).astype(BF16)

    main = lambda col: pl.BlockSpec((ts, C), lambda i: (i, col))
    past = lambda col: pl.BlockSpec((CONV_HALO, C), lambda i: (jnp.maximum(i * hb - 1, 0), col))
    nxt = pl.BlockSpec((CONV_HALO, C), lambda i: (jnp.minimum((i + 1) * hb, n_t * hb - 1), 0))
    vec = pl.BlockSpec((1, C), lambda i: (0, 0))
    full = pl.BlockSpec((CONV_HALO, C), lambda i: (0, 0))
    vshape = jax.ShapeDtypeStruct((1, C), F32)
    buf = pltpu.VMEM((ts + CONV_HALO, C), F32)
    shifted = pltpu.VMEM((N_PHASE - 1, ts + PHASE_ROWS, C), F32)
    return pl.pallas_call(
        body, grid=(n_t,),
        in_specs=[main(0), main(0), nxt, nxt, main(0), main(1), past(0), past(1), full, vec, vec],
        out_specs=[pl.BlockSpec((ts, 2 * C), lambda i: (i, 0)), full, vec, vec, vec],
        out_shape=[jax.ShapeDtypeStruct((S, 2 * C), BF16),
                   jax.ShapeDtypeStruct((CONV_HALO, C), F32), vshape, vshape, vshape],
        scratch_shapes=[buf, buf, shifted, shifted, pltpu.VMEM((ts, C), F32)],
        compiler_params=_params(1),
        name=name)(dmix, yc, dmix, yc, proj, proj, proj, proj, cw, lg, lb)


AUG = 3
ROW_CHUNK = 32
REDUCE_ROWS = 1024


def _gate_prep(name, proj, f_blk, fbias, n_pair):
    S = proj.shape[0]
    ts = _tile(S, 512)
    W = LANES * n_pair

    def body(pf_ref, fb_ref, ka_ref, carry):
        @pl.when(pl.program_id(0) == 0)
        def _():
            carry[...] = jnp.zeros_like(carry)

        f = pf_ref[...] + fb_ref[...]
        logf = jnp.minimum(f, 0.0) - jnp.log(1.0 + jnp.exp(-jnp.abs(f)))
        r = lax.broadcasted_iota(jnp.int32, (ts, ts), 0)
        c = lax.broadcasted_iota(jnp.int32, (ts, ts), 1)
        ltri = (c <= r).astype(BF16)
        hi, mid, lo = _split3(logf)
        cs = _dot(ltri, hi) + _dot(ltri, mid) + _dot(ltri, lo) + carry[...]
        carry[...] = cs[ts - 1:ts, :]
        hh = lax.broadcasted_iota(jnp.int32, (LANES, W), 0)
        ll = lax.broadcasted_iota(jnp.int32, (LANES, W), 1)
        pair, w = ll >> 7, ll & (LANES - 1)
        ka = jnp.zeros((ts, W), F32)
        for p, piece in enumerate(_split3(-cs)):
            e = (((w == HEAD_DIM + p) & (hh == 2 * pair)) | ((w == p) & (hh == 2 * pair + 1)))
            ka = ka + _dot(piece, e.astype(BF16))
        lw = lax.broadcasted_iota(jnp.int32, (1, W), 1) & (LANES - 1)
        ka = ka + ((lw == HEAD_DIM + AUG) | (lw == AUG)).astype(F32)
        ka_ref[...] = ka.astype(BF16)

    return pl.pallas_call(
        body, grid=(S // ts,),
        in_specs=[pl.BlockSpec((ts, LANES), lambda i: (i, f_blk)),
                  pl.BlockSpec((1, LANES), lambda i: (0, 0))],
        out_specs=pl.BlockSpec((ts, W), lambda i: (i, 0)),
        out_shape=jax.ShapeDtypeStruct((S, W), BF16),
        scratch_shapes=[pltpu.VMEM((1, LANES), F32)],
        compiler_params=_params(1), name=name)(proj, fbias)


def _gate_bwd(name, sp, rs, proj, f_blk, fbias, n_pair):
    S = proj.shape[0]
    ts = _tile(S, 512)
    n_t = S // ts
    W = LANES * n_pair

    def body(sp_ref, rs_ref, pf_ref, fb_ref, df_ref, dfb_ref, carry):
        @pl.when(pl.program_id(0) == 0)
        def _():
            carry[...] = jnp.zeros_like(carry)
            dfb_ref[...] = jnp.zeros_like(dfb_ref)

        ll = lax.broadcasted_iota(jnp.int32, (W, LANES), 0)
        hh = lax.broadcasted_iota(jnp.int32, (W, LANES), 1)
        pair, w = ll >> 7, ll & (LANES - 1)
        first, second = hh == 2 * pair, hh == 2 * pair + 1

        def pick(ref, lane_first, lane_second):
            sel = (((w == lane_first) & first) | ((w == lane_second) & second)).astype(BF16)
            hi, mid, lo = _split3(ref[...])
            return _dot(hi, sel) + _dot(mid, sel) + _dot(lo, sel)

        dc = pick(rs_ref, HEAD_DIM + AUG, AUG) - pick(sp_ref, HEAD_DIM, 0)
        r = lax.broadcasted_iota(jnp.int32, (ts, ts), 0)
        c = lax.broadcasted_iota(jnp.int32, (ts, ts), 1)
        utri = (c >= r).astype(BF16)
        hi, mid, lo = _split3(dc)
        dlogf = _dot(utri, hi) + _dot(utri, mid) + _dot(utri, lo) + carry[...]
        carry[...] = dlogf[0:1, :]
        f = pf_ref[...] + fb_ref[...]
        lane = lax.broadcasted_iota(jnp.int32, (ts, LANES), 1)
        df = jnp.where(lane < 2 * n_pair, dlogf * _sigmoid(-f), 0.0)
        df_ref[...] = df.astype(BF16)
        dfb_ref[...] += jnp.sum(df, axis=0, keepdims=True)

    rev = lambda blk: (lambda i: (n_t - 1 - i, blk))
    return pl.pallas_call(
        body, grid=(n_t,),
        in_specs=[pl.BlockSpec((ts, W), rev(0)), pl.BlockSpec((ts, W), rev(0)),
                  pl.BlockSpec((ts, LANES), rev(f_blk)), pl.BlockSpec((1, LANES), lambda i: (0, 0))],
        out_specs=[pl.BlockSpec((ts, LANES), rev(0)), pl.BlockSpec((1, LANES), lambda i: (0, 0))],
        out_shape=[jax.ShapeDtypeStruct((S, LANES), BF16), jax.ShapeDtypeStruct((1, LANES), F32)],
        scratch_shapes=[pltpu.VMEM((1, LANES), F32)],
        compiler_params=_params(1), name=name)(sp, rs, proj, fbias)


def _spare_lane(h):
    return HEAD_DIM if h == 0 else 0


def _head_operands(h, lane, q2, k2, ka2):
    act = (lane < HEAD_DIM) if h == 0 else (lane >= HEAD_DIM)
    base = HEAD_DIM if h == 0 else 0
    ones = ((lane >= base) & (lane < base + AUG)).astype(F32)
    qa = jnp.where(act, q2 * (1.0 / math.sqrt(HEAD_DIM)), ones).astype(BF16)
    ka = jnp.where(act, k2.astype(BF16), ka2)
    return act, qa, ka


def _attn_fwd(name, proj, ka, q_blk, k_blk, v_blk, n_pair):
    S = proj.shape[0]
    tq = _tile(S, 512)
    n_t = S // tq
    W = LANES * n_pair

    rc = _tile(tq, ROW_CHUNK)

    def body(q_ref, k_ref, v_ref, ka_ref, o_ref, o32_ref, lse_ref, m_ref, acc_ref, s_ref, p_ref):
        i, j = pl.program_id(1), pl.program_id(2)

        @pl.when(j == 0)
        def _():
            m_ref[...] = jnp.full_like(m_ref, NEG_INF)
            acc_ref[...] = jnp.zeros_like(acc_ref)

        def step(diagonal):
            lane = lax.broadcasted_iota(jnp.int32, (tq, LANES), 1)
            q2, k2, v2, ka2 = q_ref[...], k_ref[...], v_ref[...], ka_ref[...]
            for h in range(2):
                act, qa, kaug = _head_operands(h, lane, q2, k2, ka2)
                s = _dot_nt(qa, kaug)
                if diagonal:
                    row = lax.broadcasted_iota(jnp.int32, (tq, tq), 0)
                    col = lax.broadcasted_iota(jnp.int32, (tq, tq), 1)
                    s = jnp.where(row >= col, s, NEG_INF)
                s_ref[h] = s
                m_prev = m_ref[h]
                m_new = jnp.maximum(m_prev, jnp.max(s, axis=-1, keepdims=True))
                m_ref[h] = m_new
                for r in range(tq // rc):
                    rows = pl.ds(r * rc, rc)
                    p_ref[h, rows, :] = jnp.exp(s_ref[h, rows, :] - m_ref[h, rows, :]).astype(BF16)
                vm = jnp.where(act, v2, (lane == _spare_lane(h)).astype(F32)).astype(BF16)
                acc_ref[h] = jnp.exp(m_prev - m_new) * acc_ref[h] + _dot(p_ref[h], vm)

        @pl.when(j < i)
        def _():
            step(False)

        @pl.when(j == i)
        def _():
            step(True)
            lane = lax.broadcasted_iota(jnp.int32, (tq, LANES), 1)
            first = lane < HEAD_DIM
            acc = [acc_ref[0], acc_ref[1]]
            den = [acc[h][:, _spare_lane(h):_spare_lane(h) + 1] for h in range(2)]
            out = jnp.where(first, acc[0] / den[0], acc[1] / den[1])
            o_ref[...] = out.astype(BF16)
            o32_ref[...] = out
            lse_ref[...] = jnp.where(first, m_ref[0] + jnp.log(den[0]), m_ref[1] + jnp.log(den[1]))

    qspec = lambda blk: pl.BlockSpec((tq, LANES), lambda p, i, j: (i, blk + p))
    kspec = lambda blk: pl.BlockSpec((tq, LANES), lambda p, i, j: (jnp.minimum(j, i), blk + p))
    out = pl.BlockSpec((tq, LANES), lambda p, i, j: (i, p))
    return pl.pallas_call(
        body, grid=(n_pair, n_t, n_t),
        in_specs=[qspec(q_blk), kspec(k_blk), kspec(v_blk), kspec(0)],
        out_specs=[out, out, out],
        out_shape=[jax.ShapeDtypeStruct((S, W), BF16), jax.ShapeDtypeStruct((S, W), F32),
                   jax.ShapeDtypeStruct((S, W), F32)],
        scratch_shapes=[pltpu.VMEM((2, tq, 1), F32), pltpu.VMEM((2, tq, LANES), F32),
                        pltpu.VMEM((2, tq, tq), F32), pltpu.VMEM((2, tq, tq), BF16)],
        compiler_params=_params(3), name=name)(proj, proj, proj, ka)


def _attn_bwd(name, proj, ka, o, lse, dmix, q_blk, k_blk, v_blk, do_blk, n_pair):
    S = proj.shape[0]
    tq = _tile(S, 512)
    n_t = S // tq
    W = LANES * n_pair
    scale = 1.0 / math.sqrt(HEAD_DIM)
    rc = _tile(tq, ROW_CHUNK)

    def body(q_ref, k_ref, v_ref, ka_ref, o_ref, lse_ref, do_ref,
             dq_ref, dk_ref, dv_ref, sp_ref, rs_ref, dk_acc, dv_acc, s_ref, dp_ref, p_ref, ds_ref, d_ref):
        j, i = pl.program_id(1), pl.program_id(2)

        @pl.when((j == 0) & (i == 0))
        def _():
            dq_ref[...] = jnp.zeros_like(dq_ref)
            rs_ref[...] = jnp.zeros_like(rs_ref)

        @pl.when(i == 0)
        def _():
            dk_acc[...] = jnp.zeros_like(dk_acc)
            dv_acc[...] = jnp.zeros_like(dv_acc)

        def step(diagonal):
            lane = lax.broadcasted_iota(jnp.int32, (tq, LANES), 1)
            q2, k2, v2, ka2 = q_ref[...], k_ref[...], v_ref[...], ka_ref[...]
            o2, do2 = o_ref[...], do_ref[...]
            dq, heads = [], []
            for h in range(2):
                act, qa, kaug = _head_operands(h, lane, q2, k2, ka2)
                dom = jnp.where(act, do2, 0.0)
                d_ref[h] = jnp.sum(dom * o2, axis=-1, keepdims=True)
                dob = dom.astype(BF16)
                s_ref[h] = _dot_nt(qa, kaug)
                dp_ref[h] = _dot_nt(dob, jnp.where(act, v2, 0.0).astype(BF16))
                heads.append((qa, kaug, dob))
            for h in range(2):
                qa, kaug, dob = heads[h]
                for r in range(tq // rc):
                    rows = pl.ds(r * rc, rc)
                    sc = s_ref[h, rows, :]
                    if diagonal:
                        row = lax.broadcasted_iota(jnp.int32, (rc, tq), 0) + r * rc
                        col = lax.broadcasted_iota(jnp.int32, (rc, tq), 1)
                        sc = jnp.where(row >= col, sc, NEG_INF)
                    p = jnp.exp(sc - lse_ref[rows, :][:, h * HEAD_DIM:h * HEAD_DIM + 1])
                    p_ref[h, rows, :] = p.astype(BF16)
                    ds_ref[h, rows, :] = (p * (dp_ref[h, rows, :] - d_ref[h, rows, :])).astype(BF16)
                dv_acc[...] += _dot_tn(p_ref[h], dob)
                dk_acc[h] += _dot_tn(ds_ref[h], qa)
                dq.append(_dot(ds_ref[h], kaug))
            rows = pl.ds(pl.multiple_of(i * tq, tq), tq)
            first = lane < HEAD_DIM
            dq_ref[rows, :] += jnp.where(first, dq[0], dq[1])
            rs_ref[rows, :] += jnp.where(first, dq[1], dq[0])

        @pl.when(i > j)
        def _():
            step(False)

        @pl.when(i == j)
        def _():
            step(True)

        @pl.when(i == n_t - 1)
        def _():
            lane = lax.broadcasted_iota(jnp.int32, (tq, LANES), 1)
            first = lane < HEAD_DIM
            dk_ref[...] = jnp.where(first, dk_acc[0], dk_acc[1]).astype(BF16)
            sp_ref[...] = jnp.where(first, dk_acc[1], dk_acc[0])
            dv_ref[...] = dv_acc[...].astype(BF16)

        @pl.when((j == n_t - 1) & (i == n_t - 1))
        def _():
            dq_ref[...] = dq_ref[...] * scale

    qspec = lambda blk: pl.BlockSpec((tq, LANES), lambda p, j, i: (jnp.maximum(i, j), blk + p))
    kspec = lambda blk: pl.BlockSpec((tq, LANES), lambda p, j, i: (j, blk + p))
    kout = pl.BlockSpec((tq, LANES), lambda p, j, i: (j, p))
    qres = pl.BlockSpec((S, LANES), lambda p, j, i: (0, p))
    return pl.pallas_call(
        body, grid=(n_pair, n_t, n_t),
        in_specs=[qspec(q_blk), kspec(k_blk), kspec(v_blk), kspec(0),
                  qspec(0), qspec(0), qspec(do_blk)],
        out_specs=[qres, kout, kout, kout, qres],
        out_shape=[jax.ShapeDtypeStruct((S, W), F32), jax.ShapeDtypeStruct((S, W), BF16),
                   jax.ShapeDtypeStruct((S, W), BF16), jax.ShapeDtypeStruct((S, W), F32),
                   jax.ShapeDtypeStruct((S, W), F32)],
        scratch_shapes=[pltpu.VMEM((2, tq, LANES), F32), pltpu.VMEM((tq, LANES), F32),
                        pltpu.VMEM((2, tq, tq), F32), pltpu.VMEM((2, tq, tq), F32),
                        pltpu.VMEM((2, tq, tq), BF16), pltpu.VMEM((2, tq, tq), BF16),
                        pltpu.VMEM((2, tq, 1), F32)],
        compiler_params=_params(3), name=name)(proj, proj, proj, ka, o, lse, dmix)


def _ffn_block_grad(name, act, rows, scale, dep=None):
    nb, S, Fs = act.shape
    D = rows.shape[1]
    tm = _tile(S, REDUCE_ROWS)
    return _mm_tn(name, act, rows, (nb, Fs, D),
                  pl.BlockSpec((1, tm, Fs), lambda j, i: (j, i, 0)),
                  pl.BlockSpec((tm, D), lambda j, i: (i, 0)),
                  pl.BlockSpec((1, Fs, D), lambda j, i: (j, 0, 0)), (nb, S // tm), scale, BF16, dep=dep)


def kernel(x, ffn1_norm, ffn1_w_gate, ffn1_w_up, ffn1_w_down, mix_norm, w_in, fgate_bias, conv_w, conv_b, conv_ln_g, conv_ln_b, w_out, ffn2_norm, ffn2_w_gate, ffn2_w_up, ffn2_w_down, final_norm, loss_target, m_ffn1_norm, m_ffn1_w_gate, m_ffn1_w_up, m_ffn1_w_down, m_mix_norm, m_w_in, m_fgate_bias, m_conv_w, m_conv_b, m_conv_ln_g, m_conv_ln_b, m_w_out, m_ffn2_norm, m_ffn2_w_gate, m_ffn2_w_up, m_ffn2_w_down, m_final_norm, v_ffn1_norm, v_ffn1_w_gate, v_ffn1_w_up, v_ffn1_w_down, v_mix_norm, v_w_in, v_fgate_bias, v_conv_w, v_conv_b, v_conv_ln_g, v_conv_ln_b, v_w_out, v_ffn2_norm, v_ffn2_w_gate, v_ffn2_w_up, v_ffn2_w_down, v_final_norm):
    xs = x[0]
    S, D = xs.shape
    C = conv_b.shape[0]
    n_heads = fgate_bias.shape[0]
    FW = n_heads * HEAD_DIM
    n_pair = n_heads // 2
    MIX = C + FW
    in_shard = w_in.shape[1]
    in_cols = in_shard * N_DEV
    NP = -(-in_cols // 512) * 512
    q_blk, k_blk, v_blk = 2 * C // LANES, (2 * C + FW) // LANES, (2 * C + 2 * FW) // LANES
    f_blk = (2 * C + 3 * FW) // LANES
    assert C % LANES == 0 and FW % LANES == 0 and n_heads % 2 == 0 and n_heads <= LANES
    assert in_cols == 2 * C + 3 * FW + n_heads and MIX == w_out.shape[0] * N_DEV

    vec = lambda a: a.reshape(1, -1)
    bf = lambda a: a.astype(BF16)
    tm = _tile(S, REDUCE_ROWS)

    wgt1, wut1 = _all_gather_two_level("ag_ffn1", [bf(ffn1_w_gate).T, bf(ffn1_w_up).T])
    ag, _ = _xchg_start("ag_start", [bf(ffn1_w_down), bf(w_in).T, conv_w, bf(w_out), bf(ffn2_w_gate).T,
                                     bf(ffn2_w_up).T, bf(ffn2_w_down)], False, dep=wut1)
    fbias = jnp.pad(vec(fgate_bias), ((0, 0), (0, LANES - n_heads)))

    h1, G1, U1, A1f = _ffn_up("ffn1_up", xs, vec(ffn1_norm), wgt1, wut1)
    (wd1,) = _xchg_wait("ag_wait_wd1", ag[0:1], False, h1)
    x1 = _ffn_down("ffn1_down", xs, A1f, wd1)
    ag = ag[1:]
    win_g, cw_g = _xchg_wait("ag_wait_in", ag[0:2], False, x1)
    wint = jnp.pad(win_g.reshape(in_cols, D), ((0, NP - in_cols), (0, 0)))
    cw = jnp.pad(cw_g.transpose(1, 0, 2).reshape(CONV_WIDTH, C), ((0, CONV_HALO - CONV_WIDTH), (0, 0)))
    proj, h2 = _norm_mm_nt("proj_in", x1, vec(mix_norm), wint, 512)
    yc, y_conv = _conv_fwd("conv_fwd", proj, C, cw, vec(conv_b), vec(conv_ln_g), vec(conv_ln_b))
    ka = _gate_prep("gate_prep", proj, f_blk, fbias, n_pair)
    o, o32, lse = _attn_fwd("attn_fwd", proj, ka, q_blk, k_blk, v_blk, n_pair)
    (wout_g,) = _xchg_wait("ag_wait_out", ag[2:3], False, lse)
    wout = wout_g.reshape(MIX, D)
    mix = jnp.concatenate([y_conv, o], axis=1)
    x2 = _mm_res("proj_out", mix, wout, x1, 512)
    wgt2, wut2, wd2 = _xchg_wait("ag_wait_ffn2", ag[3:6], False, x2)
    x3, h3, G2, U2 = _ffn_fwd("ffn2_fwd", x2, vec(ffn2_norm), wgt2, wut2, wd2)

    dx3, d_final_norm, loss_part = _final("final", x3, vec(final_norm), loss_target[0])
    dG2, dU2, A2 = _ffn_bwd_gate("ffn2_bwd_gate", dx3, G2, U2, wd2)
    dx2, d_ffn2_norm = _ffn_bwd_in("ffn2_bwd_in", dG2, dU2, wgt2, wut2, dx3, x2, vec(ffn2_norm))
    dwd2 = _ffn_block_grad("ffn2_dwd", A2, dx3, 0.5)
    s_d2, tok = _xchg_start("a2a_start_ffn2_wd", [dwd2], True)
    dwg2 = _ffn_block_grad("ffn2_dwg", dG2, h3, 1.0, dep=tok)
    s_g2, tok = _xchg_start("a2a_start_ffn2_wg", [dwg2], True)
    dwu2 = _ffn_block_grad("ffn2_dwu", dU2, h3, 1.0, dep=tok)
    s_u2, tok = _xchg_start("a2a_start_ffn2_wu", [dwu2], True)

    dmix = _mm_k("dmix", dx2, wout, D, True, dep=tok)
    d_wout = _mm_tn("dwout", mix, dx2, (MIX, D),
                    pl.BlockSpec((tm, MIX), lambda j, i: (i, 0)), pl.BlockSpec((tm, 512), lambda j, i: (i, j)),
                    pl.BlockSpec((MIX, 512), lambda j, i: (0, j)), (D // 512, S // tm), 1.0, BF16)
    s_out, tok = _xchg_start("a2a_start_w_out", [d_wout.reshape(N_DEV, MIX // N_DEV, D)], True)
    dag, d_cw, d_cb, d_lg, d_lb = _conv_bwd("conv_bwd", dmix, yc, proj, C, cw, vec(conv_ln_g), vec(conv_ln_b))
    dq, dk, dv, sp, rs = _attn_bwd("attn_bwd", proj, ka, o32, lse, dmix, q_blk, k_blk, v_blk, C // LANES, n_pair)
    df, d_fb = _gate_bwd("gate_bwd", sp, rs, proj, f_blk, fbias, n_pair)
    dproj = jnp.concatenate([dag, bf(dq), dk, dv, df, jnp.zeros((S, NP - f_blk * LANES - LANES), BF16)], axis=1)
    d_wint = _mm_tn("dwin", dproj, h2, (NP, D),
                    pl.BlockSpec((tm, 512), lambda j, i: (i, j)), pl.BlockSpec((tm, D), lambda j, i: (i, 0)),
                    pl.BlockSpec((512, D), lambda j, i: (j, 0)), (NP // 512, S // tm), 1.0, BF16, dep=tok)
    s_in, tok = _xchg_start("a2a_start_w_in", [d_wint[:in_cols].reshape(N_DEV, in_shard, D)], True)
    dx1, d_mix_norm = _mm_k("dh2", dproj, wint, _tile(NP, NP // 4, LANES), False,
                            norm_bwd=(x1, vec(mix_norm), dx2), dep=tok)
    dG1, dU1, A1 = _ffn_bwd_gate("ffn1_bwd_gate", dx1, G1, U1, wd1)
    dwd1 = _ffn_block_grad("ffn1_dwd", A1, dx1, 0.5)
    s_d1, tok = _xchg_start("a2a_start_ffn1_wd", [dwd1], True)
    dwg1 = _ffn_block_grad("ffn1_dwg", dG1, h1, 1.0, dep=tok)
    s_g1, tok = _xchg_start("a2a_start_ffn1_wg", [dwg1], True)
    dwu1 = _ffn_block_grad("ffn1_dwu", dU1, h1, 1.0, dep=tok)
    s_u1, tok = _xchg_start("a2a_start_ffn1_wu", [dwu1], True)
    dx0, d_ffn1_norm = _ffn_bwd_in("ffn1_bwd_in", dG1, dU1, wgt1, wut1, dx1, xs, vec(ffn1_norm) + tok[:1, :1])

    r_d2, r_g2, r_u2, r_out, r_in = _xchg_wait("a2a_wait_a", s_d2 + s_g2 + s_u2 + s_out + s_in, True, dx0)
    tr = lambda a: a.T

    def adam_t(name, recv, w, m, v):
        return tuple(tr(r) for r in _reduce_adam(name, recv, tr(w), tr(m), tr(v)))

    res = {
        "ffn2_w_down": _reduce_adam("adam_ffn2_wd", r_d2, ffn2_w_down, m_ffn2_w_down, v_ffn2_w_down),
        "ffn2_w_gate": adam_t("adam_ffn2_wg", r_g2, ffn2_w_gate, m_ffn2_w_gate, v_ffn2_w_gate),
        "ffn2_w_up": adam_t("adam_ffn2_wu", r_u2, ffn2_w_up, m_ffn2_w_up, v_ffn2_w_up),
        "w_out": _reduce_adam("adam_w_out", r_out, w_out, m_w_out, v_w_out),
        "w_in": adam_t("adam_w_in", r_in, w_in, m_w_in, v_w_in),
    }
    (r_d1,) = _xchg_wait("a2a_wait_d1", s_d1, True, res["w_in"][0])
    res["ffn1_w_down"] = _reduce_adam("adam_ffn1_wd", r_d1, ffn1_w_down, m_ffn1_w_down, v_ffn1_w_down)
    (r_g1,) = _xchg_wait("a2a_wait_g1", s_g1, True, res["ffn1_w_down"][0])
    res["ffn1_w_gate"] = adam_t("adam_ffn1_wg", r_g1, ffn1_w_gate, m_ffn1_w_gate, v_ffn1_w_gate)
    (r_u1,) = _xchg_wait("a2a_wait_u1", s_u1, True, res["ffn1_w_gate"][0])
    res["ffn1_w_up"] = adam_t("adam_ffn1_wu", r_u1, ffn1_w_up, m_ffn1_w_up, v_ffn1_w_up)

    lanes = lambda a: a.reshape(-1, LANES)
    (norm1_g,) = _exchange("ag_norm1", [lanes(d_ffn1_norm)], False, dep=res["ffn1_w_up"][0])
    res["ffn1_norm"] = tuple(r.reshape(D) for r in _reduce_adam(
        "adam_norm1", norm1_g, lanes(ffn1_norm), lanes(m_ffn1_norm), lanes(v_ffn1_norm)))
    rows = lambda a: a.reshape(-1, C)
    pad_row = lambda a: jnp.pad(a.reshape(1, -1), ((0, 0), (0, C - a.size)))
    pieces = [rows(d_mix_norm), rows(d_ffn2_norm), rows(d_final_norm),
              d_cw[:CONV_WIDTH], d_cb, d_lg, d_lb, pad_row(d_fb[0, :n_heads]), pad_row(loss_part[0, :1])]
    pack = jnp.concatenate(pieces, axis=0)
    n_rows = pack.shape[0]
    pack = jnp.pad(pack, ((0, -n_rows % SUBLANES), (0, 0)))
    (pack_g,) = _exchange("ag_small", [pack], False, dep=res["ffn1_norm"][0])
    tot = _reduce_adam("sum_small", pack_g)
    nd = D // C
    g_mix_norm, g_ffn2_norm, g_final_norm = (tot[k * nd:(k + 1) * nd].reshape(D) for k in range(3))
    r0 = 3 * nd
    me = _lin(_mesh_pos())
    cs = C // N_DEV
    g_conv_w = lax.dynamic_slice(tot[r0:r0 + CONV_WIDTH], (0, me * cs), (CONV_WIDTH, cs))
    g_conv_b, g_ln_g, g_ln_b = tot[r0 + CONV_WIDTH], tot[r0 + CONV_WIDTH + 1], tot[r0 + CONV_WIDTH + 2]
    g_fb = tot[r0 + CONV_WIDTH + 3, :n_heads]
    loss = tot[r0 + CONV_WIDTH + 4, 0]

    small = [(g_mix_norm, mix_norm, m_mix_norm, v_mix_norm),
             (g_fb, fgate_bias, m_fgate_bias, v_fgate_bias), (g_conv_w, conv_w, m_conv_w, v_conv_w),
             (g_conv_b, conv_b, m_conv_b, v_conv_b), (g_ln_g, conv_ln_g, m_conv_ln_g, v_conv_ln_g),
             (g_ln_b, conv_ln_b, m_conv_ln_b, v_conv_ln_b), (g_ffn2_norm, ffn2_norm, m_ffn2_norm, v_ffn2_norm),
             (g_final_norm, final_norm, m_final_norm, v_final_norm)]
    sizes = [g.size for g, _, _, _ in small]
    total = sum(sizes)
    padded = -(-total // (SUBLANES * LANES)) * (SUBLANES * LANES)

    def flat_pack(k, fill):
        flat = jnp.concatenate([t[k].reshape(-1) for t in small])
        return jnp.pad(flat, (0, padded - total), constant_values=fill).reshape(padded // LANES, LANES)

    sg, sd, sm, sv = _reduce_adam("adam_small", flat_pack(0, 0.0)[None], flat_pack(1, 0.0), flat_pack(2, 0.0),
                                  flat_pack(3, 1.0))

    def unpack(packed):
        flat = packed.reshape(-1)
        out, off = [], 0
        for (g, _, _, _), n in zip(small, sizes):
            out.append(flat[off:off + n].reshape(g.shape))
            off += n
        return out

    s_g, s_d, s_m, s_v = unpack(sg), unpack(sd), unpack(sm), unpack(sv)

    small_names = ["mix_norm", "fgate_bias", "conv_w", "conv_b", "conv_ln_g", "conv_ln_b",
                   "ffn2_norm", "final_norm"]
    for k, n in enumerate(small_names):
        res[n] = (s_g[k], s_d[k], s_m[k], s_v[k])
    order = ["ffn1_norm", "ffn1_w_gate", "ffn1_w_up", "ffn1_w_down", "mix_norm", "w_in", "fgate_bias",
             "conv_w", "conv_b", "conv_ln_g", "conv_ln_b", "w_out", "ffn2_norm", "ffn2_w_gate", "ffn2_w_up",
             "ffn2_w_down", "final_norm"]
    outs = [loss, dx0[None]]
    for k in range(4):
        outs += [res[n][k] for n in order]
    return tuple(outs)
```
